```python
import jax, jax.numpy as jnp
from jax import lax
import numpy as np

D_MODEL = 1024
BATCH = 32
SEQ = 2048
DEPTH = 1

SB_HEAD_DIM = 64
SB_WIDTH = D_MODEL // 2
SB_HEADS = SB_WIDTH // SB_HEAD_DIM
GLA_HEADS = 4
GLA_WIDTH = D_MODEL - SB_WIDTH
GLA_DV = GLA_WIDTH // GLA_HEADS
GLA_DK = GLA_DV // 2
GLA_KEY_WIDTH = GLA_HEADS * GLA_DK
GLA_GATE_RANK = 16
GLA_GATE_NORMALIZER = 16.0
GLA_CHUNK = 64
Q_BLOCK = 128
D_FF = 2816
CONV_WIDTH = 3
EPS = 1e-6

IN_SPLITS = (SB_WIDTH, SB_WIDTH, SB_WIDTH,
             GLA_KEY_WIDTH, GLA_KEY_WIDTH, GLA_WIDTH,
             GLA_GATE_RANK, GLA_WIDTH)
IN_COLS = sum(IN_SPLITS)

kernel_name = "hybrid_stickbreak_gla_convffn"


def rms_norm(x, g):
    xf = x.astype(jnp.float32)
    y = xf * lax.rsqrt(jnp.mean(xf * xf, axis=-1, keepdims=True) + EPS)
    return (y * g.astype(jnp.float32)).astype(x.dtype)


def head_rms_norm(o, g):
    B, H, S, d = o.shape
    of = o.astype(jnp.float32)
    of = of * lax.rsqrt(jnp.mean(of * of, axis=-1, keepdims=True) + EPS)
    of = jnp.transpose(of, (0, 2, 1, 3)).reshape(B, S, H * d)
    return (of * g.astype(jnp.float32)).astype(o.dtype)


def to_heads(t, n_heads):
    B, S, W = t.shape
    return jnp.transpose(t.reshape(B, S, n_heads, W // n_heads), (0, 2, 1, 3))


def stick_breaking_attention(q, k, v):
    B, H, S, d = q.shape
    scale = d ** -0.5
    outs = []
    for i in range(S // Q_BLOCK):
        q0 = i * Q_BLOCK
        kn = q0 + Q_BLOCK
        z = jnp.einsum('bhqd,bhkd->bhqk', q[:, :, q0:kn], k[:, :, :kn]).astype(jnp.float32) * scale
        t_idx = q0 + jnp.arange(Q_BLOCK)[:, None]
        s_idx = jnp.arange(kn)[None, :]
        strict = s_idx < t_idx
        log1m = jnp.where(strict, -jax.nn.softplus(z), 0.0)
        after = lax.cumsum(log1m, axis=3, reverse=True) - log1m
        w = jnp.where(strict, jnp.exp(jax.nn.log_sigmoid(z) + after), 0.0)
        outs.append(jnp.einsum('bhqk,bhkd->bhqd', w.astype(v.dtype), v[:, :, :kn]))
    return jnp.concatenate(outs, axis=2)


def gla_chunked(q, k, v, log_a):
    B, H, S, dk = q.shape
    dv = v.shape[-1]
    C = GLA_CHUNK
    N = S // C
    f32 = jnp.float32
    qf = q.astype(f32).reshape(B, H, N, C, dk) * (dk ** -0.5)
    kf = k.astype(f32).reshape(B, H, N, C, dk)
    vf = v.astype(f32).reshape(B, H, N, C, dv)
    b = jnp.cumsum(log_a.astype(f32).reshape(B, H, N, C, dk), axis=3)
    b_last = b[:, :, :, -1:, :]
    q_dec = qf * jnp.exp(b)
    k_inv = kf * jnp.exp(-b)
    k_end = kf * jnp.exp(b_last - b)
    causal = jnp.tril(jnp.ones((C, C), dtype=f32))
    attn = jnp.einsum('bhnik,bhnjk->bhnij', q_dec, k_inv) * causal
    o_intra = jnp.einsum('bhnij,bhnjv->bhniv', attn, vf)
    chunk_kv = jnp.einsum('bhnck,bhncv->bhnkv', k_end, vf)
    decay = jnp.exp(b_last[:, :, :, 0, :])

    def step(state, inp):
        kv_n, dec_n = inp
        return dec_n[..., None] * state + kv_n, state

    init = jnp.zeros((B, H, dk, dv), dtype=f32)
    _, prev = lax.scan(step, init, (jnp.moveaxis(chunk_kv, 2, 0), jnp.moveaxis(decay, 2, 0)))
    prev = jnp.moveaxis(prev, 0, 2)
    o_inter = jnp.einsum('bhnck,bhnkv->bhncv', q_dec, prev)
    return (o_intra + o_inter).reshape(B, H, S, dv).astype(v.dtype)


def causal_depthwise_conv(u, w, bias):
    S = u.shape[1]
    up = jnp.pad(u, ((0, 0), (CONV_WIDTH - 1, 0), (0, 0)))
    out = bias
    for tap in range(CONV_WIDTH):
        out = out + w[tap] * up[:, tap:tap + S]
    return out


def _fwd_setup_inputs(seed: int = 0) -> dict:
    key = jax.random.key(seed)
    ks = jax.random.split(key, 16)
    nrm = lambda k, shape, s: jax.random.normal(k, shape, dtype=jnp.float32) * s
    gain = lambda k, shape: 1.0 + 0.02 * jax.random.normal(k, shape, dtype=jnp.float32)
    return {
        "x": nrm(ks[0], (BATCH, SEQ, D_MODEL), 1.0),
        "attn_norm_g": gain(ks[1], (DEPTH, D_MODEL)),
        "w_in": nrm(ks[2], (DEPTH, D_MODEL, IN_COLS), D_MODEL ** -0.5),
        "w_gate_up": nrm(ks[3], (DEPTH, GLA_GATE_RANK, GLA_KEY_WIDTH), GLA_GATE_RANK ** -0.5),
        "b_gate_up": nrm(ks[4], (DEPTH, GLA_KEY_WIDTH), 0.1),
        "sb_out_g": gain(ks[5], (DEPTH, SB_WIDTH)),
        "gla_out_g": gain(ks[6], (DEPTH, GLA_WIDTH)),
        "w_out": nrm(ks[7], (DEPTH, D_MODEL, D_MODEL), D_MODEL ** -0.5),
        "ffn_norm_g": gain(ks[8], (DEPTH, D_MODEL)),
        "w_ffn_up": nrm(ks[9], (DEPTH, D_MODEL, 2 * D_FF), D_MODEL ** -0.5),
        "conv_w": nrm(ks[10], (DEPTH, CONV_WIDTH, 2 * D_FF), CONV_WIDTH ** -0.5),
        "conv_b": nrm(ks[11], (DEPTH, 2 * D_FF), 0.01),
        "w_ffn_down": nrm(ks[12], (DEPTH, D_FF, D_MODEL), D_FF ** -0.5),
        "final_norm_g": gain(ks[13], (D_MODEL,)),
    }


def _fwd_reference(x, attn_norm_g, w_in, w_gate_up, b_gate_up, sb_out_g, gla_out_g, w_out,
              ffn_norm_g, w_ffn_up, conv_w, conv_b, w_ffn_down, final_norm_g):
    offsets = list(np.cumsum(IN_SPLITS)[:-1])
    for l in range(DEPTH):
        h = rms_norm(x, attn_norm_g[l])
        proj = h @ w_in[l]
        sb_q, sb_k, sb_v, g_q, g_k, g_v, g_lr, g_og = jnp.split(proj, offsets, axis=-1)

        o_sb = stick_breaking_attention(to_heads(sb_q, SB_HEADS), to_heads(sb_k, SB_HEADS),
                                        to_heads(sb_v, SB_HEADS))
        o_sb = head_rms_norm(o_sb, sb_out_g[l])

        log_a = jax.nn.log_sigmoid((g_lr @ w_gate_up[l] + b_gate_up[l]).astype(jnp.float32)) / GLA_GATE_NORMALIZER
        o_gla = gla_chunked(to_heads(g_q, GLA_HEADS), to_heads(g_k, GLA_HEADS),
                            to_heads(g_v, GLA_HEADS), to_heads(log_a, GLA_HEADS))
        o_gla = head_rms_norm(o_gla, gla_out_g[l]) * jax.nn.silu(g_og)

        x = x + jnp.concatenate([o_sb, o_gla], axis=-1) @ w_out[l]

        h = rms_norm(x, ffn_norm_g[l])
        u = causal_depthwise_conv(h @ w_ffn_up[l], conv_w[l], conv_b[l])
        a, val = jnp.split(u, 2, axis=-1)
        x = x + (jax.nn.silu(a) * val) @ w_ffn_down[l]
    return rms_norm(x, final_norm_g)


import jax as _jax
import jax.numpy as _jnp

TWIN_FORMAT = 'train_step'
FWD_PARAMS = ['x', 'attn_norm_g', 'w_in', 'w_gate_up', 'b_gate_up', 'sb_out_g', 'gla_out_g', 'w_out', 'ffn_norm_g', 'w_ffn_up', 'conv_w', 'conv_b', 'w_ffn_down', 'final_norm_g']
TWIN_WEIGHTS = ['attn_norm_g', 'w_in', 'w_gate_up', 'b_gate_up', 'sb_out_g', 'gla_out_g', 'w_out', 'ffn_norm_g', 'w_ffn_up', 'conv_w', 'conv_b', 'w_ffn_down', 'final_norm_g']
TWIN_DIFF_INPUT = 'x'
TWIN_INPUTS = ['x', 'attn_norm_g', 'w_in', 'w_gate_up', 'b_gate_up', 'sb_out_g', 'gla_out_g', 'w_out', 'ffn_norm_g', 'w_ffn_up', 'conv_w', 'conv_b', 'w_ffn_down', 'final_norm_g', 'loss_target', 'm_attn_norm_g', 'm_w_in', 'm_w_gate_up', 'm_b_gate_up', 'm_sb_out_g', 'm_gla_out_g', 'm_w_out', 'm_ffn_norm_g', 'm_w_ffn_up', 'm_conv_w', 'm_conv_b', 'm_w_ffn_down', 'm_final_norm_g', 'v_attn_norm_g', 'v_w_in', 'v_w_gate_up', 'v_b_gate_up', 'v_sb_out_g', 'v_gla_out_g', 'v_w_out', 'v_ffn_norm_g', 'v_w_ffn_up', 'v_conv_w', 'v_conv_b', 'v_w_ffn_down', 'v_final_norm_g']
TWIN_OUTPUTS = ['loss', 'grad_x', 'grad_attn_norm_g', 'grad_w_in', 'grad_w_gate_up', 'grad_b_gate_up', 'grad_sb_out_g', 'grad_gla_out_g', 'grad_w_out', 'grad_ffn_norm_g', 'grad_w_ffn_up', 'grad_conv_w', 'grad_conv_b', 'grad_w_ffn_down', 'grad_final_norm_g', 'delta_attn_norm_g', 'delta_w_in', 'delta_w_gate_up', 'delta_b_gate_up', 'delta_sb_out_g', 'delta_gla_out_g', 'delta_w_out', 'delta_ffn_norm_g', 'delta_w_ffn_up', 'delta_conv_w', 'delta_conv_b', 'delta_w_ffn_down', 'delta_final_norm_g', 'new_m_attn_norm_g', 'new_m_w_in', 'new_m_w_gate_up', 'new_m_b_gate_up', 'new_m_sb_out_g', 'new_m_gla_out_g', 'new_m_w_out', 'new_m_ffn_norm_g', 'new_m_w_ffn_up', 'new_m_conv_w', 'new_m_conv_b', 'new_m_w_ffn_down', 'new_m_final_norm_g', 'new_v_attn_norm_g', 'new_v_w_in', 'new_v_w_gate_up', 'new_v_b_gate_up', 'new_v_sb_out_g', 'new_v_gla_out_g', 'new_v_w_out', 'new_v_ffn_norm_g', 'new_v_w_ffn_up', 'new_v_conv_w', 'new_v_conv_b', 'new_v_w_ffn_down', 'new_v_final_norm_g']
TWIN_LEAF_KINDS = {'loss': 'loss', 'grad_x': 'grad_x', 'grad_attn_norm_g': 'grad_w', 'grad_w_in': 'grad_w', 'grad_w_gate_up': 'grad_w', 'grad_b_gate_up': 'grad_w', 'grad_sb_out_g': 'grad_w', 'grad_gla_out_g': 'grad_w', 'grad_w_out': 'grad_w', 'grad_ffn_norm_g': 'grad_w', 'grad_w_ffn_up': 'grad_w', 'grad_conv_w': 'grad_w', 'grad_conv_b': 'grad_w', 'grad_w_ffn_down': 'grad_w', 'grad_final_norm_g': 'grad_w', 'delta_attn_norm_g': 'delta_w', 'delta_w_in': 'delta_w', 'delta_w_gate_up': 'delta_w', 'delta_b_gate_up': 'delta_w', 'delta_sb_out_g': 'delta_w', 'delta_gla_out_g': 'delta_w', 'delta_w_out': 'delta_w', 'delta_ffn_norm_g': 'delta_w', 'delta_w_ffn_up': 'delta_w', 'delta_conv_w': 'delta_w', 'delta_conv_b': 'delta_w', 'delta_w_ffn_down': 'delta_w', 'delta_final_norm_g': 'delta_w', 'new_m_attn_norm_g': 'new_m', 'new_m_w_in': 'new_m', 'new_m_w_gate_up': 'new_m', 'new_m_b_gate_up': 'new_m', 'new_m_sb_out_g': 'new_m', 'new_m_gla_out_g': 'new_m', 'new_m_w_out': 'new_m', 'new_m_ffn_norm_g': 'new_m', 'new_m_w_ffn_up': 'new_m', 'new_m_conv_w': 'new_m', 'new_m_conv_b': 'new_m', 'new_m_w_ffn_down': 'new_m', 'new_m_final_norm_g': 'new_m', 'new_v_attn_norm_g': 'new_v', 'new_v_w_in': 'new_v', 'new_v_w_gate_up': 'new_v', 'new_v_b_gate_up': 'new_v', 'new_v_sb_out_g': 'new_v', 'new_v_gla_out_g': 'new_v', 'new_v_w_out': 'new_v', 'new_v_ffn_norm_g': 'new_v', 'new_v_w_ffn_up': 'new_v', 'new_v_conv_w': 'new_v', 'new_v_conv_b': 'new_v', 'new_v_w_ffn_down': 'new_v', 'new_v_final_norm_g': 'new_v'}


def _forward(args):
    return _fwd_reference(*[args[k] for k in FWD_PARAMS])


def _output_shape():
    out = _jax.eval_shape(lambda: _forward(_fwd_setup_inputs(0)))
    return out.shape, out.dtype

N_MICROBATCH = 1
ADAM_LR = 0.001
ADAM_B1 = 0.9
ADAM_B2 = 0.999
ADAM_EPS = 1e-08
ADAM_WD = 0.01
ADAM_STEP = 10
PER_EXAMPLE_BATCH_AXIS = {'x': 0, 'loss_target': 0}
SHARED_INPUTS = []
_WEIGHT_DTYPES = {'attn_norm_g': _jnp.float32, 'w_in': _jnp.float32, 'w_gate_up': _jnp.float32, 'b_gate_up': _jnp.float32, 'sb_out_g': _jnp.float32, 'gla_out_g': _jnp.float32, 'w_out': _jnp.float32, 'ffn_norm_g': _jnp.float32, 'w_ffn_up': _jnp.float32, 'conv_w': _jnp.float32, 'conv_b': _jnp.float32, 'w_ffn_down': _jnp.float32, 'final_norm_g': _jnp.float32}
MOMENT_SCALE = {'attn_norm_g': 2.610266e-01, 'w_in': 1.493199e-01, 'w_gate_up': 2.148908e-02, 'b_gate_up': 9.279703e-02, 'sb_out_g': 2.231238e-01, 'gla_out_g': 1.326408e-01, 'w_out': 1.741150e-01, 'ffn_norm_g': 1.536832e-01, 'w_ffn_up': 6.585289e-02, 'conv_w': 6.495751e-02, 'conv_b': 6.349436e-02, 'w_ffn_down': 1.080298e-01, 'final_norm_g': 6.399738e+01}


def _to_microbatches(a, axis):
    t = _jnp.moveaxis(a, axis, 0)
    t = t.reshape((N_MICROBATCH, t.shape[0] // N_MICROBATCH) + t.shape[1:])
    return _jnp.moveaxis(t, 1, axis + 1)


def setup_inputs(seed: int = 0) -> dict:
    inp = _fwd_setup_inputs(seed)
    key = _jax.random.fold_in(_jax.random.key(seed), 7919)
    shape, _ = _output_shape()
    out = dict(inp)
    out["loss_target"] = _jax.random.normal(_jax.random.fold_in(key, 0), shape, _jnp.float32)
    for i, name in enumerate(TWIN_WEIGHTS):
        w = inp[name].astype(_jnp.float32)
        if MOMENT_SCALE is None:
            s = _jnp.sqrt(_jnp.mean(_jnp.square(w)) + 1e-30)
        else:
            s = MOMENT_SCALE[name]
        km, kv = _jax.random.split(_jax.random.fold_in(key, i + 1))
        out[name] = w
        out["m_" + name] = s * _jax.random.normal(km, w.shape, _jnp.float32)
        out["v_" + name] = (s * s) * _jax.random.uniform(kv, w.shape, _jnp.float32, 0.5, 1.5)
    if N_MICROBATCH > 1:
        for name, axis in PER_EXAMPLE_BATCH_AXIS.items():
            out[name] = _to_microbatches(out[name], axis)
    return {'x': out['x'], 'attn_norm_g': out['attn_norm_g'], 'w_in': out['w_in'], 'w_gate_up': out['w_gate_up'], 'b_gate_up': out['b_gate_up'], 'sb_out_g': out['sb_out_g'], 'gla_out_g': out['gla_out_g'], 'w_out': out['w_out'], 'ffn_norm_g': out['ffn_norm_g'], 'w_ffn_up': out['w_ffn_up'], 'conv_w': out['conv_w'], 'conv_b': out['conv_b'], 'w_ffn_down': out['w_ffn_down'], 'final_norm_g': out['final_norm_g'], 'loss_target': out['loss_target'], 'm_attn_norm_g': out['m_attn_norm_g'], 'm_w_in': out['m_w_in'], 'm_w_gate_up': out['m_w_gate_up'], 'm_b_gate_up': out['m_b_gate_up'], 'm_sb_out_g': out['m_sb_out_g'], 'm_gla_out_g': out['m_gla_out_g'], 'm_w_out': out['m_w_out'], 'm_ffn_norm_g': out['m_ffn_norm_g'], 'm_w_ffn_up': out['m_w_ffn_up'], 'm_conv_w': out['m_conv_w'], 'm_conv_b': out['m_conv_b'], 'm_w_ffn_down': out['m_w_ffn_down'], 'm_final_norm_g': out['m_final_norm_g'], 'v_attn_norm_g': out['v_attn_norm_g'], 'v_w_in': out['v_w_in'], 'v_w_gate_up': out['v_w_gate_up'], 'v_b_gate_up': out['v_b_gate_up'], 'v_sb_out_g': out['v_sb_out_g'], 'v_gla_out_g': out['v_gla_out_g'], 'v_w_out': out['v_w_out'], 'v_ffn_norm_g': out['v_ffn_norm_g'], 'v_w_ffn_up': out['v_w_ffn_up'], 'v_conv_w': out['v_conv_w'], 'v_conv_b': out['v_conv_b'], 'v_w_ffn_down': out['v_w_ffn_down'], 'v_final_norm_g': out['v_final_norm_g']}


def _loss(weights, diff, rest, loss_target):
    with _jax.named_scope("forward"):
        args = {**rest, TWIN_DIFF_INPUT: diff, **{k: w.astype(_WEIGHT_DTYPES[k]) for k, w in weights.items()}}
        y = _forward(args)
    with _jax.named_scope("loss_head"):
        err = _jnp.square(y.astype(_jnp.float32) - loss_target)
        return 0.5 * _jnp.sum(_jnp.mean(err, axis=-1)) if err.ndim else 0.5 * err


def _adamw(w, g, m, v):
    m = ADAM_B1 * m + (1.0 - ADAM_B1) * g
    v = ADAM_B2 * v + (1.0 - ADAM_B2) * _jnp.square(g)
    m_hat = m / (1.0 - ADAM_B1 ** ADAM_STEP)
    v_hat = v / (1.0 - ADAM_B2 ** ADAM_STEP)
    delta = -ADAM_LR * (m_hat / (_jnp.sqrt(v_hat) + ADAM_EPS) + ADAM_WD * w)
    return delta, m, v


def reference(x, attn_norm_g, w_in, w_gate_up, b_gate_up, sb_out_g, gla_out_g, w_out, ffn_norm_g, w_ffn_up, conv_w, conv_b, w_ffn_down, final_norm_g, loss_target, m_attn_norm_g, m_w_in, m_w_gate_up, m_b_gate_up, m_sb_out_g, m_gla_out_g, m_w_out, m_ffn_norm_g, m_w_ffn_up, m_conv_w, m_conv_b, m_w_ffn_down, m_final_norm_g, v_attn_norm_g, v_w_in, v_w_gate_up, v_b_gate_up, v_sb_out_g, v_gla_out_g, v_w_out, v_ffn_norm_g, v_w_ffn_up, v_conv_w, v_conv_b, v_w_ffn_down, v_final_norm_g):
    given = dict(x=x, attn_norm_g=attn_norm_g, w_in=w_in, w_gate_up=w_gate_up, b_gate_up=b_gate_up, sb_out_g=sb_out_g, gla_out_g=gla_out_g, w_out=w_out, ffn_norm_g=ffn_norm_g, w_ffn_up=w_ffn_up, conv_w=conv_w, conv_b=conv_b, w_ffn_down=w_ffn_down, final_norm_g=final_norm_g, loss_target=loss_target, m_attn_norm_g=m_attn_norm_g, m_w_in=m_w_in, m_w_gate_up=m_w_gate_up, m_b_gate_up=m_b_gate_up, m_sb_out_g=m_sb_out_g, m_gla_out_g=m_gla_out_g, m_w_out=m_w_out, m_ffn_norm_g=m_ffn_norm_g, m_w_ffn_up=m_w_ffn_up, m_conv_w=m_conv_w, m_conv_b=m_conv_b, m_w_ffn_down=m_w_ffn_down, m_final_norm_g=m_final_norm_g, v_attn_norm_g=v_attn_norm_g, v_w_in=v_w_in, v_w_gate_up=v_w_gate_up, v_b_gate_up=v_b_gate_up, v_sb_out_g=v_sb_out_g, v_gla_out_g=v_gla_out_g, v_w_out=v_w_out, v_ffn_norm_g=v_ffn_norm_g, v_w_ffn_up=v_w_ffn_up, v_conv_w=v_conv_w, v_conv_b=v_conv_b, v_w_ffn_down=v_w_ffn_down, v_final_norm_g=v_final_norm_g)
    weights = {n: given[n] for n in TWIN_WEIGHTS}
    shared = {n: given[n] for n in SHARED_INPUTS}
    per_example = {n: given[n] for n in ['x']}
    grad_fn = _jax.value_and_grad(_loss, argnums=(0, 1))

    def one_microbatch(ex, loss_target):
        ex = dict(ex)
        diff = ex.pop(TWIN_DIFF_INPUT)
        return grad_fn(weights, diff, {**shared, **ex}, loss_target)

    if N_MICROBATCH == 1:
        loss, (grad_w, grad_x) = one_microbatch(per_example, given["loss_target"])
    else:
        def body(carry, xs):
            loss_sum, grad_sum = carry
            l_k, (gw_k, gx_k) = one_microbatch(xs[0], xs[1])
            with _jax.named_scope("update"):
                return (loss_sum + l_k, _jax.tree.map(_jnp.add, grad_sum, gw_k)), gx_k

        init = (_jnp.zeros((), _jnp.float32), _jax.tree.map(_jnp.zeros_like, weights))
        (loss, grad_w), grad_x = _jax.lax.scan(body, init, (per_example, given["loss_target"]))
    with _jax.named_scope("update"):
        delta_w, new_m, new_v = {}, {}, {}
        for n in TWIN_WEIGHTS:
            delta_w[n], new_m[n], new_v[n] = _adamw(weights[n], grad_w[n], given["m_" + n], given["v_" + n])
    return (loss, grad_x, *[grad_w[n] for n in TWIN_WEIGHTS], *[delta_w[n] for n in TWIN_WEIGHTS],
            *[new_m[n] for n in TWIN_WEIGHTS], *[new_v[n] for n in TWIN_WEIGHTS])
```

```python
import functools

import numpy as np
import jax
import jax.numpy as jnp
from jax import lax
from jax.experimental import pallas as pl
from jax.experimental.pallas import tpu as pltpu

F32 = jnp.float32
BF16 = jnp.bfloat16
SDS = jax.ShapeDtypeStruct
BS = pl.BlockSpec

N_DEV = 8
D = 1024
EPS = 1e-6
SB_HD = 64
SB_W = 512
GLA_DK = 64
GLA_DV = 128
GLA_KW = 256
GLA_W = 512
GATE_RANK = 16
GATE_NORM = 16.0
CHUNK = 64
QB = 128
D_FF = 2816
IN_COLS = 3088
PROJ_W = 3200
LANE = 128
VMEM_LIMIT = 56 * 1024 * 1024

ADAM_LR, ADAM_B1, ADAM_B2, ADAM_EPS, ADAM_WD, ADAM_STEP = 0.001, 0.9, 0.999, 1e-08, 0.01, 10

R_IN, R_OUT, R_UP, R_DOWN = 386, 128, 704, 352
R_SMALL = 3
R_USED = R_IN + R_OUT + R_UP + R_DOWN + R_SMALL
R_PAD = 1584
R_TILE = 176


def _proj_perm():
    sbq, sbk, sbv = 0, 512, 1024
    gq, gk, gv, glr, gog = 1536, 1792, 2048, 2560, 2576
    cols = []
    for p in range(4):
        for base in (sbq, sbk, sbv):
            cols += list(range(base + 128 * p, base + 128 * p + 128))
    for p in range(2):
        cols += list(range(gq + 128 * p, gq + 128 * p + 128))
        cols += list(range(gk + 128 * p, gk + 128 * p + 128))
        cols += list(range(gv + 256 * p, gv + 256 * p + 256))
    cols += list(range(gog, gog + 512))
    cols += list(range(glr, glr + GATE_RANK)) + [IN_COLS] * (LANE - GATE_RANK)
    perm = np.asarray(cols, np.int32)
    inv = np.zeros((IN_COLS,), np.int32)
    for new, old in enumerate(cols):
        if old < IN_COLS:
            inv[old] = new
    return perm, inv


_PERM, _INV_PERM = _proj_perm()
OG_BLK = 5
GLR_BLK = 24


def _cp(sem=None, vmem=VMEM_LIMIT):
    return pltpu.CompilerParams(dimension_semantics=sem, vmem_limit_bytes=vmem)


def _dot(a, b):
    return lax.dot_general(a, b, (((1,), (0,)), ((), ())), preferred_element_type=F32)


def _dot_nt(a, b):
    return lax.dot_general(a, b, (((1,), (1,)), ((), ())), preferred_element_type=F32)


def _dot_tn(a, b):
    return lax.dot_general(a, b, (((0,), (0,)), ((), ())), preferred_element_type=F32)


def _bf(x):
    return x.astype(BF16)


def _split_dot(x, m, passes, left=True):
    acc = None
    r = x
    for i in range(passes):
        h = r.astype(BF16)
        t = _dot(h, m) if left else _dot(m, h)
        acc = t if acc is None else acc + t
        if i + 1 < passes:
            r = r - h.astype(F32)
    return acc


def _softplus(z):
    return jnp.maximum(z, 0.0) + jnp.log1p(jnp.exp(-jnp.abs(z)))


def _tile(n, pref, mult=LANE):
    best = None
    for t in range(mult, min(n, pref) + 1, mult):
        if n % t == 0:
            best = t
    return best if best is not None else n


def _mm(a, b, mode, name, out_dtype=F32, c=None, tm=512, tn=512, tk=512):
    if mode == "nn":
        (M, K), N = a.shape, b.shape[1]
    elif mode == "nt":
        (M, K), N = a.shape, b.shape[0]
    else:
        (K, M), N = a.shape, b.shape[1]
    tm, tn, tk = _tile(M, tm), _tile(N, tn), _tile(K, tk)
    nk = K // tk
    a_spec = BS((tk, tm), lambda i, j, k: (k, i)) if mode == "tn" else BS((tm, tk), lambda i, j, k: (i, k))
    b_spec = BS((tn, tk), lambda i, j, k: (j, k)) if mode == "nt" else BS((tk, tn), lambda i, j, k: (k, j))
    dotfn = {"nn": _dot, "nt": _dot_nt, "tn": _dot_tn}[mode]
    has_c = c is not None

    def body(*refs):
        if has_c:
            a_ref, b_ref, c_ref, o_ref, acc = refs
        else:
            a_ref, b_ref, o_ref, acc = refs
        k = pl.program_id(2)

        @pl.when(k == 0)
        def _():
            acc[...] = jnp.zeros_like(acc)

        acc[...] += dotfn(_bf(a_ref[...]), _bf(b_ref[...]))

        @pl.when(k == nk - 1)
        def _():
            r = acc[...]
            if has_c:
                r = r + c_ref[...]
            o_ref[...] = r.astype(out_dtype)

    in_specs = [a_spec, b_spec]
    args = [a, b]
    if has_c:
        in_specs.append(BS((tm, tn), lambda i, j, k: (i, j)))
        args.append(c)
    return pl.pallas_call(
        body, name=name, out_shape=SDS((M, N), out_dtype), grid=(M // tm, N // tn, nk),
        in_specs=in_specs, out_specs=BS((tm, tn), lambda i, j, k: (i, j)),
        scratch_shapes=[pltpu.VMEM((tm, tn), F32)],
        compiler_params=_cp(("parallel", "parallel", "arbitrary")),
    )(*args)


def _norm_proj(x, g, w):
    T, N = x.shape[0], w.shape[1]
    tm = _tile(T, 256)

    def body(x_ref, g_ref, w_ref, p_ref, h_ref):
        xv = x_ref[...]
        r = lax.rsqrt(jnp.mean(xv * xv, axis=-1, keepdims=True) + EPS)
        h = _bf((xv * r) * g_ref[...])
        h_ref[...] = h
        p_ref[...] = _dot(h, w_ref[...])

    return pl.pallas_call(
        body, name="norm_proj", out_shape=(SDS((T, N), F32), SDS((T, D), BF16)), grid=(T // tm,),
        in_specs=[BS((tm, D), lambda i: (i, 0)), BS((1, D), lambda i: (0, 0)), BS((D, N), lambda i: (0, 0))],
        out_specs=(BS((tm, N), lambda i: (i, 0)), BS((tm, D), lambda i: (i, 0))),
        compiler_params=_cp(("parallel",)),
    )(x, g, w)


def _sb_masks():
    row = lax.broadcasted_iota(jnp.int32, (QB, QB), 0)
    col = lax.broadcasted_iota(jnp.int32, (QB, QB), 1)
    lane = lax.broadcasted_iota(jnp.int32, (1, LANE), 1)
    return row, col, lane


def _sb_fwd(proj, S):
    T = proj.shape[0]
    nq = S // QB
    scale = SB_HD ** -0.5

    def body(qkv_ref, o_ref, tt_ref):
        row, col, lane = _sb_masks()
        msuf = _bf(row > col)

        def qloop(qi, _):
            r0 = pl.multiple_of(qi * QB, QB)
            qv = qkv_ref[pl.ds(r0, QB), 0:128]
            outs, tots = [], []
            for hh in range(2):
                hm = (lane // SB_HD) == hh
                qm = _bf(jnp.where(hm, qv, 0.0))

                def kloop(it, carry, qm=qm):
                    acc, cy = carry
                    kj = qi - it
                    c0 = pl.multiple_of(kj * QB, QB)
                    kv = _bf(qkv_ref[pl.ds(c0, QB), 128:256])
                    vv = _bf(qkv_ref[pl.ds(c0, QB), 256:384])
                    z = _dot_nt(qm, kv) * scale
                    sp = _softplus(z)
                    strict = (col + (kj - qi) * QB) < row
                    lg = jnp.where(strict, -sp, 0.0)
                    after = cy + _split_dot(lg, msuf, 2)
                    w = jnp.where(strict, jnp.exp((z - sp) + after), 0.0)
                    acc = acc + _dot(_bf(w), vv)
                    cy = cy + jnp.sum(lg, axis=1, keepdims=True)
                    return acc, cy

                acc, cy = lax.fori_loop(0, qi + 1, kloop,
                                        (jnp.zeros((QB, LANE), F32), jnp.zeros((QB, 1), F32)))
                outs.append(acc)
                tots.append(cy)
            o_ref[pl.ds(r0, QB), :] = jnp.where(lane < SB_HD, outs[0], outs[1])
            tt_ref[pl.ds(r0, QB), :] = jnp.where(lane < SB_HD, tots[0], tots[1])
            return 0

        lax.fori_loop(0, nq, qloop, 0)

    return pl.pallas_call(
        body, name="sb_fwd", out_shape=(SDS((T, SB_W), F32), SDS((T, SB_W), F32)), grid=(T // S, 4),
        in_specs=[BS((S, 384), lambda b, p: (b, p))],
        out_specs=(BS((S, LANE), lambda b, p: (b, p)), BS((S, LANE), lambda b, p: (b, p))),
        compiler_params=_cp(("parallel", "parallel")),
    )(proj)


def _log_sigmoid(x):
    return jnp.minimum(x, 0.0) - jnp.log1p(jnp.exp(-jnp.abs(x)))


def _gla_chunk_terms(blk_ref, glr_ref, wgu, bgu, r0, tri_incl):
    q = blk_ref[pl.ds(r0, CHUNK), 0:128]
    k = blk_ref[pl.ds(r0, CHUNK), 128:256]
    v = blk_ref[pl.ds(r0, CHUNK), 256:512]
    pre = _dot(_bf(glr_ref[pl.ds(r0, CHUNK), :]), wgu) + bgu
    la = _log_sigmoid(pre) / GATE_NORM
    b = _split_dot(la, tri_incl, 3, left=False)
    b_last = b[CHUNK - 1:CHUNK, :]
    eb = jnp.exp(b)
    qd = (q * (GLA_DK ** -0.5)) * eb
    ki = k * jnp.exp(-b)
    ke = k * jnp.exp(b_last - b)
    decay = jnp.exp(b_last)
    return q, k, v, pre, b, b_last, eb, qd, ki, ke, decay


def _gla_fwd(proj, wgu, bgu, S):
    T = proj.shape[0]
    nc = S // CHUNK

    def body(blk_ref, glr_ref, wgu_ref, bgu_ref, o_ref):
        rr = lax.broadcasted_iota(jnp.int32, (CHUNK, CHUNK), 0)
        cc = lax.broadcasted_iota(jnp.int32, (CHUNK, CHUNK), 1)
        lane = lax.broadcasted_iota(jnp.int32, (1, LANE), 1)
        causal = rr >= cc
        tri_incl = _bf(causal)
        wg = _bf(wgu_ref[...])
        bg = bgu_ref[...]

        def chunk(n, states):
            r0 = pl.multiple_of(n * CHUNK, CHUNK)
            _, _, v, _, _, _, _, qd, ki, ke, decay = _gla_chunk_terms(blk_ref, glr_ref, wg, bg, r0, tri_incl)
            new_states, outs = [], []
            for hh in range(2):
                hm = (lane // GLA_DK) == hh
                qm = _bf(jnp.where(hm, qd, 0.0))
                vh = _bf(v[:, 128 * hh:128 * hh + 128])
                st = states[hh]
                attn = jnp.where(causal, _dot_nt(qm, _bf(ki)), 0.0)
                outs.append(_dot(_bf(attn), vh) + _dot_nt(qm, _bf(st)))
                new_states.append(st * decay + _dot_tn(vh, _bf(ke)))
            o_ref[pl.ds(r0, CHUNK), :] = jnp.concatenate(outs, axis=1)
            return tuple(new_states)

        z = jnp.zeros((GLA_DV, LANE), F32)
        lax.fori_loop(0, nc, chunk, (z, z))

    return pl.pallas_call(
        body, name="gla_fwd", out_shape=SDS((T, GLA_W), F32), grid=(T // S, 2),
        in_specs=[BS((S, 512), lambda b, p: (b, 3 + p)), BS((S, LANE), lambda b, p: (b, GLR_BLK)),
                  BS((LANE, LANE), lambda b, p: (0, p)), BS((1, LANE), lambda b, p: (0, p))],
        out_specs=BS((S, 256), lambda b, p: (b, p)),
        compiler_params=_cp(("parallel", "parallel")),
    )(proj, proj, wgu, bgu)


def _head_blockdiag(width, hd):
    r = lax.broadcasted_iota(jnp.int32, (width, width), 0) // hd
    c = lax.broadcasted_iota(jnp.int32, (width, width), 1) // hd
    return _bf(r == c)


def _mix_out(o_sb, o_gla, proj, x, g_sb, g_gla, w_out, g2):
    T = x.shape[0]
    tm = _tile(T, 256)

    def body(osb_ref, ogl_ref, og_ref, x_ref, gsb_ref, ggl_ref, w_ref, g2_ref, x1_ref, oc_ref, h2_ref):
        bd64 = _head_blockdiag(SB_W, SB_HD)
        bd128 = _head_blockdiag(GLA_W, GLA_DV)
        o = osb_ref[...]
        r = lax.rsqrt(_split_dot(o * o, bd64, 2) * (1.0 / SB_HD) + EPS)
        c_sb = (o * r) * gsb_ref[...]
        o = ogl_ref[...]
        r = lax.rsqrt(_split_dot(o * o, bd128, 2) * (1.0 / GLA_DV) + EPS)
        og = og_ref[...]
        c_gl = ((o * r) * ggl_ref[...]) * (og * jax.nn.sigmoid(og))
        oc = _bf(jnp.concatenate([c_sb, c_gl], axis=1))
        oc_ref[...] = oc
        x1 = x_ref[...] + _dot(oc, w_ref[...])
        x1_ref[...] = x1
        r2 = lax.rsqrt(jnp.mean(x1 * x1, axis=-1, keepdims=True) + EPS)
        h2_ref[...] = _bf((x1 * r2) * g2_ref[...])

    row = lambda w: BS((tm, w), lambda i: (i, 0))
    vec = lambda w: BS((1, w), lambda i: (0, 0))
    return pl.pallas_call(
        body, name="mix_out", out_shape=(SDS((T, D), F32), SDS((T, D), BF16), SDS((T, D), BF16)), grid=(T // tm,),
        in_specs=[row(SB_W), row(GLA_W), BS((tm, 512), lambda i: (i, OG_BLK)), row(D), vec(SB_W), vec(GLA_W),
                  BS((D, D), lambda i: (0, 0)), vec(D)],
        out_specs=(row(D), row(D), row(D)),
        compiler_params=_cp(("parallel",)),
    )(o_sb, o_gla, proj, x, g_sb, g_gla, w_out, g2)


CONV_ROWS = 256
CONV_TC = 256


def _rows_before(ref, r0, first):
    prev = ref[pl.ds(pl.multiple_of(jnp.maximum(r0 - 8, 0), 8), 8), :]
    return jnp.where(first, 0.0, prev)


def _rows_after(val_fn, r0, rows, last, S):
    nxt = val_fn(pl.multiple_of(jnp.minimum(r0 + rows, S - 8), 8))
    return jnp.where(last, 0.0, nxt)


def _shift_down(cur, prev8, k):
    cat = jnp.concatenate([prev8, cur], axis=0)
    return pltpu.roll(cat, k, 0)[8:]


def _shift_up(cur, next8, k):
    cat = jnp.concatenate([cur, next8], axis=0)
    return pltpu.roll(cat, cat.shape[0] - k, 0)[:cur.shape[0]]


def _conv_at(h_ref, cw, cb, r0, rows, first):
    cur = h_ref[pl.ds(r0, rows), :]
    prev8 = _rows_before(h_ref, r0, first)
    u = cb + cw[0:1, :] * _shift_down(cur, prev8, 2)
    u = u + cw[1:2, :] * _shift_down(cur, prev8, 1)
    return u + cw[2:3, :] * cur


def _conv_gate(hup_a, hup_v, cw_a, cw_v, cb_a, cb_v, S):
    T = hup_a.shape[0]
    rows = min(CONV_ROWS, S)
    nr = S // rows

    def body(ha_ref, hv_ref, cwa_ref, cwv_ref, cba_ref, cbv_ref, act_ref):
        cwa, cwv, cba, cbv = cwa_ref[...], cwv_ref[...], cba_ref[...], cbv_ref[...]

        def step(c, _):
            r0 = pl.multiple_of(c * rows, rows)
            ua = _conv_at(ha_ref, cwa, cba, r0, rows, c == 0)
            uv = _conv_at(hv_ref, cwv, cbv, r0, rows, c == 0)
            act_ref[pl.ds(r0, rows), :] = _bf((ua * jax.nn.sigmoid(ua)) * uv)
            return 0

        lax.fori_loop(0, nr, step, 0)

    blk = BS((S, CONV_TC), lambda b, j: (b, j))
    w3 = BS((3, CONV_TC), lambda b, j: (0, j))
    w1 = BS((1, CONV_TC), lambda b, j: (0, j))
    return pl.pallas_call(
        body, name="conv_gate", out_shape=SDS((T, D_FF), BF16), grid=(T // S, D_FF // CONV_TC),
        in_specs=[blk, blk, w3, w3, w1, w1], out_specs=blk,
        compiler_params=_cp(("parallel", "parallel")),
    )(hup_a, hup_v, cw_a, cw_v, cb_a, cb_v)


def _down_loss(act, w_down, x1, tgt, g3):
    T = x1.shape[0]
    tm = _tile(T, 256)

    def body(a_ref, w_ref, x1_ref, t_ref, g_ref, dx_ref, dg_ref, ls_ref):
        @pl.when(pl.program_id(0) == 0)
        def _():
            dg_ref[...] = jnp.zeros_like(dg_ref)
            ls_ref[...] = jnp.zeros_like(ls_ref)

        g = g_ref[...]
        x2 = x1_ref[...] + _dot(a_ref[...], w_ref[...])
        r = lax.rsqrt(jnp.mean(x2 * x2, axis=-1, keepdims=True) + EPS)
        xh = x2 * r
        e = xh * g - t_ref[...]
        ls_ref[...] += 0.5 * jnp.sum(jnp.mean(e * e, axis=-1, keepdims=True), axis=0, keepdims=True)
        dy = e * (1.0 / D)
        dxh = dy * g
        dx_ref[...] = r * (dxh - xh * jnp.mean(dxh * xh, axis=-1, keepdims=True))
        dg_ref[...] += jnp.sum(dy * xh, axis=0, keepdims=True)

    row = lambda w: BS((tm, w), lambda i: (i, 0))
    return pl.pallas_call(
        body, name="down_loss", out_shape=(SDS((T, D), F32), SDS((1, D), F32), SDS((1, LANE), F32)), grid=(T // tm,),
        in_specs=[row(D_FF), BS((D_FF, D), lambda i: (0, 0)), row(D), row(D), BS((1, D), lambda i: (0, 0))],
        out_specs=(row(D), BS((1, D), lambda i: (0, 0)), BS((1, LANE), lambda i: (0, 0))),
        compiler_params=_cp(("arbitrary",)),
    )(act, w_down, x1, tgt, g3)


def _conv_gate_bwd(hup_a, hup_v, dact, cw_a, cw_v, cb_a, cb_v, S):
    T = hup_a.shape[0]
    rows = min(CONV_ROWS, S)
    nr = S // rows

    def body(ha_ref, hv_ref, da_ref, cwa_ref, cwv_ref, cba_ref, cbv_ref,
             dha_ref, dhv_ref, dcwa_ref, dcwv_ref, dcba_ref, dcbv_ref):
        @pl.when(pl.program_id(1) == 0)
        def _():
            for r in (dcwa_ref, dcwv_ref, dcba_ref, dcbv_ref):
                r[...] = jnp.zeros_like(r)

        cwa, cwv, cba, cbv = cwa_ref[...], cwv_ref[...], cba_ref[...], cbv_ref[...]

        def du_at(r0, n, first):
            ua = _conv_at(ha_ref, cwa, cba, r0, n, first)
            uv = _conv_at(hv_ref, cwv, cbv, r0, n, first)
            da = da_ref[pl.ds(r0, n), :]
            sg = jax.nn.sigmoid(ua)
            dua = (da * uv) * (sg * (1.0 + ua * (1.0 - sg)))
            duv = da * (ua * sg)
            return dua, duv

        def step(c, _):
            r0 = pl.multiple_of(c * rows, rows)
            first, last = c == 0, c == nr - 1
            dua, duv = du_at(r0, rows, first)
            n0 = pl.multiple_of(jnp.minimum(r0 + rows, S - 8), 8)
            nua, nuv = du_at(n0, 8, False)
            nua = jnp.where(last, 0.0, nua)
            nuv = jnp.where(last, 0.0, nuv)
            for (h_ref, cw, du, nu, dh_ref, dcw_ref, dcb_ref) in (
                    (ha_ref, cwa, dua, nua, dha_ref, dcwa_ref, dcba_ref),
                    (hv_ref, cwv, duv, nuv, dhv_ref, dcwv_ref, dcbv_ref)):
                dh = cw[2:3, :] * du + cw[1:2, :] * _shift_up(du, nu, 1) + cw[0:1, :] * _shift_up(du, nu, 2)
                dh_ref[pl.ds(r0, rows), :] = _bf(dh)
                cur = h_ref[pl.ds(r0, rows), :]
                prev8 = _rows_before(h_ref, r0, first)
                dcw_ref[0:1, :] += jnp.sum(du * _shift_down(cur, prev8, 2), axis=0, keepdims=True)
                dcw_ref[1:2, :] += jnp.sum(du * _shift_down(cur, prev8, 1), axis=0, keepdims=True)
                dcw_ref[2:3, :] += jnp.sum(du * cur, axis=0, keepdims=True)
                dcb_ref[...] += jnp.sum(du, axis=0, keepdims=True)
            return 0

        lax.fori_loop(0, nr, step, 0)

    blk = BS((S, CONV_TC), lambda j, b: (b, j))
    w3 = BS((3, CONV_TC), lambda j, b: (0, j))
    w1 = BS((1, CONV_TC), lambda j, b: (0, j))
    return pl.pallas_call(
        body, name="conv_gate_bwd",
        out_shape=(SDS((T, D_FF), BF16), SDS((T, D_FF), BF16), SDS((3, D_FF), F32), SDS((3, D_FF), F32),
                   SDS((1, D_FF), F32), SDS((1, D_FF), F32)),
        grid=(D_FF // CONV_TC, T // S),
        in_specs=[blk, blk, blk, w3, w3, w1, w1], out_specs=(blk, blk, w3, w3, w1, w1),
        compiler_params=_cp(("parallel", "arbitrary")),
    )(hup_a, hup_v, dact, cw_a, cw_v, cb_a, cb_v)


def _rms_bwd(x, g, dh, dres, name):
    T = x.shape[0]
    tm = _tile(T, 512)

    def body(x_ref, g_ref, dh_ref, dr_ref, dx_ref, dg_ref):
        @pl.when(pl.program_id(0) == 0)
        def _():
            dg_ref[...] = jnp.zeros_like(dg_ref)

        xv = x_ref[...]
        dh = dh_ref[...]
        r = lax.rsqrt(jnp.mean(xv * xv, axis=-1, keepdims=True) + EPS)
        xh = xv * r
        dxh = dh * g_ref[...]
        dx_ref[...] = dr_ref[...] + r * (dxh - xh * jnp.mean(dxh * xh, axis=-1, keepdims=True))
        dg_ref[...] += jnp.sum(dh * xh, axis=0, keepdims=True)

    row = BS((tm, D), lambda i: (i, 0))
    vec = BS((1, D), lambda i: (0, 0))
    return pl.pallas_call(
        body, name=name, out_shape=(SDS((T, D), F32), SDS((1, D), F32)), grid=(T // tm,),
        in_specs=[row, vec, row, row], out_specs=(row, vec),
        compiler_params=_cp(("arbitrary",)),
    )(x, g, dh, dres)


def _mix_bwd(docat, o_sb, o_gla, proj, g_sb, g_gla):
    T = docat.shape[0]
    tm = _tile(T, 256)

    def body(d_ref, osb_ref, ogl_ref, og_ref, gsb_ref, ggl_ref, dsb_ref, dgl_ref, dog_ref, dgsb_ref, dggl_ref):
        @pl.when(pl.program_id(0) == 0)
        def _():
            dgsb_ref[...] = jnp.zeros_like(dgsb_ref)
            dggl_ref[...] = jnp.zeros_like(dggl_ref)

        bd64 = _head_blockdiag(SB_W, SB_HD)
        bd128 = _head_blockdiag(GLA_W, GLA_DV)
        d = d_ref[:, 0:SB_W]
        o = osb_ref[...]
        r = lax.rsqrt(_split_dot(o * o, bd64, 2) * (1.0 / SB_HD) + EPS)
        n = o * r
        dn = d * gsb_ref[...]
        dgsb_ref[...] += jnp.sum(d * n, axis=0, keepdims=True)
        dsb_ref[...] = r * (dn - n * (_split_dot(dn * n, bd64, 2) * (1.0 / SB_HD)))

        d = d_ref[:, SB_W:D]
        o = ogl_ref[...]
        r = lax.rsqrt(_split_dot(o * o, bd128, 2) * (1.0 / GLA_DV) + EPS)
        n = o * r
        og = og_ref[...]
        sg = jax.nn.sigmoid(og)
        dm = d * (og * sg)
        dog_ref[...] = _bf((d * (n * ggl_ref[...])) * (sg * (1.0 + og * (1.0 - sg))))
        dn = dm * ggl_ref[...]
        dggl_ref[...] += jnp.sum(dm * n, axis=0, keepdims=True)
        dgl_ref[...] = r * (dn - n * (_split_dot(dn * n, bd128, 2) * (1.0 / GLA_DV)))

    row = lambda w: BS((tm, w), lambda i: (i, 0))
    vec = lambda w: BS((1, w), lambda i: (0, 0))
    ogb = BS((tm, 512), lambda i: (i, OG_BLK))
    return pl.pallas_call(
        body, name="mix_bwd",
        out_shape=(SDS((T, SB_W), F32), SDS((T, GLA_W), F32), SDS((T, PROJ_W), BF16), SDS((1, SB_W), F32),
                   SDS((1, GLA_W), F32)),
        grid=(T // tm,),
        in_specs=[row(D), row(SB_W), row(GLA_W), ogb, vec(SB_W), vec(GLA_W)],
        out_specs=(row(SB_W), row(GLA_W), ogb, vec(SB_W), vec(GLA_W)),
        compiler_params=_cp(("arbitrary",)),
    )(docat, o_sb, o_gla, proj, g_sb, g_gla)


def _sb_bwd(proj, tt, do, dproj, S):
    T = proj.shape[0]
    nq = S // QB
    scale = SB_HD ** -0.5

    def body(qkv_ref, tt_ref, do_ref, dp_in_ref, dp_ref, dk_acc, dv_acc):
        del dp_in_ref
        row, col, lane = _sb_masks()
        mincl = _bf(row <= col)
        mexcl = _bf(row < col)
        dk_acc[...] = jnp.zeros_like(dk_acc)
        dv_acc[...] = jnp.zeros_like(dv_acc)

        def qloop(qi, _):
            r0 = pl.multiple_of(qi * QB, QB)
            qv = qkv_ref[pl.ds(r0, QB), 0:128]
            dov = do_ref[pl.ds(r0, QB), :]
            ttv = tt_ref[pl.ds(r0, QB), :]
            dq_tot = jnp.zeros((QB, LANE), F32)
            for hh in range(2):
                hm = (lane // SB_HD) == hh
                qm = _bf(jnp.where(hm, qv, 0.0))
                dom = _bf(jnp.where(hm, dov, 0.0))
                tot = ttv[:, SB_HD * hh:SB_HD * hh + 1]

                def kloop(kj, carry, qm=qm, dom=dom, tot=tot, hm=hm):
                    dq, lc, pc = carry
                    c0 = pl.multiple_of(kj * QB, QB)
                    kv = _bf(qkv_ref[pl.ds(c0, QB), 128:256])
                    vv = _bf(qkv_ref[pl.ds(c0, QB), 256:384])
                    z = _dot_nt(qm, kv) * scale
                    sp = _softplus(z)
                    strict = (col + (kj - qi) * QB) < row
                    lg = jnp.where(strict, -sp, 0.0)
                    after = tot - (lc + _split_dot(lg, mincl, 2))
                    gl = z - sp
                    w = jnp.where(strict, jnp.exp(gl + after), 0.0)
                    du = w * _dot_nt(dom, vv)
                    beta = jnp.exp(gl)
                    pex = pc + _split_dot(du, mexcl, 2)
                    dz = _bf(jnp.where(strict, du * (1.0 - beta) - beta * pex, 0.0) * scale)
                    dq = dq + jnp.where(hm, _dot(dz, kv), 0.0)
                    dk_acc[pl.ds(c0, QB), :] += _dot_tn(dz, qm)
                    dv_acc[pl.ds(c0, QB), :] += _dot_tn(_bf(w), dom)
                    lc = lc + jnp.sum(lg, axis=1, keepdims=True)
                    pc = pc + jnp.sum(du, axis=1, keepdims=True)
                    return dq, lc, pc

                zc = jnp.zeros((QB, 1), F32)
                dq_tot, _, _ = lax.fori_loop(0, qi + 1, kloop, (dq_tot, zc, zc))
            dp_ref[pl.ds(r0, QB), 0:128] = _bf(dq_tot)
            return 0

        lax.fori_loop(0, nq, qloop, 0)
        dp_ref[:, 128:256] = _bf(dk_acc[...])
        dp_ref[:, 256:384] = _bf(dv_acc[...])

    blk = BS((S, 384), lambda b, p: (b, p))
    col = BS((S, LANE), lambda b, p: (b, p))
    return pl.pallas_call(
        body, name="sb_bwd", out_shape=SDS((T, PROJ_W), BF16), grid=(T // S, 4),
        in_specs=[blk, col, col, BS(memory_space=pl.ANY)], out_specs=blk,
        scratch_shapes=[pltpu.VMEM((S, LANE), F32), pltpu.VMEM((S, LANE), F32)],
        input_output_aliases={3: 0},
        compiler_params=_cp(("parallel", "parallel")),
    )(proj, tt, do, dproj)


def _gla_bwd(proj, wgu, bgu, do, dproj, S):
    T = proj.shape[0]
    nc = S // CHUNK

    def body(blk_ref, glr_ref, wgu_ref, bgu_ref, do_ref, dp_in_ref, dp_ref, dpre_ref, st_ref):
        del dp_in_ref
        rr = lax.broadcasted_iota(jnp.int32, (CHUNK, CHUNK), 0)
        cc = lax.broadcasted_iota(jnp.int32, (CHUNK, CHUNK), 1)
        lane = lax.broadcasted_iota(jnp.int32, (1, LANE), 1)
        causal = rr >= cc
        tri_incl = _bf(causal)
        tri_rev = _bf(rr <= cc)
        wg = _bf(wgu_ref[...])
        bg = bgu_ref[...]

        def fwd_chunk(n, states):
            r0 = pl.multiple_of(n * CHUNK, CHUNK)
            _, _, v, _, _, _, _, _, _, ke, decay = _gla_chunk_terms(blk_ref, glr_ref, wg, bg, r0, tri_incl)
            new_states = []
            for hh in range(2):
                st_ref[hh, n] = states[hh]
                vh = _bf(v[:, 128 * hh:128 * hh + 128])
                new_states.append(states[hh] * decay + _dot_tn(vh, _bf(ke)))
            return tuple(new_states)

        z = jnp.zeros((GLA_DV, LANE), F32)
        lax.fori_loop(0, nc, fwd_chunk, (z, z))

        def bwd_chunk(it, dstates):
            n = nc - 1 - it
            r0 = pl.multiple_of(n * CHUNK, CHUNK)
            _, _, v, pre, b, b_last, eb, qd, ki, ke, decay = _gla_chunk_terms(
                blk_ref, glr_ref, wg, bg, r0, tri_incl)
            dqd = jnp.zeros((CHUNK, LANE), F32)
            dki = jnp.zeros((CHUNK, LANE), F32)
            dke = jnp.zeros((CHUNK, LANE), F32)
            ddec = jnp.zeros((1, LANE), F32)
            new_dstates, dvs = [], []
            for hh in range(2):
                hm = (lane // GLA_DK) == hh
                qm = _bf(jnp.where(hm, qd, 0.0))
                kem = _bf(jnp.where(hm, ke, 0.0))
                vh = _bf(v[:, 128 * hh:128 * hh + 128])
                doh = _bf(do_ref[pl.ds(r0, CHUNK), 128 * hh:128 * hh + 128])
                st = st_ref[hh, n]
                dst = dstates[hh]
                attn = _bf(jnp.where(causal, _dot_nt(qm, _bf(ki)), 0.0))
                dattn = _bf(jnp.where(causal, _dot_nt(doh, vh), 0.0))
                dvs.append(_dot_tn(attn, doh) + _dot_nt(kem, _bf(dst)))
                dqd = dqd + jnp.where(hm, _dot(dattn, _bf(ki)) + _dot(doh, _bf(st)), 0.0)
                dki = dki + _dot_tn(dattn, qm)
                dke = dke + jnp.where(hm, _dot(vh, _bf(dst)), 0.0)
                ddec = ddec + jnp.where(hm, jnp.sum(dst * st, axis=0, keepdims=True), 0.0)
                new_dstates.append(dst * decay + _dot_tn(doh, qm))
            einv = jnp.exp(-b)
            eend = jnp.exp(b_last - b)
            dq = (dqd * eb) * (GLA_DK ** -0.5)
            dk = dki * einv + dke * eend
            db = dqd * qd - dki * ki - dke * ke
            db_last = jnp.sum(dke * ke, axis=0, keepdims=True) + ddec * decay
            dla = _split_dot(db, tri_rev, 3, left=False) + db_last
            dpre_ref[pl.ds(r0, CHUNK), :] = (dla * (1.0 / GATE_NORM)) * (1.0 - jax.nn.sigmoid(pre))
            dp_ref[pl.ds(r0, CHUNK), 0:128] = _bf(dq)
            dp_ref[pl.ds(r0, CHUNK), 128:256] = _bf(dk)
            dp_ref[pl.ds(r0, CHUNK), 256:512] = _bf(jnp.concatenate(dvs, axis=1))
            return tuple(new_dstates)

        lax.fori_loop(0, nc, bwd_chunk, (z, z))

    return pl.pallas_call(
        body, name="gla_bwd", out_shape=(SDS((T, PROJ_W), BF16), SDS((T, GLA_KW), F32)), grid=(T // S, 2),
        in_specs=[BS((S, 512), lambda b, p: (b, 3 + p)), BS((S, LANE), lambda b, p: (b, GLR_BLK)),
                  BS((LANE, LANE), lambda b, p: (0, p)), BS((1, LANE), lambda b, p: (0, p)),
                  BS((S, 256), lambda b, p: (b, p)), BS(memory_space=pl.ANY)],
        out_specs=(BS((S, 512), lambda b, p: (b, 3 + p)), BS((S, LANE), lambda b, p: (b, p))),
        scratch_shapes=[pltpu.VMEM((2, nc, GLA_DV, LANE), F32)],
        input_output_aliases={5: 0},
        compiler_params=_cp(("parallel", "parallel")),
    )(proj, proj, wgu, bgu, do, dproj)


def _gate_bwd(dpre, proj, wgu, dproj):
    T = dpre.shape[0]
    tm = _tile(T, 512)

    def body(dpre_ref, glr_ref, wgu_ref, dp_in_ref, dp_ref, dw_ref, db_ref):
        del dp_in_ref

        @pl.when(pl.program_id(0) == 0)
        def _():
            dw_ref[...] = jnp.zeros_like(dw_ref)
            db_ref[...] = jnp.zeros_like(db_ref)

        dpre = dpre_ref[...]
        dp_ref[...] = _bf(_dot_nt(_bf(dpre), _bf(wgu_ref[...])))
        dw_ref[...] += _dot_tn(_bf(glr_ref[...]), _bf(dpre))
        db_ref[...] += jnp.sum(dpre, axis=0, keepdims=True)

    glr = BS((tm, LANE), lambda i: (i, GLR_BLK))
    return pl.pallas_call(
        body, name="gate_bwd",
        out_shape=(SDS((T, PROJ_W), BF16), SDS((LANE, GLA_KW), F32), SDS((1, GLA_KW), F32)), grid=(T // tm,),
        in_specs=[BS((tm, GLA_KW), lambda i: (i, 0)), glr, BS((LANE, GLA_KW), lambda i: (0, 0)),
                  BS(memory_space=pl.ANY)],
        out_specs=(glr, BS((LANE, GLA_KW), lambda i: (0, 0)), BS((1, GLA_KW), lambda i: (0, 0))),
        input_output_aliases={3: 0},
        compiler_params=_cp(("arbitrary",)),
    )(dpre, proj, wgu, dproj)


def _exchange(send, gather, name):
    shape = send.shape if not gather else (N_DEV,) + send.shape

    def body(s_ref, o_ref, ssem, rsem, lsem):
        x, y, c = lax.axis_index("x"), lax.axis_index("y"), lax.axis_index("c")
        me = 4 * x + 2 * y + c
        sends, peers = [], []
        for k in range(1, N_DEV):
            px = (x + ((k >> 2) & 1)) % 2
            py = (y + ((k >> 1) & 1)) % 2
            pc = (c + (k & 1)) % 2
            peer = 4 * px + 2 * py + pc
            cp = pltpu.make_async_remote_copy(
                src_ref=s_ref if gather else s_ref.at[peer], dst_ref=o_ref.at[me],
                send_sem=ssem.at[k - 1], recv_sem=rsem.at[k - 1],
                device_id=(px, py, pc), device_id_type=pl.DeviceIdType.MESH)
            cp.start()
            sends.append(cp)
            peers.append(peer)
        mine = pltpu.make_async_copy(s_ref if gather else s_ref.at[me], o_ref.at[me], lsem)
        mine.start()
        for k, (cp, peer) in enumerate(zip(sends, peers)):
            cp.wait_send()
            pltpu.make_async_remote_copy(
                src_ref=s_ref if gather else s_ref.at[me], dst_ref=o_ref.at[peer],
                send_sem=ssem.at[k], recv_sem=rsem.at[k],
                device_id=(x, y, c), device_id_type=pl.DeviceIdType.MESH).wait_recv()
        mine.wait()

    hbm = BS(memory_space=pltpu.HBM)
    return pl.pallas_call(
        body, name=name, out_shape=SDS(shape, send.dtype), in_specs=[hbm], out_specs=hbm,
        scratch_shapes=[pltpu.SemaphoreType.DMA((N_DEV - 1,)), pltpu.SemaphoreType.DMA((N_DEV - 1,)),
                        pltpu.SemaphoreType.DMA(())],
    )(send)


def _adamw_math(w, g, m, v):
    m = ADAM_B1 * m + (1.0 - ADAM_B1) * g
    v = ADAM_B2 * v + (1.0 - ADAM_B2) * (g * g)
    m_hat = m / (1.0 - ADAM_B1 ** ADAM_STEP)
    v_hat = v / (1.0 - ADAM_B2 ** ADAM_STEP)
    delta = -ADAM_LR * (m_hat / (jnp.sqrt(v_hat) + ADAM_EPS) + ADAM_WD * w)
    return delta, m, v


def _sum_adamw(parts, w, m, v, tr, name):
    R, C = w.shape

    def body(p_ref, w_ref, m_ref, v_ref, g_ref, d_ref, nm_ref, nv_ref):
        g = p_ref[0].astype(F32)
        for d in range(1, N_DEV):
            g = g + p_ref[d].astype(F32)
        delta, nm, nv = _adamw_math(w_ref[...], g, m_ref[...], v_ref[...])
        g_ref[...] = g
        d_ref[...] = delta
        nm_ref[...] = nm
        nv_ref[...] = nv

    blk = BS((tr, C), lambda i: (i, 0))
    out = SDS((R, C), F32)
    return pl.pallas_call(
        body, name=name, out_shape=(out, out, out, out), grid=(R // tr,),
        in_specs=[BS((N_DEV, tr, C), lambda i: (0, i, 0)), blk, blk, blk], out_specs=(blk, blk, blk, blk),
        compiler_params=_cp(("parallel",)),
    )(parts, w, m, v)


def _pack_rows(pieces, rows_pad):
    cat = jnp.concatenate(pieces, axis=0)
    return jnp.pad(cat, ((0, rows_pad - cat.shape[0]), (0, 0)))


def _as_rows(a):
    return a.reshape(-1, D)


def _flat_pad_rows(parts, rows):
    flat = jnp.concatenate([p.reshape(-1) for p in parts])
    return jnp.pad(flat, (0, rows * D - flat.shape[0])).reshape(rows, D)


def _cols_from_gathered(g, rows, cols):
    return jnp.transpose(g.reshape(N_DEV, rows, cols), (1, 0, 2)).reshape(rows, N_DEV * cols)


def _cols_to_pieces(a, cols):
    rows = a.shape[0]
    return jnp.transpose(a.reshape(rows, N_DEV, cols), (1, 0, 2)).reshape(N_DEV, -1, D)


SMALL_ROWS = 16


def kernel(x, attn_norm_g, w_in, w_gate_up, b_gate_up, sb_out_g, gla_out_g, w_out, ffn_norm_g, w_ffn_up, conv_w, conv_b, w_ffn_down, final_norm_g, loss_target, m_attn_norm_g, m_w_in, m_w_gate_up, m_b_gate_up, m_sb_out_g, m_gla_out_g, m_w_out, m_ffn_norm_g, m_w_ffn_up, m_conv_w, m_conv_b, m_w_ffn_down, m_final_norm_g, v_attn_norm_g, v_w_in, v_w_gate_up, v_b_gate_up, v_sb_out_g, v_gla_out_g, v_w_out, v_ffn_norm_g, v_w_ffn_up, v_conv_w, v_conv_b, v_w_ffn_down, v_final_norm_g):
    Bd, S, _ = x.shape
    T = Bd * S
    x2d = x.reshape(T, D)
    tgt = loss_target.reshape(T, D)
    c_up = w_ffn_up.shape[2]
    c_gu = w_gate_up.shape[2]
    c_in = w_in.shape[2]

    def big_pack(win, wout, wup, wdown, wgu, cw, dtype):
        small = _flat_pad_rows([wgu, cw], R_SMALL)
        rows = [_as_rows(win[0]), wout[0], _as_rows(wup[0]), wdown[0], small]
        return _pack_rows([r.astype(dtype) for r in rows], R_PAD)

    gw = _exchange(big_pack(w_in, w_out, w_ffn_up, w_ffn_down, w_gate_up, conv_w, BF16), True, "gather_weights")
    gs = _exchange(_flat_pad_rows([w_gate_up, conv_w], 8), True, "gather_small")
    o = 0
    w_in_f = _cols_from_gathered(gw[:, o:o + R_IN], D, c_in); o += R_IN
    w_out_f = gw[:, o:o + R_OUT].reshape(D, D); o += R_OUT
    w_up_f = _cols_from_gathered(gw[:, o:o + R_UP], D, c_up); o += R_UP
    w_down_f = gw[:, o:o + R_DOWN].reshape(D_FF, D)
    gsf = gs.reshape(N_DEV, -1)
    n_gu = GATE_RANK * c_gu
    wgu_f = jnp.transpose(gsf[:, :n_gu].reshape(N_DEV, GATE_RANK, c_gu), (1, 0, 2)).reshape(GATE_RANK, GLA_KW)
    cw_f = jnp.transpose(gsf[:, n_gu:n_gu + 3 * c_up].reshape(N_DEV, 3, c_up), (1, 0, 2)).reshape(3, 2 * D_FF)
    w_in_p = jnp.pad(w_in_f, ((0, 0), (0, 1)))[:, _PERM]
    wgu_p = jnp.pad(wgu_f, ((0, LANE - GATE_RANK), (0, 0)))
    w_up_a, w_up_v = w_up_f[:, :D_FF], w_up_f[:, D_FF:]
    cw_a, cw_v = cw_f[:, :D_FF], cw_f[:, D_FF:]
    cb_a, cb_v = conv_b[:, :D_FF], conv_b[:, D_FF:]
    g3 = final_norm_g.reshape(1, D)

    proj, h1 = _norm_proj(x2d, attn_norm_g, w_in_p)
    o_sb, tt = _sb_fwd(proj, S)
    o_gla = _gla_fwd(proj, wgu_p, b_gate_up, S)
    x1, ocat, h2 = _mix_out(o_sb, o_gla, proj, x2d, sb_out_g, gla_out_g, w_out_f, ffn_norm_g)
    hup_a = _mm(h2, w_up_a, "nn", "ffn_up_a", tm=512, tn=1408, tk=1024)
    hup_v = _mm(h2, w_up_v, "nn", "ffn_up_v", tm=512, tn=1408, tk=1024)
    act = _conv_gate(hup_a, hup_v, cw_a, cw_v, cb_a, cb_v, S)
    dx2, dg3, loss_dev = _down_loss(act, w_down_f, x1, tgt, g3)

    dw_down = _mm(act, dx2, "tn", "dw_down", tm=1408, tn=1024, tk=512)
    dact = _mm(dx2, w_down_f, "nt", "dact", tm=512, tn=1408, tk=1024)
    dhup_a, dhup_v, dcw_a, dcw_v, dcb_a, dcb_v = _conv_gate_bwd(hup_a, hup_v, dact, cw_a, cw_v, cb_a, cb_v, S)
    dw_up_a = _mm(h2, dhup_a, "tn", "dw_up_a", tm=1024, tn=1408, tk=512)
    dw_up_v = _mm(h2, dhup_v, "tn", "dw_up_v", tm=1024, tn=1408, tk=512)
    dh2 = _mm(dhup_a, w_up_a, "nt", "dh2_a", tm=512, tn=1024, tk=1408)
    dh2 = _mm(dhup_v, w_up_v, "nt", "dh2_v", c=dh2, tm=512, tn=1024, tk=1408)
    dx1, dg2 = _rms_bwd(x1, ffn_norm_g, dh2, dx2, "ffn_norm_bwd")

    dw_out = _mm(ocat, dx1, "tn", "dw_out", tm=1024, tn=1024, tk=512)
    docat = _mm(dx1, w_out_f, "nt", "docat", tm=512, tn=1024, tk=1024)
    do_sb, do_gla, dproj, dg_sb, dg_gla = _mix_bwd(docat, o_sb, o_gla, proj, sb_out_g, gla_out_g)
    dproj = _sb_bwd(proj, tt, do_sb, dproj, S)
    dproj, dpre = _gla_bwd(proj, wgu_p, b_gate_up, do_gla, dproj, S)
    dproj, dwgu, dbgu = _gate_bwd(dpre, proj, wgu_p, dproj)
    dw_in_p = _mm(h1, dproj, "tn", "dw_in", tm=1024, tn=640, tk=512)
    dh1 = _mm(dproj, w_in_p, "nt", "dh1", tm=512, tn=1024, tk=640)
    dx, dg1 = _rms_bwd(x2d, attn_norm_g, dh1, dx1, "attn_norm_bwd")

    dw_in = dw_in_p[:, _INV_PERM]
    dw_up = jnp.concatenate([dw_up_a, dw_up_v], axis=1)
    dcw = jnp.concatenate([dcw_a, dcw_v], axis=1)
    dwgu_pc = jnp.transpose(dwgu[:GATE_RANK].reshape(GATE_RANK, N_DEV, c_gu), (1, 0, 2)).reshape(N_DEV, -1)
    dcw_pc = jnp.transpose(dcw.reshape(3, N_DEV, c_up), (1, 0, 2)).reshape(N_DEV, -1)
    small_pc = jnp.concatenate([dwgu_pc, dcw_pc], axis=1)
    small_pc = jnp.pad(small_pc, ((0, 0), (0, R_SMALL * D - small_pc.shape[1]))).reshape(N_DEV, R_SMALL, D)
    pieces = jnp.concatenate(
        [_cols_to_pieces(dw_in, c_in), dw_out.reshape(N_DEV, R_OUT, D), _cols_to_pieces(dw_up, c_up),
         dw_down.reshape(N_DEV, R_DOWN, D), small_pc], axis=1)
    pieces = jnp.pad(pieces, ((0, 0), (0, R_PAD - R_USED), (0, 0))).astype(BF16)
    got = _exchange(pieces, False, "scatter_grads")
    wp = big_pack(w_in, w_out, w_ffn_up, w_ffn_down, w_gate_up, conv_w, F32)
    mp = big_pack(m_w_in, m_w_out, m_w_ffn_up, m_w_ffn_down, m_w_gate_up, m_conv_w, F32)
    vp = big_pack(v_w_in, v_w_out, v_w_ffn_up, v_w_ffn_down, v_w_gate_up, v_conv_w, F32)
    big = _sum_adamw(got, wp, mp, vp, R_TILE, "adamw_sharded")

    def unpack_big(a):
        o = 0
        r_in = a[o:o + R_IN].reshape(1, D, c_in); o += R_IN
        r_out = a[o:o + R_OUT].reshape(1, R_OUT, D); o += R_OUT
        r_up = a[o:o + R_UP].reshape(1, D, c_up); o += R_UP
        r_down = a[o:o + R_DOWN].reshape(1, R_DOWN, D); o += R_DOWN
        sm = a[o:o + R_SMALL].reshape(-1)
        r_gu = sm[:n_gu].reshape(1, GATE_RANK, c_gu)
        r_cw = sm[n_gu:n_gu + 3 * c_up].reshape(1, 3, c_up)
        return dict(w_in=r_in, w_out=r_out, w_ffn_up=r_up, w_ffn_down=r_down, w_gate_up=r_gu, conv_w=r_cw)

    rep_names = ["attn_norm_g", "b_gate_up", "sb_out_g", "gla_out_g", "ffn_norm_g", "conv_b", "final_norm_g"]
    rep_g = [dg1, dbgu, dg_sb, dg_gla, dg2, jnp.concatenate([dcb_a, dcb_v], axis=1), dg3]
    rep_w = [attn_norm_g, b_gate_up, sb_out_g, gla_out_g, ffn_norm_g, conv_b, final_norm_g]
    rep_m = [m_attn_norm_g, m_b_gate_up, m_sb_out_g, m_gla_out_g, m_ffn_norm_g, m_conv_b, m_final_norm_g]
    rep_v = [v_attn_norm_g, v_b_gate_up, v_sb_out_g, v_gla_out_g, v_ffn_norm_g, v_conv_b, v_final_norm_g]
    got_s = _exchange(_flat_pad_rows(rep_g, SMALL_ROWS), True, "gather_small_grads")
    small = _sum_adamw(got_s, _flat_pad_rows(rep_w, SMALL_ROWS), _flat_pad_rows(rep_m, SMALL_ROWS),
                       _flat_pad_rows(rep_v, SMALL_ROWS), SMALL_ROWS, "adamw_replicated")

    def unpack_small(a):
        flat, o, out = a.reshape(-1), 0, {}
        for n, w in zip(rep_names, rep_w):
            out[n] = flat[o:o + w.size].reshape(w.shape)
            o += w.size
        return out

    loss = lax.psum(loss_dev[0, 0], ("x", "y", "c"))
    order = ["attn_norm_g", "w_in", "w_gate_up", "b_gate_up", "sb_out_g", "gla_out_g", "w_out", "ffn_norm_g",
             "w_ffn_up", "conv_w", "conv_b", "w_ffn_down", "final_norm_g"]
    outs = [loss, dx.reshape(Bd, S, D)]
    for k in range(4):
        both = {**unpack_big(big[k]), **unpack_small(small[k])}
        outs += [both[n] for n in order]
    return tuple(outs)
```

```python
import functools

import numpy as np
import jax
import jax.numpy as jnp
from jax import lax
from jax.experimental import pallas as pl
from jax.experimental.pallas import tpu as pltpu

F32 = jnp.float32
BF16 = jnp.bfloat16
SDS = jax.ShapeDtypeStruct
BS = pl.BlockSpec

N_DEV = 8
D = 1024
EPS = 1e-6
SB_HD = 64
SB_W = 512
GLA_DK = 64
GLA_DV = 128
GLA_KW = 256
GLA_W = 512
GATE_RANK = 16
GATE_NORM = 16.0
CHUNK = 64
QB = 128
D_FF = 2816
IN_COLS = 3088
PROJ_W = 3200
LANE = 128
VMEM_LIMIT = 56 * 1024 * 1024

ADAM_LR, ADAM_B1, ADAM_B2, ADAM_EPS, ADAM_WD, ADAM_STEP = 0.001, 0.9, 0.999, 1e-08, 0.01, 10

R_IN, R_OUT, R_UP, R_DOWN = 386, 128, 704, 352
R_SMALL = 3
R_USED = R_IN + R_OUT + R_UP + R_DOWN + R_SMALL
R_PAD = 1584
R_TILE = 176


def _proj_perm():
    sbq, sbk, sbv = 0, 512, 1024
    gq, gk, gv, glr, gog = 1536, 1792, 2048, 2560, 2576
    cols = []
    for p in range(4):
        for base in (sbq, sbk, sbv):
            cols += list(range(base + 128 * p, base + 128 * p + 128))
    for p in range(2):
        cols += list(range(gq + 128 * p, gq + 128 * p + 128))
        cols += list(range(gk + 128 * p, gk + 128 * p + 128))
        cols += list(range(gv + 256 * p, gv + 256 * p + 256))
    cols += list(range(gog, gog + 512))
    cols += list(range(glr, glr + GATE_RANK)) + [IN_COLS] * (LANE - GATE_RANK)
    perm = np.asarray(cols, np.int32)
    inv = np.zeros((IN_COLS,), np.int32)
    for new, old in enumerate(cols):
        if old < IN_COLS:
            inv[old] = new
    return perm, inv


_PERM, _INV_PERM = _proj_perm()
OG_BLK = 5
GLR_BLK = 24


def _cp(sem=None, vmem=VMEM_LIMIT):
    return pltpu.CompilerParams(dimension_semantics=sem, vmem_limit_bytes=vmem)


def _dot(a, b):
    return lax.dot_general(a, b, (((1,), (0,)), ((), ())), preferred_element_type=F32)


def _dot_nt(a, b):
    return lax.dot_general(a, b, (((1,), (1,)), ((), ())), preferred_element_type=F32)


def _dot_tn(a, b):
    return lax.dot_general(a, b, (((0,), (0,)), ((), ())), preferred_element_type=F32)


def _bf(x):
    return x.astype(BF16)


def _split_dot(x, m, passes, left=True):
    acc = None
    r = x
    for i in range(passes):
        h = r.astype(BF16)
        t = _dot(h, m) if left else _dot(m, h)
        acc = t if acc is None else acc + t
        if i + 1 < passes:
            r = r - h.astype(F32)
    return acc


def _softplus(z):
    return jnp.maximum(z, 0.0) + jnp.log1p(jnp.exp(-jnp.abs(z)))


def _tile(n, pref, mult=LANE):
    best = None
    for t in range(mult, min(n, pref) + 1, mult):
        if n % t == 0:
            best = t
    return best if best is not None else n


def _mm(a, b, mode, name, out_dtype=F32, c=None, tm=512, tn=512, tk=512):
    if mode == "nn":
        (M, K), N = a.shape, b.shape[1]
    elif mode == "nt":
        (M, K), N = a.shape, b.shape[0]
    else:
        (K, M), N = a.shape, b.shape[1]
    tm, tn, tk = _tile(M, tm), _tile(N, tn), _tile(K, tk)
    nk = K // tk
    a_spec = BS((tk, tm), lambda i, j, k: (k, i)) if mode == "tn" else BS((tm, tk), lambda i, j, k: (i, k))
    b_spec = BS((tn, tk), lambda i, j, k: (j, k)) if mode == "nt" else BS((tk, tn), lambda i, j, k: (k, j))
    dotfn = {"nn": _dot, "nt": _dot_nt, "tn": _dot_tn}[mode]
    has_c = c is not None

    def body(*refs):
        if has_c:
            a_ref, b_ref, c_ref, o_ref, acc = refs
        else:
            a_ref, b_ref, o_ref, acc = refs
        k = pl.program_id(2)

        @pl.when(k == 0)
        def _():
            acc[...] = jnp.zeros_like(acc)

        acc[...] += dotfn(_bf(a_ref[...]), _bf(b_ref[...]))

        @pl.when(k == nk - 1)
        def _():
            r = acc[...]
            if has_c:
                r = r + c_ref[...]
            o_ref[...] = r.astype(out_dtype)

    in_specs = [a_spec, b_spec]
    args = [a, b]
    if has_c:
        in_specs.append(BS((tm, tn), lambda i, j, k: (i, j)))
        args.append(c)
    return pl.pallas_call(
        body, name=name, out_shape=SDS((M, N), out_dtype), grid=(M // tm, N // tn, nk),
        in_specs=in_specs, out_specs=BS((tm, tn), lambda i, j, k: (i, j)),
        scratch_shapes=[pltpu.VMEM((tm, tn), F32)],
        compiler_params=_cp(("parallel", "parallel", "arbitrary")),
    )(*args)


def _norm_proj(x, g, w):
    T, N = x.shape[0], w.shape[1]
    tm = _tile(T, 256)

    def body(x_ref, g_ref, w_ref, p_ref, h_ref):
        xv = x_ref[...]
        r = lax.rsqrt(jnp.mean(xv * xv, axis=-1, keepdims=True) + EPS)
        h = _bf((xv * r) * g_ref[...])
        h_ref[...] = h
        p_ref[...] = _dot(h, w_ref[...])

    return pl.pallas_call(
        body, name="norm_proj", out_shape=(SDS((T, N), F32), SDS((T, D), BF16)), grid=(T // tm,),
        in_specs=[BS((tm, D), lambda i: (i, 0)), BS((1, D), lambda i: (0, 0)), BS((D, N), lambda i: (0, 0))],
        out_specs=(BS((tm, N), lambda i: (i, 0)), BS((tm, D), lambda i: (i, 0))),
        compiler_params=_cp(("parallel",)),
    )(x, g, w)


TK = 256
SB_DEAD = -104.0
CNT_LANE = SB_HD - 1


def _sb_masks():
    row = lax.broadcasted_iota(jnp.int32, (QB, TK), 0)
    col = lax.broadcasted_iota(jnp.int32, (QB, TK), 1)
    lane = lax.broadcasted_iota(jnp.int32, (1, LANE), 1)
    kr = lax.broadcasted_iota(jnp.int32, (TK, TK), 0)
    kc = lax.broadcasted_iota(jnp.int32, (TK, TK), 1)
    return row, col, lane, kr, kc


def _sb_fwd(proj, S):
    T = proj.shape[0]
    nq = S // QB
    scale = SB_HD ** -0.5

    def body(qkv_ref, o_ref, tt_ref):
        row, col, lane, kr, kc = _sb_masks()
        msuf = _bf(kr > kc)

        def qloop(qi, _):
            r0 = pl.multiple_of(qi * QB, QB)
            qv = qkv_ref[pl.ds(r0, QB), 0:128]
            qms = [_bf(jnp.where((lane // SB_HD) == hh, qv, 0.0)) for hh in range(2)]
            top = (qi * QB) // TK

            def live(st):
                it, _, _, c0, c1 = st
                return jnp.logical_and(it <= top, jnp.maximum(jnp.max(c0), jnp.max(c1)) > SB_DEAD)

            def step(st):
                it, a0, a1, c0, c1 = st
                kt = top - it
                k0 = pl.multiple_of(kt * TK, TK)
                kv = _bf(qkv_ref[pl.ds(k0, TK), 128:256])
                vv = _bf(qkv_ref[pl.ds(k0, TK), 256:384])
                strict = (col + (kt * TK - qi * QB)) < row
                accs, cys = [], []
                for qm, acc, cy in ((qms[0], a0, c0), (qms[1], a1, c1)):
                    z = _dot_nt(qm, kv) * scale
                    sp = _softplus(z)
                    lg = jnp.where(strict, -sp, 0.0)
                    after = cy + _split_dot(lg, msuf, 2)
                    w = jnp.where(strict, jnp.exp((z - sp) + after), 0.0)
                    accs.append(acc + _dot(_bf(w), vv))
                    cys.append(cy + jnp.sum(lg, axis=1, keepdims=True))
                return it + 1, accs[0], accs[1], cys[0], cys[1]

            za, zc = jnp.zeros((QB, LANE), F32), jnp.zeros((QB, 1), F32)
            it, a0, a1, c0, c1 = lax.while_loop(live, step, (jnp.int32(0), za, za, zc, zc))
            o_ref[pl.ds(r0, QB), :] = jnp.where(lane < SB_HD, a0, a1)
            tt = jnp.where(lane < SB_HD, c0, c1)
            tt_ref[pl.ds(r0, QB), :] = jnp.where(lane == CNT_LANE, it.astype(F32), tt)
            return 0

        lax.fori_loop(0, nq, qloop, 0)

    return pl.pallas_call(
        body, name="sb_fwd", out_shape=(SDS((T, SB_W), F32), SDS((T, SB_W), F32)), grid=(T // S, 4),
        in_specs=[BS((S, 384), lambda b, p: (b, p))],
        out_specs=(BS((S, LANE), lambda b, p: (b, p)), BS((S, LANE), lambda b, p: (b, p))),
        compiler_params=_cp(("parallel", "parallel")),
    )(proj)


def _log_sigmoid(x):
    return jnp.minimum(x, 0.0) - jnp.log1p(jnp.exp(-jnp.abs(x)))


def _gla_chunk_terms(blk_ref, glr_ref, wgu, bgu, r0, tri_incl):
    q = blk_ref[pl.ds(r0, CHUNK), 0:128]
    k = blk_ref[pl.ds(r0, CHUNK), 128:256]
    v = blk_ref[pl.ds(r0, CHUNK), 256:512]
    pre = _dot(_bf(glr_ref[pl.ds(r0, CHUNK), :]), wgu) + bgu
    la = _log_sigmoid(pre) / GATE_NORM
    b = _split_dot(la, tri_incl, 3, left=False)
    b_last = b[CHUNK - 1:CHUNK, :]
    eb = jnp.exp(b)
    qd = (q * (GLA_DK ** -0.5)) * eb
    ki = k * jnp.exp(-b)
    ke = k * jnp.exp(b_last - b)
    decay = jnp.exp(b_last)
    return q, k, v, pre, b, b_last, eb, qd, ki, ke, decay


def _gla_fwd(proj, wgu, bgu, S):
    T = proj.shape[0]
    nc = S // CHUNK

    def body(blk_ref, glr_ref, wgu_ref, bgu_ref, o_ref):
        rr = lax.broadcasted_iota(jnp.int32, (CHUNK, CHUNK), 0)
        cc = lax.broadcasted_iota(jnp.int32, (CHUNK, CHUNK), 1)
        lane = lax.broadcasted_iota(jnp.int32, (1, LANE), 1)
        causal = rr >= cc
        tri_incl = _bf(causal)
        wg = _bf(wgu_ref[...])
        bg = bgu_ref[...]

        def chunk(n, states):
            r0 = pl.multiple_of(n * CHUNK, CHUNK)
            _, _, v, _, _, _, _, qd, ki, ke, decay = _gla_chunk_terms(blk_ref, glr_ref, wg, bg, r0, tri_incl)
            new_states, outs = [], []
            for hh in range(2):
                hm = (lane // GLA_DK) == hh
                qm = _bf(jnp.where(hm, qd, 0.0))
                vh = _bf(v[:, 128 * hh:128 * hh + 128])
                st = states[hh]
                attn = jnp.where(causal, _dot_nt(qm, _bf(ki)), 0.0)
                outs.append(_dot(_bf(attn), vh) + _dot_nt(qm, _bf(st)))
                new_states.append(st * decay + _dot_tn(vh, _bf(ke)))
            o_ref[pl.ds(r0, CHUNK), :] = jnp.concatenate(outs, axis=1)
            return tuple(new_states)

        z = jnp.zeros((GLA_DV, LANE), F32)
        lax.fori_loop(0, nc, chunk, (z, z))

    return pl.pallas_call(
        body, name="gla_fwd", out_shape=SDS((T, GLA_W), F32), grid=(T // S, 2),
        in_specs=[BS((S, 512), lambda b, p: (b, 3 + p)), BS((S, LANE), lambda b, p: (b, GLR_BLK)),
                  BS((LANE, LANE), lambda b, p: (0, p)), BS((1, LANE), lambda b, p: (0, p))],
        out_specs=BS((S, 256), lambda b, p: (b, p)),
        compiler_params=_cp(("parallel", "parallel")),
    )(proj, proj, wgu, bgu)


def _head_blockdiag(width, hd):
    r = lax.broadcasted_iota(jnp.int32, (width, width), 0) // hd
    c = lax.broadcasted_iota(jnp.int32, (width, width), 1) // hd
    return _bf(r == c)


def _mix_out(o_sb, o_gla, proj, x, g_sb, g_gla, w_out, g2):
    T = x.shape[0]
    tm = _tile(T, 256)

    def body(osb_ref, ogl_ref, og_ref, x_ref, gsb_ref, ggl_ref, w_ref, g2_ref, x1_ref, oc_ref, h2_ref):
        bd64 = _head_blockdiag(SB_W, SB_HD)
        bd128 = _head_blockdiag(GLA_W, GLA_DV)
        o = osb_ref[...]
        r = lax.rsqrt(_split_dot(o * o, bd64, 2) * (1.0 / SB_HD) + EPS)
        c_sb = (o * r) * gsb_ref[...]
        o = ogl_ref[...]
        r = lax.rsqrt(_split_dot(o * o, bd128, 2) * (1.0 / GLA_DV) + EPS)
        og = og_ref[...]
        c_gl = ((o * r) * ggl_ref[...]) * (og * jax.nn.sigmoid(og))
        oc = _bf(jnp.concatenate([c_sb, c_gl], axis=1))
        oc_ref[...] = oc
        x1 = x_ref[...] + _dot(oc, w_ref[...])
        x1_ref[...] = x1
        r2 = lax.rsqrt(jnp.mean(x1 * x1, axis=-1, keepdims=True) + EPS)
        h2_ref[...] = _bf((x1 * r2) * g2_ref[...])

    row = lambda w: BS((tm, w), lambda i: (i, 0))
    vec = lambda w: BS((1, w), lambda i: (0, 0))
    return pl.pallas_call(
        body, name="mix_out", out_shape=(SDS((T, D), F32), SDS((T, D), BF16), SDS((T, D), BF16)), grid=(T // tm,),
        in_specs=[row(SB_W), row(GLA_W), BS((tm, 512), lambda i: (i, OG_BLK)), row(D), vec(SB_W), vec(GLA_W),
                  BS((D, D), lambda i: (0, 0)), vec(D)],
        out_specs=(row(D), row(D), row(D)),
        compiler_params=_cp(("parallel",)),
    )(o_sb, o_gla, proj, x, g_sb, g_gla, w_out, g2)


CONV_ROWS = 256
CONV_TC = 256


def _rows_before(ref, r0, first):
    prev = ref[pl.ds(pl.multiple_of(jnp.maximum(r0 - 8, 0), 8), 8), :]
    return jnp.where(first, 0.0, prev)


def _rows_after(val_fn, r0, rows, last, S):
    nxt = val_fn(pl.multiple_of(jnp.minimum(r0 + rows, S - 8), 8))
    return jnp.where(last, 0.0, nxt)


def _shift_down(cur, prev8, k):
    cat = jnp.concatenate([prev8, cur], axis=0)
    return pltpu.roll(cat, k, 0)[8:]


def _shift_up(cur, next8, k):
    cat = jnp.concatenate([cur, next8], axis=0)
    return pltpu.roll(cat, cat.shape[0] - k, 0)[:cur.shape[0]]


def _conv_at(h_ref, cw, cb, r0, rows, first):
    cur = h_ref[pl.ds(r0, rows), :]
    prev8 = _rows_before(h_ref, r0, first)
    u = cb + cw[0:1, :] * _shift_down(cur, prev8, 2)
    u = u + cw[1:2, :] * _shift_down(cur, prev8, 1)
    return u + cw[2:3, :] * cur


def _conv_gate(hup_a, hup_v, cw_a, cw_v, cb_a, cb_v, S):
    T = hup_a.shape[0]
    rows = min(CONV_ROWS, S)
    nr = S // rows

    def body(ha_ref, hv_ref, cwa_ref, cwv_ref, cba_ref, cbv_ref, act_ref):
        cwa, cwv, cba, cbv = cwa_ref[...], cwv_ref[...], cba_ref[...], cbv_ref[...]

        def step(c, _):
            r0 = pl.multiple_of(c * rows, rows)
            ua = _conv_at(ha_ref, cwa, cba, r0, rows, c == 0)
            uv = _conv_at(hv_ref, cwv, cbv, r0, rows, c == 0)
            act_ref[pl.ds(r0, rows), :] = _bf((ua * jax.nn.sigmoid(ua)) * uv)
            return 0

        lax.fori_loop(0, nr, step, 0)

    blk = BS((S, CONV_TC), lambda b, j: (b, j))
    w3 = BS((3, CONV_TC), lambda b, j: (0, j))
    w1 = BS((1, CONV_TC), lambda b, j: (0, j))
    return pl.pallas_call(
        body, name="conv_gate", out_shape=SDS((T, D_FF), BF16), grid=(T // S, D_FF // CONV_TC),
        in_specs=[blk, blk, w3, w3, w1, w1], out_specs=blk,
        compiler_params=_cp(("parallel", "parallel")),
    )(hup_a, hup_v, cw_a, cw_v, cb_a, cb_v)


def _down_loss(act, w_down, x1, tgt, g3):
    T = x1.shape[0]
    tm = _tile(T, 256)

    def body(a_ref, w_ref, x1_ref, t_ref, g_ref, dx_ref, dg_ref, ls_ref):
        @pl.when(pl.program_id(0) == 0)
        def _():
            dg_ref[...] = jnp.zeros_like(dg_ref)
            ls_ref[...] = jnp.zeros_like(ls_ref)

        g = g_ref[...]
        x2 = x1_ref[...] + _dot(a_ref[...], w_ref[...])
        r = lax.rsqrt(jnp.mean(x2 * x2, axis=-1, keepdims=True) + EPS)
        xh = x2 * r
        e = xh * g - t_ref[...]
        ls_ref[...] += 0.5 * jnp.sum(jnp.mean(e * e, axis=-1, keepdims=True), axis=0, keepdims=True)
        dy = e * (1.0 / D)
        dxh = dy * g
        dx_ref[...] = r * (dxh - xh * jnp.mean(dxh * xh, axis=-1, keepdims=True))
        dg_ref[...] += jnp.sum(dy * xh, axis=0, keepdims=True)

    row = lambda w: BS((tm, w), lambda i: (i, 0))
    return pl.pallas_call(
        body, name="down_loss", out_shape=(SDS((T, D), F32), SDS((1, D), F32), SDS((1, LANE), F32)), grid=(T // tm,),
        in_specs=[row(D_FF), BS((D_FF, D), lambda i: (0, 0)), row(D), row(D), BS((1, D), lambda i: (0, 0))],
        out_specs=(row(D), BS((1, D), lambda i: (0, 0)), BS((1, LANE), lambda i: (0, 0))),
        compiler_params=_cp(("arbitrary",)),
    )(act, w_down, x1, tgt, g3)


def _conv_gate_bwd(hup_a, hup_v, dact, cw_a, cw_v, cb_a, cb_v, S):
    T = hup_a.shape[0]
    rows = min(CONV_ROWS, S)
    nr = S // rows

    def body(ha_ref, hv_ref, da_ref, cwa_ref, cwv_ref, cba_ref, cbv_ref,
             dha_ref, dhv_ref, dcwa_ref, dcwv_ref, dcba_ref, dcbv_ref):
        @pl.when(pl.program_id(1) == 0)
        def _():
            for r in (dcwa_ref, dcwv_ref, dcba_ref, dcbv_ref):
                r[...] = jnp.zeros_like(r)

        cwa, cwv, cba, cbv = cwa_ref[...], cwv_ref[...], cba_ref[...], cbv_ref[...]

        def du_at(r0, n, first):
            ua = _conv_at(ha_ref, cwa, cba, r0, n, first)
            uv = _conv_at(hv_ref, cwv, cbv, r0, n, first)
            da = da_ref[pl.ds(r0, n), :]
            sg = jax.nn.sigmoid(ua)
            dua = (da * uv) * (sg * (1.0 + ua * (1.0 - sg)))
            duv = da * (ua * sg)
            return dua, duv

        def step(c, _):
            r0 = pl.multiple_of(c * rows, rows)
            first, last = c == 0, c == nr - 1
            dua, duv = du_at(r0, rows, first)
            n0 = pl.multiple_of(jnp.minimum(r0 + rows, S - 8), 8)
            nua, nuv = du_at(n0, 8, False)
            nua = jnp.where(last, 0.0, nua)
            nuv = jnp.where(last, 0.0, nuv)
            for (h_ref, cw, du, nu, dh_ref, dcw_ref, dcb_ref) in (
                    (ha_ref, cwa, dua, nua, dha_ref, dcwa_ref, dcba_ref),
                    (hv_ref, cwv, duv, nuv, dhv_ref, dcwv_ref, dcbv_ref)):
                dh = cw[2:3, :] * du + cw[1:2, :] * _shift_up(du, nu, 1) + cw[0:1, :] * _shift_up(du, nu, 2)
                dh_ref[pl.ds(r0, rows), :] = _bf(dh)
                cur = h_ref[pl.ds(r0, rows), :]
                prev8 = _rows_before(h_ref, r0, first)
                dcw_ref[0:1, :] += jnp.sum(du * _shift_down(cur, prev8, 2), axis=0, keepdims=True)
                dcw_ref[1:2, :] += jnp.sum(du * _shift_down(cur, prev8, 1), axis=0, keepdims=True)
                dcw_ref[2:3, :] += jnp.sum(du * cur, axis=0, keepdims=True)
                dcb_ref[...] += jnp.sum(du, axis=0, keepdims=True)
            return 0

        lax.fori_loop(0, nr, step, 0)

    blk = BS((S, CONV_TC), lambda j, b: (b, j))
    w3 = BS((3, CONV_TC), lambda j, b: (0, j))
    w1 = BS((1, CONV_TC), lambda j, b: (0, j))
    return pl.pallas_call(
        body, name="conv_gate_bwd",
        out_shape=(SDS((T, D_FF), BF16), SDS((T, D_FF), BF16), SDS((3, D_FF), F32), SDS((3, D_FF), F32),
                   SDS((1, D_FF), F32), SDS((1, D_FF), F32)),
        grid=(D_FF // CONV_TC, T // S),
        in_specs=[blk, blk, blk, w3, w3, w1, w1], out_specs=(blk, blk, w3, w3, w1, w1),
        compiler_params=_cp(("parallel", "arbitrary")),
    )(hup_a, hup_v, dact, cw_a, cw_v, cb_a, cb_v)


def _rms_bwd(x, g, dh, dres, name):
    T = x.shape[0]
    tm = _tile(T, 512)

    def body(x_ref, g_ref, dh_ref, dr_ref, dx_ref, dg_ref):
        @pl.when(pl.program_id(0) == 0)
        def _():
            dg_ref[...] = jnp.zeros_like(dg_ref)

        xv = x_ref[...]
        dh = dh_ref[...]
        r = lax.rsqrt(jnp.mean(xv * xv, axis=-1, keepdims=True) + EPS)
        xh = xv * r
        dxh = dh * g_ref[...]
        dx_ref[...] = dr_ref[...] + r * (dxh - xh * jnp.mean(dxh * xh, axis=-1, keepdims=True))
        dg_ref[...] += jnp.sum(dh * xh, axis=0, keepdims=True)

    row = BS((tm, D), lambda i: (i, 0))
    vec = BS((1, D), lambda i: (0, 0))
    return pl.pallas_call(
        body, name=name, out_shape=(SDS((T, D), F32), SDS((1, D), F32)), grid=(T // tm,),
        in_specs=[row, vec, row, row], out_specs=(row, vec),
        compiler_params=_cp(("arbitrary",)),
    )(x, g, dh, dres)


def _mix_bwd(docat, o_sb, o_gla, proj, g_sb, g_gla):
    T = docat.shape[0]
    tm = _tile(T, 256)

    def body(d_ref, osb_ref, ogl_ref, og_ref, gsb_ref, ggl_ref, dsb_ref, dgl_ref, dog_ref, dgsb_ref, dggl_ref):
        @pl.when(pl.program_id(0) == 0)
        def _():
            dgsb_ref[...] = jnp.zeros_like(dgsb_ref)
            dggl_ref[...] = jnp.zeros_like(dggl_ref)

        bd64 = _head_blockdiag(SB_W, SB_HD)
        bd128 = _head_blockdiag(GLA_W, GLA_DV)
        d = d_ref[:, 0:SB_W]
        o = osb_ref[...]
        r = lax.rsqrt(_split_dot(o * o, bd64, 2) * (1.0 / SB_HD) + EPS)
        n = o * r
        dn = d * gsb_ref[...]
        dgsb_ref[...] += jnp.sum(d * n, axis=0, keepdims=True)
        dsb_ref[...] = r * (dn - n * (_split_dot(dn * n, bd64, 2) * (1.0 / SB_HD)))

        d = d_ref[:, SB_W:D]
        o = ogl_ref[...]
        r = lax.rsqrt(_split_dot(o * o, bd128, 2) * (1.0 / GLA_DV) + EPS)
        n = o * r
        og = og_ref[...]
        sg = jax.nn.sigmoid(og)
        dm = d * (og * sg)
        dog_ref[...] = _bf((d * (n * ggl_ref[...])) * (sg * (1.0 + og * (1.0 - sg))))
        dn = dm * ggl_ref[...]
        dggl_ref[...] += jnp.sum(dm * n, axis=0, keepdims=True)
        dgl_ref[...] = r * (dn - n * (_split_dot(dn * n, bd128, 2) * (1.0 / GLA_DV)))

    row = lambda w: BS((tm, w), lambda i: (i, 0))
    vec = lambda w: BS((1, w), lambda i: (0, 0))
    ogb = BS((tm, 512), lambda i: (i, OG_BLK))
    return pl.pallas_call(
        body, name="mix_bwd",
        out_shape=(SDS((T, SB_W), F32), SDS((T, GLA_W), F32), SDS((T, PROJ_W), BF16), SDS((1, SB_W), F32),
                   SDS((1, GLA_W), F32)),
        grid=(T // tm,),
        in_specs=[row(D), row(SB_W), row(GLA_W), ogb, vec(SB_W), vec(GLA_W)],
        out_specs=(row(SB_W), row(GLA_W), ogb, vec(SB_W), vec(GLA_W)),
        compiler_params=_cp(("arbitrary",)),
    )(docat, o_sb, o_gla, proj, g_sb, g_gla)


def _sb_bwd(proj, tt, do, dproj, S):
    T = proj.shape[0]
    nq = S // QB
    scale = SB_HD ** -0.5

    def body(qkv_ref, tt_ref, do_ref, dp_in_ref, dp_ref, dk_acc, dv_acc):
        del dp_in_ref
        row, col, lane, kr, kc = _sb_masks()
        mincl = _bf(kr <= kc)
        mexcl = _bf(kr < kc)
        dk_acc[...] = jnp.zeros_like(dk_acc)
        dv_acc[...] = jnp.zeros_like(dv_acc)

        def qloop(qi, _):
            r0 = pl.multiple_of(qi * QB, QB)
            qv = qkv_ref[pl.ds(r0, QB), 0:128]
            dov = do_ref[pl.ds(r0, QB), :]
            ttv = tt_ref[pl.ds(r0, QB), :]
            hms = [(lane // SB_HD) == hh for hh in range(2)]
            qms = [_bf(jnp.where(hm, qv, 0.0)) for hm in hms]
            doms = [_bf(jnp.where(hm, dov, 0.0)) for hm in hms]
            tots = [ttv[:, SB_HD * hh:SB_HD * hh + 1] for hh in range(2)]
            top = (qi * QB) // TK
            walked = jnp.max(ttv[:, CNT_LANE:CNT_LANE + 1]).astype(jnp.int32)

            def step(kt, st):
                dq, l0, l1, p0, p1 = st
                k0 = pl.multiple_of(kt * TK, TK)
                kv = _bf(qkv_ref[pl.ds(k0, TK), 128:256])
                vv = _bf(qkv_ref[pl.ds(k0, TK), 256:384])
                strict = (col + (kt * TK - qi * QB)) < row
                dk_t = dv_t = None
                lcs, pcs = [], []
                for hm, qm, dom, tot, lc, pc in ((hms[0], qms[0], doms[0], tots[0], l0, p0),
                                                 (hms[1], qms[1], doms[1], tots[1], l1, p1)):
                    z = _dot_nt(qm, kv) * scale
                    sp = _softplus(z)
                    lg = jnp.where(strict, -sp, 0.0)
                    after = tot - (lc + _split_dot(lg, mincl, 2))
                    gl = z - sp
                    w = jnp.where(strict, jnp.exp(gl + after), 0.0)
                    du = w * _dot_nt(dom, vv)
                    beta = jnp.exp(gl)
                    pex = pc + _split_dot(du, mexcl, 2)
                    dz = _bf(jnp.where(strict, du * (1.0 - beta) - beta * pex, 0.0) * scale)
                    dq = dq + jnp.where(hm, _dot(dz, kv), 0.0)
                    dk_h = _dot_tn(dz, qm)
                    dv_h = _dot_tn(_bf(w), dom)
                    dk_t = dk_h if dk_t is None else dk_t + dk_h
                    dv_t = dv_h if dv_t is None else dv_t + dv_h
                    lcs.append(lc + jnp.sum(lg, axis=1, keepdims=True))
                    pcs.append(pc + jnp.sum(du, axis=1, keepdims=True))
                dk_acc[pl.ds(k0, TK), :] += dk_t
                dv_acc[pl.ds(k0, TK), :] += dv_t
                return dq, lcs[0], lcs[1], pcs[0], pcs[1]

            zc = jnp.zeros((QB, 1), F32)
            dq, _, _, _, _ = lax.fori_loop(top - walked + 1, top + 1, step,
                                           (jnp.zeros((QB, LANE), F32), zc, zc, zc, zc))
            dp_ref[pl.ds(r0, QB), 0:128] = _bf(dq)
            return 0

        lax.fori_loop(0, nq, qloop, 0)
        dp_ref[:, 128:256] = _bf(dk_acc[...])
        dp_ref[:, 256:384] = _bf(dv_acc[...])

    blk = BS((S, 384), lambda b, p: (b, p))
    col_spec = BS((S, LANE), lambda b, p: (b, p))
    return pl.pallas_call(
        body, name="sb_bwd", out_shape=SDS((T, PROJ_W), BF16), grid=(T // S, 4),
        in_specs=[blk, col_spec, col_spec, BS(memory_space=pl.ANY)], out_specs=blk,
        scratch_shapes=[pltpu.VMEM((S, LANE), F32), pltpu.VMEM((S, LANE), F32)],
        input_output_aliases={3: 0},
        compiler_params=_cp(("parallel", "parallel")),
    )(proj, tt, do, dproj)


def _gla_bwd(proj, wgu, bgu, do, dproj, S):
    T = proj.shape[0]
    nc = S // CHUNK

    def body(blk_ref, glr_ref, wgu_ref, bgu_ref, do_ref, dp_in_ref, dp_ref, dpre_ref, st_ref):
        del dp_in_ref
        rr = lax.broadcasted_iota(jnp.int32, (CHUNK, CHUNK), 0)
        cc = lax.broadcasted_iota(jnp.int32, (CHUNK, CHUNK), 1)
        lane = lax.broadcasted_iota(jnp.int32, (1, LANE), 1)
        causal = rr >= cc
        tri_incl = _bf(causal)
        tri_rev = _bf(rr <= cc)
        wg = _bf(wgu_ref[...])
        bg = bgu_ref[...]

        def fwd_chunk(n, states):
            r0 = pl.multiple_of(n * CHUNK, CHUNK)
            _, _, v, _, _, _, _, _, _, ke, decay = _gla_chunk_terms(blk_ref, glr_ref, wg, bg, r0, tri_incl)
            new_states = []
            for hh in range(2):
                st_ref[hh, n] = states[hh]
                vh = _bf(v[:, 128 * hh:128 * hh + 128])
                new_states.append(states[hh] * decay + _dot_tn(vh, _bf(ke)))
            return tuple(new_states)

        z = jnp.zeros((GLA_DV, LANE), F32)
        lax.fori_loop(0, nc, fwd_chunk, (z, z))

        def bwd_chunk(it, dstates):
            n = nc - 1 - it
            r0 = pl.multiple_of(n * CHUNK, CHUNK)
            _, _, v, pre, b, b_last, eb, qd, ki, ke, decay = _gla_chunk_terms(
                blk_ref, glr_ref, wg, bg, r0, tri_incl)
            dqd = jnp.zeros((CHUNK, LANE), F32)
            dki = jnp.zeros((CHUNK, LANE), F32)
            dke = jnp.zeros((CHUNK, LANE), F32)
            ddec = jnp.zeros((1, LANE), F32)
            new_dstates, dvs = [], []
            for hh in range(2):
                hm = (lane // GLA_DK) == hh
                qm = _bf(jnp.where(hm, qd, 0.0))
                kem = _bf(jnp.where(hm, ke, 0.0))
                vh = _bf(v[:, 128 * hh:128 * hh + 128])
                doh = _bf(do_ref[pl.ds(r0, CHUNK), 128 * hh:128 * hh + 128])
                st = st_ref[hh, n]
                dst = dstates[hh]
                attn = _bf(jnp.where(causal, _dot_nt(qm, _bf(ki)), 0.0))
                dattn = _bf(jnp.where(causal, _dot_nt(doh, vh), 0.0))
                dvs.append(_dot_tn(attn, doh) + _dot_nt(kem, _bf(dst)))
                dqd = dqd + jnp.where(hm, _dot(dattn, _bf(ki)) + _dot(doh, _bf(st)), 0.0)
                dki = dki + _dot_tn(dattn, qm)
                dke = dke + jnp.where(hm, _dot(vh, _bf(dst)), 0.0)
                ddec = ddec + jnp.where(hm, jnp.sum(dst * st, axis=0, keepdims=True), 0.0)
                new_dstates.append(dst * decay + _dot_tn(doh, qm))
            einv = jnp.exp(-b)
            eend = jnp.exp(b_last - b)
            dq = (dqd * eb) * (GLA_DK ** -0.5)
            dk = dki * einv + dke * eend
            db = dqd * qd - dki * ki - dke * ke
            db_last = jnp.sum(dke * ke, axis=0, keepdims=True) + ddec * decay
            dla = _split_dot(db, tri_rev, 3, left=False) + db_last
            dpre_ref[pl.ds(r0, CHUNK), :] = (dla * (1.0 / GATE_NORM)) * (1.0 - jax.nn.sigmoid(pre))
            dp_ref[pl.ds(r0, CHUNK), 0:128] = _bf(dq)
            dp_ref[pl.ds(r0, CHUNK), 128:256] = _bf(dk)
            dp_ref[pl.ds(r0, CHUNK), 256:512] = _bf(jnp.concatenate(dvs, axis=1))
            return tuple(new_dstates)

        lax.fori_loop(0, nc, bwd_chunk, (z, z))

    return pl.pallas_call(
        body, name="gla_bwd", out_shape=(SDS((T, PROJ_W), BF16), SDS((T, GLA_KW), F32)), grid=(T // S, 2),
        in_specs=[BS((S, 512), lambda b, p: (b, 3 + p)), BS((S, LANE), lambda b, p: (b, GLR_BLK)),
                  BS((LANE, LANE), lambda b, p: (0, p)), BS((1, LANE), lambda b, p: (0, p)),
                  BS((S, 256), lambda b, p: (b, p)), BS(memory_space=pl.ANY)],
        out_specs=(BS((S, 512), lambda b, p: (b, 3 + p)), BS((S, LANE), lambda b, p: (b, p))),
        scratch_shapes=[pltpu.VMEM((2, nc, GLA_DV, LANE), F32)],
        input_output_aliases={5: 0},
        compiler_params=_cp(("parallel", "parallel")),
    )(proj, proj, wgu, bgu, do, dproj)


def _gate_bwd(dpre, proj, wgu, dproj):
    T = dpre.shape[0]
    tm = _tile(T, 512)

    def body(dpre_ref, glr_ref, wgu_ref, dp_in_ref, dp_ref, dw_ref, db_ref):
        del dp_in_ref

        @pl.when(pl.program_id(0) == 0)
        def _():
            dw_ref[...] = jnp.zeros_like(dw_ref)
            db_ref[...] = jnp.zeros_like(db_ref)

        dpre = dpre_ref[...]
        dp_ref[...] = _bf(_dot_nt(_bf(dpre), _bf(wgu_ref[...])))
        dw_ref[...] += _dot_tn(_bf(glr_ref[...]), _bf(dpre))
        db_ref[...] += jnp.sum(dpre, axis=0, keepdims=True)

    glr = BS((tm, LANE), lambda i: (i, GLR_BLK))
    return pl.pallas_call(
        body, name="gate_bwd",
        out_shape=(SDS((T, PROJ_W), BF16), SDS((LANE, GLA_KW), F32), SDS((1, GLA_KW), F32)), grid=(T // tm,),
        in_specs=[BS((tm, GLA_KW), lambda i: (i, 0)), glr, BS((LANE, GLA_KW), lambda i: (0, 0)),
                  BS(memory_space=pl.ANY)],
        out_specs=(glr, BS((LANE, GLA_KW), lambda i: (0, 0)), BS((1, GLA_KW), lambda i: (0, 0))),
        input_output_aliases={3: 0},
        compiler_params=_cp(("arbitrary",)),
    )(dpre, proj, wgu, dproj)


def _exchange(send, gather, name):
    shape = send.shape if not gather else (N_DEV,) + send.shape

    def body(s_ref, o_ref, ssem, rsem, lsem):
        x, y, c = lax.axis_index("x"), lax.axis_index("y"), lax.axis_index("c")
        me = 4 * x + 2 * y + c
        sends, peers = [], []
        for k in range(1, N_DEV):
            px = (x + ((k >> 2) & 1)) % 2
            py = (y + ((k >> 1) & 1)) % 2
            pc = (c + (k & 1)) % 2
            peer = 4 * px + 2 * py + pc
            cp = pltpu.make_async_remote_copy(
                src_ref=s_ref if gather else s_ref.at[peer], dst_ref=o_ref.at[me],
                send_sem=ssem.at[k - 1], recv_sem=rsem.at[k - 1],
                device_id=(px, py, pc), device_id_type=pl.DeviceIdType.MESH)
            cp.start()
            sends.append(cp)
            peers.append(peer)
        mine = pltpu.make_async_copy(s_ref if gather else s_ref.at[me], o_ref.at[me], lsem)
        mine.start()
        for k, (cp, peer) in enumerate(zip(sends, peers)):
            cp.wait_send()
            pltpu.make_async_remote_copy(
                src_ref=s_ref if gather else s_ref.at[me], dst_ref=o_ref.at[peer],
                send_sem=ssem.at[k], recv_sem=rsem.at[k],
                device_id=(x, y, c), device_id_type=pl.DeviceIdType.MESH).wait_recv()
        mine.wait()

    hbm = BS(memory_space=pltpu.HBM)
    return pl.pallas_call(
        body, name=name, out_shape=SDS(shape, send.dtype), in_specs=[hbm], out_specs=hbm,
        scratch_shapes=[pltpu.SemaphoreType.DMA((N_DEV - 1,)), pltpu.SemaphoreType.DMA((N_DEV - 1,)),
                        pltpu.SemaphoreType.DMA(())],
    )(send)


def _adamw_math(w, g, m, v):
    m = ADAM_B1 * m + (1.0 - ADAM_B1) * g
    v = ADAM_B2 * v + (1.0 - ADAM_B2) * (g * g)
    m_hat = m / (1.0 - ADAM_B1 ** ADAM_STEP)
    v_hat = v / (1.0 - ADAM_B2 ** ADAM_STEP)
    delta = -ADAM_LR * (m_hat / (jnp.sqrt(v_hat) + ADAM_EPS) + ADAM_WD * w)
    return delta, m, v


def _sum_adamw(parts, w, m, v, tr, name):
    R, C = w.shape

    def body(p_ref, w_ref, m_ref, v_ref, g_ref, d_ref, nm_ref, nv_ref):
        g = p_ref[0].astype(F32)
        for d in range(1, N_DEV):
            g = g + p_ref[d].astype(F32)
        delta, nm, nv = _adamw_math(w_ref[...], g, m_ref[...], v_ref[...])
        g_ref[...] = g
        d_ref[...] = delta
        nm_ref[...] = nm
        nv_ref[...] = nv

    blk = BS((tr, C), lambda i: (i, 0))
    out = SDS((R, C), F32)
    return pl.pallas_call(
        body, name=name, out_shape=(out, out, out, out), grid=(R // tr,),
        in_specs=[BS((N_DEV, tr, C), lambda i: (0, i, 0)), blk, blk, blk], out_specs=(blk, blk, blk, blk),
        compiler_params=_cp(("parallel",)),
    )(parts, w, m, v)


def _pack_rows(pieces, rows_pad):
    cat = jnp.concatenate(pieces, axis=0)
    return jnp.pad(cat, ((0, rows_pad - cat.shape[0]), (0, 0)))


def _as_rows(a):
    return a.reshape(-1, D)


def _flat_pad_rows(parts, rows):
    flat = jnp.concatenate([p.reshape(-1) for p in parts])
    return jnp.pad(flat, (0, rows * D - flat.shape[0])).reshape(rows, D)


def _cols_from_gathered(g, rows, cols):
    return jnp.transpose(g.reshape(N_DEV, rows, cols), (1, 0, 2)).reshape(rows, N_DEV * cols)


def _cols_to_pieces(a, cols):
    rows = a.shape[0]
    return jnp.transpose(a.reshape(rows, N_DEV, cols), (1, 0, 2)).reshape(N_DEV, -1, D)


SMALL_ROWS = 16


def kernel(x, attn_norm_g, w_in, w_gate_up, b_gate_up, sb_out_g, gla_out_g, w_out, ffn_norm_g, w_ffn_up, conv_w, conv_b, w_ffn_down, final_norm_g, loss_target, m_attn_norm_g, m_w_in, m_w_gate_up, m_b_gate_up, m_sb_out_g, m_gla_out_g, m_w_out, m_ffn_norm_g, m_w_ffn_up, m_conv_w, m_conv_b, m_w_ffn_down, m_final_norm_g, v_attn_norm_g, v_w_in, v_w_gate_up, v_b_gate_up, v_sb_out_g, v_gla_out_g, v_w_out, v_ffn_norm_g, v_w_ffn_up, v_conv_w, v_conv_b, v_w_ffn_down, v_final_norm_g):
    Bd, S, _ = x.shape
    T = Bd * S
    x2d = x.reshape(T, D)
    tgt = loss_target.reshape(T, D)
    c_up = w_ffn_up.shape[2]
    c_gu = w_gate_up.shape[2]
    c_in = w_in.shape[2]

    def big_pack(win, wout, wup, wdown, wgu, cw, dtype):
        small = _flat_pad_rows([wgu, cw], R_SMALL)
        rows = [_as_rows(win[0]), wout[0], _as_rows(wup[0]), wdown[0], small]
        return _pack_rows([r.astype(dtype) for r in rows], R_PAD)

    gw = _exchange(big_pack(w_in, w_out, w_ffn_up, w_ffn_down, w_gate_up, conv_w, BF16), True, "gather_weights")
    gs = _exchange(_flat_pad_rows([w_gate_up, conv_w], 8), True, "gather_small")
    o = 0
    w_in_f = _cols_from_gathered(gw[:, o:o + R_IN], D, c_in); o += R_IN
    w_out_f = gw[:, o:o + R_OUT].reshape(D, D); o += R_OUT
    w_up_f = _cols_from_gathered(gw[:, o:o + R_UP], D, c_up); o += R_UP
    w_down_f = gw[:, o:o + R_DOWN].reshape(D_FF, D)
    gsf = gs.reshape(N_DEV, -1)
    n_gu = GATE_RANK * c_gu
    wgu_f = jnp.transpose(gsf[:, :n_gu].reshape(N_DEV, GATE_RANK, c_gu), (1, 0, 2)).reshape(GATE_RANK, GLA_KW)
    cw_f = jnp.transpose(gsf[:, n_gu:n_gu + 3 * c_up].reshape(N_DEV, 3, c_up), (1, 0, 2)).reshape(3, 2 * D_FF)
    w_in_p = jnp.pad(w_in_f, ((0, 0), (0, 1)))[:, _PERM]
    wgu_p = jnp.pad(wgu_f, ((0, LANE - GATE_RANK), (0, 0)))
    w_up_a, w_up_v = w_up_f[:, :D_FF], w_up_f[:, D_FF:]
    cw_a, cw_v = cw_f[:, :D_FF], cw_f[:, D_FF:]
    cb_a, cb_v = conv_b[:, :D_FF], conv_b[:, D_FF:]
    g3 = final_norm_g.reshape(1, D)

    proj, h1 = _norm_proj(x2d, attn_norm_g, w_in_p)
    o_sb, tt = _sb_fwd(proj, S)
    o_gla = _gla_fwd(proj, wgu_p, b_gate_up, S)
    x1, ocat, h2 = _mix_out(o_sb, o_gla, proj, x2d, sb_out_g, gla_out_g, w_out_f, ffn_norm_g)
    hup_a = _mm(h2, w_up_a, "nn", "ffn_up_a", tm=512, tn=1408, tk=1024)
    hup_v = _mm(h2, w_up_v, "nn", "ffn_up_v", tm=512, tn=1408, tk=1024)
    act = _conv_gate(hup_a, hup_v, cw_a, cw_v, cb_a, cb_v, S)
    dx2, dg3, loss_dev = _down_loss(act, w_down_f, x1, tgt, g3)

    dw_down = _mm(act, dx2, "tn", "dw_down", tm=1408, tn=1024, tk=512)
    dact = _mm(dx2, w_down_f, "nt", "dact", tm=512, tn=1408, tk=1024)
    dhup_a, dhup_v, dcw_a, dcw_v, dcb_a, dcb_v = _conv_gate_bwd(hup_a, hup_v, dact, cw_a, cw_v, cb_a, cb_v, S)
    dw_up_a = _mm(h2, dhup_a, "tn", "dw_up_a", tm=1024, tn=1408, tk=512)
    dw_up_v = _mm(h2, dhup_v, "tn", "dw_up_v", tm=1024, tn=1408, tk=512)
    dh2 = _mm(dhup_a, w_up_a, "nt", "dh2_a", tm=512, tn=1024, tk=1408)
    dh2 = _mm(dhup_v, w_up_v, "nt", "dh2_v", c=dh2, tm=512, tn=1024, tk=1408)
    dx1, dg2 = _rms_bwd(x1, ffn_norm_g, dh2, dx2, "ffn_norm_bwd")

    dw_out = _mm(ocat, dx1, "tn", "dw_out", tm=1024, tn=1024, tk=512)
    docat = _mm(dx1, w_out_f, "nt", "docat", tm=512, tn=1024, tk=1024)
    do_sb, do_gla, dproj, dg_sb, dg_gla = _mix_bwd(docat, o_sb, o_gla, proj, sb_out_g, gla_out_g)
    dproj = _sb_bwd(proj, tt, do_sb, dproj, S)
    dproj, dpre = _gla_bwd(proj, wgu_p, b_gate_up, do_gla, dproj, S)
    dproj, dwgu, dbgu = _gate_bwd(dpre, proj, wgu_p, dproj)
    dw_in_p = _mm(h1, dproj, "tn", "dw_in", tm=1024, tn=640, tk=512)
    dh1 = _mm(dproj, w_in_p, "nt", "dh1", tm=512, tn=1024, tk=640)
    dx, dg1 = _rms_bwd(x2d, attn_norm_g, dh1, dx1, "attn_norm_bwd")

    dw_in = dw_in_p[:, _INV_PERM]
    dw_up = jnp.concatenate([dw_up_a, dw_up_v], axis=1)
    dcw = jnp.concatenate([dcw_a, dcw_v], axis=1)
    dwgu_pc = jnp.transpose(dwgu[:GATE_RANK].reshape(GATE_RANK, N_DEV, c_gu), (1, 0, 2)).reshape(N_DEV, -1)
    dcw_pc = jnp.transpose(dcw.reshape(3, N_DEV, c_up), (1, 0, 2)).reshape(N_DEV, -1)
    small_pc = jnp.concatenate([dwgu_pc, dcw_pc], axis=1)
    small_pc = jnp.pad(small_pc, ((0, 0), (0, R_SMALL * D - small_pc.shape[1]))).reshape(N_DEV, R_SMALL, D)
    pieces = jnp.concatenate(
        [_cols_to_pieces(dw_in, c_in), dw_out.reshape(N_DEV, R_OUT, D), _cols_to_pieces(dw_up, c_up),
         dw_down.reshape(N_DEV, R_DOWN, D), small_pc], axis=1)
    pieces = jnp.pad(pieces, ((0, 0), (0, R_PAD - R_USED), (0, 0))).astype(BF16)
    got = _exchange(pieces, False, "scatter_grads")
    wp = big_pack(w_in, w_out, w_ffn_up, w_ffn_down, w_gate_up, conv_w, F32)
    mp = big_pack(m_w_in, m_w_out, m_w_ffn_up, m_w_ffn_down, m_w_gate_up, m_conv_w, F32)
    vp = big_pack(v_w_in, v_w_out, v_w_ffn_up, v_w_ffn_down, v_w_gate_up, v_conv_w, F32)
    big = _sum_adamw(got, wp, mp, vp, R_TILE, "adamw_sharded")

    def unpack_big(a):
        o = 0
        r_in = a[o:o + R_IN].reshape(1, D, c_in); o += R_IN
        r_out = a[o:o + R_OUT].reshape(1, R_OUT, D); o += R_OUT
        r_up = a[o:o + R_UP].reshape(1, D, c_up); o += R_UP
        r_down = a[o:o + R_DOWN].reshape(1, R_DOWN, D); o += R_DOWN
        sm = a[o:o + R_SMALL].reshape(-1)
        r_gu = sm[:n_gu].reshape(1, GATE_RANK, c_gu)
        r_cw = sm[n_gu:n_gu + 3 * c_up].reshape(1, 3, c_up)
        return dict(w_in=r_in, w_out=r_out, w_ffn_up=r_up, w_ffn_down=r_down, w_gate_up=r_gu, conv_w=r_cw)

    rep_names = ["attn_norm_g", "b_gate_up", "sb_out_g", "gla_out_g", "ffn_norm_g", "conv_b", "final_norm_g"]
    rep_g = [dg1, dbgu, dg_sb, dg_gla, dg2, jnp.concatenate([dcb_a, dcb_v], axis=1), dg3]
    rep_w = [attn_norm_g, b_gate_up, sb_out_g, gla_out_g, ffn_norm_g, conv_b, final_norm_g]
    rep_m = [m_attn_norm_g, m_b_gate_up, m_sb_out_g, m_gla_out_g, m_ffn_norm_g, m_conv_b, m_final_norm_g]
    rep_v = [v_attn_norm_g, v_b_gate_up, v_sb_out_g, v_gla_out_g, v_ffn_norm_g, v_conv_b, v_final_norm_g]
    got_s = _exchange(_flat_pad_rows(rep_g, SMALL_ROWS), True, "gather_small_grads")
    small = _sum_adamw(got_s, _flat_pad_rows(rep_w, SMALL_ROWS), _flat_pad_rows(rep_m, SMALL_ROWS),
                       _flat_pad_rows(rep_v, SMALL_ROWS), SMALL_ROWS, "adamw_replicated")

    def unpack_small(a):
        flat, o, out = a.reshape(-1), 0, {}
        for n, w in zip(rep_names, rep_w):
            out[n] = flat[o:o + w.size].reshape(w.shape)
            o += w.size
        return out

    loss = lax.psum(loss_dev[0, 0], ("x", "y", "c"))
    order = ["attn_norm_g", "w_in", "w_gate_up", "b_gate_up", "sb_out_g", "gla_out_g", "w_out", "ffn_norm_g",
             "w_ffn_up", "conv_w", "conv_b", "w_ffn_down", "final_norm_g"]
    outs = [loss, dx.reshape(Bd, S, D)]
    for k in range(4):
        both = {**unpack_big(big[k]), **unpack_small(small[k])}
        outs += [both[n] for n in order]
    return tuple(outs)
```

```python
import functools

import numpy as np
import jax
import jax.numpy as jnp
from jax import lax
from jax.experimental import pallas as pl
from jax.experimental.pallas import tpu as pltpu

F32 = jnp.float32
BF16 = jnp.bfloat16
SDS = jax.ShapeDtypeStruct
BS = pl.BlockSpec

N_DEV = 8
D = 1024
EPS = 1e-6
SB_HD = 64
SB_W = 512
GLA_DK = 64
GLA_DV = 128
GLA_KW = 256
GLA_W = 512
GATE_RANK = 16
GATE_NORM = 16.0
CHUNK = 64
QB = 128
D_FF = 2816
IN_COLS = 3088
PROJ_W = 3200
LANE = 128
VMEM_LIMIT = 56 * 1024 * 1024

ADAM_LR, ADAM_B1, ADAM_B2, ADAM_EPS, ADAM_WD, ADAM_STEP = 0.001, 0.9, 0.999, 1e-08, 0.01, 10


def _proj_perm():
    sbq, sbk, sbv = 0, 512, 1024
    gq, gk, gv, glr, gog = 1536, 1792, 2048, 2560, 2576
    cols = []
    for p in range(4):
        for base in (sbq, sbk, sbv):
            cols += list(range(base + 128 * p, base + 128 * p + 128))
    for p in range(2):
        cols += list(range(gq + 128 * p, gq + 128 * p + 128))
        cols += list(range(gk + 128 * p, gk + 128 * p + 128))
        cols += list(range(gv + 256 * p, gv + 256 * p + 256))
    cols += list(range(gog, gog + 512))
    cols += list(range(glr, glr + GATE_RANK)) + [IN_COLS] * (LANE - GATE_RANK)
    perm = np.asarray(cols, np.int32)
    inv = np.zeros((IN_COLS,), np.int32)
    for new, old in enumerate(cols):
        if old < IN_COLS:
            inv[old] = new
    return perm, inv


_PERM, _INV_PERM = _proj_perm()
OG_BLK = 5
GLR_BLK = 24


def _cp(sem=None, vmem=VMEM_LIMIT):
    return pltpu.CompilerParams(dimension_semantics=sem, vmem_limit_bytes=vmem)


def _dot(a, b):
    return lax.dot_general(a, b, (((1,), (0,)), ((), ())), preferred_element_type=F32)


def _dot_nt(a, b):
    return lax.dot_general(a, b, (((1,), (1,)), ((), ())), preferred_element_type=F32)


def _dot_tn(a, b):
    return lax.dot_general(a, b, (((0,), (0,)), ((), ())), preferred_element_type=F32)


def _bf(x):
    return x.astype(BF16)


def _split_dot(x, m, passes, left=True):
    acc = None
    r = x
    for i in range(passes):
        h = r.astype(BF16)
        t = _dot(h, m) if left else _dot(m, h)
        acc = t if acc is None else acc + t
        if i + 1 < passes:
            r = r - h.astype(F32)
    return acc


def _softplus(z):
    return jnp.maximum(z, 0.0) + jnp.log1p(jnp.exp(-jnp.abs(z)))


def _tile(n, pref, mult=LANE):
    best = None
    for t in range(mult, min(n, pref) + 1, mult):
        if n % t == 0:
            best = t
    return best if best is not None else n


def _mm(a, b, mode, name, out_dtype=F32, c=None, tm=512, tn=512, tk=512, b_row0=0, out_rows=None, out_row0=0,
        into=None):
    if mode == "nn":
        (M, K), N = a.shape, b.shape[1]
    elif mode == "nt":
        (M, K), N = a.shape, b.shape[0]
    else:
        (K, M), N = a.shape, b.shape[1]
    tm, tn, tk = _tile(M, tm), _tile(N, tn), _tile(K, tk)
    nk = K // tk
    kb0, ob0 = b_row0 // tk, out_row0 // tm
    assert kb0 * tk == b_row0 and ob0 * tm == out_row0 and (mode == "nn" or b_row0 == 0)
    a_spec = BS((tk, tm), lambda i, j, k: (k, i)) if mode == "tn" else BS((tm, tk), lambda i, j, k: (i, k))
    b_spec = BS((tn, tk), lambda i, j, k: (j, k)) if mode == "nt" else BS((tk, tn), lambda i, j, k: (k + kb0, j))
    dotfn = {"nn": _dot, "nt": _dot_nt, "tn": _dot_tn}[mode]
    has_c = c is not None
    has_into = into is not None

    def body(*refs):
        if has_into:
            refs = refs[:-3] + refs[-2:]
        if has_c:
            a_ref, b_ref, c_ref, o_ref, acc = refs
        else:
            a_ref, b_ref, o_ref, acc = refs
        k = pl.program_id(2)

        @pl.when(k == 0)
        def _():
            acc[...] = jnp.zeros_like(acc)

        acc[...] += dotfn(_bf(a_ref[...]), _bf(b_ref[...]))

        @pl.when(k == nk - 1)
        def _():
            r = acc[...]
            if has_c:
                r = r + c_ref[...]
            o_ref[...] = r.astype(out_dtype)

    in_specs = [a_spec, b_spec]
    args = [a, b]
    if has_c:
        in_specs.append(BS((tm, tn), lambda i, j, k: (i, j)))
        args.append(c)
    aliases = {}
    if has_into:
        aliases = {len(args): 0}
        in_specs.append(BS(memory_space=pl.ANY))
        args.append(into)
    return pl.pallas_call(
        body, name=name, out_shape=SDS((out_rows or M, N), out_dtype), grid=(M // tm, N // tn, nk),
        in_specs=in_specs, out_specs=BS((tm, tn), lambda i, j, k: (i + ob0, j)),
        scratch_shapes=[pltpu.VMEM((tm, tn), F32)], input_output_aliases=aliases,
        compiler_params=_cp(("parallel", "parallel", "arbitrary")),
    )(*args)


def _norm_proj(x, g, w):
    T, N = x.shape[0], w.shape[0]
    tm = _tile(T, 256)

    def body(x_ref, g_ref, w_ref, p_ref, h_ref):
        xv = x_ref[...]
        r = lax.rsqrt(jnp.mean(xv * xv, axis=-1, keepdims=True) + EPS)
        h = _bf((xv * r) * g_ref[...])
        h_ref[...] = h
        p_ref[...] = _dot_nt(h, w_ref[...])

    return pl.pallas_call(
        body, name="norm_proj", out_shape=(SDS((T, N), F32), SDS((T, D), BF16)), grid=(T // tm,),
        in_specs=[BS((tm, D), lambda i: (i, 0)), BS((1, D), lambda i: (0, 0)), BS((N, D), lambda i: (0, 0))],
        out_specs=(BS((tm, N), lambda i: (i, 0)), BS((tm, D), lambda i: (i, 0))),
        compiler_params=_cp(("parallel",)),
    )(x, g, w)


TK = 256
SB_DEAD = -104.0
CNT_LANE = SB_HD - 1


def _sb_masks():
    row = lax.broadcasted_iota(jnp.int32, (QB, TK), 0)
    col = lax.broadcasted_iota(jnp.int32, (QB, TK), 1)
    lane = lax.broadcasted_iota(jnp.int32, (1, LANE), 1)
    kr = lax.broadcasted_iota(jnp.int32, (TK, TK), 0)
    kc = lax.broadcasted_iota(jnp.int32, (TK, TK), 1)
    return row, col, lane, kr, kc


def _sb_fwd(proj, S, shards):
    T = proj.shape[0]
    nq = S // QB
    scale = SB_HD ** -0.5
    nb, ns = T // S, len(shards)

    def body(qkv_ref, *rest):
        sh_refs, (o_ref, tt_ref), g_refs = rest[:ns], rest[ns:ns + 2], rest[ns + 2:2 * ns + 2]
        sems = rest[2 * ns + 2:]
        first = jnp.logical_and(pl.program_id(0) == 0, pl.program_id(1) == 0)
        last = jnp.logical_and(pl.program_id(0) == nb - 1, pl.program_id(1) == 3)

        @pl.when(first)
        def _():
            _exchange_ops(sh_refs, g_refs, True, sems, "start")

        row, col, lane, kr, kc = _sb_masks()
        msuf = _bf(kr > kc)

        def qloop(qi, _):
            r0 = pl.multiple_of(qi * QB, QB)
            qv = qkv_ref[pl.ds(r0, QB), 0:128]
            qms = [_bf(jnp.where((lane // SB_HD) == hh, qv, 0.0)) for hh in range(2)]
            top = (qi * QB) // TK

            def live(st):
                it, _, _, c0, c1 = st
                return jnp.logical_and(it <= top, jnp.maximum(jnp.max(c0), jnp.max(c1)) > SB_DEAD)

            def step(st):
                it, a0, a1, c0, c1 = st
                kt = top - it
                k0 = pl.multiple_of(kt * TK, TK)
                kv = _bf(qkv_ref[pl.ds(k0, TK), 128:256])
                vv = _bf(qkv_ref[pl.ds(k0, TK), 256:384])
                strict = (col + (kt * TK - qi * QB)) < row
                accs, cys = [], []
                for qm, acc, cy in ((qms[0], a0, c0), (qms[1], a1, c1)):
                    z = _dot_nt(qm, kv) * scale
                    sp = _softplus(z)
                    lg = jnp.where(strict, -sp, 0.0)
                    after = cy + _split_dot(lg, msuf, 2)
                    w = jnp.where(strict, jnp.exp((z - sp) + after), 0.0)
                    accs.append(acc + _dot(_bf(w), vv))
                    cys.append(cy + jnp.sum(lg, axis=1, keepdims=True))
                return it + 1, accs[0], accs[1], cys[0], cys[1]

            za, zc = jnp.zeros((QB, LANE), F32), jnp.zeros((QB, 1), F32)
            it, a0, a1, c0, c1 = lax.while_loop(live, step, (jnp.int32(0), za, za, zc, zc))
            o_ref[pl.ds(r0, QB), :] = jnp.where(lane < SB_HD, a0, a1)
            tt = jnp.where(lane < SB_HD, c0, c1)
            tt_ref[pl.ds(r0, QB), :] = jnp.where(lane == CNT_LANE, it.astype(F32), tt)
            return 0

        lax.fori_loop(0, nq, qloop, 0)

        @pl.when(last)
        def _():
            _exchange_ops(sh_refs, g_refs, True, sems, "wait")

    hbm = BS(memory_space=pltpu.HBM)
    col_spec = BS((S, LANE), lambda b, p: (b, p))
    return pl.pallas_call(
        body, name="sb_fwd",
        out_shape=(SDS((T, SB_W), F32), SDS((T, SB_W), F32)) + tuple(SDS((N_DEV,) + s.shape, s.dtype) for s in shards),
        grid=(nb, 4),
        in_specs=[BS((S, 384), lambda b, p: (b, p))] + [hbm] * ns,
        out_specs=(col_spec, col_spec) + (hbm,) * ns,
        scratch_shapes=_exchange_sems(ns),
        compiler_params=_cp(("arbitrary", "arbitrary")),
    )(proj, *shards)


def _log_sigmoid(x):
    return jnp.minimum(x, 0.0) - jnp.log1p(jnp.exp(-jnp.abs(x)))


def _gla_chunk_terms(blk_ref, glr_ref, wgu, bgu, r0, tri_incl):
    q = blk_ref[pl.ds(r0, CHUNK), 0:128]
    k = blk_ref[pl.ds(r0, CHUNK), 128:256]
    v = blk_ref[pl.ds(r0, CHUNK), 256:512]
    pre = _dot(_bf(glr_ref[pl.ds(r0, CHUNK), :]), wgu) + bgu
    la = _log_sigmoid(pre) / GATE_NORM
    b = _split_dot(la, tri_incl, 3, left=False)
    b_last = b[CHUNK - 1:CHUNK, :]
    eb = jnp.exp(b)
    qd = (q * (GLA_DK ** -0.5)) * eb
    ki = k * jnp.exp(-b)
    ke = k * jnp.exp(b_last - b)
    decay = jnp.exp(b_last)
    return q, k, v, pre, b, b_last, eb, qd, ki, ke, decay


def _gla_fwd(proj, wgu, bgu, S):
    T = proj.shape[0]
    nc = S // CHUNK

    def body(blk_ref, glr_ref, wgu_ref, bgu_ref, o_ref):
        rr = lax.broadcasted_iota(jnp.int32, (CHUNK, CHUNK), 0)
        cc = lax.broadcasted_iota(jnp.int32, (CHUNK, CHUNK), 1)
        lane = lax.broadcasted_iota(jnp.int32, (1, LANE), 1)
        causal = rr >= cc
        tri_incl = _bf(causal)
        wg = _bf(wgu_ref[...])
        bg = bgu_ref[...]

        def chunk(n, states):
            r0 = pl.multiple_of(n * CHUNK, CHUNK)
            _, _, v, _, _, _, _, qd, ki, ke, decay = _gla_chunk_terms(blk_ref, glr_ref, wg, bg, r0, tri_incl)
            new_states, outs = [], []
            for hh in range(2):
                hm = (lane // GLA_DK) == hh
                qm = _bf(jnp.where(hm, qd, 0.0))
                vh = _bf(v[:, 128 * hh:128 * hh + 128])
                st = states[hh]
                attn = jnp.where(causal, _dot_nt(qm, _bf(ki)), 0.0)
                outs.append(_dot(_bf(attn), vh) + _dot_nt(qm, _bf(st)))
                new_states.append(st * decay + _dot_tn(vh, _bf(ke)))
            o_ref[pl.ds(r0, CHUNK), :] = jnp.concatenate(outs, axis=1)
            return tuple(new_states)

        z = jnp.zeros((GLA_DV, LANE), F32)
        lax.fori_loop(0, nc, chunk, (z, z))

    return pl.pallas_call(
        body, name="gla_fwd", out_shape=SDS((T, GLA_W), F32), grid=(T // S, 2),
        in_specs=[BS((S, 512), lambda b, p: (b, 3 + p)), BS((S, LANE), lambda b, p: (b, GLR_BLK)),
                  BS((LANE, LANE), lambda b, p: (0, p)), BS((1, LANE), lambda b, p: (0, p))],
        out_specs=BS((S, 256), lambda b, p: (b, p)),
        compiler_params=_cp(("parallel", "parallel")),
    )(proj, proj, wgu, bgu)


def _head_blockdiag(width, hd):
    r = lax.broadcasted_iota(jnp.int32, (width, width), 0) // hd
    c = lax.broadcasted_iota(jnp.int32, (width, width), 1) // hd
    return _bf(r == c)


def _mix_out(o_sb, o_gla, proj, x, g_sb, g_gla, w_out, g2):
    T = x.shape[0]
    tm = _tile(T, 256)

    def body(osb_ref, ogl_ref, og_ref, x_ref, gsb_ref, ggl_ref, w_ref, g2_ref, x1_ref, oc_ref, h2_ref):
        bd64 = _head_blockdiag(SB_W, SB_HD)
        bd128 = _head_blockdiag(GLA_W, GLA_DV)
        o = osb_ref[...]
        r = lax.rsqrt(_split_dot(o * o, bd64, 2) * (1.0 / SB_HD) + EPS)
        c_sb = (o * r) * gsb_ref[...]
        o = ogl_ref[...]
        r = lax.rsqrt(_split_dot(o * o, bd128, 2) * (1.0 / GLA_DV) + EPS)
        og = og_ref[...]
        c_gl = ((o * r) * ggl_ref[...]) * (og * jax.nn.sigmoid(og))
        oc = _bf(jnp.concatenate([c_sb, c_gl], axis=1))
        oc_ref[...] = oc
        x1 = x_ref[...] + _dot(oc, w_ref[...])
        x1_ref[...] = x1
        r2 = lax.rsqrt(jnp.mean(x1 * x1, axis=-1, keepdims=True) + EPS)
        h2_ref[...] = _bf((x1 * r2) * g2_ref[...])

    row = lambda w: BS((tm, w), lambda i: (i, 0))
    vec = lambda w: BS((1, w), lambda i: (0, 0))
    return pl.pallas_call(
        body, name="mix_out", out_shape=(SDS((T, D), F32), SDS((T, D), BF16), SDS((T, D), BF16)), grid=(T // tm,),
        in_specs=[row(SB_W), row(GLA_W), BS((tm, 512), lambda i: (i, OG_BLK)), row(D), vec(SB_W), vec(GLA_W),
                  BS((D, D), lambda i: (0, 0)), vec(D)],
        out_specs=(row(D), row(D), row(D)),
        compiler_params=_cp(("parallel",)),
    )(o_sb, o_gla, proj, x, g_sb, g_gla, w_out, g2)


CONV_ROWS = 256
CONV_TC = 256


def _rows_before(ref, r0, first):
    prev = ref[pl.ds(pl.multiple_of(jnp.maximum(r0 - 8, 0), 8), 8), :]
    return jnp.where(first, 0.0, prev)


def _rows_after(val_fn, r0, rows, last, S):
    nxt = val_fn(pl.multiple_of(jnp.minimum(r0 + rows, S - 8), 8))
    return jnp.where(last, 0.0, nxt)


def _shift_down(cur, prev8, k):
    cat = jnp.concatenate([prev8, cur], axis=0)
    return pltpu.roll(cat, k, 0)[8:]


def _shift_up(cur, next8, k):
    cat = jnp.concatenate([cur, next8], axis=0)
    return pltpu.roll(cat, cat.shape[0] - k, 0)[:cur.shape[0]]


def _conv_at(h_ref, cw, cb, r0, rows, first):
    cur = h_ref[pl.ds(r0, rows), :]
    prev8 = _rows_before(h_ref, r0, first)
    u = cb + cw[0:1, :] * _shift_down(cur, prev8, 2)
    u = u + cw[1:2, :] * _shift_down(cur, prev8, 1)
    return u + cw[2:3, :] * cur


NJ = D_FF // CONV_TC


def _conv_gate(hup, cw, cb, S):
    T = hup.shape[0]
    rows = min(CONV_ROWS, S)
    nr = S // rows

    def body(ha_ref, hv_ref, cwa_ref, cwv_ref, cba_ref, cbv_ref, act_ref):
        cwa, cwv, cba, cbv = cwa_ref[...], cwv_ref[...], cba_ref[...], cbv_ref[...]

        def step(c, _):
            r0 = pl.multiple_of(c * rows, rows)
            ua = _conv_at(ha_ref, cwa, cba, r0, rows, c == 0)
            uv = _conv_at(hv_ref, cwv, cbv, r0, rows, c == 0)
            act_ref[pl.ds(r0, rows), :] = _bf((ua * jax.nn.sigmoid(ua)) * uv)
            return 0

        lax.fori_loop(0, nr, step, 0)

    blk = lambda o: BS((S, CONV_TC), lambda b, j: (b, j + o))
    w3 = lambda o: BS((3, CONV_TC), lambda b, j: (0, j + o))
    w1 = lambda o: BS((1, CONV_TC), lambda b, j: (0, j + o))
    return pl.pallas_call(
        body, name="conv_gate", out_shape=SDS((T, D_FF), BF16), grid=(T // S, NJ),
        in_specs=[blk(0), blk(NJ), w3(0), w3(NJ), w1(0), w1(NJ)], out_specs=blk(0),
        compiler_params=_cp(("parallel", "parallel")),
    )(hup, hup, cw, cw, cb, cb)


def _down_loss(act, w_down, x1, tgt, g3):
    T = x1.shape[0]
    tm = _tile(T, 256)

    def body(a_ref, w_ref, x1_ref, t_ref, g_ref, dx_ref, dg_ref, ls_ref):
        @pl.when(pl.program_id(0) == 0)
        def _():
            dg_ref[...] = jnp.zeros_like(dg_ref)
            ls_ref[...] = jnp.zeros_like(ls_ref)

        g = g_ref[...]
        x2 = x1_ref[...] + _dot(a_ref[...], w_ref[...])
        r = lax.rsqrt(jnp.mean(x2 * x2, axis=-1, keepdims=True) + EPS)
        xh = x2 * r
        e = xh * g - t_ref[...]
        ls_ref[...] += 0.5 * jnp.sum(jnp.mean(e * e, axis=-1, keepdims=True), axis=0, keepdims=True)
        dy = e * (1.0 / D)
        dxh = dy * g
        dx_ref[...] = r * (dxh - xh * jnp.mean(dxh * xh, axis=-1, keepdims=True))
        dg_ref[...] += jnp.sum(dy * xh, axis=0, keepdims=True)

    row = lambda w: BS((tm, w), lambda i: (i, 0))
    return pl.pallas_call(
        body, name="down_loss", out_shape=(SDS((T, D), F32), SDS((1, D), F32), SDS((1, LANE), F32)), grid=(T // tm,),
        in_specs=[row(D_FF), BS((D_FF, D), lambda i: (0, 0)), row(D), row(D), BS((1, D), lambda i: (0, 0))],
        out_specs=(row(D), BS((1, D), lambda i: (0, 0)), BS((1, LANE), lambda i: (0, 0))),
        compiler_params=_cp(("arbitrary",)),
    )(act, w_down, x1, tgt, g3)


def _conv_gate_bwd(hup, dact, cw, cb, S):
    T = hup.shape[0]
    rows = min(CONV_ROWS, S)
    nr = S // rows

    def body(ha_ref, hv_ref, da_ref, cwa_ref, cwv_ref, cba_ref, cbv_ref,
             dha_ref, dhv_ref, dcwa_ref, dcwv_ref, dcba_ref, dcbv_ref):
        @pl.when(pl.program_id(1) == 0)
        def _():
            for r in (dcwa_ref, dcwv_ref, dcba_ref, dcbv_ref):
                r[...] = jnp.zeros_like(r)

        cwa, cwv, cba, cbv = cwa_ref[...], cwv_ref[...], cba_ref[...], cbv_ref[...]

        def du_at(r0, n, first):
            ua = _conv_at(ha_ref, cwa, cba, r0, n, first)
            uv = _conv_at(hv_ref, cwv, cbv, r0, n, first)
            da = da_ref[pl.ds(r0, n), :]
            sg = jax.nn.sigmoid(ua)
            dua = (da * uv) * (sg * (1.0 + ua * (1.0 - sg)))
            duv = da * (ua * sg)
            return dua, duv

        def step(c, _):
            r0 = pl.multiple_of(c * rows, rows)
            first, last = c == 0, c == nr - 1
            dua, duv = du_at(r0, rows, first)
            n0 = pl.multiple_of(jnp.minimum(r0 + rows, S - 8), 8)
            nua, nuv = du_at(n0, 8, False)
            nua = jnp.where(last, 0.0, nua)
            nuv = jnp.where(last, 0.0, nuv)
            for (h_ref, cw, du, nu, dh_ref, dcw_ref, dcb_ref) in (
                    (ha_ref, cwa, dua, nua, dha_ref, dcwa_ref, dcba_ref),
                    (hv_ref, cwv, duv, nuv, dhv_ref, dcwv_ref, dcbv_ref)):
                dh = cw[2:3, :] * du + cw[1:2, :] * _shift_up(du, nu, 1) + cw[0:1, :] * _shift_up(du, nu, 2)
                dh_ref[pl.ds(r0, rows), :] = _bf(dh)
                cur = h_ref[pl.ds(r0, rows), :]
                prev8 = _rows_before(h_ref, r0, first)
                dcw_ref[0:1, :] += jnp.sum(du * _shift_down(cur, prev8, 2), axis=0, keepdims=True)
                dcw_ref[1:2, :] += jnp.sum(du * _shift_down(cur, prev8, 1), axis=0, keepdims=True)
                dcw_ref[2:3, :] += jnp.sum(du * cur, axis=0, keepdims=True)
                dcb_ref[...] += jnp.sum(du, axis=0, keepdims=True)
            return 0

        lax.fori_loop(0, nr, step, 0)

    blk = lambda o: BS((S, CONV_TC), lambda j, b: (b, j + o))
    w3 = lambda o: BS((3, CONV_TC), lambda j, b: (0, j + o))
    w1 = lambda o: BS((1, CONV_TC), lambda j, b: (0, j + o))
    return pl.pallas_call(
        body, name="conv_gate_bwd",
        out_shape=(SDS((T, D_FF), BF16), SDS((T, D_FF), BF16), SDS((3, D_FF), F32), SDS((3, D_FF), F32),
                   SDS((1, D_FF), F32), SDS((1, D_FF), F32)),
        grid=(NJ, T // S),
        in_specs=[blk(0), blk(NJ), blk(0), w3(0), w3(NJ), w1(0), w1(NJ)],
        out_specs=(blk(0), blk(0), w3(0), w3(0), w1(0), w1(0)),
        compiler_params=_cp(("parallel", "arbitrary")),
    )(hup, hup, dact, cw, cw, cb, cb)


def _rms_bwd(x, g, dh, dres, name):
    T = x.shape[0]
    tm = _tile(T, 512)

    def body(x_ref, g_ref, dh_ref, dr_ref, dx_ref, dg_ref):
        @pl.when(pl.program_id(0) == 0)
        def _():
            dg_ref[...] = jnp.zeros_like(dg_ref)

        xv = x_ref[...]
        dh = dh_ref[...]
        r = lax.rsqrt(jnp.mean(xv * xv, axis=-1, keepdims=True) + EPS)
        xh = xv * r
        dxh = dh * g_ref[...]
        dx_ref[...] = dr_ref[...] + r * (dxh - xh * jnp.mean(dxh * xh, axis=-1, keepdims=True))
        dg_ref[...] += jnp.sum(dh * xh, axis=0, keepdims=True)

    row = BS((tm, D), lambda i: (i, 0))
    vec = BS((1, D), lambda i: (0, 0))
    return pl.pallas_call(
        body, name=name, out_shape=(SDS((T, D), F32), SDS((1, D), F32)), grid=(T // tm,),
        in_specs=[row, vec, row, row], out_specs=(row, vec),
        compiler_params=_cp(("arbitrary",)),
    )(x, g, dh, dres)


def _mix_bwd(docat, o_sb, o_gla, proj, g_sb, g_gla):
    T = docat.shape[0]
    tm = _tile(T, 256)

    def body(d_ref, osb_ref, ogl_ref, og_ref, gsb_ref, ggl_ref, dsb_ref, dgl_ref, dog_ref, dgsb_ref, dggl_ref):
        @pl.when(pl.program_id(0) == 0)
        def _():
            dgsb_ref[...] = jnp.zeros_like(dgsb_ref)
            dggl_ref[...] = jnp.zeros_like(dggl_ref)

        bd64 = _head_blockdiag(SB_W, SB_HD)
        bd128 = _head_blockdiag(GLA_W, GLA_DV)
        d = d_ref[:, 0:SB_W]
        o = osb_ref[...]
        r = lax.rsqrt(_split_dot(o * o, bd64, 2) * (1.0 / SB_HD) + EPS)
        n = o * r
        dn = d * gsb_ref[...]
        dgsb_ref[...] += jnp.sum(d * n, axis=0, keepdims=True)
        dsb_ref[...] = r * (dn - n * (_split_dot(dn * n, bd64, 2) * (1.0 / SB_HD)))

        d = d_ref[:, SB_W:D]
        o = ogl_ref[...]
        r = lax.rsqrt(_split_dot(o * o, bd128, 2) * (1.0 / GLA_DV) + EPS)
        n = o * r
        og = og_ref[...]
        sg = jax.nn.sigmoid(og)
        dm = d * (og * sg)
        dog_ref[...] = _bf((d * (n * ggl_ref[...])) * (sg * (1.0 + og * (1.0 - sg))))
        dn = dm * ggl_ref[...]
        dggl_ref[...] += jnp.sum(dm * n, axis=0, keepdims=True)
        dgl_ref[...] = r * (dn - n * (_split_dot(dn * n, bd128, 2) * (1.0 / GLA_DV)))

    row = lambda w: BS((tm, w), lambda i: (i, 0))
    vec = lambda w: BS((1, w), lambda i: (0, 0))
    ogb = BS((tm, 512), lambda i: (i, OG_BLK))
    return pl.pallas_call(
        body, name="mix_bwd",
        out_shape=(SDS((T, SB_W), F32), SDS((T, GLA_W), F32), SDS((T, PROJ_W), BF16), SDS((1, SB_W), F32),
                   SDS((1, GLA_W), F32)),
        grid=(T // tm,),
        in_specs=[row(D), row(SB_W), row(GLA_W), ogb, vec(SB_W), vec(GLA_W)],
        out_specs=(row(SB_W), row(GLA_W), ogb, vec(SB_W), vec(GLA_W)),
        compiler_params=_cp(("arbitrary",)),
    )(docat, o_sb, o_gla, proj, g_sb, g_gla)


def _sb_bwd(proj, tt, do, dproj, S, pieces):
    T = proj.shape[0]
    nq = S // QB
    scale = SB_HD ** -0.5
    nb, ns = T // S, len(pieces)

    def body(qkv_ref, tt_ref, do_ref, dp_in_ref, *rest):
        del dp_in_ref
        pc_refs, dp_ref, got_refs = rest[:ns], rest[ns], rest[ns + 1:2 * ns + 1]
        dk_acc, dv_acc = rest[2 * ns + 1:2 * ns + 3]
        sems = rest[2 * ns + 3:]
        first = jnp.logical_and(pl.program_id(0) == 0, pl.program_id(1) == 0)
        last = jnp.logical_and(pl.program_id(0) == nb - 1, pl.program_id(1) == 3)

        @pl.when(first)
        def _():
            _exchange_ops(pc_refs, got_refs, False, sems, "start")

        row, col, lane, kr, kc = _sb_masks()
        mincl = _bf(kr <= kc)
        mexcl = _bf(kr < kc)
        dk_acc[...] = jnp.zeros_like(dk_acc)
        dv_acc[...] = jnp.zeros_like(dv_acc)

        def qloop(qi, _):
            r0 = pl.multiple_of(qi * QB, QB)
            qv = qkv_ref[pl.ds(r0, QB), 0:128]
            dov = do_ref[pl.ds(r0, QB), :]
            ttv = tt_ref[pl.ds(r0, QB), :]
            hms = [(lane // SB_HD) == hh for hh in range(2)]
            qms = [_bf(jnp.where(hm, qv, 0.0)) for hm in hms]
            doms = [_bf(jnp.where(hm, dov, 0.0)) for hm in hms]
            tots = [ttv[:, SB_HD * hh:SB_HD * hh + 1] for hh in range(2)]
            top = (qi * QB) // TK
            walked = jnp.max(ttv[:, CNT_LANE:CNT_LANE + 1]).astype(jnp.int32)

            def step(kt, st):
                dq, l0, l1, p0, p1 = st
                k0 = pl.multiple_of(kt * TK, TK)
                kv = _bf(qkv_ref[pl.ds(k0, TK), 128:256])
                vv = _bf(qkv_ref[pl.ds(k0, TK), 256:384])
                strict = (col + (kt * TK - qi * QB)) < row
                dk_t = dv_t = None
                lcs, pcs = [], []
                for hm, qm, dom, tot, lc, pc in ((hms[0], qms[0], doms[0], tots[0], l0, p0),
                                                 (hms[1], qms[1], doms[1], tots[1], l1, p1)):
                    z = _dot_nt(qm, kv) * scale
                    sp = _softplus(z)
                    lg = jnp.where(strict, -sp, 0.0)
                    after = tot - (lc + _split_dot(lg, mincl, 2))
                    gl = z - sp
                    w = jnp.where(strict, jnp.exp(gl + after), 0.0)
                    du = w * _dot_nt(dom, vv)
                    beta = jnp.exp(gl)
                    pex = pc + _split_dot(du, mexcl, 2)
                    dz = _bf(jnp.where(strict, du * (1.0 - beta) - beta * pex, 0.0) * scale)
                    dq = dq + jnp.where(hm, _dot(dz, kv), 0.0)
                    dk_h = _dot_tn(dz, qm)
                    dv_h = _dot_tn(_bf(w), dom)
                    dk_t = dk_h if dk_t is None else dk_t + dk_h
                    dv_t = dv_h if dv_t is None else dv_t + dv_h
                    lcs.append(lc + jnp.sum(lg, axis=1, keepdims=True))
                    pcs.append(pc + jnp.sum(du, axis=1, keepdims=True))
                dk_acc[pl.ds(k0, TK), :] += dk_t
                dv_acc[pl.ds(k0, TK), :] += dv_t
                return dq, lcs[0], lcs[1], pcs[0], pcs[1]

            zc = jnp.zeros((QB, 1), F32)
            dq, _, _, _, _ = lax.fori_loop(top - walked + 1, top + 1, step,
                                           (jnp.zeros((QB, LANE), F32), zc, zc, zc, zc))
            dp_ref[pl.ds(r0, QB), 0:128] = _bf(dq)
            return 0

        lax.fori_loop(0, nq, qloop, 0)
        dp_ref[:, 128:256] = _bf(dk_acc[...])
        dp_ref[:, 256:384] = _bf(dv_acc[...])

        @pl.when(last)
        def _():
            _exchange_ops(pc_refs, got_refs, False, sems, "wait")

    blk = BS((S, 384), lambda b, p: (b, p))
    col_spec = BS((S, LANE), lambda b, p: (b, p))
    hbm = BS(memory_space=pltpu.HBM)
    return pl.pallas_call(
        body, name="sb_bwd", out_shape=(SDS((T, PROJ_W), BF16),) + tuple(SDS(s.shape, s.dtype) for s in pieces),
        grid=(nb, 4),
        in_specs=[blk, col_spec, col_spec, BS(memory_space=pl.ANY)] + [hbm] * ns, out_specs=(blk,) + (hbm,) * ns,
        scratch_shapes=[pltpu.VMEM((S, LANE), F32), pltpu.VMEM((S, LANE), F32)] + _exchange_sems(ns),
        input_output_aliases={3: 0},
        compiler_params=_cp(("arbitrary", "arbitrary")),
    )(proj, tt, do, dproj, *pieces)


def _gla_bwd(proj, wgu, bgu, do, dproj, S):
    T = proj.shape[0]
    nc = S // CHUNK

    def body(blk_ref, glr_ref, wgu_ref, bgu_ref, do_ref, dp_in_ref, dp_ref, dpre_ref, st_ref):
        del dp_in_ref
        rr = lax.broadcasted_iota(jnp.int32, (CHUNK, CHUNK), 0)
        cc = lax.broadcasted_iota(jnp.int32, (CHUNK, CHUNK), 1)
        lane = lax.broadcasted_iota(jnp.int32, (1, LANE), 1)
        causal = rr >= cc
        tri_incl = _bf(causal)
        tri_rev = _bf(rr <= cc)
        wg = _bf(wgu_ref[...])
        bg = bgu_ref[...]

        def fwd_chunk(n, states):
            r0 = pl.multiple_of(n * CHUNK, CHUNK)
            _, _, v, _, _, _, _, _, _, ke, decay = _gla_chunk_terms(blk_ref, glr_ref, wg, bg, r0, tri_incl)
            new_states = []
            for hh in range(2):
                st_ref[hh, n] = states[hh]
                vh = _bf(v[:, 128 * hh:128 * hh + 128])
                new_states.append(states[hh] * decay + _dot_tn(vh, _bf(ke)))
            return tuple(new_states)

        z = jnp.zeros((GLA_DV, LANE), F32)
        lax.fori_loop(0, nc, fwd_chunk, (z, z))

        def bwd_chunk(it, dstates):
            n = nc - 1 - it
            r0 = pl.multiple_of(n * CHUNK, CHUNK)
            _, _, v, pre, b, b_last, eb, qd, ki, ke, decay = _gla_chunk_terms(
                blk_ref, glr_ref, wg, bg, r0, tri_incl)
            dqd = jnp.zeros((CHUNK, LANE), F32)
            dki = jnp.zeros((CHUNK, LANE), F32)
            dke = jnp.zeros((CHUNK, LANE), F32)
            ddec = jnp.zeros((1, LANE), F32)
            new_dstates, dvs = [], []
            for hh in range(2):
                hm = (lane // GLA_DK) == hh
                qm = _bf(jnp.where(hm, qd, 0.0))
                kem = _bf(jnp.where(hm, ke, 0.0))
                vh = _bf(v[:, 128 * hh:128 * hh + 128])
                doh = _bf(do_ref[pl.ds(r0, CHUNK), 128 * hh:128 * hh + 128])
                st = st_ref[hh, n]
                dst = dstates[hh]
                attn = _bf(jnp.where(causal, _dot_nt(qm, _bf(ki)), 0.0))
                dattn = _bf(jnp.where(causal, _dot_nt(doh, vh), 0.0))
                dvs.append(_dot_tn(attn, doh) + _dot_nt(kem, _bf(dst)))
                dqd = dqd + jnp.where(hm, _dot(dattn, _bf(ki)) + _dot(doh, _bf(st)), 0.0)
                dki = dki + _dot_tn(dattn, qm)
                dke = dke + jnp.where(hm, _dot(vh, _bf(dst)), 0.0)
                ddec = ddec + jnp.where(hm, jnp.sum(dst * st, axis=0, keepdims=True), 0.0)
                new_dstates.append(dst * decay + _dot_tn(doh, qm))
            einv = jnp.exp(-b)
            eend = jnp.exp(b_last - b)
            dq = (dqd * eb) * (GLA_DK ** -0.5)
            dk = dki * einv + dke * eend
            db = dqd * qd - dki * ki - dke * ke
            db_last = jnp.sum(dke * ke, axis=0, keepdims=True) + ddec * decay
            dla = _split_dot(db, tri_rev, 3, left=False) + db_last
            dpre_ref[pl.ds(r0, CHUNK), :] = (dla * (1.0 / GATE_NORM)) * (1.0 - jax.nn.sigmoid(pre))
            dp_ref[pl.ds(r0, CHUNK), 0:128] = _bf(dq)
            dp_ref[pl.ds(r0, CHUNK), 128:256] = _bf(dk)
            dp_ref[pl.ds(r0, CHUNK), 256:512] = _bf(jnp.concatenate(dvs, axis=1))
            return tuple(new_dstates)

        lax.fori_loop(0, nc, bwd_chunk, (z, z))

    return pl.pallas_call(
        body, name="gla_bwd", out_shape=(SDS((T, PROJ_W), BF16), SDS((T, GLA_KW), F32)), grid=(T // S, 2),
        in_specs=[BS((S, 512), lambda b, p: (b, 3 + p)), BS((S, LANE), lambda b, p: (b, GLR_BLK)),
                  BS((LANE, LANE), lambda b, p: (0, p)), BS((1, LANE), lambda b, p: (0, p)),
                  BS((S, 256), lambda b, p: (b, p)), BS(memory_space=pl.ANY)],
        out_specs=(BS((S, 512), lambda b, p: (b, 3 + p)), BS((S, LANE), lambda b, p: (b, p))),
        scratch_shapes=[pltpu.VMEM((2, nc, GLA_DV, LANE), F32)],
        input_output_aliases={5: 0},
        compiler_params=_cp(("parallel", "parallel")),
    )(proj, proj, wgu, bgu, do, dproj)


def _gate_bwd(dpre, proj, wgu, dproj):
    T = dpre.shape[0]
    tm = _tile(T, 512)

    def body(dpre_ref, glr_ref, wgu_ref, dp_in_ref, dp_ref, dw_ref, db_ref):
        del dp_in_ref

        @pl.when(pl.program_id(0) == 0)
        def _():
            dw_ref[...] = jnp.zeros_like(dw_ref)
            db_ref[...] = jnp.zeros_like(db_ref)

        dpre = dpre_ref[...]
        dp_ref[...] = _bf(_dot_nt(_bf(dpre), _bf(wgu_ref[...])))
        dw_ref[...] += _dot_tn(_bf(glr_ref[...]), _bf(dpre))
        db_ref[...] += jnp.sum(dpre, axis=0, keepdims=True)

    glr = BS((tm, LANE), lambda i: (i, GLR_BLK))
    return pl.pallas_call(
        body, name="gate_bwd",
        out_shape=(SDS((T, PROJ_W), BF16), SDS((LANE, GLA_KW), F32), SDS((1, GLA_KW), F32)), grid=(T // tm,),
        in_specs=[BS((tm, GLA_KW), lambda i: (i, 0)), glr, BS((LANE, GLA_KW), lambda i: (0, 0)),
                  BS(memory_space=pl.ANY)],
        out_specs=(glr, BS((LANE, GLA_KW), lambda i: (0, 0)), BS((1, GLA_KW), lambda i: (0, 0))),
        input_output_aliases={3: 0},
        compiler_params=_cp(("arbitrary",)),
    )(dpre, proj, wgu, dproj)


def _exchange_sems(n):
    return [pltpu.SemaphoreType.DMA((n * (N_DEV - 1),)), pltpu.SemaphoreType.DMA((n * (N_DEV - 1),)),
            pltpu.SemaphoreType.DMA((n,))]


def _exchange_ops(srcs, outs, gather, sems, act):
    ssem, rsem, lsem = sems
    x, y, c = lax.axis_index("x"), lax.axis_index("y"), lax.axis_index("c")
    me = 4 * x + 2 * y + c
    for i, (s_ref, o_ref) in enumerate(zip(srcs, outs)):
        for k in range(1, N_DEV):
            px = (x + ((k >> 2) & 1)) % 2
            py = (y + ((k >> 1) & 1)) % 2
            pc = (c + (k & 1)) % 2
            peer = 4 * px + 2 * py + pc
            n = i * (N_DEV - 1) + k - 1
            out = pltpu.make_async_remote_copy(
                src_ref=s_ref if gather else s_ref.at[peer], dst_ref=o_ref.at[me],
                send_sem=ssem.at[n], recv_sem=rsem.at[n],
                device_id=(px, py, pc), device_id_type=pl.DeviceIdType.MESH)
            if act == "start":
                out.start()
            else:
                out.wait_send()
                pltpu.make_async_remote_copy(
                    src_ref=s_ref if gather else s_ref.at[me], dst_ref=o_ref.at[peer],
                    send_sem=ssem.at[n], recv_sem=rsem.at[n],
                    device_id=(x, y, c), device_id_type=pl.DeviceIdType.MESH).wait_recv()
        mine = pltpu.make_async_copy(s_ref if gather else s_ref.at[me], o_ref.at[me], lsem.at[i])
        if act == "start":
            mine.start()
        else:
            mine.wait()


def _exchange(srcs, gather, name):
    n = len(srcs)
    shapes = [SDS((N_DEV,) + s.shape if gather else s.shape, s.dtype) for s in srcs]

    def body(*refs):
        s_refs, o_refs, sems = refs[:n], refs[n:2 * n], refs[2 * n:]
        _exchange_ops(s_refs, o_refs, gather, sems, "start")
        _exchange_ops(s_refs, o_refs, gather, sems, "wait")

    hbm = BS(memory_space=pltpu.HBM)
    return pl.pallas_call(
        body, name=name, out_shape=tuple(shapes), in_specs=[hbm] * n, out_specs=(hbm,) * n,
        scratch_shapes=_exchange_sems(n),
    )(*srcs)


def _adamw_math(w, g, m, v):
    m = ADAM_B1 * m + (1.0 - ADAM_B1) * g
    v = ADAM_B2 * v + (1.0 - ADAM_B2) * (g * g)
    m_hat = m / (1.0 - ADAM_B1 ** ADAM_STEP)
    v_hat = v / (1.0 - ADAM_B2 ** ADAM_STEP)
    delta = -ADAM_LR * (m_hat / (jnp.sqrt(v_hat) + ADAM_EPS) + ADAM_WD * w)
    return delta, m, v


def _sum_adamw(parts, w, m, v, tr, name):
    R, C = w.shape

    def body(p_ref, w_ref, m_ref, v_ref, g_ref, d_ref, nm_ref, nv_ref):
        g = p_ref[0].astype(F32)
        for d in range(1, N_DEV):
            g = g + p_ref[d].astype(F32)
        delta, nm, nv = _adamw_math(w_ref[...], g, m_ref[...], v_ref[...])
        g_ref[...] = g
        d_ref[...] = delta
        nm_ref[...] = nm
        nv_ref[...] = nv

    blk = BS((tr, C), lambda i: (i, 0))
    out = SDS((R, C), F32)
    return pl.pallas_call(
        body, name=name, out_shape=(out, out, out, out), grid=(R // tr,),
        in_specs=[BS((N_DEV, tr, C), lambda i: (0, i, 0)), blk, blk, blk], out_specs=(blk, blk, blk, blk),
        compiler_params=_cp(("parallel",)),
    )(parts, w, m, v)


def _flat_pad_rows(parts, rows):
    flat = jnp.concatenate([p.reshape(-1) for p in parts])
    return jnp.pad(flat, (0, rows * D - flat.shape[0])).reshape(rows, D)


SMALL_ROWS = 16
SHARD_SMALL_ROWS = 3


def kernel(x, attn_norm_g, w_in, w_gate_up, b_gate_up, sb_out_g, gla_out_g, w_out, ffn_norm_g, w_ffn_up, conv_w, conv_b, w_ffn_down, final_norm_g, loss_target, m_attn_norm_g, m_w_in, m_w_gate_up, m_b_gate_up, m_sb_out_g, m_gla_out_g, m_w_out, m_ffn_norm_g, m_w_ffn_up, m_conv_w, m_conv_b, m_w_ffn_down, m_final_norm_g, v_attn_norm_g, v_w_in, v_w_gate_up, v_b_gate_up, v_sb_out_g, v_gla_out_g, v_w_out, v_ffn_norm_g, v_w_ffn_up, v_conv_w, v_conv_b, v_w_ffn_down, v_final_norm_g):
    Bd, S, _ = x.shape
    T = Bd * S
    x2d = x.reshape(T, D)
    tgt = loss_target.reshape(T, D)
    c_up = w_ffn_up.shape[2]
    c_gu = w_gate_up.shape[2]
    c_in = w_in.shape[2]

    n_gu = GATE_RANK * c_gu
    rows_bf = lambda w: w[0].T.astype(BF16)
    small_w = lambda wgu, cw: _flat_pad_rows([wgu, cw], SHARD_SMALL_ROWS)

    g_in, g_out, gs = _exchange([rows_bf(w_in), w_out[0].astype(BF16), _flat_pad_rows([w_gate_up, conv_w], 8)],
                                True, "gather_mix_weights")
    w_in_pt = jnp.pad(g_in.reshape(IN_COLS, D), ((0, 1), (0, 0)))[_PERM]
    w_out_f = g_out.reshape(D, D)
    gsf = gs.reshape(N_DEV, -1)
    wgu_f = jnp.transpose(gsf[:, :n_gu].reshape(N_DEV, GATE_RANK, c_gu), (1, 0, 2)).reshape(GATE_RANK, GLA_KW)
    cw_f = jnp.transpose(gsf[:, n_gu:n_gu + 3 * c_up].reshape(N_DEV, 3, c_up), (1, 0, 2)).reshape(3, 2 * D_FF)
    wgu_p = jnp.pad(wgu_f, ((0, LANE - GATE_RANK), (0, 0)))
    g3 = final_norm_g.reshape(1, D)

    proj, h1 = _norm_proj(x2d, attn_norm_g, w_in_pt)
    o_sb, tt, g_up, g_down = _sb_fwd(proj, S, [rows_bf(w_ffn_up), w_ffn_down[0].astype(BF16)])
    w_up_t = g_up.reshape(2 * D_FF, D)
    w_down_f = g_down.reshape(D_FF, D)
    o_gla = _gla_fwd(proj, wgu_p, b_gate_up, S)
    x1, ocat, h2 = _mix_out(o_sb, o_gla, proj, x2d, sb_out_g, gla_out_g, w_out_f, ffn_norm_g)
    hup = _mm(h2, w_up_t, "nt", "ffn_up", tm=512, tn=1408, tk=1024)
    act = _conv_gate(hup, cw_f, conv_b, S)
    dx2, dg3, loss_dev = _down_loss(act, w_down_f, x1, tgt, g3)

    dw_down = _mm(act, dx2, "tn", "dw_down", out_dtype=BF16, tm=1408, tn=1024, tk=512)
    dact = _mm(dx2, w_down_f, "nt", "dact", tm=512, tn=1408, tk=1024)
    dhup_a, dhup_v, dcw_a, dcw_v, dcb_a, dcb_v = _conv_gate_bwd(hup, dact, cw_f, conv_b, S)
    dw_up_t = _mm(dhup_a, h2, "tn", "dw_up_a", out_dtype=BF16, tm=1408, tn=1024, tk=512, out_rows=2 * D_FF)
    dw_up_t = _mm(dhup_v, h2, "tn", "dw_up_v", out_dtype=BF16, tm=1408, tn=1024, tk=512, out_rows=2 * D_FF,
                  out_row0=D_FF, into=dw_up_t)
    dh2 = _mm(dhup_a, w_up_t, "nn", "dh2_a", tm=512, tn=1024, tk=1408)
    dh2 = _mm(dhup_v, w_up_t, "nn", "dh2_v", c=dh2, tm=512, tn=1024, tk=1408, b_row0=D_FF)
    dx1, dg2 = _rms_bwd(x1, ffn_norm_g, dh2, dx2, "ffn_norm_bwd")

    dw_out = _mm(ocat, dx1, "tn", "dw_out", out_dtype=BF16, tm=1024, tn=1024, tk=512)
    docat = _mm(dx1, w_out_f, "nt", "docat", tm=512, tn=1024, tk=1024)
    do_sb, do_gla, dproj, dg_sb, dg_gla = _mix_bwd(docat, o_sb, o_gla, proj, sb_out_g, gla_out_g)
    dproj, got_up, got_down, got_out = _sb_bwd(
        proj, tt, do_sb, dproj, S,
        [dw_up_t.reshape(N_DEV, c_up, D), dw_down.reshape(N_DEV, -1, D), dw_out.reshape(N_DEV, -1, D)])
    dproj, dpre = _gla_bwd(proj, wgu_p, b_gate_up, do_gla, dproj, S)
    dproj, dwgu, dbgu = _gate_bwd(dpre, proj, wgu_p, dproj)
    dw_in_pt = _mm(dproj, h1, "tn", "dw_in", out_dtype=BF16, tm=640, tn=1024, tk=512)
    dh1 = _mm(dproj, w_in_pt, "nn", "dh1", tm=512, tn=1024, tk=640)
    dx, dg1 = _rms_bwd(x2d, attn_norm_g, dh1, dx1, "attn_norm_bwd")

    dcw = jnp.concatenate([dcw_a, dcw_v], axis=1)
    dwgu_pc = jnp.transpose(dwgu[:GATE_RANK].reshape(GATE_RANK, N_DEV, c_gu), (1, 0, 2)).reshape(N_DEV, -1)
    dcw_pc = jnp.transpose(dcw.reshape(3, N_DEV, c_up), (1, 0, 2)).reshape(N_DEV, -1)
    small_pc = jnp.concatenate([dwgu_pc, dcw_pc], axis=1)
    small_pc = jnp.pad(small_pc, ((0, 0), (0, SHARD_SMALL_ROWS * D - small_pc.shape[1])))
    small_pc = small_pc.reshape(N_DEV, SHARD_SMALL_ROWS, D).astype(BF16)
    rep_names = ["attn_norm_g", "b_gate_up", "sb_out_g", "gla_out_g", "ffn_norm_g", "conv_b", "final_norm_g"]
    rep_g = [dg1, dbgu, dg_sb, dg_gla, dg2, jnp.concatenate([dcb_a, dcb_v], axis=1), dg3]
    rep_w = [attn_norm_g, b_gate_up, sb_out_g, gla_out_g, ffn_norm_g, conv_b, final_norm_g]
    rep_m = [m_attn_norm_g, m_b_gate_up, m_sb_out_g, m_gla_out_g, m_ffn_norm_g, m_conv_b, m_final_norm_g]
    rep_v = [v_attn_norm_g, v_b_gate_up, v_sb_out_g, v_gla_out_g, v_ffn_norm_g, v_conv_b, v_final_norm_g]
    rep_pc = jnp.broadcast_to(_flat_pad_rows(rep_g, SMALL_ROWS), (N_DEV, SMALL_ROWS, D))
    got_in, got_sm, got_rep = _exchange(
        [dw_in_pt[_INV_PERM].reshape(N_DEV, c_in, D), small_pc, rep_pc], False, "scatter_tail")

    rows = lambda w: w[0].T
    cols = lambda r: r.T[None]
    res = {}
    res["w_in"] = [cols(r) for r in _sum_adamw(got_in, rows(w_in), rows(m_w_in), rows(v_w_in), c_in, "adamw_w_in")]
    res["w_out"] = [r[None] for r in _sum_adamw(got_out, w_out[0], m_w_out[0], v_w_out[0], w_out.shape[1],
                                                 "adamw_w_out")]
    res["w_ffn_up"] = [cols(r) for r in _sum_adamw(got_up, rows(w_ffn_up), rows(m_w_ffn_up), rows(v_w_ffn_up),
                                                    c_up // 2, "adamw_w_up")]
    res["w_ffn_down"] = [r[None] for r in _sum_adamw(got_down, w_ffn_down[0], m_w_ffn_down[0], v_w_ffn_down[0],
                                                      w_ffn_down.shape[1], "adamw_w_down")]
    sm = _sum_adamw(got_sm, small_w(w_gate_up, conv_w), small_w(m_w_gate_up, m_conv_w),
                    small_w(v_w_gate_up, v_conv_w), SHARD_SMALL_ROWS, "adamw_small_sharded")
    res["w_gate_up"] = [r.reshape(-1)[:n_gu].reshape(1, GATE_RANK, c_gu) for r in sm]
    res["conv_w"] = [r.reshape(-1)[n_gu:n_gu + 3 * c_up].reshape(1, 3, c_up) for r in sm]
    rep = _sum_adamw(got_rep, _flat_pad_rows(rep_w, SMALL_ROWS), _flat_pad_rows(rep_m, SMALL_ROWS),
                     _flat_pad_rows(rep_v, SMALL_ROWS), SMALL_ROWS, "adamw_replicated")
    o = 0
    for n, w in zip(rep_names, rep_w):
        res[n] = [r.reshape(-1)[o:o + w.size].reshape(w.shape) for r in rep]
        o += w.size

    loss = lax.psum(loss_dev[0, 0], ("x", "y", "c"))
    order = ["attn_norm_g", "w_in", "w_gate_up", "b_gate_up", "sb_out_g", "gla_out_g", "w_out", "ffn_norm_g",
             "w_ffn_up", "conv_w", "conv_b", "w_ffn_down", "final_norm_g"]
    outs = [loss, dx.reshape(Bd, S, D)]
    for k in range(4):
        outs += [res[n][k] for n in order]
    return tuple(outs)
```

```python
import functools

import numpy as np
import jax
import jax.numpy as jnp
from jax import lax
from jax.experimental import pallas as pl
from jax.experimental.pallas import tpu as pltpu

F32 = jnp.float32
BF16 = jnp.bfloat16
SDS = jax.ShapeDtypeStruct
BS = pl.BlockSpec

N_DEV = 8
D = 1024
EPS = 1e-6
SB_HD = 64
SB_W = 512
GLA_DK = 64
GLA_DV = 128
GLA_KW = 256
GLA_W = 512
GATE_RANK = 16
GATE_NORM = 16.0
CHUNK = 64
GLA_UNROLL = 4
QT = 256
D_FF = 2816
IN_COLS = 3088
PROJ_W = 3200
LANE = 128
VMEM_LIMIT = 56 * 1024 * 1024

ADAM_LR, ADAM_B1, ADAM_B2, ADAM_EPS, ADAM_WD, ADAM_STEP = 0.001, 0.9, 0.999, 1e-08, 0.01, 10


def _proj_perm():
    sbq, sbk, sbv = 0, 512, 1024
    gq, gk, gv, glr, gog = 1536, 1792, 2048, 2560, 2576
    cols = []
    for p in range(4):
        for base in (sbq, sbk, sbv):
            cols += list(range(base + 128 * p, base + 128 * p + 128))
    for p in range(2):
        cols += list(range(gq + 128 * p, gq + 128 * p + 128))
        cols += list(range(gk + 128 * p, gk + 128 * p + 128))
        cols += list(range(gv + 256 * p, gv + 256 * p + 256))
    cols += list(range(gog, gog + 512))
    cols += list(range(glr, glr + GATE_RANK)) + [IN_COLS] * (LANE - GATE_RANK)
    perm = np.asarray(cols, np.int32)
    inv = np.zeros((IN_COLS,), np.int32)
    for new, old in enumerate(cols):
        if old < IN_COLS:
            inv[old] = new
    return perm, inv


_PERM, _INV_PERM = _proj_perm()
OG_BLK = 5
GLR_BLK = 24


def _cp(sem=None, vmem=VMEM_LIMIT):
    return pltpu.CompilerParams(dimension_semantics=sem, vmem_limit_bytes=vmem)


def _dot(a, b):
    return lax.dot_general(a, b, (((1,), (0,)), ((), ())), preferred_element_type=F32)


def _dot_nt(a, b):
    return lax.dot_general(a, b, (((1,), (1,)), ((), ())), preferred_element_type=F32)


def _dot_tn(a, b):
    return lax.dot_general(a, b, (((0,), (0,)), ((), ())), preferred_element_type=F32)


def _bf(x):
    return x.astype(BF16)


def _split_dot(x, m, passes, left=True):
    acc = None
    r = x
    for i in range(passes):
        h = r.astype(BF16)
        t = _dot(h, m) if left else _dot(m, h)
        acc = t if acc is None else acc + t
        if i + 1 < passes:
            r = r - h.astype(F32)
    return acc


def _softplus(z):
    return jnp.maximum(z, 0.0) + jnp.log1p(jnp.exp(-jnp.abs(z)))


def _tile(n, pref, mult=LANE):
    best = None
    for t in range(mult, min(n, pref) + 1, mult):
        if n % t == 0:
            best = t
    return best if best is not None else n


def _mm(a, b, mode, name, out_dtype=F32, c=None, tm=512, tn=512, tk=512, b_row0=0, out_rows=None, out_row0=0,
        into=None, xchg=None):
    if mode == "nn":
        (M, K), N = a.shape, b.shape[1]
    elif mode == "nt":
        (M, K), N = a.shape, b.shape[0]
    else:
        (K, M), N = a.shape, b.shape[1]
    tm, tn, tk = _tile(M, tm), _tile(N, tn), _tile(K, tk)
    nk = K // tk
    kb0, ob0 = b_row0 // tk, out_row0 // tm
    assert kb0 * tk == b_row0 and ob0 * tm == out_row0 and (mode == "nn" or b_row0 == 0)
    ni, nj = M // tm, N // tn
    j_outer = nk == 1 and (nj - 1) * a.size * a.dtype.itemsize < (ni - 1) * K * N * b.dtype.itemsize
    ix = (lambda f: (lambda j, i, k: f(i, j, k))) if j_outer else (lambda f: f)
    a_spec = BS((tk, tm), ix(lambda i, j, k: (k, i))) if mode == "tn" else BS((tm, tk), ix(lambda i, j, k: (i, k)))
    b_spec = (BS((tn, tk), ix(lambda i, j, k: (j, k))) if mode == "nt"
              else BS((tk, tn), ix(lambda i, j, k: (k + kb0, j))))
    dotfn = {"nn": _dot, "nt": _dot_nt, "tn": _dot_tn}[mode]
    has_c = c is not None
    has_into = into is not None
    nx = 0 if xchg is None else len(xchg[0])
    n_in = 2 + has_c + has_into

    def body(*refs):
        a_ref, b_ref = refs[:2]
        c_ref = refs[2] if has_c else None
        x_src = refs[n_in:n_in + nx]
        o_ref = refs[n_in + nx]
        x_out = refs[n_in + nx + 1:n_in + 2 * nx + 1]
        acc = refs[n_in + 2 * nx + 1]
        sems = refs[n_in + 2 * nx + 2:]
        k = pl.program_id(2)
        if nx:
            g0, g1 = pl.program_id(0), pl.program_id(1)
            n0, n1 = (nj, ni) if j_outer else (ni, nj)

            @pl.when(jnp.logical_and(jnp.logical_and(g0 == 0, g1 == 0), k == 0))
            def _():
                _exchange_ops(x_src, x_out, xchg[1], sems, "start")

        @pl.when(k == 0)
        def _():
            acc[...] = jnp.zeros_like(acc)

        acc[...] += dotfn(_bf(a_ref[...]), _bf(b_ref[...]))

        @pl.when(k == nk - 1)
        def _():
            r = acc[...]
            if has_c:
                r = r + c_ref[...]
            o_ref[...] = r.astype(out_dtype)

        if nx:
            @pl.when(jnp.logical_and(jnp.logical_and(g0 == n0 - 1, g1 == n1 - 1), k == nk - 1))
            def _():
                _exchange_ops(x_src, x_out, xchg[1], sems, "wait")

    in_specs = [a_spec, b_spec]
    args = [a, b]
    if has_c:
        in_specs.append(BS((tm, tn), ix(lambda i, j, k: (i, j))))
        args.append(c)
    aliases = {}
    if has_into:
        aliases = {len(args): 0}
        in_specs.append(BS(memory_space=pl.ANY))
        args.append(into)
    out_shape = [SDS((out_rows or M, N), out_dtype)]
    out_specs = [BS((tm, tn), ix(lambda i, j, k: (i + ob0, j)))]
    scratch = [pltpu.VMEM((tm, tn), F32)]
    if nx:
        hbm = BS(memory_space=pltpu.HBM)
        in_specs += [hbm] * nx
        args += list(xchg[0])
        out_shape += [SDS((N_DEV,) + s.shape if xchg[1] else s.shape, s.dtype) for s in xchg[0]]
        out_specs += [hbm] * nx
        scratch += _exchange_sems(nx)
    res = pl.pallas_call(
        body, name=name, out_shape=tuple(out_shape), grid=(nj, ni, nk) if j_outer else (ni, nj, nk),
        in_specs=in_specs, out_specs=tuple(out_specs),
        scratch_shapes=scratch, input_output_aliases=aliases,
        compiler_params=_cp(("arbitrary",) * 3 if nx else ("parallel", "parallel", "arbitrary")),
    )(*args)
    return res if nx else res[0]


def _norm_proj(x, g, w):
    T, N = x.shape[0], w.shape[0]
    tm = _tile(T, 256)

    def body(x_ref, g_ref, w_ref, p_ref, h_ref):
        xv = x_ref[...]
        r = lax.rsqrt(jnp.mean(xv * xv, axis=-1, keepdims=True) + EPS)
        h = _bf((xv * r) * g_ref[...])
        h_ref[...] = h
        p_ref[...] = _dot_nt(h, w_ref[...])

    return pl.pallas_call(
        body, name="norm_proj", out_shape=(SDS((T, N), F32), SDS((T, D), BF16)), grid=(T // tm,),
        in_specs=[BS((tm, D), lambda i: (i, 0)), BS((1, D), lambda i: (0, 0)), BS((N, D), lambda i: (0, 0))],
        out_specs=(BS((tm, N), lambda i: (i, 0)), BS((tm, D), lambda i: (i, 0))),
        compiler_params=_cp(("parallel",)),
    )(x, g, w)


TK = 256
SB_DEAD = -104.0
CNT_LANE = SB_HD - 1


def _sb_masks():
    row = lax.broadcasted_iota(jnp.int32, (2 * QT, TK), 0) & (QT - 1)
    col = lax.broadcasted_iota(jnp.int32, (2 * QT, TK), 1)
    lane = lax.broadcasted_iota(jnp.int32, (1, LANE), 1)
    kr = lax.broadcasted_iota(jnp.int32, (TK, TK), 0)
    kc = lax.broadcasted_iota(jnp.int32, (TK, TK), 1)
    return row, col, lane, kr, kc


def _stack_heads(x, lane):
    return jnp.concatenate([_bf(jnp.where((lane // SB_HD) == hh, x, 0.0)) for hh in range(2)], axis=0)


def _sb_fwd(proj, S, shards):
    T = proj.shape[0]
    nq = S // QT
    scale = SB_HD ** -0.5
    nb, ns = T // S, len(shards)

    def body(qkv_ref, *rest):
        sh_refs, (o_ref, tt_ref), g_refs = rest[:ns], rest[ns:ns + 2], rest[ns + 2:2 * ns + 2]
        sems = rest[2 * ns + 2:]
        first = jnp.logical_and(pl.program_id(0) == 0, pl.program_id(1) == 0)
        last = jnp.logical_and(pl.program_id(0) == nb - 1, pl.program_id(1) == 3)

        @pl.when(first)
        def _():
            _exchange_ops(sh_refs, g_refs, True, sems, "start")

        row, col, lane, kr, kc = _sb_masks()
        msuf = _bf(kr > kc)

        def qloop(qt, _):
            r0 = pl.multiple_of(qt * QT, QT)
            qs = _stack_heads(qkv_ref[pl.ds(r0, QT), 0:128], lane)

            def live(st):
                it, _, cy = st
                return jnp.logical_and(it <= qt, jnp.max(cy) > SB_DEAD)

            def step(st):
                it, acc, cy = st
                kt = qt - it
                k0 = pl.multiple_of(kt * TK, TK)
                kv = _bf(qkv_ref[pl.ds(k0, TK), 128:256])
                vv = _bf(qkv_ref[pl.ds(k0, TK), 256:384])
                strict = (col + (kt - qt) * TK) < row
                z = _dot_nt(qs, kv) * scale
                sp = _softplus(z)
                lg = jnp.where(strict, -sp, 0.0)
                after = cy + _split_dot(lg, msuf, 2)
                w = jnp.where(strict, jnp.exp((z - sp) + after), 0.0)
                return it + 1, acc + _dot(_bf(w), vv), cy + jnp.sum(lg, axis=1, keepdims=True)

            it, acc, cy = lax.while_loop(
                live, step, (jnp.int32(0), jnp.zeros((2 * QT, LANE), F32), jnp.zeros((2 * QT, 1), F32)))
            o_ref[pl.ds(r0, QT), :] = jnp.where(lane < SB_HD, acc[:QT], acc[QT:])
            tt = jnp.where(lane < SB_HD, cy[:QT], cy[QT:])
            tt_ref[pl.ds(r0, QT), :] = jnp.where(lane == CNT_LANE, it.astype(F32), tt)
            return 0

        lax.fori_loop(0, nq, qloop, 0)

        @pl.when(last)
        def _():
            _exchange_ops(sh_refs, g_refs, True, sems, "wait")

    hbm = BS(memory_space=pltpu.HBM)
    col_spec = BS((S, LANE), lambda b, p: (b, p))
    return pl.pallas_call(
        body, name="sb_fwd",
        out_shape=(SDS((T, SB_W), F32), SDS((T, SB_W), F32)) + tuple(SDS((N_DEV,) + s.shape, s.dtype) for s in shards),
        grid=(nb, 4),
        in_specs=[BS((S, 384), lambda b, p: (b, p))] + [hbm] * ns,
        out_specs=(col_spec, col_spec) + (hbm,) * ns,
        scratch_shapes=_exchange_sems(ns),
        compiler_params=_cp(("arbitrary", "arbitrary")),
    )(proj, *shards)


def _log_sigmoid(x):
    return jnp.minimum(x, 0.0) - jnp.log1p(jnp.exp(-jnp.abs(x)))


def _loop_unrolled(n, step, init):
    assert n % GLA_UNROLL == 0

    def body(i, st):
        for u in range(GLA_UNROLL):
            st = step(i * GLA_UNROLL + u, st)
        return st

    return lax.fori_loop(0, n // GLA_UNROLL, body, init)


def _gla_chunk_terms(blk_ref, glr_ref, wgu, bgu, r0, tri_incl):
    q = blk_ref[pl.ds(r0, CHUNK), 0:128]
    k = blk_ref[pl.ds(r0, CHUNK), 128:256]
    v = blk_ref[pl.ds(r0, CHUNK), 256:512]
    pre = _dot(_bf(glr_ref[pl.ds(r0, CHUNK), :]), wgu) + bgu
    la = _log_sigmoid(pre) / GATE_NORM
    b = _split_dot(la, tri_incl, 3, left=False)
    b_last = b[CHUNK - 1:CHUNK, :]
    eb = jnp.exp(b)
    qd = (q * (GLA_DK ** -0.5)) * eb
    ki = k * jnp.exp(-b)
    ke = k * jnp.exp(b_last - b)
    decay = jnp.exp(b_last)
    return q, k, v, pre, b, b_last, eb, qd, ki, ke, decay


def _gla_fwd(proj, wgu, bgu, S):
    T = proj.shape[0]
    nc = S // CHUNK

    def body(blk_ref, glr_ref, wgu_ref, bgu_ref, o_ref):
        rr = lax.broadcasted_iota(jnp.int32, (CHUNK, CHUNK), 0)
        cc = lax.broadcasted_iota(jnp.int32, (CHUNK, CHUNK), 1)
        lane = lax.broadcasted_iota(jnp.int32, (1, LANE), 1)
        causal = rr >= cc
        tri_incl = _bf(causal)
        wg = _bf(wgu_ref[...])
        bg = bgu_ref[...]

        def chunk(n, states):
            r0 = pl.multiple_of(n * CHUNK, CHUNK)
            _, _, v, _, _, _, _, qd, ki, ke, decay = _gla_chunk_terms(blk_ref, glr_ref, wg, bg, r0, tri_incl)
            new_states, outs = [], []
            for hh in range(2):
                hm = (lane // GLA_DK) == hh
                qm = _bf(jnp.where(hm, qd, 0.0))
                vh = _bf(v[:, 128 * hh:128 * hh + 128])
                st = states[hh]
                attn = jnp.where(causal, _dot_nt(qm, _bf(ki)), 0.0)
                outs.append(_dot(_bf(attn), vh) + _dot_nt(qm, _bf(st)))
                new_states.append(st * decay + _dot_tn(vh, _bf(ke)))
            o_ref[pl.ds(r0, CHUNK), :] = jnp.concatenate(outs, axis=1)
            return tuple(new_states)

        z = jnp.zeros((GLA_DV, LANE), F32)
        _loop_unrolled(nc, chunk, (z, z))

    return pl.pallas_call(
        body, name="gla_fwd", out_shape=SDS((T, GLA_W), F32), grid=(T // S, 2),
        in_specs=[BS((S, 512), lambda b, p: (b, 3 + p)), BS((S, LANE), lambda b, p: (b, GLR_BLK)),
                  BS((LANE, LANE), lambda b, p: (0, p)), BS((1, LANE), lambda b, p: (0, p))],
        out_specs=BS((S, 256), lambda b, p: (b, p)),
        compiler_params=_cp(("parallel", "parallel")),
    )(proj, proj, wgu, bgu)


def _head_blockdiag(width, hd):
    r = lax.broadcasted_iota(jnp.int32, (width, width), 0) // hd
    c = lax.broadcasted_iota(jnp.int32, (width, width), 1) // hd
    return _bf(r == c)


def _mix_out(o_sb, o_gla, proj, x, g_sb, g_gla, w_out, g2):
    T = x.shape[0]
    tm = _tile(T, 256)

    def body(osb_ref, ogl_ref, og_ref, x_ref, gsb_ref, ggl_ref, w_ref, g2_ref, x1_ref, oc_ref, h2_ref):
        bd64 = _head_blockdiag(SB_W, SB_HD)
        bd128 = _head_blockdiag(GLA_W, GLA_DV)
        o = osb_ref[...]
        r = lax.rsqrt(_split_dot(o * o, bd64, 2) * (1.0 / SB_HD) + EPS)
        c_sb = (o * r) * gsb_ref[...]
        o = ogl_ref[...]
        r = lax.rsqrt(_split_dot(o * o, bd128, 2) * (1.0 / GLA_DV) + EPS)
        og = og_ref[...]
        c_gl = ((o * r) * ggl_ref[...]) * (og * jax.nn.sigmoid(og))
        oc = _bf(jnp.concatenate([c_sb, c_gl], axis=1))
        oc_ref[...] = oc
        x1 = x_ref[...] + _dot(oc, w_ref[...])
        x1_ref[...] = x1
        r2 = lax.rsqrt(jnp.mean(x1 * x1, axis=-1, keepdims=True) + EPS)
        h2_ref[...] = _bf((x1 * r2) * g2_ref[...])

    row = lambda w: BS((tm, w), lambda i: (i, 0))
    vec = lambda w: BS((1, w), lambda i: (0, 0))
    return pl.pallas_call(
        body, name="mix_out", out_shape=(SDS((T, D), F32), SDS((T, D), BF16), SDS((T, D), BF16)), grid=(T // tm,),
        in_specs=[row(SB_W), row(GLA_W), BS((tm, 512), lambda i: (i, OG_BLK)), row(D), vec(SB_W), vec(GLA_W),
                  BS((D, D), lambda i: (0, 0)), vec(D)],
        out_specs=(row(D), row(D), row(D)),
        compiler_params=_cp(("parallel",)),
    )(o_sb, o_gla, proj, x, g_sb, g_gla, w_out, g2)


CONV_ROWS = 256
CONV_TC = 256


def _rows_before(ref, r0, first):
    prev = ref[pl.ds(pl.multiple_of(jnp.maximum(r0 - 8, 0), 8), 8), :]
    return jnp.where(first, 0.0, prev)


def _shift_down(cur, prev8, k):
    cat = jnp.concatenate([prev8, cur], axis=0)
    return pltpu.roll(cat, k, 0)[8:]


def _shift_up(cur, next8, k):
    cat = jnp.concatenate([cur, next8], axis=0)
    return pltpu.roll(cat, cat.shape[0] - k, 0)[:cur.shape[0]]


def _conv_at(h_ref, cw, cb, r0, rows, first):
    cur = h_ref[pl.ds(r0, rows), :]
    prev8 = _rows_before(h_ref, r0, first)
    u = cb + cw[0:1, :] * _shift_down(cur, prev8, 2)
    u = u + cw[1:2, :] * _shift_down(cur, prev8, 1)
    return u + cw[2:3, :] * cur


NJ = D_FF // CONV_TC


def _conv_gate(hup, cw, cb, S):
    T = hup.shape[0]
    rows = min(CONV_ROWS, S)
    nr = S // rows

    def body(ha_ref, hv_ref, cwa_ref, cwv_ref, cba_ref, cbv_ref, act_ref):
        cwa, cwv, cba, cbv = cwa_ref[...], cwv_ref[...], cba_ref[...], cbv_ref[...]

        def step(c, _):
            r0 = pl.multiple_of(c * rows, rows)
            ua = _conv_at(ha_ref, cwa, cba, r0, rows, c == 0)
            uv = _conv_at(hv_ref, cwv, cbv, r0, rows, c == 0)
            act_ref[pl.ds(r0, rows), :] = _bf((ua * jax.nn.sigmoid(ua)) * uv)
            return 0

        lax.fori_loop(0, nr, step, 0)

    blk = lambda o: BS((S, CONV_TC), lambda b, j: (b, j + o))
    w3 = lambda o: BS((3, CONV_TC), lambda b, j: (0, j + o))
    w1 = lambda o: BS((1, CONV_TC), lambda b, j: (0, j + o))
    return pl.pallas_call(
        body, name="conv_gate", out_shape=SDS((T, D_FF), BF16), grid=(T // S, NJ),
        in_specs=[blk(0), blk(NJ), w3(0), w3(NJ), w1(0), w1(NJ)], out_specs=blk(0),
        compiler_params=_cp(("parallel", "parallel")),
    )(hup, hup, cw, cw, cb, cb)


def _down_loss(act, w_down, x1, tgt, g3):
    T = x1.shape[0]
    tm = _tile(T, 256)

    def body(a_ref, w_ref, x1_ref, t_ref, g_ref, dx_ref, dg_ref, ls_ref):
        @pl.when(pl.program_id(0) == 0)
        def _():
            dg_ref[...] = jnp.zeros_like(dg_ref)
            ls_ref[...] = jnp.zeros_like(ls_ref)

        g = g_ref[...]
        x2 = x1_ref[...] + _dot(a_ref[...], w_ref[...])
        r = lax.rsqrt(jnp.mean(x2 * x2, axis=-1, keepdims=True) + EPS)
        xh = x2 * r
        e = xh * g - t_ref[...]
        ls_ref[...] += 0.5 * jnp.sum(jnp.mean(e * e, axis=-1, keepdims=True), axis=0, keepdims=True)
        dy = e * (1.0 / D)
        dxh = dy * g
        dx_ref[...] = r * (dxh - xh * jnp.mean(dxh * xh, axis=-1, keepdims=True))
        dg_ref[...] += jnp.sum(dy * xh, axis=0, keepdims=True)

    row = lambda w: BS((tm, w), lambda i: (i, 0))
    return pl.pallas_call(
        body, name="down_loss", out_shape=(SDS((T, D), F32), SDS((1, D), F32), SDS((1, LANE), F32)), grid=(T // tm,),
        in_specs=[row(D_FF), BS((D_FF, D), lambda i: (0, 0)), row(D), row(D), BS((1, D), lambda i: (0, 0))],
        out_specs=(row(D), BS((1, D), lambda i: (0, 0)), BS((1, LANE), lambda i: (0, 0))),
        compiler_params=_cp(("arbitrary",)),
    )(act, w_down, x1, tgt, g3)


def _conv_gate_bwd(hup, dact, cw, cb, S):
    T = hup.shape[0]
    rows = min(CONV_ROWS, S)
    nr = S // rows

    def body(ha_ref, hv_ref, da_ref, cwa_ref, cwv_ref, cba_ref, cbv_ref,
             dha_ref, dhv_ref, dcwa_ref, dcwv_ref, dcba_ref, dcbv_ref):
        @pl.when(pl.program_id(1) == 0)
        def _():
            for r in (dcwa_ref, dcwv_ref, dcba_ref, dcbv_ref):
                r[...] = jnp.zeros_like(r)

        cwa, cwv, cba, cbv = cwa_ref[...], cwv_ref[...], cba_ref[...], cbv_ref[...]

        def du_at(r0, n, first):
            ua = _conv_at(ha_ref, cwa, cba, r0, n, first)
            uv = _conv_at(hv_ref, cwv, cbv, r0, n, first)
            da = da_ref[pl.ds(r0, n), :]
            sg = jax.nn.sigmoid(ua)
            dua = (da * uv) * (sg * (1.0 + ua * (1.0 - sg)))
            duv = da * (ua * sg)
            return dua, duv

        def step(c, _):
            r0 = pl.multiple_of(c * rows, rows)
            first, last = c == 0, c == nr - 1
            dua, duv = du_at(r0, rows, first)
            n0 = pl.multiple_of(jnp.minimum(r0 + rows, S - 8), 8)
            nua, nuv = du_at(n0, 8, False)
            nua = jnp.where(last, 0.0, nua)
            nuv = jnp.where(last, 0.0, nuv)
            for (h_ref, cw, du, nu, dh_ref, dcw_ref, dcb_ref) in (
                    (ha_ref, cwa, dua, nua, dha_ref, dcwa_ref, dcba_ref),
                    (hv_ref, cwv, duv, nuv, dhv_ref, dcwv_ref, dcbv_ref)):
                dh = cw[2:3, :] * du + cw[1:2, :] * _shift_up(du, nu, 1) + cw[0:1, :] * _shift_up(du, nu, 2)
                dh_ref[pl.ds(r0, rows), :] = _bf(dh)
                cur = h_ref[pl.ds(r0, rows), :]
                prev8 = _rows_before(h_ref, r0, first)
                dcw_ref[0:1, :] += jnp.sum(du * _shift_down(cur, prev8, 2), axis=0, keepdims=True)
                dcw_ref[1:2, :] += jnp.sum(du * _shift_down(cur, prev8, 1), axis=0, keepdims=True)
                dcw_ref[2:3, :] += jnp.sum(du * cur, axis=0, keepdims=True)
                dcb_ref[...] += jnp.sum(du, axis=0, keepdims=True)
            return 0

        lax.fori_loop(0, nr, step, 0)

    blk = lambda o: BS((S, CONV_TC), lambda j, b: (b, j + o))
    w3 = lambda o: BS((3, CONV_TC), lambda j, b: (0, j + o))
    w1 = lambda o: BS((1, CONV_TC), lambda j, b: (0, j + o))
    return pl.pallas_call(
        body, name="conv_gate_bwd",
        out_shape=(SDS((T, D_FF), BF16), SDS((T, D_FF), BF16), SDS((3, D_FF), F32), SDS((3, D_FF), F32),
                   SDS((1, D_FF), F32), SDS((1, D_FF), F32)),
        grid=(NJ, T // S),
        in_specs=[blk(0), blk(NJ), blk(0), w3(0), w3(NJ), w1(0), w1(NJ)],
        out_specs=(blk(0), blk(0), w3(0), w3(0), w1(0), w1(0)),
        compiler_params=_cp(("parallel", "arbitrary")),
    )(hup, hup, dact, cw, cw, cb, cb)


def _rms_bwd(x, g, dh, dres, name):
    T = x.shape[0]
    tm = _tile(T, 512)

    def body(x_ref, g_ref, dh_ref, dr_ref, dx_ref, dg_ref):
        @pl.when(pl.program_id(0) == 0)
        def _():
            dg_ref[...] = jnp.zeros_like(dg_ref)

        xv = x_ref[...]
        dh = dh_ref[...]
        r = lax.rsqrt(jnp.mean(xv * xv, axis=-1, keepdims=True) + EPS)
        xh = xv * r
        dxh = dh * g_ref[...]
        dx_ref[...] = dr_ref[...] + r * (dxh - xh * jnp.mean(dxh * xh, axis=-1, keepdims=True))
        dg_ref[...] += jnp.sum(dh * xh, axis=0, keepdims=True)

    row = BS((tm, D), lambda i: (i, 0))
    vec = BS((1, D), lambda i: (0, 0))
    return pl.pallas_call(
        body, name=name, out_shape=(SDS((T, D), F32), SDS((1, D), F32)), grid=(T // tm,),
        in_specs=[row, vec, row, row], out_specs=(row, vec),
        compiler_params=_cp(("arbitrary",)),
    )(x, g, dh, dres)


def _mix_bwd(docat, o_sb, o_gla, proj, g_sb, g_gla):
    T = docat.shape[0]
    tm = _tile(T, 256)

    def body(d_ref, osb_ref, ogl_ref, og_ref, gsb_ref, ggl_ref, dsb_ref, dgl_ref, dog_ref, dgsb_ref, dggl_ref):
        @pl.when(pl.program_id(0) == 0)
        def _():
            dgsb_ref[...] = jnp.zeros_like(dgsb_ref)
            dggl_ref[...] = jnp.zeros_like(dggl_ref)

        bd64 = _head_blockdiag(SB_W, SB_HD)
        bd128 = _head_blockdiag(GLA_W, GLA_DV)
        d = d_ref[:, 0:SB_W]
        o = osb_ref[...]
        r = lax.rsqrt(_split_dot(o * o, bd64, 2) * (1.0 / SB_HD) + EPS)
        n = o * r
        dn = d * gsb_ref[...]
        dgsb_ref[...] += jnp.sum(d * n, axis=0, keepdims=True)
        dsb_ref[...] = r * (dn - n * (_split_dot(dn * n, bd64, 2) * (1.0 / SB_HD)))

        d = d_ref[:, SB_W:D]
        o = ogl_ref[...]
        r = lax.rsqrt(_split_dot(o * o, bd128, 2) * (1.0 / GLA_DV) + EPS)
        n = o * r
        og = og_ref[...]
        sg = jax.nn.sigmoid(og)
        dm = d * (og * sg)
        dog_ref[...] = _bf((d * (n * ggl_ref[...])) * (sg * (1.0 + og * (1.0 - sg))))
        dn = dm * ggl_ref[...]
        dggl_ref[...] += jnp.sum(dm * n, axis=0, keepdims=True)
        dgl_ref[...] = r * (dn - n * (_split_dot(dn * n, bd128, 2) * (1.0 / GLA_DV)))

    row = lambda w: BS((tm, w), lambda i: (i, 0))
    vec = lambda w: BS((1, w), lambda i: (0, 0))
    ogb = BS((tm, 512), lambda i: (i, OG_BLK))
    return pl.pallas_call(
        body, name="mix_bwd",
        out_shape=(SDS((T, SB_W), F32), SDS((T, GLA_W), F32), SDS((T, PROJ_W), BF16), SDS((1, SB_W), F32),
                   SDS((1, GLA_W), F32)),
        grid=(T // tm,),
        in_specs=[row(D), row(SB_W), row(GLA_W), ogb, vec(SB_W), vec(GLA_W)],
        out_specs=(row(SB_W), row(GLA_W), ogb, vec(SB_W), vec(GLA_W)),
        compiler_params=_cp(("arbitrary",)),
    )(docat, o_sb, o_gla, proj, g_sb, g_gla)


def _sb_bwd(proj, tt, do, dproj, S, pieces):
    T = proj.shape[0]
    nq = S // QT
    scale = SB_HD ** -0.5
    nb, ns = T // S, len(pieces)

    def body(qkv_ref, tt_ref, do_ref, dp_in_ref, *rest):
        del dp_in_ref
        pc_refs, dp_ref, got_refs = rest[:ns], rest[ns], rest[ns + 1:2 * ns + 1]
        dk_acc, dv_acc = rest[2 * ns + 1:2 * ns + 3]
        sems = rest[2 * ns + 3:]
        first = jnp.logical_and(pl.program_id(0) == 0, pl.program_id(1) == 0)
        last = jnp.logical_and(pl.program_id(0) == nb - 1, pl.program_id(1) == 3)

        @pl.when(first)
        def _():
            _exchange_ops(pc_refs, got_refs, False, sems, "start")

        row, col, lane, kr, kc = _sb_masks()
        mincl = _bf(kr <= kc)
        mexcl = _bf(kr < kc)
        dk_acc[...] = jnp.zeros_like(dk_acc)
        dv_acc[...] = jnp.zeros_like(dv_acc)

        def qloop(qt, _):
            r0 = pl.multiple_of(qt * QT, QT)
            qs = _stack_heads(qkv_ref[pl.ds(r0, QT), 0:128], lane)
            dos = _stack_heads(do_ref[pl.ds(r0, QT), :], lane)
            ttv = tt_ref[pl.ds(r0, QT), :]
            tot = jnp.concatenate([ttv[:, 0:1], ttv[:, SB_HD:SB_HD + 1]], axis=0)
            walked = jnp.max(ttv[:, CNT_LANE:CNT_LANE + 1]).astype(jnp.int32)

            def step(kt, st):
                dq, lc, pc = st
                k0 = pl.multiple_of(kt * TK, TK)
                kv = _bf(qkv_ref[pl.ds(k0, TK), 128:256])
                vv = _bf(qkv_ref[pl.ds(k0, TK), 256:384])
                strict = (col + (kt - qt) * TK) < row
                z = _dot_nt(qs, kv) * scale
                sp = _softplus(z)
                lg = jnp.where(strict, -sp, 0.0)
                after = tot - (lc + _split_dot(lg, mincl, 2))
                gl = z - sp
                w = jnp.where(strict, jnp.exp(gl + after), 0.0)
                du = w * _dot_nt(dos, vv)
                beta = jnp.exp(gl)
                pex = pc + _split_dot(du, mexcl, 2)
                dz = _bf(jnp.where(strict, du * (1.0 - beta) - beta * pex, 0.0) * scale)
                dk_acc[pl.ds(k0, TK), :] += _dot_tn(dz, qs)
                dv_acc[pl.ds(k0, TK), :] += _dot_tn(_bf(w), dos)
                return (dq + _dot(dz, kv), lc + jnp.sum(lg, axis=1, keepdims=True),
                        pc + jnp.sum(du, axis=1, keepdims=True))

            zc = jnp.zeros((2 * QT, 1), F32)
            dq, _, _ = lax.fori_loop(qt - walked + 1, qt + 1, step, (jnp.zeros((2 * QT, LANE), F32), zc, zc))
            dp_ref[pl.ds(r0, QT), 0:128] = _bf(jnp.where(lane < SB_HD, dq[:QT], dq[QT:]))
            return 0

        lax.fori_loop(0, nq, qloop, 0)
        dp_ref[:, 128:256] = _bf(dk_acc[...])
        dp_ref[:, 256:384] = _bf(dv_acc[...])

        @pl.when(last)
        def _():
            _exchange_ops(pc_refs, got_refs, False, sems, "wait")

    blk = BS((S, 384), lambda b, p: (b, p))
    col_spec = BS((S, LANE), lambda b, p: (b, p))
    hbm = BS(memory_space=pltpu.HBM)
    return pl.pallas_call(
        body, name="sb_bwd", out_shape=(SDS((T, PROJ_W), BF16),) + tuple(SDS(s.shape, s.dtype) for s in pieces),
        grid=(nb, 4),
        in_specs=[blk, col_spec, col_spec, BS(memory_space=pl.ANY)] + [hbm] * ns, out_specs=(blk,) + (hbm,) * ns,
        scratch_shapes=[pltpu.VMEM((S, LANE), F32), pltpu.VMEM((S, LANE), F32)] + _exchange_sems(ns),
        input_output_aliases={3: 0},
        compiler_params=_cp(("arbitrary", "arbitrary")),
    )(proj, tt, do, dproj, *pieces)


def _gla_bwd(proj, wgu, bgu, do, dproj, S):
    T = proj.shape[0]
    nc = S // CHUNK

    def body(blk_ref, glr_ref, wgu_ref, bgu_ref, do_ref, dp_in_ref, dp_ref, dpre_ref, st_ref):
        del dp_in_ref
        rr = lax.broadcasted_iota(jnp.int32, (CHUNK, CHUNK), 0)
        cc = lax.broadcasted_iota(jnp.int32, (CHUNK, CHUNK), 1)
        lane = lax.broadcasted_iota(jnp.int32, (1, LANE), 1)
        causal = rr >= cc
        tri_incl = _bf(causal)
        tri_rev = _bf(rr <= cc)
        wg = _bf(wgu_ref[...])
        bg = bgu_ref[...]

        def fwd_chunk(n, states):
            r0 = pl.multiple_of(n * CHUNK, CHUNK)
            _, _, v, _, _, _, _, _, _, ke, decay = _gla_chunk_terms(blk_ref, glr_ref, wg, bg, r0, tri_incl)
            new_states = []
            for hh in range(2):
                st_ref[hh, n] = states[hh]
                vh = _bf(v[:, 128 * hh:128 * hh + 128])
                new_states.append(states[hh] * decay + _dot_tn(vh, _bf(ke)))
            return tuple(new_states)

        z = jnp.zeros((GLA_DV, LANE), F32)
        _loop_unrolled(nc, fwd_chunk, (z, z))

        def bwd_chunk(it, dstates):
            n = nc - 1 - it
            r0 = pl.multiple_of(n * CHUNK, CHUNK)
            _, _, v, pre, b, b_last, eb, qd, ki, ke, decay = _gla_chunk_terms(
                blk_ref, glr_ref, wg, bg, r0, tri_incl)
            dqd = jnp.zeros((CHUNK, LANE), F32)
            dki = jnp.zeros((CHUNK, LANE), F32)
            dke = jnp.zeros((CHUNK, LANE), F32)
            ddec = jnp.zeros((1, LANE), F32)
            new_dstates, dvs = [], []
            for hh in range(2):
                hm = (lane // GLA_DK) == hh
                qm = _bf(jnp.where(hm, qd, 0.0))
                kem = _bf(jnp.where(hm, ke, 0.0))
                vh = _bf(v[:, 128 * hh:128 * hh + 128])
                doh = _bf(do_ref[pl.ds(r0, CHUNK), 128 * hh:128 * hh + 128])
                st = st_ref[hh, n]
                dst = dstates[hh]
                attn = _bf(jnp.where(causal, _dot_nt(qm, _bf(ki)), 0.0))
                dattn = _bf(jnp.where(causal, _dot_nt(doh, vh), 0.0))
                dvs.append(_dot_tn(attn, doh) + _dot_nt(kem, _bf(dst)))
                dqd = dqd + jnp.where(hm, _dot(dattn, _bf(ki)) + _dot(doh, _bf(st)), 0.0)
                dki = dki + _dot_tn(dattn, qm)
                dke = dke + jnp.where(hm, _dot(vh, _bf(dst)), 0.0)
                ddec = ddec + jnp.where(hm, jnp.sum(dst * st, axis=0, keepdims=True), 0.0)
                new_dstates.append(dst * decay + _dot_tn(doh, qm))
            einv = jnp.exp(-b)
            eend = jnp.exp(b_last - b)
            dq = (dqd * eb) * (GLA_DK ** -0.5)
            dk = dki * einv + dke * eend
            db = dqd * qd - dki * ki - dke * ke
            db_last = jnp.sum(dke * ke, axis=0, keepdims=True) + ddec * decay
            dla = _split_dot(db, tri_rev, 3, left=False) + db_last
            dpre_ref[pl.ds(r0, CHUNK), :] = (dla * (1.0 / GATE_NORM)) * (1.0 - jax.nn.sigmoid(pre))
            dp_ref[pl.ds(r0, CHUNK), 0:128] = _bf(dq)
            dp_ref[pl.ds(r0, CHUNK), 128:256] = _bf(dk)
            dp_ref[pl.ds(r0, CHUNK), 256:512] = _bf(jnp.concatenate(dvs, axis=1))
            return tuple(new_dstates)

        _loop_unrolled(nc, bwd_chunk, (z, z))

    return pl.pallas_call(
        body, name="gla_bwd", out_shape=(SDS((T, PROJ_W), BF16), SDS((T, GLA_KW), F32)), grid=(T // S, 2),
        in_specs=[BS((S, 512), lambda b, p: (b, 3 + p)), BS((S, LANE), lambda b, p: (b, GLR_BLK)),
                  BS((LANE, LANE), lambda b, p: (0, p)), BS((1, LANE), lambda b, p: (0, p)),
                  BS((S, 256), lambda b, p: (b, p)), BS(memory_space=pl.ANY)],
        out_specs=(BS((S, 512), lambda b, p: (b, 3 + p)), BS((S, LANE), lambda b, p: (b, p))),
        scratch_shapes=[pltpu.VMEM((2, nc, GLA_DV, LANE), F32)],
        input_output_aliases={5: 0},
        compiler_params=_cp(("parallel", "parallel")),
    )(proj, proj, wgu, bgu, do, dproj)


def _gate_bwd(dpre, proj, wgu, dproj):
    T = dpre.shape[0]
    tm = _tile(T, 512)

    def body(dpre_ref, glr_ref, wgu_ref, dp_in_ref, dp_ref, dw_ref, db_ref):
        del dp_in_ref

        @pl.when(pl.program_id(0) == 0)
        def _():
            dw_ref[...] = jnp.zeros_like(dw_ref)
            db_ref[...] = jnp.zeros_like(db_ref)

        dpre = dpre_ref[...]
        dp_ref[...] = _bf(_dot_nt(_bf(dpre), _bf(wgu_ref[...])))
        dw_ref[...] += _dot_tn(_bf(glr_ref[...]), _bf(dpre))
        db_ref[...] += jnp.sum(dpre, axis=0, keepdims=True)

    glr = BS((tm, LANE), lambda i: (i, GLR_BLK))
    return pl.pallas_call(
        body, name="gate_bwd",
        out_shape=(SDS((T, PROJ_W), BF16), SDS((LANE, GLA_KW), F32), SDS((1, GLA_KW), F32)), grid=(T // tm,),
        in_specs=[BS((tm, GLA_KW), lambda i: (i, 0)), glr, BS((LANE, GLA_KW), lambda i: (0, 0)),
                  BS(memory_space=pl.ANY)],
        out_specs=(glr, BS((LANE, GLA_KW), lambda i: (0, 0)), BS((1, GLA_KW), lambda i: (0, 0))),
        input_output_aliases={3: 0},
        compiler_params=_cp(("arbitrary",)),
    )(dpre, proj, wgu, dproj)


def _exchange_sems(n):
    return [pltpu.SemaphoreType.DMA((n * (N_DEV - 1),)), pltpu.SemaphoreType.DMA((n * (N_DEV - 1),)),
            pltpu.SemaphoreType.DMA((n,))]


def _exchange_ops(srcs, outs, gather, sems, act):
    ssem, rsem, lsem = sems
    x, y, c = lax.axis_index("x"), lax.axis_index("y"), lax.axis_index("c")
    me = 4 * x + 2 * y + c
    for i, (s_ref, o_ref) in enumerate(zip(srcs, outs)):
        for k in range(1, N_DEV):
            px = (x + ((k >> 2) & 1)) % 2
            py = (y + ((k >> 1) & 1)) % 2
            pc = (c + (k & 1)) % 2
            peer = 4 * px + 2 * py + pc
            n = i * (N_DEV - 1) + k - 1
            out = pltpu.make_async_remote_copy(
                src_ref=s_ref if gather else s_ref.at[peer], dst_ref=o_ref.at[me],
                send_sem=ssem.at[n], recv_sem=rsem.at[n],
                device_id=(px, py, pc), device_id_type=pl.DeviceIdType.MESH)
            if act == "start":
                out.start()
            else:
                out.wait_send()
                pltpu.make_async_remote_copy(
                    src_ref=s_ref if gather else s_ref.at[me], dst_ref=o_ref.at[peer],
                    send_sem=ssem.at[n], recv_sem=rsem.at[n],
                    device_id=(x, y, c), device_id_type=pl.DeviceIdType.MESH).wait_recv()
        mine = pltpu.make_async_copy(s_ref if gather else s_ref.at[me], o_ref.at[me], lsem.at[i])
        if act == "start":
            mine.start()
        else:
            mine.wait()


def _exchange(srcs, gather, name):
    n = len(srcs)
    shapes = [SDS((N_DEV,) + s.shape if gather else s.shape, s.dtype) for s in srcs]

    def body(*refs):
        s_refs, o_refs, sems = refs[:n], refs[n:2 * n], refs[2 * n:]
        _exchange_ops(s_refs, o_refs, gather, sems, "start")
        _exchange_ops(s_refs, o_refs, gather, sems, "wait")

    hbm = BS(memory_space=pltpu.HBM)
    return pl.pallas_call(
        body, name=name, out_shape=tuple(shapes), in_specs=[hbm] * n, out_specs=(hbm,) * n,
        scratch_shapes=_exchange_sems(n),
    )(*srcs)


def _adamw_math(w, g, m, v):
    m = ADAM_B1 * m + (1.0 - ADAM_B1) * g
    v = ADAM_B2 * v + (1.0 - ADAM_B2) * (g * g)
    m_hat = m / (1.0 - ADAM_B1 ** ADAM_STEP)
    v_hat = v / (1.0 - ADAM_B2 ** ADAM_STEP)
    delta = -ADAM_LR * (m_hat / (jnp.sqrt(v_hat) + ADAM_EPS) + ADAM_WD * w)
    return delta, m, v


def _sum_adamw(parts, w, m, v, tr, name):
    R, C = w.shape

    def body(p_ref, w_ref, m_ref, v_ref, g_ref, d_ref, nm_ref, nv_ref):
        g = p_ref[0].astype(F32)
        for d in range(1, N_DEV):
            g = g + p_ref[d].astype(F32)
        delta, nm, nv = _adamw_math(w_ref[...], g, m_ref[...], v_ref[...])
        g_ref[...] = g
        d_ref[...] = delta
        nm_ref[...] = nm
        nv_ref[...] = nv

    blk = BS((tr, C), lambda i: (i, 0))
    out = SDS((R, C), F32)
    return pl.pallas_call(
        body, name=name, out_shape=(out, out, out, out), grid=(R // tr,),
        in_specs=[BS((N_DEV, tr, C), lambda i: (0, i, 0)), blk, blk, blk], out_specs=(blk, blk, blk, blk),
        compiler_params=_cp(("parallel",)),
    )(parts, w, m, v)


def _flat_pad_rows(parts, rows):
    flat = jnp.concatenate([p.reshape(-1) for p in parts])
    return jnp.pad(flat, (0, rows * D - flat.shape[0])).reshape(rows, D)


SMALL_ROWS = 16
SHARD_SMALL_ROWS = 3


def kernel(x, attn_norm_g, w_in, w_gate_up, b_gate_up, sb_out_g, gla_out_g, w_out, ffn_norm_g, w_ffn_up, conv_w, conv_b, w_ffn_down, final_norm_g, loss_target, m_attn_norm_g, m_w_in, m_w_gate_up, m_b_gate_up, m_sb_out_g, m_gla_out_g, m_w_out, m_ffn_norm_g, m_w_ffn_up, m_conv_w, m_conv_b, m_w_ffn_down, m_final_norm_g, v_attn_norm_g, v_w_in, v_w_gate_up, v_b_gate_up, v_sb_out_g, v_gla_out_g, v_w_out, v_ffn_norm_g, v_w_ffn_up, v_conv_w, v_conv_b, v_w_ffn_down, v_final_norm_g):
    Bd, S, _ = x.shape
    T = Bd * S
    x2d = x.reshape(T, D)
    tgt = loss_target.reshape(T, D)
    c_up = w_ffn_up.shape[2]
    c_gu = w_gate_up.shape[2]
    c_in = w_in.shape[2]

    n_gu = GATE_RANK * c_gu
    rows_bf = lambda w: w[0].T.astype(BF16)
    small_w = lambda wgu, cw: _flat_pad_rows([wgu, cw], SHARD_SMALL_ROWS)

    (g_in,) = _exchange([rows_bf(w_in)], True, "gather_w_in")
    w_in_pt = jnp.pad(g_in.reshape(IN_COLS, D), ((0, 1), (0, 0)))[_PERM]
    g3 = final_norm_g.reshape(1, D)

    proj, h1 = _norm_proj(x2d, attn_norm_g, w_in_pt)
    o_sb, tt, g_up, g_down, g_out, gs = _sb_fwd(
        proj, S, [rows_bf(w_ffn_up), w_ffn_down[0].astype(BF16), w_out[0].astype(BF16),
                  _flat_pad_rows([w_gate_up, conv_w], 8)])
    w_out_f = g_out.reshape(D, D)
    gsf = gs.reshape(N_DEV, -1)
    wgu_f = jnp.transpose(gsf[:, :n_gu].reshape(N_DEV, GATE_RANK, c_gu), (1, 0, 2)).reshape(GATE_RANK, GLA_KW)
    cw_f = jnp.transpose(gsf[:, n_gu:n_gu + 3 * c_up].reshape(N_DEV, 3, c_up), (1, 0, 2)).reshape(3, 2 * D_FF)
    wgu_p = jnp.pad(wgu_f, ((0, LANE - GATE_RANK), (0, 0)))
    w_up_t = g_up.reshape(2 * D_FF, D)
    w_down_f = g_down.reshape(D_FF, D)
    o_gla = _gla_fwd(proj, wgu_p, b_gate_up, S)
    x1, ocat, h2 = _mix_out(o_sb, o_gla, proj, x2d, sb_out_g, gla_out_g, w_out_f, ffn_norm_g)
    hup = _mm(h2, w_up_t, "nt", "ffn_up", tm=512, tn=1408, tk=1024)
    act = _conv_gate(hup, cw_f, conv_b, S)
    dx2, dg3, loss_dev = _down_loss(act, w_down_f, x1, tgt, g3)

    dw_down = _mm(act, dx2, "tn", "dw_down", out_dtype=BF16, tm=D_FF, tn=1024, tk=512)
    dact = _mm(dx2, w_down_f, "nt", "dact", tm=512, tn=1408, tk=1024)
    dhup_a, dhup_v, dcw_a, dcw_v, dcb_a, dcb_v = _conv_gate_bwd(hup, dact, cw_f, conv_b, S)
    dw_up_t = _mm(dhup_a, h2, "tn", "dw_up_a", out_dtype=BF16, tm=D_FF, tn=1024, tk=512, out_rows=2 * D_FF)
    dw_up_t = _mm(dhup_v, h2, "tn", "dw_up_v", out_dtype=BF16, tm=D_FF, tn=1024, tk=512, out_rows=2 * D_FF,
                  out_row0=D_FF, into=dw_up_t)
    dh2 = _mm(dhup_a, w_up_t, "nn", "dh2_a", tm=1024, tn=1024, tk=1408)
    dh2 = _mm(dhup_v, w_up_t, "nn", "dh2_v", c=dh2, tm=1024, tn=1024, tk=1408, b_row0=D_FF)
    dx1, dg2 = _rms_bwd(x1, ffn_norm_g, dh2, dx2, "ffn_norm_bwd")

    dw_out = _mm(ocat, dx1, "tn", "dw_out", out_dtype=BF16, tm=1024, tn=1024, tk=512)
    docat = _mm(dx1, w_out_f, "nt", "docat", tm=512, tn=1024, tk=1024)
    do_sb, do_gla, dproj, dg_sb, dg_gla = _mix_bwd(docat, o_sb, o_gla, proj, sb_out_g, gla_out_g)
    dproj, got_up, got_down, got_out = _sb_bwd(
        proj, tt, do_sb, dproj, S,
        [dw_up_t.reshape(N_DEV, c_up, D), dw_down.reshape(N_DEV, -1, D), dw_out.reshape(N_DEV, -1, D)])
    dproj, dpre = _gla_bwd(proj, wgu_p, b_gate_up, do_gla, dproj, S)
    dproj, dwgu, dbgu = _gate_bwd(dpre, proj, wgu_p, dproj)
    dw_in_pt = _mm(dproj, h1, "tn", "dw_in", out_dtype=BF16, tm=PROJ_W, tn=1024, tk=512)
    dh1, got_in = _mm(dproj, w_in_pt, "nn", "dh1", tm=1024, tn=1024, tk=640,
                      xchg=([dw_in_pt[_INV_PERM].reshape(N_DEV, c_in, D)], False))
    dx, dg1 = _rms_bwd(x2d, attn_norm_g, dh1, dx1, "attn_norm_bwd")

    dcw = jnp.concatenate([dcw_a, dcw_v], axis=1)
    dwgu_pc = jnp.transpose(dwgu[:GATE_RANK].reshape(GATE_RANK, N_DEV, c_gu), (1, 0, 2)).reshape(N_DEV, -1)
    dcw_pc = jnp.transpose(dcw.reshape(3, N_DEV, c_up), (1, 0, 2)).reshape(N_DEV, -1)
    small_pc = jnp.concatenate([dwgu_pc, dcw_pc], axis=1)
    small_pc = jnp.pad(small_pc, ((0, 0), (0, SHARD_SMALL_ROWS * D - small_pc.shape[1])))
    small_pc = small_pc.reshape(N_DEV, SHARD_SMALL_ROWS, D).astype(BF16)
    rep_names = ["attn_norm_g", "b_gate_up", "sb_out_g", "gla_out_g", "ffn_norm_g", "conv_b", "final_norm_g"]
    rep_g = [dg1, dbgu, dg_sb, dg_gla, dg2, jnp.concatenate([dcb_a, dcb_v], axis=1), dg3]
    rep_w = [attn_norm_g, b_gate_up, sb_out_g, gla_out_g, ffn_norm_g, conv_b, final_norm_g]
    rep_m = [m_attn_norm_g, m_b_gate_up, m_sb_out_g, m_gla_out_g, m_ffn_norm_g, m_conv_b, m_final_norm_g]
    rep_v = [v_attn_norm_g, v_b_gate_up, v_sb_out_g, v_gla_out_g, v_ffn_norm_g, v_conv_b, v_final_norm_g]
    rep_pc = jnp.broadcast_to(_flat_pad_rows(rep_g, SMALL_ROWS), (N_DEV, SMALL_ROWS, D))
    got_sm, got_rep = _exchange([small_pc, rep_pc], False, "scatter_tail")

    rows = lambda w: w[0].T
    cols = lambda r: r.T[None]
    res = {}
    res["w_in"] = [cols(r) for r in _sum_adamw(got_in, rows(w_in), rows(m_w_in), rows(v_w_in), c_in, "adamw_w_in")]
    res["w_out"] = [r[None] for r in _sum_adamw(got_out, w_out[0], m_w_out[0], v_w_out[0], w_out.shape[1],
                                                 "adamw_w_out")]
    res["w_ffn_up"] = [cols(r) for r in _sum_adamw(got_up, rows(w_ffn_up), rows(m_w_ffn_up), rows(v_w_ffn_up),
                                                    c_up // 2, "adamw_w_up")]
    res["w_ffn_down"] = [r[None] for r in _sum_adamw(got_down, w_ffn_down[0], m_w_ffn_down[0], v_w_ffn_down[0],
                                                      w_ffn_down.shape[1], "adamw_w_down")]
    sm = _sum_adamw(got_sm, small_w(w_gate_up, conv_w), small_w(m_w_gate_up, m_conv_w),
                    small_w(v_w_gate_up, v_conv_w), SHARD_SMALL_ROWS, "adamw_small_sharded")
    res["w_gate_up"] = [r.reshape(-1)[:n_gu].reshape(1, GATE_RANK, c_gu) for r in sm]
    res["conv_w"] = [r.reshape(-1)[n_gu:n_gu + 3 * c_up].reshape(1, 3, c_up) for r in sm]
    rep = _sum_adamw(got_rep, _flat_pad_rows(rep_w, SMALL_ROWS), _flat_pad_rows(rep_m, SMALL_ROWS),
                     _flat_pad_rows(rep_v, SMALL_ROWS), SMALL_ROWS, "adamw_replicated")
    o = 0
    for n, w in zip(rep_names, rep_w):
        res[n] = [r.reshape(-1)[o:o + w.size].reshape(w.shape) for r in rep]
        o += w.size

    loss = lax.psum(loss_dev[0, 0], ("x", "y", "c"))
    order = ["attn_norm_g", "w_in", "w_gate_up", "b_gate_up", "sb_out_g", "gla_out_g", "w_out", "ffn_norm_g",
             "w_ffn_up", "conv_w", "conv_b", "w_ffn_down", "final_norm_g"]
    outs = [loss, dx.reshape(Bd, S, D)]
    for k in range(4):
        outs += [res[n][k] for n in order]
    return tuple(outs)
```

```python
import functools

import numpy as np
import jax
import jax.numpy as jnp
from jax import lax
from jax.experimental import pallas as pl
from jax.experimental.pallas import tpu as pltpu

F32 = jnp.float32
BF16 = jnp.bfloat16
SDS = jax.ShapeDtypeStruct
BS = pl.BlockSpec

N_DEV = 8
D = 1024
EPS = 1e-6
SB_HD = 64
SB_W = 512
GLA_DK = 64
GLA_DV = 128
GLA_KW = 256
GLA_W = 512
GATE_RANK = 16
GATE_NORM = 16.0
CHUNK = 64
GLA_UNROLL = 4
QT = 256
D_FF = 2816
IN_COLS = 3088
PROJ_W = 3200
LANE = 128
VMEM_LIMIT = 56 * 1024 * 1024

ADAM_LR, ADAM_B1, ADAM_B2, ADAM_EPS, ADAM_WD, ADAM_STEP = 0.001, 0.9, 0.999, 1e-08, 0.01, 10


def _proj_perm():
    sbq, sbk, sbv = 0, 512, 1024
    gq, gk, gv, glr, gog = 1536, 1792, 2048, 2560, 2576
    cols = []
    for p in range(4):
        for base in (sbq, sbk, sbv):
            cols += list(range(base + 128 * p, base + 128 * p + 128))
    for p in range(2):
        cols += list(range(gq + 128 * p, gq + 128 * p + 128))
        cols += list(range(gk + 128 * p, gk + 128 * p + 128))
        cols += list(range(gv + 256 * p, gv + 256 * p + 256))
    cols += list(range(gog, gog + 512))
    cols += list(range(glr, glr + GATE_RANK)) + [IN_COLS] * (LANE - GATE_RANK)
    perm = np.asarray(cols, np.int32)
    inv = np.zeros((IN_COLS,), np.int32)
    for new, old in enumerate(cols):
        if old < IN_COLS:
            inv[old] = new
    return perm, inv


_PERM, _INV_PERM = _proj_perm()
OG_BLK = 5
GLR_BLK = 24


def _cp(sem=None, vmem=VMEM_LIMIT):
    return pltpu.CompilerParams(dimension_semantics=sem, vmem_limit_bytes=vmem)


def _dot(a, b):
    return lax.dot_general(a, b, (((1,), (0,)), ((), ())), preferred_element_type=F32)


def _dot_nt(a, b):
    return lax.dot_general(a, b, (((1,), (1,)), ((), ())), preferred_element_type=F32)


def _dot_tn(a, b):
    return lax.dot_general(a, b, (((0,), (0,)), ((), ())), preferred_element_type=F32)


def _bf(x):
    return x.astype(BF16)


def _split_dot(x, m, passes, left=True):
    acc = None
    r = x
    for i in range(passes):
        h = r.astype(BF16)
        t = _dot(h, m) if left else _dot(m, h)
        acc = t if acc is None else acc + t
        if i + 1 < passes:
            r = r - h.astype(F32)
    return acc


def _softplus(z):
    return jnp.maximum(z, 0.0) + jnp.log(1.0 + jnp.exp(-jnp.abs(z)))


def _rms_bwd_math(x, g, dh, dres):
    r = lax.rsqrt(jnp.mean(x * x, axis=-1, keepdims=True) + EPS)
    xh = x * r
    dxh = dh * g
    dx = dres + r * (dxh - xh * jnp.mean(dxh * xh, axis=-1, keepdims=True))
    return dx, jnp.sum(dh * xh, axis=0, keepdims=True)


def _tile(n, pref, mult=LANE):
    best = None
    for t in range(mult, min(n, pref) + 1, mult):
        if n % t == 0:
            best = t
    return best if best is not None else n


def _mm(a, b, mode, name, out_dtype=F32, c=None, tm=512, tn=512, tk=512, b_row0=0, out_rows=None, out_row0=0,
        into=None, xchg=None, norm_bwd=None):
    if mode == "nn":
        (M, K), N = a.shape, b.shape[1]
    elif mode == "nt":
        (M, K), N = a.shape, b.shape[0]
    else:
        (K, M), N = a.shape, b.shape[1]
    tm, tn, tk = _tile(M, tm), _tile(N, tn), _tile(K, tk)
    nk = K // tk
    kb0, ob0 = b_row0 // tk, out_row0 // tm
    assert kb0 * tk == b_row0 and ob0 * tm == out_row0 and (mode == "nn" or b_row0 == 0)
    ni, nj = M // tm, N // tn
    j_outer = nk == 1 and (nj - 1) * a.size * a.dtype.itemsize < (ni - 1) * K * N * b.dtype.itemsize
    ix = (lambda f: (lambda j, i, k: f(i, j, k))) if j_outer else (lambda f: f)
    a_spec = BS((tk, tm), ix(lambda i, j, k: (k, i))) if mode == "tn" else BS((tm, tk), ix(lambda i, j, k: (i, k)))
    b_spec = (BS((tn, tk), ix(lambda i, j, k: (j, k))) if mode == "nt"
              else BS((tk, tn), ix(lambda i, j, k: (k + kb0, j))))
    dotfn = {"nn": _dot, "nt": _dot_nt, "tn": _dot_tn}[mode]
    has_c = c is not None
    has_into = into is not None
    nx = 0 if xchg is None else len(xchg[0])
    has_nb = norm_bwd is not None
    assert not has_nb or (nj == 1 and not j_outer and out_dtype == F32)
    n_in = 2 + has_c + has_into + 3 * has_nb

    def body(*refs):
        a_ref, b_ref = refs[:2]
        c_ref = refs[2] if has_c else None
        x_src = refs[n_in:n_in + nx]
        outs = refs[n_in + nx:n_in + 2 * nx + 1 + has_nb]
        o_ref, x_out = outs[0], outs[1 + has_nb:]
        acc = refs[n_in + 2 * nx + 1 + has_nb]
        sems = refs[n_in + 2 * nx + 2 + has_nb:]
        k = pl.program_id(2)
        g0, g1 = pl.program_id(0), pl.program_id(1)
        n0, n1 = (nj, ni) if j_outer else (ni, nj)
        first = jnp.logical_and(jnp.logical_and(g0 == 0, g1 == 0), k == 0)
        if nx:
            @pl.when(first)
            def _():
                _exchange_ops(x_src, x_out, xchg[1], sems, "start")

        if has_nb:
            @pl.when(first)
            def _():
                outs[1][...] = jnp.zeros_like(outs[1])

        @pl.when(k == 0)
        def _():
            acc[...] = jnp.zeros_like(acc)

        acc[...] += dotfn(_bf(a_ref[...]), _bf(b_ref[...]))

        @pl.when(k == nk - 1)
        def _():
            r = acc[...]
            if has_c:
                r = r + c_ref[...]
            if has_nb:
                x_ref, g_ref, dres_ref = refs[n_in - 3:n_in]
                dx, dg = _rms_bwd_math(x_ref[...], g_ref[...], r, dres_ref[...])
                o_ref[...] = dx
                outs[1][...] += dg
            else:
                o_ref[...] = r.astype(out_dtype)

        if nx:
            @pl.when(jnp.logical_and(jnp.logical_and(g0 == n0 - 1, g1 == n1 - 1), k == nk - 1))
            def _():
                _exchange_ops(x_src, x_out, xchg[1], sems, "wait")

    tile = BS((tm, tn), ix(lambda i, j, k: (i, j)))
    in_specs = [a_spec, b_spec]
    args = [a, b]
    if has_c:
        in_specs.append(tile)
        args.append(c)
    aliases = {}
    if has_into:
        aliases = {len(args): 0}
        in_specs.append(BS(memory_space=pl.ANY))
        args.append(into)
    out_shape = [SDS((out_rows or M, N), out_dtype)]
    out_specs = [BS((tm, tn), ix(lambda i, j, k: (i + ob0, j)))]
    scratch = [pltpu.VMEM((tm, tn), F32)]
    if has_nb:
        in_specs += [tile, BS((1, tn), lambda i, j, k: (0, 0)), tile]
        args += list(norm_bwd)
        out_shape.append(SDS((1, N), F32))
        out_specs.append(BS((1, tn), lambda i, j, k: (0, 0)))
    if nx:
        hbm = BS(memory_space=pltpu.HBM)
        in_specs += [hbm] * nx
        args += list(xchg[0])
        out_shape += [SDS((N_DEV,) + s.shape if xchg[1] else s.shape, s.dtype) for s in xchg[0]]
        out_specs += [hbm] * nx
        scratch += _exchange_sems(nx)
    serial = nx or has_nb
    res = pl.pallas_call(
        body, name=name, out_shape=tuple(out_shape), grid=(nj, ni, nk) if j_outer else (ni, nj, nk),
        in_specs=in_specs, out_specs=tuple(out_specs),
        scratch_shapes=scratch, input_output_aliases=aliases,
        compiler_params=_cp(("arbitrary",) * 3 if serial else ("parallel", "parallel", "arbitrary")),
    )(*args)
    return res if serial else res[0]


def _norm_proj(x, g, w):
    T, N = x.shape[0], w.shape[0]
    tm = _tile(T, 256)

    def body(x_ref, g_ref, w_ref, p_ref, h_ref):
        xv = x_ref[...]
        r = lax.rsqrt(jnp.mean(xv * xv, axis=-1, keepdims=True) + EPS)
        h = _bf((xv * r) * g_ref[...])
        h_ref[...] = h
        p_ref[...] = _dot_nt(h, w_ref[...])

    return pl.pallas_call(
        body, name="norm_proj", out_shape=(SDS((T, N), F32), SDS((T, D), BF16)), grid=(T // tm,),
        in_specs=[BS((tm, D), lambda i: (i, 0)), BS((1, D), lambda i: (0, 0)), BS((N, D), lambda i: (0, 0))],
        out_specs=(BS((tm, N), lambda i: (i, 0)), BS((tm, D), lambda i: (i, 0))),
        compiler_params=_cp(("parallel",)),
    )(x, g, w)


TK = 256
SB_DEAD = -104.0
SB_MASKED = -1e30
CNT_LANE = SB_HD - 1


def _sb_masks():
    row = lax.broadcasted_iota(jnp.int32, (2 * QT, TK), 0) & (QT - 1)
    col = lax.broadcasted_iota(jnp.int32, (2 * QT, TK), 1)
    lane = lax.broadcasted_iota(jnp.int32, (1, LANE), 1)
    kr = lax.broadcasted_iota(jnp.int32, (TK, TK), 0)
    kc = lax.broadcasted_iota(jnp.int32, (TK, TK), 1)
    return row, col, lane, kr, kc


def _stack_heads(x, lane):
    return jnp.concatenate([_bf(jnp.where((lane // SB_HD) == hh, x, 0.0)) for hh in range(2)], axis=0)


def _sb_fwd(proj, S, shards):
    T = proj.shape[0]
    nq = S // QT
    scale = SB_HD ** -0.5
    nb, ns = T // S, len(shards)

    def body(qkv_ref, *rest):
        sh_refs, (o_ref, tt_ref), g_refs = rest[:ns], rest[ns:ns + 2], rest[ns + 2:2 * ns + 2]
        sems = rest[2 * ns + 2:]
        first = jnp.logical_and(pl.program_id(0) == 0, pl.program_id(1) == 0)
        last = jnp.logical_and(pl.program_id(0) == nb - 1, pl.program_id(1) == 3)

        @pl.when(first)
        def _():
            _exchange_ops(sh_refs, g_refs, True, sems, "start")

        row, col, lane, kr, kc = _sb_masks()
        msuf = _bf(kr > kc)

        def qloop(qt, _):
            r0 = pl.multiple_of(qt * QT, QT)
            qs = _stack_heads(qkv_ref[pl.ds(r0, QT), 0:128] * scale, lane)

            def live(st):
                it, _, cy = st
                return jnp.logical_and(it <= qt, jnp.max(cy) > SB_DEAD)

            def step(st):
                it, acc, cy = st
                kt = qt - it
                k0 = pl.multiple_of(kt * TK, TK)
                kv = _bf(qkv_ref[pl.ds(k0, TK), 128:256])
                vv = _bf(qkv_ref[pl.ds(k0, TK), 256:384])
                strict = (col + (kt - qt) * TK) < row
                z = jnp.where(strict, _dot_nt(qs, kv), SB_MASKED)
                sp = _softplus(z)
                lg = -sp
                after = cy + _split_dot(lg, msuf, 2)
                w = jnp.exp((z - sp) + after)
                return it + 1, acc + _dot(_bf(w), vv), cy + jnp.sum(lg, axis=1, keepdims=True)

            it, acc, cy = lax.while_loop(
                live, step, (jnp.int32(0), jnp.zeros((2 * QT, LANE), F32), jnp.zeros((2 * QT, 1), F32)))
            o_ref[pl.ds(r0, QT), :] = jnp.where(lane < SB_HD, acc[:QT], acc[QT:])
            tt = jnp.where(lane < SB_HD, cy[:QT], cy[QT:])
            tt_ref[pl.ds(r0, QT), :] = jnp.where(lane == CNT_LANE, it.astype(F32), tt)
            return 0

        lax.fori_loop(0, nq, qloop, 0)

        @pl.when(last)
        def _():
            _exchange_ops(sh_refs, g_refs, True, sems, "wait")

    hbm = BS(memory_space=pltpu.HBM)
    col_spec = BS((S, LANE), lambda b, p: (b, p))
    return pl.pallas_call(
        body, name="sb_fwd",
        out_shape=(SDS((T, SB_W), F32), SDS((T, SB_W), F32)) + tuple(SDS((N_DEV,) + s.shape, s.dtype) for s in shards),
        grid=(nb, 4),
        in_specs=[BS((S, 384), lambda b, p: (b, p))] + [hbm] * ns,
        out_specs=(col_spec, col_spec) + (hbm,) * ns,
        scratch_shapes=_exchange_sems(ns),
        compiler_params=_cp(("arbitrary", "arbitrary")),
    )(proj, *shards)


def _log_sigmoid(x):
    return jnp.minimum(x, 0.0) - jnp.log1p(jnp.exp(-jnp.abs(x)))


def _loop_unrolled(n, step, init):
    assert n % GLA_UNROLL == 0

    def body(i, st):
        for u in range(GLA_UNROLL):
            st = step(i * GLA_UNROLL + u, st)
        return st

    return lax.fori_loop(0, n // GLA_UNROLL, body, init)


def _gla_chunk_terms(blk_ref, glr_ref, wgu, bgu, r0, tri_incl):
    q = blk_ref[pl.ds(r0, CHUNK), 0:128]
    k = blk_ref[pl.ds(r0, CHUNK), 128:256]
    v = blk_ref[pl.ds(r0, CHUNK), 256:512]
    pre = _dot(_bf(glr_ref[pl.ds(r0, CHUNK), :]), wgu) + bgu
    la = _log_sigmoid(pre) / GATE_NORM
    b = _split_dot(la, tri_incl, 3, left=False)
    b_last = b[CHUNK - 1:CHUNK, :]
    eb = jnp.exp(b)
    qd = (q * (GLA_DK ** -0.5)) * eb
    ki = k * jnp.exp(-b)
    ke = k * jnp.exp(b_last - b)
    decay = jnp.exp(b_last)
    return q, k, v, pre, b, b_last, eb, qd, ki, ke, decay


def _gla_fwd(proj, wgu, bgu, S):
    T = proj.shape[0]
    nc = S // CHUNK

    def body(blk_ref, glr_ref, wgu_ref, bgu_ref, o_ref):
        rr = lax.broadcasted_iota(jnp.int32, (CHUNK, CHUNK), 0)
        cc = lax.broadcasted_iota(jnp.int32, (CHUNK, CHUNK), 1)
        lane = lax.broadcasted_iota(jnp.int32, (1, LANE), 1)
        causal = rr >= cc
        tri_incl = _bf(causal)
        wg = _bf(wgu_ref[...])
        bg = bgu_ref[...]

        def chunk(n, states):
            r0 = pl.multiple_of(n * CHUNK, CHUNK)
            _, _, v, _, _, _, _, qd, ki, ke, decay = _gla_chunk_terms(blk_ref, glr_ref, wg, bg, r0, tri_incl)
            new_states, outs = [], []
            for hh in range(2):
                hm = (lane // GLA_DK) == hh
                qm = _bf(jnp.where(hm, qd, 0.0))
                vh = _bf(v[:, 128 * hh:128 * hh + 128])
                st = states[hh]
                attn = jnp.where(causal, _dot_nt(qm, _bf(ki)), 0.0)
                outs.append(_dot(_bf(attn), vh) + _dot_nt(qm, _bf(st)))
                new_states.append(st * decay + _dot_tn(vh, _bf(ke)))
            o_ref[pl.ds(r0, CHUNK), :] = jnp.concatenate(outs, axis=1)
            return tuple(new_states)

        z = jnp.zeros((GLA_DV, LANE), F32)
        _loop_unrolled(nc, chunk, (z, z))

    return pl.pallas_call(
        body, name="gla_fwd", out_shape=SDS((T, GLA_W), F32), grid=(T // S, 2),
        in_specs=[BS((S, 512), lambda b, p: (b, 3 + p)), BS((S, LANE), lambda b, p: (b, GLR_BLK)),
                  BS((LANE, LANE), lambda b, p: (0, p)), BS((1, LANE), lambda b, p: (0, p))],
        out_specs=BS((S, 256), lambda b, p: (b, p)),
        compiler_params=_cp(("parallel", "parallel")),
    )(proj, proj, wgu, bgu)


def _head_blockdiag(width, hd):
    r = lax.broadcasted_iota(jnp.int32, (width, width), 0) // hd
    c = lax.broadcasted_iota(jnp.int32, (width, width), 1) // hd
    return _bf(r == c)


def _mix_out(o_sb, o_gla, proj, x, g_sb, g_gla, w_out, g2):
    T = x.shape[0]
    tm = _tile(T, 256)

    def body(osb_ref, ogl_ref, og_ref, x_ref, gsb_ref, ggl_ref, w_ref, g2_ref, x1_ref, oc_ref, h2_ref):
        bd64 = _head_blockdiag(SB_W, SB_HD)
        bd128 = _head_blockdiag(GLA_W, GLA_DV)
        o = osb_ref[...]
        r = lax.rsqrt(_split_dot(o * o, bd64, 2) * (1.0 / SB_HD) + EPS)
        c_sb = (o * r) * gsb_ref[...]
        o = ogl_ref[...]
        r = lax.rsqrt(_split_dot(o * o, bd128, 2) * (1.0 / GLA_DV) + EPS)
        og = og_ref[...]
        c_gl = ((o * r) * ggl_ref[...]) * (og * jax.nn.sigmoid(og))
        oc = _bf(jnp.concatenate([c_sb, c_gl], axis=1))
        oc_ref[...] = oc
        x1 = x_ref[...] + _dot(oc, w_ref[...])
        x1_ref[...] = x1
        r2 = lax.rsqrt(jnp.mean(x1 * x1, axis=-1, keepdims=True) + EPS)
        h2_ref[...] = _bf((x1 * r2) * g2_ref[...])

    row = lambda w: BS((tm, w), lambda i: (i, 0))
    vec = lambda w: BS((1, w), lambda i: (0, 0))
    return pl.pallas_call(
        body, name="mix_out", out_shape=(SDS((T, D), F32), SDS((T, D), BF16), SDS((T, D), BF16)), grid=(T // tm,),
        in_specs=[row(SB_W), row(GLA_W), BS((tm, 512), lambda i: (i, OG_BLK)), row(D), vec(SB_W), vec(GLA_W),
                  BS((D, D), lambda i: (0, 0)), vec(D)],
        out_specs=(row(D), row(D), row(D)),
        compiler_params=_cp(("parallel",)),
    )(o_sb, o_gla, proj, x, g_sb, g_gla, w_out, g2)


CONV_ROWS = 256
CONV_TC = 256


def _rows_before(ref, r0, first):
    prev = ref[pl.ds(pl.multiple_of(jnp.maximum(r0 - 8, 0), 8), 8), :]
    return jnp.where(first, 0.0, prev)


def _shift_down(cur, prev8, k):
    cat = jnp.concatenate([prev8, cur], axis=0)
    return pltpu.roll(cat, k, 0)[8:]


def _shift_up(cur, next8, k):
    cat = jnp.concatenate([cur, next8], axis=0)
    return pltpu.roll(cat, cat.shape[0] - k, 0)[:cur.shape[0]]


def _conv_at(h_ref, cw, cb, r0, rows, first):
    cur = h_ref[pl.ds(r0, rows), :]
    prev8 = _rows_before(h_ref, r0, first)
    u = cb + cw[0:1, :] * _shift_down(cur, prev8, 2)
    u = u + cw[1:2, :] * _shift_down(cur, prev8, 1)
    return u + cw[2:3, :] * cur


NJ = D_FF // CONV_TC


def _conv_gate(hup, cw, cb, S):
    T = hup.shape[0]
    rows = min(CONV_ROWS, S)
    nr = S // rows

    def body(ha_ref, hv_ref, cwa_ref, cwv_ref, cba_ref, cbv_ref, act_ref):
        cwa, cwv, cba, cbv = cwa_ref[...], cwv_ref[...], cba_ref[...], cbv_ref[...]

        def step(c, _):
            r0 = pl.multiple_of(c * rows, rows)
            ua = _conv_at(ha_ref, cwa, cba, r0, rows, c == 0)
            uv = _conv_at(hv_ref, cwv, cbv, r0, rows, c == 0)
            act_ref[pl.ds(r0, rows), :] = _bf((ua * jax.nn.sigmoid(ua)) * uv)
            return 0

        lax.fori_loop(0, nr, step, 0)

    blk = lambda o: BS((S, CONV_TC), lambda b, j: (b, j + o))
    w3 = lambda o: BS((3, CONV_TC), lambda b, j: (0, j + o))
    w1 = lambda o: BS((1, CONV_TC), lambda b, j: (0, j + o))
    return pl.pallas_call(
        body, name="conv_gate", out_shape=SDS((T, D_FF), BF16), grid=(T // S, NJ),
        in_specs=[blk(0), blk(NJ), w3(0), w3(NJ), w1(0), w1(NJ)], out_specs=blk(0),
        compiler_params=_cp(("parallel", "parallel")),
    )(hup, hup, cw, cw, cb, cb)


def _down_loss(act, w_down, x1, tgt, g3):
    T = x1.shape[0]
    tm = _tile(T, 256)

    def body(a_ref, w_ref, x1_ref, t_ref, g_ref, dx_ref, dg_ref, ls_ref):
        @pl.when(pl.program_id(0) == 0)
        def _():
            dg_ref[...] = jnp.zeros_like(dg_ref)
            ls_ref[...] = jnp.zeros_like(ls_ref)

        g = g_ref[...]
        x2 = x1_ref[...] + _dot(a_ref[...], w_ref[...])
        r = lax.rsqrt(jnp.mean(x2 * x2, axis=-1, keepdims=True) + EPS)
        xh = x2 * r
        e = xh * g - t_ref[...]
        ls_ref[...] += 0.5 * jnp.sum(jnp.mean(e * e, axis=-1, keepdims=True), axis=0, keepdims=True)
        dy = e * (1.0 / D)
        dxh = dy * g
        dx_ref[...] = r * (dxh - xh * jnp.mean(dxh * xh, axis=-1, keepdims=True))
        dg_ref[...] += jnp.sum(dy * xh, axis=0, keepdims=True)

    row = lambda w: BS((tm, w), lambda i: (i, 0))
    return pl.pallas_call(
        body, name="down_loss", out_shape=(SDS((T, D), F32), SDS((1, D), F32), SDS((1, LANE), F32)), grid=(T // tm,),
        in_specs=[row(D_FF), BS((D_FF, D), lambda i: (0, 0)), row(D), row(D), BS((1, D), lambda i: (0, 0))],
        out_specs=(row(D), BS((1, D), lambda i: (0, 0)), BS((1, LANE), lambda i: (0, 0))),
        compiler_params=_cp(("arbitrary",)),
    )(act, w_down, x1, tgt, g3)


def _conv_gate_bwd(hup, dact, cw, cb, S):
    T = hup.shape[0]
    rows = min(CONV_ROWS, S)
    nr = S // rows

    def body(ha_ref, hv_ref, da_ref, cwa_ref, cwv_ref, cba_ref, cbv_ref,
             dha_ref, dhv_ref, dcwa_ref, dcwv_ref, dcba_ref, dcbv_ref):
        @pl.when(pl.program_id(1) == 0)
        def _():
            for r in (dcwa_ref, dcwv_ref, dcba_ref, dcbv_ref):
                r[...] = jnp.zeros_like(r)

        cwa, cwv, cba, cbv = cwa_ref[...], cwv_ref[...], cba_ref[...], cbv_ref[...]

        def du_at(r0, n, first):
            ua = _conv_at(ha_ref, cwa, cba, r0, n, first)
            uv = _conv_at(hv_ref, cwv, cbv, r0, n, first)
            da = da_ref[pl.ds(r0, n), :]
            sg = jax.nn.sigmoid(ua)
            dua = (da * uv) * (sg * (1.0 + ua * (1.0 - sg)))
            duv = da * (ua * sg)
            return dua, duv

        def step(c, _):
            r0 = pl.multiple_of(c * rows, rows)
            first, last = c == 0, c == nr - 1
            dua, duv = du_at(r0, rows, first)
            n0 = pl.multiple_of(jnp.minimum(r0 + rows, S - 8), 8)
            nua, nuv = du_at(n0, 8, False)
            nua = jnp.where(last, 0.0, nua)
            nuv = jnp.where(last, 0.0, nuv)
            for (h_ref, cw, du, nu, dh_ref, dcw_ref, dcb_ref) in (
                    (ha_ref, cwa, dua, nua, dha_ref, dcwa_ref, dcba_ref),
                    (hv_ref, cwv, duv, nuv, dhv_ref, dcwv_ref, dcbv_ref)):
                dh = cw[2:3, :] * du + cw[1:2, :] * _shift_up(du, nu, 1) + cw[0:1, :] * _shift_up(du, nu, 2)
                dh_ref[pl.ds(r0, rows), :] = _bf(dh)
                cur = h_ref[pl.ds(r0, rows), :]
                prev8 = _rows_before(h_ref, r0, first)
                dcw_ref[0:1, :] += jnp.sum(du * _shift_down(cur, prev8, 2), axis=0, keepdims=True)
                dcw_ref[1:2, :] += jnp.sum(du * _shift_down(cur, prev8, 1), axis=0, keepdims=True)
                dcw_ref[2:3, :] += jnp.sum(du * cur, axis=0, keepdims=True)
                dcb_ref[...] += jnp.sum(du, axis=0, keepdims=True)
            return 0

        lax.fori_loop(0, nr, step, 0)

    blk = lambda o: BS((S, CONV_TC), lambda j, b: (b, j + o))
    w3 = lambda o: BS((3, CONV_TC), lambda j, b: (0, j + o))
    w1 = lambda o: BS((1, CONV_TC), lambda j, b: (0, j + o))
    return pl.pallas_call(
        body, name="conv_gate_bwd",
        out_shape=(SDS((T, D_FF), BF16), SDS((T, D_FF), BF16), SDS((3, D_FF), F32), SDS((3, D_FF), F32),
                   SDS((1, D_FF), F32), SDS((1, D_FF), F32)),
        grid=(NJ, T // S),
        in_specs=[blk(0), blk(NJ), blk(0), w3(0), w3(NJ), w1(0), w1(NJ)],
        out_specs=(blk(0), blk(0), w3(0), w3(0), w1(0), w1(0)),
        compiler_params=_cp(("parallel", "arbitrary")),
    )(hup, hup, dact, cw, cw, cb, cb)


def _mix_bwd(docat, o_sb, o_gla, proj, g_sb, g_gla):
    T = docat.shape[0]
    tm = _tile(T, 256)

    def body(d_ref, osb_ref, ogl_ref, og_ref, gsb_ref, ggl_ref, dsb_ref, dgl_ref, dog_ref, dgsb_ref, dggl_ref):
        @pl.when(pl.program_id(0) == 0)
        def _():
            dgsb_ref[...] = jnp.zeros_like(dgsb_ref)
            dggl_ref[...] = jnp.zeros_like(dggl_ref)

        bd64 = _head_blockdiag(SB_W, SB_HD)
        bd128 = _head_blockdiag(GLA_W, GLA_DV)
        d = d_ref[:, 0:SB_W]
        o = osb_ref[...]
        r = lax.rsqrt(_split_dot(o * o, bd64, 2) * (1.0 / SB_HD) + EPS)
        n = o * r
        dn = d * gsb_ref[...]
        dgsb_ref[...] += jnp.sum(d * n, axis=0, keepdims=True)
        dsb_ref[...] = r * (dn - n * (_split_dot(dn * n, bd64, 2) * (1.0 / SB_HD)))

        d = d_ref[:, SB_W:D]
        o = ogl_ref[...]
        r = lax.rsqrt(_split_dot(o * o, bd128, 2) * (1.0 / GLA_DV) + EPS)
        n = o * r
        og = og_ref[...]
        sg = jax.nn.sigmoid(og)
        dm = d * (og * sg)
        dog_ref[...] = _bf((d * (n * ggl_ref[...])) * (sg * (1.0 + og * (1.0 - sg))))
        dn = dm * ggl_ref[...]
        dggl_ref[...] += jnp.sum(dm * n, axis=0, keepdims=True)
        dgl_ref[...] = r * (dn - n * (_split_dot(dn * n, bd128, 2) * (1.0 / GLA_DV)))

    row = lambda w: BS((tm, w), lambda i: (i, 0))
    vec = lambda w: BS((1, w), lambda i: (0, 0))
    ogb = BS((tm, 512), lambda i: (i, OG_BLK))
    return pl.pallas_call(
        body, name="mix_bwd",
        out_shape=(SDS((T, SB_W), F32), SDS((T, GLA_W), F32), SDS((T, PROJ_W), BF16), SDS((1, SB_W), F32),
                   SDS((1, GLA_W), F32)),
        grid=(T // tm,),
        in_specs=[row(D), row(SB_W), row(GLA_W), ogb, vec(SB_W), vec(GLA_W)],
        out_specs=(row(SB_W), row(GLA_W), ogb, vec(SB_W), vec(GLA_W)),
        compiler_params=_cp(("arbitrary",)),
    )(docat, o_sb, o_gla, proj, g_sb, g_gla)


def _sb_bwd(proj, tt, do, dproj, S, pieces):
    T = proj.shape[0]
    nq = S // QT
    scale = SB_HD ** -0.5
    nb, ns = T // S, len(pieces)

    def body(qkv_ref, tt_ref, do_ref, dp_in_ref, *rest):
        del dp_in_ref
        pc_refs, dp_ref, got_refs = rest[:ns], rest[ns], rest[ns + 1:2 * ns + 1]
        dk_acc, dv_acc = rest[2 * ns + 1:2 * ns + 3]
        sems = rest[2 * ns + 3:]
        first = jnp.logical_and(pl.program_id(0) == 0, pl.program_id(1) == 0)
        last = jnp.logical_and(pl.program_id(0) == nb - 1, pl.program_id(1) == 3)

        @pl.when(first)
        def _():
            _exchange_ops(pc_refs, got_refs, False, sems, "start")

        row, col, lane, kr, kc = _sb_masks()
        mincl = _bf(kr <= kc)
        mexcl = _bf(kr < kc)
        dk_acc[...] = jnp.zeros_like(dk_acc)
        dv_acc[...] = jnp.zeros_like(dv_acc)

        def qloop(qt, _):
            r0 = pl.multiple_of(qt * QT, QT)
            qs = _stack_heads(qkv_ref[pl.ds(r0, QT), 0:128] * scale, lane)
            dos = _stack_heads(do_ref[pl.ds(r0, QT), :], lane)
            ttv = tt_ref[pl.ds(r0, QT), :]
            tot = jnp.concatenate([ttv[:, 0:1], ttv[:, SB_HD:SB_HD + 1]], axis=0)
            walked = jnp.max(ttv[:, CNT_LANE:CNT_LANE + 1]).astype(jnp.int32)

            def step(kt, st):
                dq, lc, pc = st
                k0 = pl.multiple_of(kt * TK, TK)
                kv = _bf(qkv_ref[pl.ds(k0, TK), 128:256])
                vv = _bf(qkv_ref[pl.ds(k0, TK), 256:384])
                strict = (col + (kt - qt) * TK) < row
                z = jnp.where(strict, _dot_nt(qs, kv), SB_MASKED)
                sp = _softplus(z)
                lg = -sp
                after = tot - (lc + _split_dot(lg, mincl, 2))
                gl = z - sp
                w = jnp.exp(gl + after)
                du = w * _dot_nt(dos, vv)
                beta = jnp.exp(gl)
                pex = pc + _split_dot(du, mexcl, 2)
                dz = _bf(du - beta * (du + pex))
                dk_acc[pl.ds(k0, TK), :] += _dot_tn(dz, qs)
                dv_acc[pl.ds(k0, TK), :] += _dot_tn(_bf(w), dos)
                return (dq + _dot(dz, kv), lc + jnp.sum(lg, axis=1, keepdims=True),
                        pc + jnp.sum(du, axis=1, keepdims=True))

            zc = jnp.zeros((2 * QT, 1), F32)
            dq, _, _ = lax.fori_loop(qt - walked + 1, qt + 1, step, (jnp.zeros((2 * QT, LANE), F32), zc, zc))
            dp_ref[pl.ds(r0, QT), 0:128] = _bf(jnp.where(lane < SB_HD, dq[:QT], dq[QT:]) * scale)
            return 0

        lax.fori_loop(0, nq, qloop, 0)
        dp_ref[:, 128:256] = _bf(dk_acc[...])
        dp_ref[:, 256:384] = _bf(dv_acc[...])

        @pl.when(last)
        def _():
            _exchange_ops(pc_refs, got_refs, False, sems, "wait")

    blk = BS((S, 384), lambda b, p: (b, p))
    col_spec = BS((S, LANE), lambda b, p: (b, p))
    hbm = BS(memory_space=pltpu.HBM)
    return pl.pallas_call(
        body, name="sb_bwd", out_shape=(SDS((T, PROJ_W), BF16),) + tuple(SDS(s.shape, s.dtype) for s in pieces),
        grid=(nb, 4),
        in_specs=[blk, col_spec, col_spec, BS(memory_space=pl.ANY)] + [hbm] * ns, out_specs=(blk,) + (hbm,) * ns,
        scratch_shapes=[pltpu.VMEM((S, LANE), F32), pltpu.VMEM((S, LANE), F32)] + _exchange_sems(ns),
        input_output_aliases={3: 0},
        compiler_params=_cp(("arbitrary", "arbitrary")),
    )(proj, tt, do, dproj, *pieces)


def _gla_bwd(proj, wgu, bgu, do, dproj, S):
    T = proj.shape[0]
    nc = S // CHUNK

    def body(blk_ref, glr_ref, wgu_ref, bgu_ref, do_ref, dp_in_ref, dp_ref, dpre_ref, st_ref):
        del dp_in_ref
        rr = lax.broadcasted_iota(jnp.int32, (CHUNK, CHUNK), 0)
        cc = lax.broadcasted_iota(jnp.int32, (CHUNK, CHUNK), 1)
        lane = lax.broadcasted_iota(jnp.int32, (1, LANE), 1)
        causal = rr >= cc
        tri_incl = _bf(causal)
        tri_rev = _bf(rr <= cc)
        wg = _bf(wgu_ref[...])
        bg = bgu_ref[...]

        def fwd_chunk(n, states):
            r0 = pl.multiple_of(n * CHUNK, CHUNK)
            _, _, v, _, _, _, _, _, _, ke, decay = _gla_chunk_terms(blk_ref, glr_ref, wg, bg, r0, tri_incl)
            new_states = []
            for hh in range(2):
                st_ref[hh, n] = states[hh]
                vh = _bf(v[:, 128 * hh:128 * hh + 128])
                new_states.append(states[hh] * decay + _dot_tn(vh, _bf(ke)))
            return tuple(new_states)

        z = jnp.zeros((GLA_DV, LANE), F32)
        _loop_unrolled(nc, fwd_chunk, (z, z))

        def bwd_chunk(it, dstates):
            n = nc - 1 - it
            r0 = pl.multiple_of(n * CHUNK, CHUNK)
            _, _, v, pre, b, b_last, eb, qd, ki, ke, decay = _gla_chunk_terms(
                blk_ref, glr_ref, wg, bg, r0, tri_incl)
            dqd = jnp.zeros((CHUNK, LANE), F32)
            dki = jnp.zeros((CHUNK, LANE), F32)
            dke = jnp.zeros((CHUNK, LANE), F32)
            ddec = jnp.zeros((1, LANE), F32)
            new_dstates, dvs = [], []
            for hh in range(2):
                hm = (lane // GLA_DK) == hh
                qm = _bf(jnp.where(hm, qd, 0.0))
                kem = _bf(jnp.where(hm, ke, 0.0))
                vh = _bf(v[:, 128 * hh:128 * hh + 128])
                doh = _bf(do_ref[pl.ds(r0, CHUNK), 128 * hh:128 * hh + 128])
                st = st_ref[hh, n]
                dst = dstates[hh]
                attn = _bf(jnp.where(causal, _dot_nt(qm, _bf(ki)), 0.0))
                dattn = _bf(jnp.where(causal, _dot_nt(doh, vh), 0.0))
                dvs.append(_dot_tn(attn, doh) + _dot_nt(kem, _bf(dst)))
                dqd = dqd + jnp.where(hm, _dot(dattn, _bf(ki)) + _dot(doh, _bf(st)), 0.0)
                dki = dki + _dot_tn(dattn, qm)
                dke = dke + jnp.where(hm, _dot(vh, _bf(dst)), 0.0)
                ddec = ddec + jnp.where(hm, jnp.sum(dst * st, axis=0, keepdims=True), 0.0)
                new_dstates.append(dst * decay + _dot_tn(doh, qm))
            einv = jnp.exp(-b)
            eend = jnp.exp(b_last - b)
            dq = (dqd * eb) * (GLA_DK ** -0.5)
            dk = dki * einv + dke * eend
            db = dqd * qd - dki * ki - dke * ke
            db_last = jnp.sum(dke * ke, axis=0, keepdims=True) + ddec * decay
            dla = _split_dot(db, tri_rev, 3, left=False) + db_last
            dpre_ref[pl.ds(r0, CHUNK), :] = (dla * (1.0 / GATE_NORM)) * (1.0 - jax.nn.sigmoid(pre))
            dp_ref[pl.ds(r0, CHUNK), 0:128] = _bf(dq)
            dp_ref[pl.ds(r0, CHUNK), 128:256] = _bf(dk)
            dp_ref[pl.ds(r0, CHUNK), 256:512] = _bf(jnp.concatenate(dvs, axis=1))
            return tuple(new_dstates)

        _loop_unrolled(nc, bwd_chunk, (z, z))

    return pl.pallas_call(
        body, name="gla_bwd", out_shape=(SDS((T, PROJ_W), BF16), SDS((T, GLA_KW), F32)), grid=(T // S, 2),
        in_specs=[BS((S, 512), lambda b, p: (b, 3 + p)), BS((S, LANE), lambda b, p: (b, GLR_BLK)),
                  BS((LANE, LANE), lambda b, p: (0, p)), BS((1, LANE), lambda b, p: (0, p)),
                  BS((S, 256), lambda b, p: (b, p)), BS(memory_space=pl.ANY)],
        out_specs=(BS((S, 512), lambda b, p: (b, 3 + p)), BS((S, LANE), lambda b, p: (b, p))),
        scratch_shapes=[pltpu.VMEM((2, nc, GLA_DV, LANE), F32)],
        input_output_aliases={5: 0},
        compiler_params=_cp(("parallel", "parallel")),
    )(proj, proj, wgu, bgu, do, dproj)


def _gate_bwd(dpre, proj, wgu, dproj):
    T = dpre.shape[0]
    tm = _tile(T, 512)

    def body(dpre_ref, glr_ref, wgu_ref, dp_in_ref, dp_ref, dw_ref, db_ref):
        del dp_in_ref

        @pl.when(pl.program_id(0) == 0)
        def _():
            dw_ref[...] = jnp.zeros_like(dw_ref)
            db_ref[...] = jnp.zeros_like(db_ref)

        dpre = dpre_ref[...]
        dp_ref[...] = _bf(_dot_nt(_bf(dpre), _bf(wgu_ref[...])))
        dw_ref[...] += _dot_tn(_bf(glr_ref[...]), _bf(dpre))
        db_ref[...] += jnp.sum(dpre, axis=0, keepdims=True)

    glr = BS((tm, LANE), lambda i: (i, GLR_BLK))
    return pl.pallas_call(
        body, name="gate_bwd",
        out_shape=(SDS((T, PROJ_W), BF16), SDS((LANE, GLA_KW), F32), SDS((1, GLA_KW), F32)), grid=(T // tm,),
        in_specs=[BS((tm, GLA_KW), lambda i: (i, 0)), glr, BS((LANE, GLA_KW), lambda i: (0, 0)),
                  BS(memory_space=pl.ANY)],
        out_specs=(glr, BS((LANE, GLA_KW), lambda i: (0, 0)), BS((1, GLA_KW), lambda i: (0, 0))),
        input_output_aliases={3: 0},
        compiler_params=_cp(("arbitrary",)),
    )(dpre, proj, wgu, dproj)


def _exchange_sems(n):
    return [pltpu.SemaphoreType.DMA((n * (N_DEV - 1),)), pltpu.SemaphoreType.DMA((n * (N_DEV - 1),)),
            pltpu.SemaphoreType.DMA((n,))]


def _exchange_ops(srcs, outs, gather, sems, act):
    ssem, rsem, lsem = sems
    x, y, c = lax.axis_index("x"), lax.axis_index("y"), lax.axis_index("c")
    me = 4 * x + 2 * y + c
    for i, (s_ref, o_ref) in enumerate(zip(srcs, outs)):
        for k in range(1, N_DEV):
            px = (x + ((k >> 2) & 1)) % 2
            py = (y + ((k >> 1) & 1)) % 2
            pc = (c + (k & 1)) % 2
            peer = 4 * px + 2 * py + pc
            n = i * (N_DEV - 1) + k - 1
            out = pltpu.make_async_remote_copy(
                src_ref=s_ref if gather else s_ref.at[peer], dst_ref=o_ref.at[me],
                send_sem=ssem.at[n], recv_sem=rsem.at[n],
                device_id=(px, py, pc), device_id_type=pl.DeviceIdType.MESH)
            if act == "start":
                out.start()
            else:
                out.wait_send()
                pltpu.make_async_remote_copy(
                    src_ref=s_ref if gather else s_ref.at[me], dst_ref=o_ref.at[peer],
                    send_sem=ssem.at[n], recv_sem=rsem.at[n],
                    device_id=(x, y, c), device_id_type=pl.DeviceIdType.MESH).wait_recv()
        mine = pltpu.make_async_copy(s_ref if gather else s_ref.at[me], o_ref.at[me], lsem.at[i])
        if act == "start":
            mine.start()
        else:
            mine.wait()


def _exchange(srcs, gather, name):
    n = len(srcs)
    shapes = [SDS((N_DEV,) + s.shape if gather else s.shape, s.dtype) for s in srcs]

    def body(*refs):
        s_refs, o_refs, sems = refs[:n], refs[n:2 * n], refs[2 * n:]
        _exchange_ops(s_refs, o_refs, gather, sems, "start")
        _exchange_ops(s_refs, o_refs, gather, sems, "wait")

    hbm = BS(memory_space=pltpu.HBM)
    return pl.pallas_call(
        body, name=name, out_shape=tuple(shapes), in_specs=[hbm] * n, out_specs=(hbm,) * n,
        scratch_shapes=_exchange_sems(n),
    )(*srcs)


def _adamw_math(w, g, m, v):
    m = ADAM_B1 * m + (1.0 - ADAM_B1) * g
    v = ADAM_B2 * v + (1.0 - ADAM_B2) * (g * g)
    m_hat = m / (1.0 - ADAM_B1 ** ADAM_STEP)
    v_hat = v / (1.0 - ADAM_B2 ** ADAM_STEP)
    delta = -ADAM_LR * (m_hat / (jnp.sqrt(v_hat) + ADAM_EPS) + ADAM_WD * w)
    return delta, m, v


def _sum_adamw(parts, w, m, v, tr, name):
    R, C = w.shape

    def body(p_ref, w_ref, m_ref, v_ref, g_ref, d_ref, nm_ref, nv_ref):
        g = p_ref[0].astype(F32)
        for d in range(1, N_DEV):
            g = g + p_ref[d].astype(F32)
        delta, nm, nv = _adamw_math(w_ref[...], g, m_ref[...], v_ref[...])
        g_ref[...] = g
        d_ref[...] = delta
        nm_ref[...] = nm
        nv_ref[...] = nv

    blk = BS((tr, C), lambda i: (i, 0))
    out = SDS((R, C), F32)
    return pl.pallas_call(
        body, name=name, out_shape=(out, out, out, out), grid=(R // tr,),
        in_specs=[BS((N_DEV, tr, C), lambda i: (0, i, 0)), blk, blk, blk], out_specs=(blk, blk, blk, blk),
        compiler_params=_cp(("parallel",)),
    )(parts, w, m, v)


def _flat_pad_rows(parts, rows):
    flat = jnp.concatenate([p.reshape(-1) for p in parts])
    return jnp.pad(flat, (0, rows * D - flat.shape[0])).reshape(rows, D)


SMALL_ROWS = 16
SHARD_SMALL_ROWS = 3


def kernel(x, attn_norm_g, w_in, w_gate_up, b_gate_up, sb_out_g, gla_out_g, w_out, ffn_norm_g, w_ffn_up, conv_w, conv_b, w_ffn_down, final_norm_g, loss_target, m_attn_norm_g, m_w_in, m_w_gate_up, m_b_gate_up, m_sb_out_g, m_gla_out_g, m_w_out, m_ffn_norm_g, m_w_ffn_up, m_conv_w, m_conv_b, m_w_ffn_down, m_final_norm_g, v_attn_norm_g, v_w_in, v_w_gate_up, v_b_gate_up, v_sb_out_g, v_gla_out_g, v_w_out, v_ffn_norm_g, v_w_ffn_up, v_conv_w, v_conv_b, v_w_ffn_down, v_final_norm_g):
    Bd, S, _ = x.shape
    T = Bd * S
    x2d = x.reshape(T, D)
    tgt = loss_target.reshape(T, D)
    c_up = w_ffn_up.shape[2]
    c_gu = w_gate_up.shape[2]
    c_in = w_in.shape[2]

    n_gu = GATE_RANK * c_gu
    rows_bf = lambda w: w[0].T.astype(BF16)
    small_w = lambda wgu, cw: _flat_pad_rows([wgu, cw], SHARD_SMALL_ROWS)

    (g_in,) = _exchange([rows_bf(w_in)], True, "gather_w_in")
    w_in_pt = jnp.pad(g_in.reshape(IN_COLS, D), ((0, 1), (0, 0)))[_PERM]
    g3 = final_norm_g.reshape(1, D)

    proj, h1 = _norm_proj(x2d, attn_norm_g, w_in_pt)
    o_sb, tt, g_up, g_down, g_out, gs = _sb_fwd(
        proj, S, [rows_bf(w_ffn_up), w_ffn_down[0].astype(BF16), w_out[0].astype(BF16),
                  _flat_pad_rows([w_gate_up, conv_w], 8)])
    w_out_f = g_out.reshape(D, D)
    gsf = gs.reshape(N_DEV, -1)
    wgu_f = jnp.transpose(gsf[:, :n_gu].reshape(N_DEV, GATE_RANK, c_gu), (1, 0, 2)).reshape(GATE_RANK, GLA_KW)
    cw_f = jnp.transpose(gsf[:, n_gu:n_gu + 3 * c_up].reshape(N_DEV, 3, c_up), (1, 0, 2)).reshape(3, 2 * D_FF)
    wgu_p = jnp.pad(wgu_f, ((0, LANE - GATE_RANK), (0, 0)))
    w_up_t = g_up.reshape(2 * D_FF, D)
    w_down_f = g_down.reshape(D_FF, D)
    o_gla = _gla_fwd(proj, wgu_p, b_gate_up, S)
    x1, ocat, h2 = _mix_out(o_sb, o_gla, proj, x2d, sb_out_g, gla_out_g, w_out_f, ffn_norm_g)
    hup = _mm(h2, w_up_t, "nt", "ffn_up", tm=512, tn=1408, tk=1024)
    act = _conv_gate(hup, cw_f, conv_b, S)
    dx2, dg3, loss_dev = _down_loss(act, w_down_f, x1, tgt, g3)

    dw_down = _mm(act, dx2, "tn", "dw_down", out_dtype=BF16, tm=D_FF, tn=1024, tk=512)
    dact = _mm(dx2, w_down_f, "nt", "dact", tm=512, tn=1408, tk=1024)
    dhup_a, dhup_v, dcw_a, dcw_v, dcb_a, dcb_v = _conv_gate_bwd(hup, dact, cw_f, conv_b, S)
    dw_up_t = _mm(dhup_a, h2, "tn", "dw_up_a", out_dtype=BF16, tm=D_FF, tn=1024, tk=512, out_rows=2 * D_FF)
    dw_up_t = _mm(dhup_v, h2, "tn", "dw_up_v", out_dtype=BF16, tm=D_FF, tn=1024, tk=512, out_rows=2 * D_FF,
                  out_row0=D_FF, into=dw_up_t)
    dh2 = _mm(dhup_a, w_up_t, "nn", "dh2_a", tm=1024, tn=1024, tk=1408)
    dx1, dg2 = _mm(dhup_v, w_up_t, "nn", "dh2_v", c=dh2, tm=512, tn=1024, tk=1408, b_row0=D_FF,
                   norm_bwd=(x1, ffn_norm_g, dx2))

    dw_out = _mm(ocat, dx1, "tn", "dw_out", out_dtype=BF16, tm=1024, tn=1024, tk=512)
    docat = _mm(dx1, w_out_f, "nt", "docat", tm=512, tn=1024, tk=1024)
    do_sb, do_gla, dproj, dg_sb, dg_gla = _mix_bwd(docat, o_sb, o_gla, proj, sb_out_g, gla_out_g)
    dproj, got_up, got_down, got_out = _sb_bwd(
        proj, tt, do_sb, dproj, S,
        [dw_up_t.reshape(N_DEV, c_up, D), dw_down.reshape(N_DEV, -1, D), dw_out.reshape(N_DEV, -1, D)])
    dproj, dpre = _gla_bwd(proj, wgu_p, b_gate_up, do_gla, dproj, S)
    dproj, dwgu, dbgu = _gate_bwd(dpre, proj, wgu_p, dproj)
    dw_in_pt = _mm(dproj, h1, "tn", "dw_in", out_dtype=BF16, tm=PROJ_W, tn=1024, tk=512)
    dx, dg1, got_in = _mm(dproj, w_in_pt, "nn", "dh1", tm=1024, tn=1024, tk=640,
                          xchg=([dw_in_pt[_INV_PERM].reshape(N_DEV, c_in, D)], False),
                          norm_bwd=(x2d, attn_norm_g, dx1))

    dcw = jnp.concatenate([dcw_a, dcw_v], axis=1)
    dwgu_pc = jnp.transpose(dwgu[:GATE_RANK].reshape(GATE_RANK, N_DEV, c_gu), (1, 0, 2)).reshape(N_DEV, -1)
    dcw_pc = jnp.transpose(dcw.reshape(3, N_DEV, c_up), (1, 0, 2)).reshape(N_DEV, -1)
    small_pc = jnp.concatenate([dwgu_pc, dcw_pc], axis=1)
    small_pc = jnp.pad(small_pc, ((0, 0), (0, SHARD_SMALL_ROWS * D - small_pc.shape[1])))
    small_pc = small_pc.reshape(N_DEV, SHARD_SMALL_ROWS, D).astype(BF16)
    rep_names = ["attn_norm_g", "b_gate_up", "sb_out_g", "gla_out_g", "ffn_norm_g", "conv_b", "final_norm_g"]
    rep_g = [dg1, dbgu, dg_sb, dg_gla, dg2, jnp.concatenate([dcb_a, dcb_v], axis=1), dg3]
    rep_w = [attn_norm_g, b_gate_up, sb_out_g, gla_out_g, ffn_norm_g, conv_b, final_norm_g]
    rep_m = [m_attn_norm_g, m_b_gate_up, m_sb_out_g, m_gla_out_g, m_ffn_norm_g, m_conv_b, m_final_norm_g]
    rep_v = [v_attn_norm_g, v_b_gate_up, v_sb_out_g, v_gla_out_g, v_ffn_norm_g, v_conv_b, v_final_norm_g]
    rep_pc = jnp.broadcast_to(_flat_pad_rows(rep_g, SMALL_ROWS), (N_DEV, SMALL_ROWS, D))
    got_sm, got_rep = _exchange([small_pc, rep_pc], False, "scatter_tail")

    rows = lambda w: w[0].T
    cols = lambda r: r.T[None]
    res = {}
    res["w_in"] = [cols(r) for r in _sum_adamw(got_in, rows(w_in), rows(m_w_in), rows(v_w_in), c_in, "adamw_w_in")]
    res["w_out"] = [r[None] for r in _sum_adamw(got_out, w_out[0], m_w_out[0], v_w_out[0], w_out.shape[1],
                                                 "adamw_w_out")]
    res["w_ffn_up"] = [cols(r) for r in _sum_adamw(got_up, rows(w_ffn_up), rows(m_w_ffn_up), rows(v_w_ffn_up),
                                                    c_up // 2, "adamw_w_up")]
    res["w_ffn_down"] = [r[None] for r in _sum_adamw(got_down, w_ffn_down[0], m_w_ffn_down[0], v_w_ffn_down[0],
                                                      w_ffn_down.shape[1], "adamw_w_down")]
    sm = _sum_adamw(got_sm, small_w(w_gate_up, conv_w), small_w(m_w_gate_up, m_conv_w),
                    small_w(v_w_gate_up, v_conv_w), SHARD_SMALL_ROWS, "adamw_small_sharded")
    res["w_gate_up"] = [r.reshape(-1)[:n_gu].reshape(1, GATE_RANK, c_gu) for r in sm]
    res["conv_w"] = [r.reshape(-1)[n_gu:n_gu + 3 * c_up].reshape(1, 3, c_up) for r in sm]
    rep = _sum_adamw(got_rep, _flat_pad_rows(rep_w, SMALL_ROWS), _flat_pad_rows(rep_m, SMALL_ROWS),
                     _flat_pad_rows(rep_v, SMALL_ROWS), SMALL_ROWS, "adamw_replicated")
    o = 0
    for n, w in zip(rep_names, rep_w):
        res[n] = [r.reshape(-1)[o:o + w.size].reshape(w.shape) for r in rep]
        o += w.size

    loss = lax.psum(loss_dev[0, 0], ("x", "y", "c"))
    order = ["attn_norm_g", "w_in", "w_gate_up", "b_gate_up", "sb_out_g", "gla_out_g", "w_out", "ffn_norm_g",
             "w_ffn_up", "conv_w", "conv_b", "w_ffn_down", "final_norm_g"]
    outs = [loss, dx.reshape(Bd, S, D)]
    for k in range(4):
        outs += [res[n][k] for n in order]
    return tuple(outs)
```

```python
import functools

import numpy as np
import jax
import jax.numpy as jnp
from jax import lax
from jax.experimental import pallas as pl
from jax.experimental.pallas import tpu as pltpu

F32 = jnp.float32
BF16 = jnp.bfloat16
SDS = jax.ShapeDtypeStruct
BS = pl.BlockSpec

N_DEV = 8
D = 1024
EPS = 1e-6
SB_HD = 64
SB_W = 512
GLA_DK = 64
GLA_DV = 128
GLA_KW = 256
GLA_W = 512
GATE_RANK = 16
GATE_NORM = 16.0
CHUNK = 64
GLA_G = 4
GR = GLA_G * CHUNK
QT = 256
D_FF = 2816
IN_COLS = 3088
PROJ_W = 3200
LANE = 128
VMEM_LIMIT = 56 * 1024 * 1024

ADAM_LR, ADAM_B1, ADAM_B2, ADAM_EPS, ADAM_WD, ADAM_STEP = 0.001, 0.9, 0.999, 1e-08, 0.01, 10


def _proj_perm():
    sbq, sbk, sbv = 0, 512, 1024
    gq, gk, gv, glr, gog = 1536, 1792, 2048, 2560, 2576
    cols = []
    for p in range(4):
        for base in (sbq, sbk, sbv):
            cols += list(range(base + 128 * p, base + 128 * p + 128))
    for p in range(2):
        cols += list(range(gq + 128 * p, gq + 128 * p + 128))
        cols += list(range(gk + 128 * p, gk + 128 * p + 128))
        cols += list(range(gv + 256 * p, gv + 256 * p + 256))
    cols += list(range(gog, gog + 512))
    cols += list(range(glr, glr + GATE_RANK)) + [IN_COLS] * (LANE - GATE_RANK)
    perm = np.asarray(cols, np.int32)
    inv = np.zeros((IN_COLS,), np.int32)
    for new, old in enumerate(cols):
        if old < IN_COLS:
            inv[old] = new
    return perm, inv


_PERM, _INV_PERM = _proj_perm()
OG_BLK = 5
GLR_BLK = 24


def _cp(sem=None, vmem=VMEM_LIMIT):
    return pltpu.CompilerParams(dimension_semantics=sem, vmem_limit_bytes=vmem)


def _dot(a, b):
    return lax.dot_general(a, b, (((1,), (0,)), ((), ())), preferred_element_type=F32)


def _dot_nt(a, b):
    return lax.dot_general(a, b, (((1,), (1,)), ((), ())), preferred_element_type=F32)


def _dot_tn(a, b):
    return lax.dot_general(a, b, (((0,), (0,)), ((), ())), preferred_element_type=F32)


def _bf(x):
    return x.astype(BF16)


def _split_dot(x, m, passes, left=True):
    acc = None
    r = x
    for i in range(passes):
        h = r.astype(BF16)
        t = _dot(h, m) if left else _dot(m, h)
        acc = t if acc is None else acc + t
        if i + 1 < passes:
            r = r - h.astype(F32)
    return acc


def _softplus(z):
    return jnp.maximum(z, 0.0) + jnp.log(1.0 + jnp.exp(-jnp.abs(z)))


def _rms_bwd_math(x, g, dh, dres):
    r = lax.rsqrt(jnp.mean(x * x, axis=-1, keepdims=True) + EPS)
    xh = x * r
    dxh = dh * g
    dx = dres + r * (dxh - xh * jnp.mean(dxh * xh, axis=-1, keepdims=True))
    return dx, jnp.sum(dh * xh, axis=0, keepdims=True)


def _tile(n, pref, mult=LANE):
    best = None
    for t in range(mult, min(n, pref) + 1, mult):
        if n % t == 0:
            best = t
    return best if best is not None else n


def _mm(a, b, mode, name, out_dtype=F32, c=None, tm=512, tn=512, tk=512, b_row0=0, out_rows=None, out_row0=0,
        into=None, xchg=None, norm_bwd=None):
    if mode == "nn":
        (M, K), N = a.shape, b.shape[1]
    elif mode == "nt":
        (M, K), N = a.shape, b.shape[0]
    else:
        (K, M), N = a.shape, b.shape[1]
    tm, tn, tk = _tile(M, tm), _tile(N, tn), _tile(K, tk)
    nk = K // tk
    kb0, ob0 = b_row0 // tk, out_row0 // tm
    assert kb0 * tk == b_row0 and ob0 * tm == out_row0 and (mode == "nn" or b_row0 == 0)
    ni, nj = M // tm, N // tn
    j_outer = nk == 1 and (nj - 1) * a.size * a.dtype.itemsize < (ni - 1) * K * N * b.dtype.itemsize
    ix = (lambda f: (lambda j, i, k: f(i, j, k))) if j_outer else (lambda f: f)
    a_spec = BS((tk, tm), ix(lambda i, j, k: (k, i))) if mode == "tn" else BS((tm, tk), ix(lambda i, j, k: (i, k)))
    b_spec = (BS((tn, tk), ix(lambda i, j, k: (j, k))) if mode == "nt"
              else BS((tk, tn), ix(lambda i, j, k: (k + kb0, j))))
    dotfn = {"nn": _dot, "nt": _dot_nt, "tn": _dot_tn}[mode]
    has_c = c is not None
    has_into = into is not None
    nx = 0 if xchg is None else len(xchg[0])
    has_nb = norm_bwd is not None
    assert not has_nb or (nj == 1 and not j_outer and out_dtype == F32)
    n_in = 2 + has_c + has_into + 3 * has_nb

    def body(*refs):
        a_ref, b_ref = refs[:2]
        c_ref = refs[2] if has_c else None
        x_src = refs[n_in:n_in + nx]
        outs = refs[n_in + nx:n_in + 2 * nx + 1 + has_nb]
        o_ref, x_out = outs[0], outs[1 + has_nb:]
        acc = refs[n_in + 2 * nx + 1 + has_nb]
        sems = refs[n_in + 2 * nx + 2 + has_nb:]
        k = pl.program_id(2)
        g0, g1 = pl.program_id(0), pl.program_id(1)
        n0, n1 = (nj, ni) if j_outer else (ni, nj)
        first = jnp.logical_and(jnp.logical_and(g0 == 0, g1 == 0), k == 0)
        if nx:
            @pl.when(first)
            def _():
                _exchange_ops(x_src, x_out, xchg[1], sems, "start")

        if has_nb:
            @pl.when(first)
            def _():
                outs[1][...] = jnp.zeros_like(outs[1])

        @pl.when(k == 0)
        def _():
            acc[...] = jnp.zeros_like(acc)

        acc[...] += dotfn(_bf(a_ref[...]), _bf(b_ref[...]))

        @pl.when(k == nk - 1)
        def _():
            r = acc[...]
            if has_c:
                r = r + c_ref[...]
            if has_nb:
                x_ref, g_ref, dres_ref = refs[n_in - 3:n_in]
                dx, dg = _rms_bwd_math(x_ref[...], g_ref[...], r, dres_ref[...])
                o_ref[...] = dx
                outs[1][...] += dg
            else:
                o_ref[...] = r.astype(out_dtype)

        if nx:
            @pl.when(jnp.logical_and(jnp.logical_and(g0 == n0 - 1, g1 == n1 - 1), k == nk - 1))
            def _():
                _exchange_ops(x_src, x_out, xchg[1], sems, "wait")

    tile = BS((tm, tn), ix(lambda i, j, k: (i, j)))
    in_specs = [a_spec, b_spec]
    args = [a, b]
    if has_c:
        in_specs.append(tile)
        args.append(c)
    aliases = {}
    if has_into:
        aliases = {len(args): 0}
        in_specs.append(BS(memory_space=pl.ANY))
        args.append(into)
    out_shape = [SDS((out_rows or M, N), out_dtype)]
    out_specs = [BS((tm, tn), ix(lambda i, j, k: (i + ob0, j)))]
    scratch = [pltpu.VMEM((tm, tn), F32)]
    if has_nb:
        in_specs += [tile, BS((1, tn), lambda i, j, k: (0, 0)), tile]
        args += list(norm_bwd)
        out_shape.append(SDS((1, N), F32))
        out_specs.append(BS((1, tn), lambda i, j, k: (0, 0)))
    if nx:
        hbm = BS(memory_space=pltpu.HBM)
        in_specs += [hbm] * nx
        args += list(xchg[0])
        out_shape += [SDS((N_DEV,) + s.shape if xchg[1] else s.shape, s.dtype) for s in xchg[0]]
        out_specs += [hbm] * nx
        scratch += _exchange_sems(nx)
    serial = nx or has_nb
    res = pl.pallas_call(
        body, name=name, out_shape=tuple(out_shape), grid=(nj, ni, nk) if j_outer else (ni, nj, nk),
        in_specs=in_specs, out_specs=tuple(out_specs),
        scratch_shapes=scratch, input_output_aliases=aliases,
        compiler_params=_cp(("arbitrary",) * 3 if serial else ("parallel", "parallel", "arbitrary")),
    )(*args)
    return res if serial else res[0]


def _norm_proj(x, g, w):
    T, N = x.shape[0], w.shape[0]
    tm = _tile(T, 256)

    def body(x_ref, g_ref, w_ref, p_ref, h_ref):
        xv = x_ref[...]
        r = lax.rsqrt(jnp.mean(xv * xv, axis=-1, keepdims=True) + EPS)
        h = _bf((xv * r) * g_ref[...])
        h_ref[...] = h
        p_ref[...] = _dot_nt(h, w_ref[...])

    return pl.pallas_call(
        body, name="norm_proj", out_shape=(SDS((T, N), F32), SDS((T, D), BF16)), grid=(T // tm,),
        in_specs=[BS((tm, D), lambda i: (i, 0)), BS((1, D), lambda i: (0, 0)), BS((N, D), lambda i: (0, 0))],
        out_specs=(BS((tm, N), lambda i: (i, 0)), BS((tm, D), lambda i: (i, 0))),
        compiler_params=_cp(("parallel",)),
    )(x, g, w)


TK = 256
SB_DEAD = -104.0
SB_MASKED = -1e30
CNT_LANE = SB_HD - 1


def _sb_masks():
    row = lax.broadcasted_iota(jnp.int32, (2 * QT, TK), 0) & (QT - 1)
    col = lax.broadcasted_iota(jnp.int32, (2 * QT, TK), 1)
    lane = lax.broadcasted_iota(jnp.int32, (1, LANE), 1)
    kr = lax.broadcasted_iota(jnp.int32, (TK, TK), 0)
    kc = lax.broadcasted_iota(jnp.int32, (TK, TK), 1)
    return row, col, lane, kr, kc


def _stack_heads(x, lane):
    return jnp.concatenate([_bf(jnp.where((lane // SB_HD) == hh, x, 0.0)) for hh in range(2)], axis=0)


def _sb_fwd(proj, S, shards):
    T = proj.shape[0]
    nq = S // QT
    scale = SB_HD ** -0.5
    nb, ns = T // S, len(shards)

    def body(qkv_ref, *rest):
        sh_refs, (o_ref, tt_ref), g_refs = rest[:ns], rest[ns:ns + 2], rest[ns + 2:2 * ns + 2]
        sems = rest[2 * ns + 2:]
        first = jnp.logical_and(pl.program_id(0) == 0, pl.program_id(1) == 0)
        last = jnp.logical_and(pl.program_id(0) == nb - 1, pl.program_id(1) == 3)

        @pl.when(first)
        def _():
            _exchange_ops(sh_refs, g_refs, True, sems, "start")

        row, col, lane, kr, kc = _sb_masks()
        msuf = _bf(kr > kc)

        def qloop(qt, _):
            r0 = pl.multiple_of(qt * QT, QT)
            qs = _stack_heads(qkv_ref[pl.ds(r0, QT), 0:128] * scale, lane)

            def live(st):
                it, _, cy = st
                return jnp.logical_and(it <= qt, jnp.max(cy) > SB_DEAD)

            def step(st):
                it, acc, cy = st
                kt = qt - it
                k0 = pl.multiple_of(kt * TK, TK)
                kv = _bf(qkv_ref[pl.ds(k0, TK), 128:256])
                vv = _bf(qkv_ref[pl.ds(k0, TK), 256:384])
                strict = (col + (kt - qt) * TK) < row
                z = jnp.where(strict, _dot_nt(qs, kv), SB_MASKED)
                sp = _softplus(z)
                lg = -sp
                after = cy + _split_dot(lg, msuf, 2)
                w = jnp.exp((z - sp) + after)
                return it + 1, acc + _dot(_bf(w), vv), cy + jnp.sum(lg, axis=1, keepdims=True)

            it, acc, cy = lax.while_loop(
                live, step, (jnp.int32(0), jnp.zeros((2 * QT, LANE), F32), jnp.zeros((2 * QT, 1), F32)))
            o_ref[pl.ds(r0, QT), :] = jnp.where(lane < SB_HD, acc[:QT], acc[QT:])
            tt = jnp.where(lane < SB_HD, cy[:QT], cy[QT:])
            tt_ref[pl.ds(r0, QT), :] = jnp.where(lane == CNT_LANE, it.astype(F32), tt)
            return 0

        lax.fori_loop(0, nq, qloop, 0)

        @pl.when(last)
        def _():
            _exchange_ops(sh_refs, g_refs, True, sems, "wait")

    hbm = BS(memory_space=pltpu.HBM)
    col_spec = BS((S, LANE), lambda b, p: (b, p))
    return pl.pallas_call(
        body, name="sb_fwd",
        out_shape=(SDS((T, SB_W), F32), SDS((T, SB_W), F32)) + tuple(SDS((N_DEV,) + s.shape, s.dtype) for s in shards),
        grid=(nb, 4),
        in_specs=[BS((S, 384), lambda b, p: (b, p))] + [hbm] * ns,
        out_specs=(col_spec, col_spec) + (hbm,) * ns,
        scratch_shapes=_exchange_sems(ns),
        compiler_params=_cp(("arbitrary", "arbitrary")),
    )(proj, *shards)


def _log_sigmoid(x):
    return jnp.minimum(x, 0.0) - jnp.log1p(jnp.exp(-jnp.abs(x)))


def _gla_masks():
    r = lax.broadcasted_iota(jnp.int32, (GR, GR), 0)
    c = lax.broadcasted_iota(jnp.int32, (GR, GR), 1)
    same = (r // CHUNK) == (c // CHUNK)
    causal = jnp.logical_and(same, r >= c)
    lane = lax.broadcasted_iota(jnp.int32, (1, LANE), 1)
    return causal, _bf(causal), _bf(jnp.logical_and(same, r <= c)), lane


def _gla_group_terms(blk_ref, glr_ref, wgu, bgu, r0, tri_incl):
    q = blk_ref[pl.ds(r0, GR), 0:128]
    k = blk_ref[pl.ds(r0, GR), 128:256]
    v = blk_ref[pl.ds(r0, GR), 256:512]
    pre = _dot(_bf(glr_ref[pl.ds(r0, GR), :]), wgu) + bgu
    la = _log_sigmoid(pre) / GATE_NORM
    b = _split_dot(la, tri_incl, 3, left=False)
    b_last = _per_chunk(lambda rows: b[rows.stop - 1:rows.stop])
    eb = jnp.exp(b)
    qd = (q * (GLA_DK ** -0.5)) * eb
    ki = k * jnp.exp(-b)
    ke = k * jnp.exp(b_last - b)
    decay = jnp.exp(b_last)
    return v, pre, b, b_last, eb, qd, ki, ke, decay


def _chunk_rows(n):
    return slice(n * CHUNK, (n + 1) * CHUNK)


def _per_chunk(row_fn):
    return jnp.concatenate(
        [jnp.broadcast_to(row_fn(_chunk_rows(n)), (CHUNK, LANE)) for n in range(GLA_G)], axis=0)


def _gla_fwd(proj, wgu, bgu, S):
    T = proj.shape[0]
    ng = S // GR

    def body(blk_ref, glr_ref, wgu_ref, bgu_ref, o_ref):
        causal, tri_incl, _, lane = _gla_masks()
        wg = _bf(wgu_ref[...])
        bg = bgu_ref[...]

        def group(g, states):
            r0 = pl.multiple_of(g * GR, GR)
            v, _, _, _, _, qd, ki, ke, decay = _gla_group_terms(blk_ref, glr_ref, wg, bg, r0, tri_incl)
            kib, keb = _bf(ki), _bf(ke)
            new_states, outs = [], []
            for hh in range(2):
                hm = (lane // GLA_DK) == hh
                qm = _bf(jnp.where(hm, qd, 0.0))
                vh = _bf(v[:, 128 * hh:128 * hh + 128])
                attn = jnp.where(causal, _dot_nt(qm, kib), 0.0)
                o_intra = _dot(_bf(attn), vh)
                st = states[hh]
                parts = []
                for n in range(GLA_G):
                    rows = _chunk_rows(n)
                    parts.append(o_intra[rows] + _dot_nt(qm[rows], _bf(st)))
                    st = st * decay[n * CHUNK:n * CHUNK + 1] + _dot_tn(vh[rows], keb[rows])
                outs.append(jnp.concatenate(parts, axis=0))
                new_states.append(st)
            o_ref[pl.ds(r0, GR), :] = jnp.concatenate(outs, axis=1)
            return tuple(new_states)

        z = jnp.zeros((GLA_DV, LANE), F32)
        lax.fori_loop(0, ng, group, (z, z))

    return pl.pallas_call(
        body, name="gla_fwd", out_shape=SDS((T, GLA_W), F32), grid=(T // S, 2),
        in_specs=[BS((S, 512), lambda b, p: (b, 3 + p)), BS((S, LANE), lambda b, p: (b, GLR_BLK)),
                  BS((LANE, LANE), lambda b, p: (0, p)), BS((1, LANE), lambda b, p: (0, p))],
        out_specs=BS((S, 256), lambda b, p: (b, p)),
        compiler_params=_cp(("parallel", "parallel")),
    )(proj, proj, wgu, bgu)


def _head_blockdiag(width, hd):
    r = lax.broadcasted_iota(jnp.int32, (width, width), 0) // hd
    c = lax.broadcasted_iota(jnp.int32, (width, width), 1) // hd
    return _bf(r == c)


def _mix_out(o_sb, o_gla, proj, x, g_sb, g_gla, w_out, g2):
    T = x.shape[0]
    tm = _tile(T, 256)

    def body(osb_ref, ogl_ref, og_ref, x_ref, gsb_ref, ggl_ref, w_ref, g2_ref, x1_ref, oc_ref, h2_ref):
        bd64 = _head_blockdiag(SB_W, SB_HD)
        bd128 = _head_blockdiag(GLA_W, GLA_DV)
        o = osb_ref[...]
        r = lax.rsqrt(_split_dot(o * o, bd64, 2) * (1.0 / SB_HD) + EPS)
        c_sb = (o * r) * gsb_ref[...]
        o = ogl_ref[...]
        r = lax.rsqrt(_split_dot(o * o, bd128, 2) * (1.0 / GLA_DV) + EPS)
        og = og_ref[...]
        c_gl = ((o * r) * ggl_ref[...]) * (og * jax.nn.sigmoid(og))
        oc = _bf(jnp.concatenate([c_sb, c_gl], axis=1))
        oc_ref[...] = oc
        x1 = x_ref[...] + _dot(oc, w_ref[...])
        x1_ref[...] = x1
        r2 = lax.rsqrt(jnp.mean(x1 * x1, axis=-1, keepdims=True) + EPS)
        h2_ref[...] = _bf((x1 * r2) * g2_ref[...])

    row = lambda w: BS((tm, w), lambda i: (i, 0))
    vec = lambda w: BS((1, w), lambda i: (0, 0))
    return pl.pallas_call(
        body, name="mix_out", out_shape=(SDS((T, D), F32), SDS((T, D), BF16), SDS((T, D), BF16)), grid=(T // tm,),
        in_specs=[row(SB_W), row(GLA_W), BS((tm, 512), lambda i: (i, OG_BLK)), row(D), vec(SB_W), vec(GLA_W),
                  BS((D, D), lambda i: (0, 0)), vec(D)],
        out_specs=(row(D), row(D), row(D)),
        compiler_params=_cp(("parallel",)),
    )(o_sb, o_gla, proj, x, g_sb, g_gla, w_out, g2)


CONV_ROWS = 256
CONV_TC = 256


def _rows_before(ref, r0, first):
    prev = ref[pl.ds(pl.multiple_of(jnp.maximum(r0 - 8, 0), 8), 8), :]
    return jnp.where(first, 0.0, prev)


def _shift_down(cur, prev8, k):
    cat = jnp.concatenate([prev8, cur], axis=0)
    return pltpu.roll(cat, k, 0)[8:]


def _shift_up(cur, next8, k):
    cat = jnp.concatenate([cur, next8], axis=0)
    return pltpu.roll(cat, cat.shape[0] - k, 0)[:cur.shape[0]]


def _conv_at(h_ref, cw, cb, r0, rows, first):
    cur = h_ref[pl.ds(r0, rows), :]
    prev8 = _rows_before(h_ref, r0, first)
    u = cb + cw[0:1, :] * _shift_down(cur, prev8, 2)
    u = u + cw[1:2, :] * _shift_down(cur, prev8, 1)
    return u + cw[2:3, :] * cur


NJ = D_FF // CONV_TC


def _conv_gate(hup, cw, cb, S):
    T = hup.shape[0]
    rows = min(CONV_ROWS, S)
    nr = S // rows

    def body(ha_ref, hv_ref, cwa_ref, cwv_ref, cba_ref, cbv_ref, act_ref):
        cwa, cwv, cba, cbv = cwa_ref[...], cwv_ref[...], cba_ref[...], cbv_ref[...]

        def step(c, _):
            r0 = pl.multiple_of(c * rows, rows)
            ua = _conv_at(ha_ref, cwa, cba, r0, rows, c == 0)
            uv = _conv_at(hv_ref, cwv, cbv, r0, rows, c == 0)
            act_ref[pl.ds(r0, rows), :] = _bf((ua * jax.nn.sigmoid(ua)) * uv)
            return 0

        lax.fori_loop(0, nr, step, 0)

    blk = lambda o: BS((S, CONV_TC), lambda b, j: (b, j + o))
    w3 = lambda o: BS((3, CONV_TC), lambda b, j: (0, j + o))
    w1 = lambda o: BS((1, CONV_TC), lambda b, j: (0, j + o))
    return pl.pallas_call(
        body, name="conv_gate", out_shape=SDS((T, D_FF), BF16), grid=(T // S, NJ),
        in_specs=[blk(0), blk(NJ), w3(0), w3(NJ), w1(0), w1(NJ)], out_specs=blk(0),
        compiler_params=_cp(("parallel", "parallel")),
    )(hup, hup, cw, cw, cb, cb)


def _down_loss(act, w_down, x1, tgt, g3):
    T = x1.shape[0]
    tm = _tile(T, 256)

    def body(a_ref, w_ref, x1_ref, t_ref, g_ref, dx_ref, dg_ref, ls_ref):
        @pl.when(pl.program_id(0) == 0)
        def _():
            dg_ref[...] = jnp.zeros_like(dg_ref)
            ls_ref[...] = jnp.zeros_like(ls_ref)

        g = g_ref[...]
        x2 = x1_ref[...] + _dot(a_ref[...], w_ref[...])
        r = lax.rsqrt(jnp.mean(x2 * x2, axis=-1, keepdims=True) + EPS)
        xh = x2 * r
        e = xh * g - t_ref[...]
        ls_ref[...] += 0.5 * jnp.sum(jnp.mean(e * e, axis=-1, keepdims=True), axis=0, keepdims=True)
        dy = e * (1.0 / D)
        dxh = dy * g
        dx_ref[...] = r * (dxh - xh * jnp.mean(dxh * xh, axis=-1, keepdims=True))
        dg_ref[...] += jnp.sum(dy * xh, axis=0, keepdims=True)

    row = lambda w: BS((tm, w), lambda i: (i, 0))
    return pl.pallas_call(
        body, name="down_loss", out_shape=(SDS((T, D), F32), SDS((1, D), F32), SDS((1, LANE), F32)), grid=(T // tm,),
        in_specs=[row(D_FF), BS((D_FF, D), lambda i: (0, 0)), row(D), row(D), BS((1, D), lambda i: (0, 0))],
        out_specs=(row(D), BS((1, D), lambda i: (0, 0)), BS((1, LANE), lambda i: (0, 0))),
        compiler_params=_cp(("arbitrary",)),
    )(act, w_down, x1, tgt, g3)


def _conv_gate_bwd(hup, dact, cw, cb, S):
    T = hup.shape[0]
    rows = min(CONV_ROWS, S)
    nr = S // rows

    def body(ha_ref, hv_ref, da_ref, cwa_ref, cwv_ref, cba_ref, cbv_ref,
             dha_ref, dhv_ref, dcwa_ref, dcwv_ref, dcba_ref, dcbv_ref):
        @pl.when(pl.program_id(1) == 0)
        def _():
            for r in (dcwa_ref, dcwv_ref, dcba_ref, dcbv_ref):
                r[...] = jnp.zeros_like(r)

        cwa, cwv, cba, cbv = cwa_ref[...], cwv_ref[...], cba_ref[...], cbv_ref[...]

        def du_at(r0, n, first):
            ua = _conv_at(ha_ref, cwa, cba, r0, n, first)
            uv = _conv_at(hv_ref, cwv, cbv, r0, n, first)
            da = da_ref[pl.ds(r0, n), :]
            sg = jax.nn.sigmoid(ua)
            dua = (da * uv) * (sg * (1.0 + ua * (1.0 - sg)))
            duv = da * (ua * sg)
            return dua, duv

        def step(c, _):
            r0 = pl.multiple_of(c * rows, rows)
            first, last = c == 0, c == nr - 1
            dua, duv = du_at(r0, rows, first)
            n0 = pl.multiple_of(jnp.minimum(r0 + rows, S - 8), 8)
            nua, nuv = du_at(n0, 8, False)
            nua = jnp.where(last, 0.0, nua)
            nuv = jnp.where(last, 0.0, nuv)
            for (h_ref, cw, du, nu, dh_ref, dcw_ref, dcb_ref) in (
                    (ha_ref, cwa, dua, nua, dha_ref, dcwa_ref, dcba_ref),
                    (hv_ref, cwv, duv, nuv, dhv_ref, dcwv_ref, dcbv_ref)):
                dh = cw[2:3, :] * du + cw[1:2, :] * _shift_up(du, nu, 1) + cw[0:1, :] * _shift_up(du, nu, 2)
                dh_ref[pl.ds(r0, rows), :] = _bf(dh)
                cur = h_ref[pl.ds(r0, rows), :]
                prev8 = _rows_before(h_ref, r0, first)
                dcw_ref[0:1, :] += jnp.sum(du * _shift_down(cur, prev8, 2), axis=0, keepdims=True)
                dcw_ref[1:2, :] += jnp.sum(du * _shift_down(cur, prev8, 1), axis=0, keepdims=True)
                dcw_ref[2:3, :] += jnp.sum(du * cur, axis=0, keepdims=True)
                dcb_ref[...] += jnp.sum(du, axis=0, keepdims=True)
            return 0

        lax.fori_loop(0, nr, step, 0)

    blk = lambda o: BS((S, CONV_TC), lambda j, b: (b, j + o))
    w3 = lambda o: BS((3, CONV_TC), lambda j, b: (0, j + o))
    w1 = lambda o: BS((1, CONV_TC), lambda j, b: (0, j + o))
    return pl.pallas_call(
        body, name="conv_gate_bwd",
        out_shape=(SDS((T, D_FF), BF16), SDS((T, D_FF), BF16), SDS((3, D_FF), F32), SDS((3, D_FF), F32),
                   SDS((1, D_FF), F32), SDS((1, D_FF), F32)),
        grid=(NJ, T // S),
        in_specs=[blk(0), blk(NJ), blk(0), w3(0), w3(NJ), w1(0), w1(NJ)],
        out_specs=(blk(0), blk(0), w3(0), w3(0), w1(0), w1(0)),
        compiler_params=_cp(("parallel", "arbitrary")),
    )(hup, hup, dact, cw, cw, cb, cb)


def _mix_bwd(docat, o_sb, o_gla, proj, g_sb, g_gla):
    T = docat.shape[0]
    tm = _tile(T, 256)

    def body(d_ref, osb_ref, ogl_ref, og_ref, gsb_ref, ggl_ref, dsb_ref, dgl_ref, dog_ref, dgsb_ref, dggl_ref):
        @pl.when(pl.program_id(0) == 0)
        def _():
            dgsb_ref[...] = jnp.zeros_like(dgsb_ref)
            dggl_ref[...] = jnp.zeros_like(dggl_ref)

        bd64 = _head_blockdiag(SB_W, SB_HD)
        bd128 = _head_blockdiag(GLA_W, GLA_DV)
        d = d_ref[:, 0:SB_W]
        o = osb_ref[...]
        r = lax.rsqrt(_split_dot(o * o, bd64, 2) * (1.0 / SB_HD) + EPS)
        n = o * r
        dn = d * gsb_ref[...]
        dgsb_ref[...] += jnp.sum(d * n, axis=0, keepdims=True)
        dsb_ref[...] = r * (dn - n * (_split_dot(dn * n, bd64, 2) * (1.0 / SB_HD)))

        d = d_ref[:, SB_W:D]
        o = ogl_ref[...]
        r = lax.rsqrt(_split_dot(o * o, bd128, 2) * (1.0 / GLA_DV) + EPS)
        n = o * r
        og = og_ref[...]
        sg = jax.nn.sigmoid(og)
        dm = d * (og * sg)
        dog_ref[...] = _bf((d * (n * ggl_ref[...])) * (sg * (1.0 + og * (1.0 - sg))))
        dn = dm * ggl_ref[...]
        dggl_ref[...] += jnp.sum(dm * n, axis=0, keepdims=True)
        dgl_ref[...] = r * (dn - n * (_split_dot(dn * n, bd128, 2) * (1.0 / GLA_DV)))

    row = lambda w: BS((tm, w), lambda i: (i, 0))
    vec = lambda w: BS((1, w), lambda i: (0, 0))
    ogb = BS((tm, 512), lambda i: (i, OG_BLK))
    return pl.pallas_call(
        body, name="mix_bwd",
        out_shape=(SDS((T, SB_W), F32), SDS((T, GLA_W), F32), SDS((T, PROJ_W), BF16), SDS((1, SB_W), F32),
                   SDS((1, GLA_W), F32)),
        grid=(T // tm,),
        in_specs=[row(D), row(SB_W), row(GLA_W), ogb, vec(SB_W), vec(GLA_W)],
        out_specs=(row(SB_W), row(GLA_W), ogb, vec(SB_W), vec(GLA_W)),
        compiler_params=_cp(("arbitrary",)),
    )(docat, o_sb, o_gla, proj, g_sb, g_gla)


def _sb_bwd(proj, tt, do, dproj, S, pieces):
    T = proj.shape[0]
    nq = S // QT
    scale = SB_HD ** -0.5
    nb, ns = T // S, len(pieces)

    def body(qkv_ref, tt_ref, do_ref, dp_in_ref, *rest):
        del dp_in_ref
        pc_refs, dp_ref, got_refs = rest[:ns], rest[ns], rest[ns + 1:2 * ns + 1]
        dk_acc, dv_acc = rest[2 * ns + 1:2 * ns + 3]
        sems = rest[2 * ns + 3:]
        first = jnp.logical_and(pl.program_id(0) == 0, pl.program_id(1) == 0)
        last = jnp.logical_and(pl.program_id(0) == nb - 1, pl.program_id(1) == 3)

        @pl.when(first)
        def _():
            _exchange_ops(pc_refs, got_refs, False, sems, "start")

        row, col, lane, kr, kc = _sb_masks()
        mincl = _bf(kr <= kc)
        mexcl = _bf(kr < kc)
        dk_acc[...] = jnp.zeros_like(dk_acc)
        dv_acc[...] = jnp.zeros_like(dv_acc)

        def qloop(qt, _):
            r0 = pl.multiple_of(qt * QT, QT)
            qs = _stack_heads(qkv_ref[pl.ds(r0, QT), 0:128] * scale, lane)
            dos = _stack_heads(do_ref[pl.ds(r0, QT), :], lane)
            ttv = tt_ref[pl.ds(r0, QT), :]
            tot = jnp.concatenate([ttv[:, 0:1], ttv[:, SB_HD:SB_HD + 1]], axis=0)
            walked = jnp.max(ttv[:, CNT_LANE:CNT_LANE + 1]).astype(jnp.int32)

            def step(kt, st):
                dq, lc, pc = st
                k0 = pl.multiple_of(kt * TK, TK)
                kv = _bf(qkv_ref[pl.ds(k0, TK), 128:256])
                vv = _bf(qkv_ref[pl.ds(k0, TK), 256:384])
                strict = (col + (kt - qt) * TK) < row
                z = jnp.where(strict, _dot_nt(qs, kv), SB_MASKED)
                sp = _softplus(z)
                lg = -sp
                after = tot - (lc + _split_dot(lg, mincl, 2))
                gl = z - sp
                w = jnp.exp(gl + after)
                du = w * _dot_nt(dos, vv)
                beta = jnp.exp(gl)
                pex = pc + _split_dot(du, mexcl, 2)
                dz = _bf(du - beta * (du + pex))
                dk_acc[pl.ds(k0, TK), :] += _dot_tn(dz, qs)
                dv_acc[pl.ds(k0, TK), :] += _dot_tn(_bf(w), dos)
                return (dq + _dot(dz, kv), lc + jnp.sum(lg, axis=1, keepdims=True),
                        pc + jnp.sum(du, axis=1, keepdims=True))

            zc = jnp.zeros((2 * QT, 1), F32)
            dq, _, _ = lax.fori_loop(qt - walked + 1, qt + 1, step, (jnp.zeros((2 * QT, LANE), F32), zc, zc))
            dp_ref[pl.ds(r0, QT), 0:128] = _bf(jnp.where(lane < SB_HD, dq[:QT], dq[QT:]) * scale)
            return 0

        lax.fori_loop(0, nq, qloop, 0)
        dp_ref[:, 128:256] = _bf(dk_acc[...])
        dp_ref[:, 256:384] = _bf(dv_acc[...])

        @pl.when(last)
        def _():
            _exchange_ops(pc_refs, got_refs, False, sems, "wait")

    blk = BS((S, 384), lambda b, p: (b, p))
    col_spec = BS((S, LANE), lambda b, p: (b, p))
    hbm = BS(memory_space=pltpu.HBM)
    return pl.pallas_call(
        body, name="sb_bwd", out_shape=(SDS((T, PROJ_W), BF16),) + tuple(SDS(s.shape, s.dtype) for s in pieces),
        grid=(nb, 4),
        in_specs=[blk, col_spec, col_spec, BS(memory_space=pl.ANY)] + [hbm] * ns, out_specs=(blk,) + (hbm,) * ns,
        scratch_shapes=[pltpu.VMEM((S, LANE), F32), pltpu.VMEM((S, LANE), F32)] + _exchange_sems(ns),
        input_output_aliases={3: 0},
        compiler_params=_cp(("arbitrary", "arbitrary")),
    )(proj, tt, do, dproj, *pieces)


def _gla_bwd(proj, wgu, bgu, do, dproj, S):
    T = proj.shape[0]
    nc, ng = S // CHUNK, S // GR

    def body(blk_ref, glr_ref, wgu_ref, bgu_ref, do_ref, dp_in_ref, dp_ref, dpre_ref, st_ref):
        del dp_in_ref
        causal, tri_incl, tri_rev, lane = _gla_masks()
        wg = _bf(wgu_ref[...])
        bg = bgu_ref[...]

        def fwd_group(g, states):
            r0 = pl.multiple_of(g * GR, GR)
            v, _, _, _, _, _, _, ke, decay = _gla_group_terms(blk_ref, glr_ref, wg, bg, r0, tri_incl)
            keb = _bf(ke)
            new_states = []
            for hh in range(2):
                vh = _bf(v[:, 128 * hh:128 * hh + 128])
                st = states[hh]
                for n in range(GLA_G):
                    rows = _chunk_rows(n)
                    st_ref[hh, g * GLA_G + n] = st
                    st = st * decay[n * CHUNK:n * CHUNK + 1] + _dot_tn(vh[rows], keb[rows])
                new_states.append(st)
            return tuple(new_states)

        z = jnp.zeros((GLA_DV, LANE), F32)
        lax.fori_loop(0, ng, fwd_group, (z, z))

        def bwd_group(it, dstates):
            g = ng - 1 - it
            r0 = pl.multiple_of(g * GR, GR)
            v, pre, b, b_last, eb, qd, ki, ke, decay = _gla_group_terms(
                blk_ref, glr_ref, wg, bg, r0, tri_incl)
            kib = _bf(ki)
            dqd = jnp.zeros((GR, LANE), F32)
            dki = jnp.zeros((GR, LANE), F32)
            dke = jnp.zeros((GR, LANE), F32)
            ddec = jnp.zeros((GR, LANE), F32)
            new_dstates, dvs = [], []
            for hh in range(2):
                hm = (lane // GLA_DK) == hh
                qm = _bf(jnp.where(hm, qd, 0.0))
                kem = _bf(jnp.where(hm, ke, 0.0))
                vh = _bf(v[:, 128 * hh:128 * hh + 128])
                doh = _bf(do_ref[pl.ds(r0, GR), 128 * hh:128 * hh + 128])
                attn = _bf(jnp.where(causal, _dot_nt(qm, kib), 0.0))
                dattn = _bf(jnp.where(causal, _dot_nt(doh, vh), 0.0))
                dv_intra = _dot_tn(attn, doh)
                dqd_intra = _dot(dattn, kib)
                dki = dki + _dot_tn(dattn, qm)
                dst = dstates[hh]
                dv_p, dqd_p, dke_p, ddec_p = [None] * GLA_G, [None] * GLA_G, [None] * GLA_G, [None] * GLA_G
                for n in reversed(range(GLA_G)):
                    rows = _chunk_rows(n)
                    st = st_ref[hh, g * GLA_G + n]
                    dv_p[n] = dv_intra[rows] + _dot_nt(kem[rows], _bf(dst))
                    dqd_p[n] = dqd_intra[rows] + _dot(doh[rows], _bf(st))
                    dke_p[n] = _dot(vh[rows], _bf(dst))
                    ddec_p[n] = jnp.broadcast_to(jnp.sum(dst * st, axis=0, keepdims=True), (CHUNK, LANE))
                    dst = dst * decay[n * CHUNK:n * CHUNK + 1] + _dot_tn(doh[rows], qm[rows])
                dvs.append(jnp.concatenate(dv_p, axis=0))
                dqd = dqd + jnp.where(hm, jnp.concatenate(dqd_p, axis=0), 0.0)
                dke = dke + jnp.where(hm, jnp.concatenate(dke_p, axis=0), 0.0)
                ddec = ddec + jnp.where(hm, jnp.concatenate(ddec_p, axis=0), 0.0)
                new_dstates.append(dst)
            einv = jnp.exp(-b)
            eend = jnp.exp(b_last - b)
            dq = (dqd * eb) * (GLA_DK ** -0.5)
            dk = dki * einv + dke * eend
            db = dqd * qd - dki * ki - dke * ke
            dkk = dke * ke
            db_last = _per_chunk(lambda rows: jnp.sum(dkk[rows], axis=0, keepdims=True)) + ddec * decay
            dla = _split_dot(db, tri_rev, 2, left=False) + db_last
            dpre_ref[pl.ds(r0, GR), :] = (dla * (1.0 / GATE_NORM)) * (1.0 - jax.nn.sigmoid(pre))
            dp_ref[pl.ds(r0, GR), 0:128] = _bf(dq)
            dp_ref[pl.ds(r0, GR), 128:256] = _bf(dk)
            dp_ref[pl.ds(r0, GR), 256:512] = _bf(jnp.concatenate(dvs, axis=1))
            return tuple(new_dstates)

        lax.fori_loop(0, ng, bwd_group, (z, z))

    return pl.pallas_call(
        body, name="gla_bwd", out_shape=(SDS((T, PROJ_W), BF16), SDS((T, GLA_KW), F32)), grid=(T // S, 2),
        in_specs=[BS((S, 512), lambda b, p: (b, 3 + p)), BS((S, LANE), lambda b, p: (b, GLR_BLK)),
                  BS((LANE, LANE), lambda b, p: (0, p)), BS((1, LANE), lambda b, p: (0, p)),
                  BS((S, 256), lambda b, p: (b, p)), BS(memory_space=pl.ANY)],
        out_specs=(BS((S, 512), lambda b, p: (b, 3 + p)), BS((S, LANE), lambda b, p: (b, p))),
        scratch_shapes=[pltpu.VMEM((2, nc, GLA_DV, LANE), F32)],
        input_output_aliases={5: 0},
        compiler_params=_cp(("parallel", "parallel")),
    )(proj, proj, wgu, bgu, do, dproj)


def _gate_bwd(dpre, proj, wgu, dproj):
    T = dpre.shape[0]
    tm = _tile(T, 512)

    def body(dpre_ref, glr_ref, wgu_ref, dp_in_ref, dp_ref, dw_ref, db_ref):
        del dp_in_ref

        @pl.when(pl.program_id(0) == 0)
        def _():
            dw_ref[...] = jnp.zeros_like(dw_ref)
            db_ref[...] = jnp.zeros_like(db_ref)

        dpre = dpre_ref[...]
        dp_ref[...] = _bf(_dot_nt(_bf(dpre), _bf(wgu_ref[...])))
        dw_ref[...] += _dot_tn(_bf(glr_ref[...]), _bf(dpre))
        db_ref[...] += jnp.sum(dpre, axis=0, keepdims=True)

    glr = BS((tm, LANE), lambda i: (i, GLR_BLK))
    return pl.pallas_call(
        body, name="gate_bwd",
        out_shape=(SDS((T, PROJ_W), BF16), SDS((LANE, GLA_KW), F32), SDS((1, GLA_KW), F32)), grid=(T // tm,),
        in_specs=[BS((tm, GLA_KW), lambda i: (i, 0)), glr, BS((LANE, GLA_KW), lambda i: (0, 0)),
                  BS(memory_space=pl.ANY)],
        out_specs=(glr, BS((LANE, GLA_KW), lambda i: (0, 0)), BS((1, GLA_KW), lambda i: (0, 0))),
        input_output_aliases={3: 0},
        compiler_params=_cp(("arbitrary",)),
    )(dpre, proj, wgu, dproj)


def _exchange_sems(n):
    return [pltpu.SemaphoreType.DMA((n * (N_DEV - 1),)), pltpu.SemaphoreType.DMA((n * (N_DEV - 1),)),
            pltpu.SemaphoreType.DMA((n,))]


def _exchange_ops(srcs, outs, gather, sems, act):
    ssem, rsem, lsem = sems
    x, y, c = lax.axis_index("x"), lax.axis_index("y"), lax.axis_index("c")
    me = 4 * x + 2 * y + c
    for i, (s_ref, o_ref) in enumerate(zip(srcs, outs)):
        for k in range(1, N_DEV):
            px = (x + ((k >> 2) & 1)) % 2
            py = (y + ((k >> 1) & 1)) % 2
            pc = (c + (k & 1)) % 2
            peer = 4 * px + 2 * py + pc
            n = i * (N_DEV - 1) + k - 1
            out = pltpu.make_async_remote_copy(
                src_ref=s_ref if gather else s_ref.at[peer], dst_ref=o_ref.at[me],
                send_sem=ssem.at[n], recv_sem=rsem.at[n],
                device_id=(px, py, pc), device_id_type=pl.DeviceIdType.MESH)
            if act == "start":
                out.start()
            else:
                out.wait_send()
                pltpu.make_async_remote_copy(
                    src_ref=s_ref if gather else s_ref.at[me], dst_ref=o_ref.at[peer],
                    send_sem=ssem.at[n], recv_sem=rsem.at[n],
                    device_id=(x, y, c), device_id_type=pl.DeviceIdType.MESH).wait_recv()
        mine = pltpu.make_async_copy(s_ref if gather else s_ref.at[me], o_ref.at[me], lsem.at[i])
        if act == "start":
            mine.start()
        else:
            mine.wait()


def _gather_two_level(src, name):
    def body(s_ref, o_ref, ssem, rsem, lsem):
        x, y, c = lax.axis_index("x"), lax.axis_index("y"), lax.axis_index("c")
        me, sibling = (x, y, c), (x, y, 1 - c)
        chips = [(1 - x, y), (x, 1 - y), (1 - x, 1 - y)]

        def slab(px, py, pc):
            return o_ref.at[4 * px + 2 * py + pc]

        def copy(k, block, to, src_ref=None):
            return pltpu.make_async_remote_copy(
                src_ref=slab(*block) if src_ref is None else src_ref, dst_ref=slab(*block),
                send_sem=ssem.at[k], recv_sem=rsem.at[k], device_id=to, device_id_type=pl.DeviceIdType.MESH)

        mine = pltpu.make_async_copy(s_ref, slab(*me), lsem)
        mine.start()
        first = [copy(0, me, sibling, s_ref)] + [copy(1 + j, me, (*chip, c), s_ref) for j, chip in enumerate(chips)]
        for cp in first:
            cp.start()
        passed = [copy(4 + j, (*chip, c), sibling) for j, chip in enumerate(chips)]
        for j, chip in enumerate(chips):
            copy(1 + j, (*chip, c), me).wait_recv()
            passed[j].start()
        copy(0, sibling, me).wait_recv()
        for j, chip in enumerate(chips):
            copy(4 + j, (*chip, 1 - c), me).wait_recv()
        for cp in first + passed:
            cp.wait_send()
        mine.wait()

    hbm = BS(memory_space=pltpu.HBM)
    return pl.pallas_call(
        body, name=name, out_shape=SDS((N_DEV,) + src.shape, src.dtype), in_specs=[hbm], out_specs=hbm,
        scratch_shapes=[pltpu.SemaphoreType.DMA((N_DEV - 1,)), pltpu.SemaphoreType.DMA((N_DEV - 1,)),
                        pltpu.SemaphoreType.DMA(())],
    )(src)


def _exchange(srcs, gather, name):
    n = len(srcs)
    shapes = [SDS((N_DEV,) + s.shape if gather else s.shape, s.dtype) for s in srcs]

    def body(*refs):
        s_refs, o_refs, sems = refs[:n], refs[n:2 * n], refs[2 * n:]
        _exchange_ops(s_refs, o_refs, gather, sems, "start")
        _exchange_ops(s_refs, o_refs, gather, sems, "wait")

    hbm = BS(memory_space=pltpu.HBM)
    return pl.pallas_call(
        body, name=name, out_shape=tuple(shapes), in_specs=[hbm] * n, out_specs=(hbm,) * n,
        scratch_shapes=_exchange_sems(n),
    )(*srcs)


def _adamw_math(w, g, m, v):
    m = ADAM_B1 * m + (1.0 - ADAM_B1) * g
    v = ADAM_B2 * v + (1.0 - ADAM_B2) * (g * g)
    m_hat = m / (1.0 - ADAM_B1 ** ADAM_STEP)
    v_hat = v / (1.0 - ADAM_B2 ** ADAM_STEP)
    delta = -ADAM_LR * (m_hat / (jnp.sqrt(v_hat) + ADAM_EPS) + ADAM_WD * w)
    return delta, m, v


def _sum_adamw(parts, w, m, v, tr, name):
    R, C = w.shape

    def body(p_ref, w_ref, m_ref, v_ref, g_ref, d_ref, nm_ref, nv_ref):
        g = p_ref[0].astype(F32)
        for d in range(1, N_DEV):
            g = g + p_ref[d].astype(F32)
        delta, nm, nv = _adamw_math(w_ref[...], g, m_ref[...], v_ref[...])
        g_ref[...] = g
        d_ref[...] = delta
        nm_ref[...] = nm
        nv_ref[...] = nv

    blk = BS((tr, C), lambda i: (i, 0))
    out = SDS((R, C), F32)
    return pl.pallas_call(
        body, name=name, out_shape=(out, out, out, out), grid=(R // tr,),
        in_specs=[BS((N_DEV, tr, C), lambda i: (0, i, 0)), blk, blk, blk], out_specs=(blk, blk, blk, blk),
        compiler_params=_cp(("parallel",)),
    )(parts, w, m, v)


def _flat_pad_rows(parts, rows):
    flat = jnp.concatenate([p.reshape(-1) for p in parts])
    return jnp.pad(flat, (0, rows * D - flat.shape[0])).reshape(rows, D)


SMALL_ROWS = 16
SHARD_SMALL_ROWS = 3


def kernel(x, attn_norm_g, w_in, w_gate_up, b_gate_up, sb_out_g, gla_out_g, w_out, ffn_norm_g, w_ffn_up, conv_w, conv_b, w_ffn_down, final_norm_g, loss_target, m_attn_norm_g, m_w_in, m_w_gate_up, m_b_gate_up, m_sb_out_g, m_gla_out_g, m_w_out, m_ffn_norm_g, m_w_ffn_up, m_conv_w, m_conv_b, m_w_ffn_down, m_final_norm_g, v_attn_norm_g, v_w_in, v_w_gate_up, v_b_gate_up, v_sb_out_g, v_gla_out_g, v_w_out, v_ffn_norm_g, v_w_ffn_up, v_conv_w, v_conv_b, v_w_ffn_down, v_final_norm_g):
    Bd, S, _ = x.shape
    T = Bd * S
    x2d = x.reshape(T, D)
    tgt = loss_target.reshape(T, D)
    c_up = w_ffn_up.shape[2]
    c_gu = w_gate_up.shape[2]
    c_in = w_in.shape[2]

    n_gu = GATE_RANK * c_gu
    rows_bf = lambda w: w[0].T.astype(BF16)
    small_w = lambda wgu, cw: _flat_pad_rows([wgu, cw], SHARD_SMALL_ROWS)

    g_in = _gather_two_level(rows_bf(w_in), "gather_w_in")
    w_in_pt = jnp.pad(g_in.reshape(IN_COLS, D), ((0, 1), (0, 0)))[_PERM]
    g3 = final_norm_g.reshape(1, D)

    proj, h1 = _norm_proj(x2d, attn_norm_g, w_in_pt)
    o_sb, tt, g_up, g_down, g_out, gs = _sb_fwd(
        proj, S, [rows_bf(w_ffn_up), w_ffn_down[0].astype(BF16), w_out[0].astype(BF16),
                  _flat_pad_rows([w_gate_up, conv_w], 8)])
    w_out_f = g_out.reshape(D, D)
    gsf = gs.reshape(N_DEV, -1)
    wgu_f = jnp.transpose(gsf[:, :n_gu].reshape(N_DEV, GATE_RANK, c_gu), (1, 0, 2)).reshape(GATE_RANK, GLA_KW)
    cw_f = jnp.transpose(gsf[:, n_gu:n_gu + 3 * c_up].reshape(N_DEV, 3, c_up), (1, 0, 2)).reshape(3, 2 * D_FF)
    wgu_p = jnp.pad(wgu_f, ((0, LANE - GATE_RANK), (0, 0)))
    w_up_t = g_up.reshape(2 * D_FF, D)
    w_down_f = g_down.reshape(D_FF, D)
    o_gla = _gla_fwd(proj, wgu_p, b_gate_up, S)
    x1, ocat, h2 = _mix_out(o_sb, o_gla, proj, x2d, sb_out_g, gla_out_g, w_out_f, ffn_norm_g)
    hup = _mm(h2, w_up_t, "nt", "ffn_up", tm=512, tn=1408, tk=1024)
    act = _conv_gate(hup, cw_f, conv_b, S)
    dx2, dg3, loss_dev = _down_loss(act, w_down_f, x1, tgt, g3)

    dw_down = _mm(act, dx2, "tn", "dw_down", out_dtype=BF16, tm=D_FF, tn=1024, tk=512)
    dact = _mm(dx2, w_down_f, "nt", "dact", tm=512, tn=1408, tk=1024)
    dhup_a, dhup_v, dcw_a, dcw_v, dcb_a, dcb_v = _conv_gate_bwd(hup, dact, cw_f, conv_b, S)
    dw_up_t = _mm(dhup_a, h2, "tn", "dw_up_a", out_dtype=BF16, tm=D_FF, tn=1024, tk=512, out_rows=2 * D_FF)
    dw_up_t = _mm(dhup_v, h2, "tn", "dw_up_v", out_dtype=BF16, tm=D_FF, tn=1024, tk=512, out_rows=2 * D_FF,
                  out_row0=D_FF, into=dw_up_t)
    dh2 = _mm(dhup_a, w_up_t, "nn", "dh2_a", tm=1024, tn=1024, tk=1408)
    dx1, dg2 = _mm(dhup_v, w_up_t, "nn", "dh2_v", c=dh2, tm=512, tn=1024, tk=1408, b_row0=D_FF,
                   norm_bwd=(x1, ffn_norm_g, dx2))

    dw_out = _mm(ocat, dx1, "tn", "dw_out", out_dtype=BF16, tm=1024, tn=1024, tk=512)
    docat = _mm(dx1, w_out_f, "nt", "docat", tm=512, tn=1024, tk=1024)
    do_sb, do_gla, dproj, dg_sb, dg_gla = _mix_bwd(docat, o_sb, o_gla, proj, sb_out_g, gla_out_g)
    dproj, got_up, got_down, got_out = _sb_bwd(
        proj, tt, do_sb, dproj, S,
        [dw_up_t.reshape(N_DEV, c_up, D), dw_down.reshape(N_DEV, -1, D), dw_out.reshape(N_DEV, -1, D)])
    dproj, dpre = _gla_bwd(proj, wgu_p, b_gate_up, do_gla, dproj, S)
    dproj, dwgu, dbgu = _gate_bwd(dpre, proj, wgu_p, dproj)
    dw_in_pt = _mm(dproj, h1, "tn", "dw_in", out_dtype=BF16, tm=PROJ_W, tn=1024, tk=512)
    dx, dg1, got_in = _mm(dproj, w_in_pt, "nn", "dh1", tm=1024, tn=1024, tk=640,
                          xchg=([dw_in_pt[_INV_PERM].reshape(N_DEV, c_in, D)], False),
                          norm_bwd=(x2d, attn_norm_g, dx1))

    dcw = jnp.concatenate([dcw_a, dcw_v], axis=1)
    dwgu_pc = jnp.transpose(dwgu[:GATE_RANK].reshape(GATE_RANK, N_DEV, c_gu), (1, 0, 2)).reshape(N_DEV, -1)
    dcw_pc = jnp.transpose(dcw.reshape(3, N_DEV, c_up), (1, 0, 2)).reshape(N_DEV, -1)
    small_pc = jnp.concatenate([dwgu_pc, dcw_pc], axis=1)
    small_pc = jnp.pad(small_pc, ((0, 0), (0, SHARD_SMALL_ROWS * D - small_pc.shape[1])))
    small_pc = small_pc.reshape(N_DEV, SHARD_SMALL_ROWS, D).astype(BF16)
    rep_names = ["attn_norm_g", "b_gate_up", "sb_out_g", "gla_out_g", "ffn_norm_g", "conv_b", "final_norm_g"]
    rep_g = [dg1, dbgu, dg_sb, dg_gla, dg2, jnp.concatenate([dcb_a, dcb_v], axis=1), dg3]
    rep_w = [attn_norm_g, b_gate_up, sb_out_g, gla_out_g, ffn_norm_g, conv_b, final_norm_g]
    rep_m = [m_attn_norm_g, m_b_gate_up, m_sb_out_g, m_gla_out_g, m_ffn_norm_g, m_conv_b, m_final_norm_g]
    rep_v = [v_attn_norm_g, v_b_gate_up, v_sb_out_g, v_gla_out_g, v_ffn_norm_g, v_conv_b, v_final_norm_g]
    rep_pc = jnp.broadcast_to(_flat_pad_rows(rep_g, SMALL_ROWS), (N_DEV, SMALL_ROWS, D))
    got_sm, got_rep = _exchange([small_pc, rep_pc], False, "scatter_tail")

    rows = lambda w: w[0].T
    cols = lambda r: r.T[None]
    res = {}
    res["w_in"] = [cols(r) for r in _sum_adamw(got_in, rows(w_in), rows(m_w_in), rows(v_w_in), c_in, "adamw_w_in")]
    res["w_out"] = [r[None] for r in _sum_adamw(got_out, w_out[0], m_w_out[0], v_w_out[0], w_out.shape[1],
                                                 "adamw_w_out")]
    res["w_ffn_up"] = [cols(r) for r in _sum_adamw(got_up, rows(w_ffn_up), rows(m_w_ffn_up), rows(v_w_ffn_up),
                                                    c_up // 2, "adamw_w_up")]
    res["w_ffn_down"] = [r[None] for r in _sum_adamw(got_down, w_ffn_down[0], m_w_ffn_down[0], v_w_ffn_down[0],
                                                      w_ffn_down.shape[1], "adamw_w_down")]
    sm = _sum_adamw(got_sm, small_w(w_gate_up, conv_w), small_w(m_w_gate_up, m_conv_w),
                    small_w(v_w_gate_up, v_conv_w), SHARD_SMALL_ROWS, "adamw_small_sharded")
    res["w_gate_up"] = [r.reshape(-1)[:n_gu].reshape(1, GATE_RANK, c_gu) for r in sm]
    res["conv_w"] = [r.reshape(-1)[n_gu:n_gu + 3 * c_up].reshape(1, 3, c_up) for r in sm]
    rep = _sum_adamw(got_rep, _flat_pad_rows(rep_w, SMALL_ROWS), _flat_pad_rows(rep_m, SMALL_ROWS),
                     _flat_pad_rows(rep_v, SMALL_ROWS), SMALL_ROWS, "adamw_replicated")
    o = 0
    for n, w in zip(rep_names, rep_w):
        res[n] = [r.reshape(-1)[o:o + w.size].reshape(w.shape) for r in rep]
        o += w.size

    loss = lax.psum(loss_dev[0, 0], ("x", "y", "c"))
    order = ["attn_norm_g", "w_in", "w_gate_up", "b_gate_up", "sb_out_g", "gla_out_g", "w_out", "ffn_norm_g",
             "w_ffn_up", "conv_w", "conv_b", "w_ffn_down", "final_norm_g"]
    outs = [loss, dx.reshape(Bd, S, D)]
    for k in range(4):
        outs += [res[n][k] for n in order]
    return tuple(outs)
```

```python
import functools

import numpy as np
import jax
import jax.numpy as jnp
from jax import lax
from jax.experimental import pallas as pl
from jax.experimental.pallas import tpu as pltpu

F32 = jnp.float32
BF16 = jnp.bfloat16
SDS = jax.ShapeDtypeStruct
BS = pl.BlockSpec

N_DEV = 8
D = 1024
EPS = 1e-6
SB_HD = 64
SB_W = 512
GLA_DK = 64
GLA_DV = 128
GLA_KW = 256
GLA_W = 512
GATE_RANK = 16
GATE_NORM = 16.0
CHUNK = 64
GLA_G = 4
GR = GLA_G * CHUNK
QT = 256
D_FF = 2816
IN_COLS = 3088
PROJ_W = 3200
LANE = 128
VMEM_LIMIT = 56 * 1024 * 1024

ADAM_LR, ADAM_B1, ADAM_B2, ADAM_EPS, ADAM_WD, ADAM_STEP = 0.001, 0.9, 0.999, 1e-08, 0.01, 10


def _proj_perm():
    sbq, sbk, sbv = 0, 512, 1024
    gq, gk, gv, glr, gog = 1536, 1792, 2048, 2560, 2576
    cols = []
    for p in range(4):
        for base in (sbq, sbk, sbv):
            cols += list(range(base + 128 * p, base + 128 * p + 128))
    for p in range(2):
        cols += list(range(gq + 128 * p, gq + 128 * p + 128))
        cols += list(range(gk + 128 * p, gk + 128 * p + 128))
        cols += list(range(gv + 256 * p, gv + 256 * p + 256))
    cols += list(range(gog, gog + 512))
    cols += list(range(glr, glr + GATE_RANK)) + [IN_COLS] * (LANE - GATE_RANK)
    perm = np.asarray(cols, np.int32)
    inv = np.zeros((IN_COLS,), np.int32)
    for new, old in enumerate(cols):
        if old < IN_COLS:
            inv[old] = new
    return perm, inv


_PERM, _INV_PERM = _proj_perm()
OG_BLK = 5
GLR_BLK = 24


def _cp(sem=None, vmem=VMEM_LIMIT):
    return pltpu.CompilerParams(dimension_semantics=sem, vmem_limit_bytes=vmem)


def _dot(a, b):
    return lax.dot_general(a, b, (((1,), (0,)), ((), ())), preferred_element_type=F32)


def _dot_nt(a, b):
    return lax.dot_general(a, b, (((1,), (1,)), ((), ())), preferred_element_type=F32)


def _dot_tn(a, b):
    return lax.dot_general(a, b, (((0,), (0,)), ((), ())), preferred_element_type=F32)


def _bf(x):
    return x.astype(BF16)


def _split_dot(x, m, passes, left=True):
    acc = None
    r = x
    for i in range(passes):
        h = r.astype(BF16)
        t = _dot(h, m) if left else _dot(m, h)
        acc = t if acc is None else acc + t
        if i + 1 < passes:
            r = r - h.astype(F32)
    return acc


def _softplus(z):
    return jnp.maximum(z, 0.0) + jnp.log(1.0 + jnp.exp(-jnp.abs(z)))


def _rms_bwd_math(x, g, dh, dres):
    r = lax.rsqrt(jnp.mean(x * x, axis=-1, keepdims=True) + EPS)
    xh = x * r
    dxh = dh * g
    dx = dres + r * (dxh - xh * jnp.mean(dxh * xh, axis=-1, keepdims=True))
    return dx, jnp.sum(dh * xh, axis=0, keepdims=True)


def _tile(n, pref, mult=LANE):
    best = None
    for t in range(mult, min(n, pref) + 1, mult):
        if n % t == 0:
            best = t
    return best if best is not None else n


def _mm(a, b, mode, name, out_dtype=F32, c=None, tm=512, tn=512, tk=512, b_row0=0, out_rows=None, out_row0=0,
        into=None, xchg=None, norm_bwd=None):
    if mode == "nn":
        (M, K), N = a.shape, b.shape[1]
    elif mode == "nt":
        (M, K), N = a.shape, b.shape[0]
    else:
        (K, M), N = a.shape, b.shape[1]
    tm, tn, tk = _tile(M, tm), _tile(N, tn), _tile(K, tk)
    nk = K // tk
    kb0, ob0 = b_row0 // tk, out_row0 // tm
    assert kb0 * tk == b_row0 and ob0 * tm == out_row0 and (mode == "nn" or b_row0 == 0)
    ni, nj = M // tm, N // tn
    j_outer = nk == 1 and (nj - 1) * a.size * a.dtype.itemsize < (ni - 1) * K * N * b.dtype.itemsize
    ix = (lambda f: (lambda j, i, k: f(i, j, k))) if j_outer else (lambda f: f)
    a_spec = BS((tk, tm), ix(lambda i, j, k: (k, i))) if mode == "tn" else BS((tm, tk), ix(lambda i, j, k: (i, k)))
    b_spec = (BS((tn, tk), ix(lambda i, j, k: (j, k))) if mode == "nt"
              else BS((tk, tn), ix(lambda i, j, k: (k + kb0, j))))
    dotfn = {"nn": _dot, "nt": _dot_nt, "tn": _dot_tn}[mode]
    has_c = c is not None
    has_into = into is not None
    nx = 0 if xchg is None else len(xchg[0])
    has_nb = norm_bwd is not None
    assert not has_nb or (nj == 1 and not j_outer and out_dtype == F32)
    n_in = 2 + has_c + has_into + 3 * has_nb

    def body(*refs):
        a_ref, b_ref = refs[:2]
        c_ref = refs[2] if has_c else None
        x_src = refs[n_in:n_in + nx]
        outs = refs[n_in + nx:n_in + 2 * nx + 1 + has_nb]
        o_ref, x_out = outs[0], outs[1 + has_nb:]
        acc = refs[n_in + 2 * nx + 1 + has_nb]
        sems = refs[n_in + 2 * nx + 2 + has_nb:]
        k = pl.program_id(2)
        g0, g1 = pl.program_id(0), pl.program_id(1)
        n0, n1 = (nj, ni) if j_outer else (ni, nj)
        first = jnp.logical_and(jnp.logical_and(g0 == 0, g1 == 0), k == 0)
        if nx:
            @pl.when(first)
            def _():
                _exchange_ops(x_src, x_out, xchg[1], sems, "start")

        if has_nb:
            @pl.when(first)
            def _():
                outs[1][...] = jnp.zeros_like(outs[1])

        @pl.when(k == 0)
        def _():
            acc[...] = jnp.zeros_like(acc)

        acc[...] += dotfn(_bf(a_ref[...]), _bf(b_ref[...]))

        @pl.when(k == nk - 1)
        def _():
            r = acc[...]
            if has_c:
                r = r + c_ref[...]
            if has_nb:
                x_ref, g_ref, dres_ref = refs[n_in - 3:n_in]
                dx, dg = _rms_bwd_math(x_ref[...], g_ref[...], r, dres_ref[...])
                o_ref[...] = dx
                outs[1][...] += dg
            else:
                o_ref[...] = r.astype(out_dtype)

        if nx:
            @pl.when(jnp.logical_and(jnp.logical_and(g0 == n0 - 1, g1 == n1 - 1), k == nk - 1))
            def _():
                _exchange_ops(x_src, x_out, xchg[1], sems, "wait")

    tile = BS((tm, tn), ix(lambda i, j, k: (i, j)))
    in_specs = [a_spec, b_spec]
    args = [a, b]
    if has_c:
        in_specs.append(tile)
        args.append(c)
    aliases = {}
    if has_into:
        aliases = {len(args): 0}
        in_specs.append(BS(memory_space=pl.ANY))
        args.append(into)
    out_shape = [SDS((out_rows or M, N), out_dtype)]
    out_specs = [BS((tm, tn), ix(lambda i, j, k: (i + ob0, j)))]
    scratch = [pltpu.VMEM((tm, tn), F32)]
    if has_nb:
        in_specs += [tile, BS((1, tn), lambda i, j, k: (0, 0)), tile]
        args += list(norm_bwd)
        out_shape.append(SDS((1, N), F32))
        out_specs.append(BS((1, tn), lambda i, j, k: (0, 0)))
    if nx:
        hbm = BS(memory_space=pltpu.HBM)
        in_specs += [hbm] * nx
        args += list(xchg[0])
        out_shape += [SDS((N_DEV,) + s.shape if xchg[1] else s.shape, s.dtype) for s in xchg[0]]
        out_specs += [hbm] * nx
        scratch += _exchange_sems(nx)
    serial = nx or has_nb
    res = pl.pallas_call(
        body, name=name, out_shape=tuple(out_shape), grid=(nj, ni, nk) if j_outer else (ni, nj, nk),
        in_specs=in_specs, out_specs=tuple(out_specs),
        scratch_shapes=scratch, input_output_aliases=aliases,
        compiler_params=_cp(("arbitrary",) * 3 if serial else ("parallel", "parallel", "arbitrary")),
    )(*args)
    return res if serial else res[0]


def _norm_proj(x, g, w):
    T, N = x.shape[0], w.shape[0]
    tm = _tile(T, 512)

    def body(x_ref, g_ref, w_ref, p_ref, h_ref):
        xv = x_ref[...]
        r = lax.rsqrt(jnp.mean(xv * xv, axis=-1, keepdims=True) + EPS)
        h = _bf((xv * r) * g_ref[...])
        h_ref[...] = h
        p_ref[...] = _dot_nt(h, w_ref[...])

    return pl.pallas_call(
        body, name="norm_proj", out_shape=(SDS((T, N), F32), SDS((T, D), BF16)), grid=(T // tm,),
        in_specs=[BS((tm, D), lambda i: (i, 0)), BS((1, D), lambda i: (0, 0)), BS((N, D), lambda i: (0, 0))],
        out_specs=(BS((tm, N), lambda i: (i, 0)), BS((tm, D), lambda i: (i, 0))),
        compiler_params=_cp(("parallel",)),
    )(x, g, w)


TK = 256
SB_DEAD = -104.0
SB_MASKED = -1e30
CNT_LANE = SB_HD - 1


def _sb_masks():
    row = lax.broadcasted_iota(jnp.int32, (2 * QT, TK), 0) & (QT - 1)
    col = lax.broadcasted_iota(jnp.int32, (2 * QT, TK), 1)
    lane = lax.broadcasted_iota(jnp.int32, (1, LANE), 1)
    kr = lax.broadcasted_iota(jnp.int32, (TK, TK), 0)
    kc = lax.broadcasted_iota(jnp.int32, (TK, TK), 1)
    return row, col, lane, kr, kc


def _stack_heads(x, lane):
    return jnp.concatenate([_bf(jnp.where((lane // SB_HD) == hh, x, 0.0)) for hh in range(2)], axis=0)


def _sb_fwd(proj, S, shards):
    T = proj.shape[0]
    nq = S // QT
    scale = SB_HD ** -0.5
    nb, ns = T // S, len(shards)

    def body(qkv_ref, *rest):
        sh_refs, (o_ref, tt_ref), g_refs = rest[:ns], rest[ns:ns + 2], rest[ns + 2:2 * ns + 2]
        sems = rest[2 * ns + 2:]
        first = jnp.logical_and(pl.program_id(0) == 0, pl.program_id(1) == 0)
        last = jnp.logical_and(pl.program_id(0) == nb - 1, pl.program_id(1) == 3)

        @pl.when(first)
        def _():
            _exchange_ops(sh_refs, g_refs, True, sems, "start")

        row, col, lane, kr, kc = _sb_masks()
        msuf = _bf(kr > kc)

        def qloop(qt, _):
            r0 = pl.multiple_of(qt * QT, QT)
            qs = _stack_heads(qkv_ref[pl.ds(r0, QT), 0:128] * scale, lane)

            def live(st):
                it, _, cy = st
                return jnp.logical_and(it <= qt, jnp.max(cy) > SB_DEAD)

            def step(st):
                it, acc, cy = st
                kt = qt - it
                k0 = pl.multiple_of(kt * TK, TK)
                kv = _bf(qkv_ref[pl.ds(k0, TK), 128:256])
                vv = _bf(qkv_ref[pl.ds(k0, TK), 256:384])
                strict = (col + (kt - qt) * TK) < row
                z = jnp.where(strict, _dot_nt(qs, kv), SB_MASKED)
                sp = _softplus(z)
                lg = -sp
                after = cy + _split_dot(lg, msuf, 2)
                w = jnp.exp((z - sp) + after)
                return it + 1, acc + _dot(_bf(w), vv), cy + jnp.sum(lg, axis=1, keepdims=True)

            it, acc, cy = lax.while_loop(
                live, step, (jnp.int32(0), jnp.zeros((2 * QT, LANE), F32), jnp.zeros((2 * QT, 1), F32)))
            o_ref[pl.ds(r0, QT), :] = jnp.where(lane < SB_HD, acc[:QT], acc[QT:])
            tt = jnp.where(lane < SB_HD, cy[:QT], cy[QT:])
            tt_ref[pl.ds(r0, QT), :] = jnp.where(lane == CNT_LANE, it.astype(F32), tt)
            return 0

        lax.fori_loop(0, nq, qloop, 0)

        @pl.when(last)
        def _():
            _exchange_ops(sh_refs, g_refs, True, sems, "wait")

    hbm = BS(memory_space=pltpu.HBM)
    col_spec = BS((S, LANE), lambda b, p: (b, p))
    return pl.pallas_call(
        body, name="sb_fwd",
        out_shape=(SDS((T, SB_W), F32), SDS((T, SB_W), F32)) + tuple(SDS((N_DEV,) + s.shape, s.dtype) for s in shards),
        grid=(nb, 4),
        in_specs=[BS((S, 384), lambda b, p: (b, p))] + [hbm] * ns,
        out_specs=(col_spec, col_spec) + (hbm,) * ns,
        scratch_shapes=_exchange_sems(ns),
        compiler_params=_cp(("arbitrary", "arbitrary")),
    )(proj, *shards)


def _log_sigmoid(x):
    return jnp.minimum(x, 0.0) - jnp.log1p(jnp.exp(-jnp.abs(x)))


def _gla_masks():
    r = lax.broadcasted_iota(jnp.int32, (GR, GR), 0)
    c = lax.broadcasted_iota(jnp.int32, (GR, GR), 1)
    same = (r // CHUNK) == (c // CHUNK)
    causal = jnp.logical_and(same, r >= c)
    lane = lax.broadcasted_iota(jnp.int32, (1, LANE), 1)
    return causal, _bf(causal), _bf(jnp.logical_and(same, r <= c)), lane


def _gla_group_terms(blk_ref, glr_ref, wgu, bgu, r0, tri_incl):
    q = blk_ref[pl.ds(r0, GR), 0:128]
    k = blk_ref[pl.ds(r0, GR), 128:256]
    v = blk_ref[pl.ds(r0, GR), 256:512]
    pre = _dot(_bf(glr_ref[pl.ds(r0, GR), :]), wgu) + bgu
    la = _log_sigmoid(pre) / GATE_NORM
    b = _split_dot(la, tri_incl, 3, left=False)
    b_last = _per_chunk(lambda rows: b[rows.stop - 1:rows.stop])
    eb = jnp.exp(b)
    qd = (q * (GLA_DK ** -0.5)) * eb
    ki = k * jnp.exp(-b)
    ke = k * jnp.exp(b_last - b)
    decay = jnp.exp(b_last)
    return v, pre, b, b_last, eb, qd, ki, ke, decay


def _chunk_rows(n):
    return slice(n * CHUNK, (n + 1) * CHUNK)


def _per_chunk(row_fn):
    return jnp.concatenate(
        [jnp.broadcast_to(row_fn(_chunk_rows(n)), (CHUNK, LANE)) for n in range(GLA_G)], axis=0)


def _gla_fwd(proj, wgu, bgu, S):
    T = proj.shape[0]
    ng = S // GR

    def body(blk_ref, glr_ref, wgu_ref, bgu_ref, o_ref):
        causal, tri_incl, _, lane = _gla_masks()
        wg = _bf(wgu_ref[...])
        bg = bgu_ref[...]

        def group(g, states):
            r0 = pl.multiple_of(g * GR, GR)
            v, _, _, _, _, qd, ki, ke, decay = _gla_group_terms(blk_ref, glr_ref, wg, bg, r0, tri_incl)
            kib, keb = _bf(ki), _bf(ke)
            new_states, outs = [], []
            for hh in range(2):
                hm = (lane // GLA_DK) == hh
                qm = _bf(jnp.where(hm, qd, 0.0))
                vh = _bf(v[:, 128 * hh:128 * hh + 128])
                attn = jnp.where(causal, _dot_nt(qm, kib), 0.0)
                o_intra = _dot(_bf(attn), vh)
                st = states[hh]
                parts = []
                for n in range(GLA_G):
                    rows = _chunk_rows(n)
                    parts.append(o_intra[rows] + _dot_nt(qm[rows], _bf(st)))
                    st = st * decay[n * CHUNK:n * CHUNK + 1] + _dot_tn(vh[rows], keb[rows])
                outs.append(jnp.concatenate(parts, axis=0))
                new_states.append(st)
            o_ref[pl.ds(r0, GR), :] = jnp.concatenate(outs, axis=1)
            return tuple(new_states)

        z = jnp.zeros((GLA_DV, LANE), F32)
        lax.fori_loop(0, ng, group, (z, z))

    return pl.pallas_call(
        body, name="gla_fwd", out_shape=SDS((T, GLA_W), F32), grid=(T // S, 2),
        in_specs=[BS((S, 512), lambda b, p: (b, 3 + p)), BS((S, LANE), lambda b, p: (b, GLR_BLK)),
                  BS((LANE, LANE), lambda b, p: (0, p)), BS((1, LANE), lambda b, p: (0, p))],
        out_specs=BS((S, 256), lambda b, p: (b, p)),
        compiler_params=_cp(("parallel", "parallel")),
    )(proj, proj, wgu, bgu)


def _head_blockdiag(width, hd):
    r = lax.broadcasted_iota(jnp.int32, (width, width), 0) // hd
    c = lax.broadcasted_iota(jnp.int32, (width, width), 1) // hd
    return _bf(r == c)


def _mix_out(o_sb, o_gla, proj, x, g_sb, g_gla, w_out, g2):
    T = x.shape[0]
    tm = _tile(T, 512)

    def body(osb_ref, ogl_ref, og_ref, x_ref, gsb_ref, ggl_ref, w_ref, g2_ref, x1_ref, oc_ref, h2_ref):
        bd64 = _head_blockdiag(SB_W, SB_HD)
        bd128 = _head_blockdiag(GLA_W, GLA_DV)
        o = osb_ref[...]
        r = lax.rsqrt(_split_dot(o * o, bd64, 2) * (1.0 / SB_HD) + EPS)
        c_sb = (o * r) * gsb_ref[...]
        o = ogl_ref[...]
        r = lax.rsqrt(_split_dot(o * o, bd128, 2) * (1.0 / GLA_DV) + EPS)
        og = og_ref[...]
        c_gl = ((o * r) * ggl_ref[...]) * (og * jax.nn.sigmoid(og))
        oc = _bf(jnp.concatenate([c_sb, c_gl], axis=1))
        oc_ref[...] = oc
        x1 = x_ref[...] + _dot(oc, w_ref[...])
        x1_ref[...] = x1
        r2 = lax.rsqrt(jnp.mean(x1 * x1, axis=-1, keepdims=True) + EPS)
        h2_ref[...] = _bf((x1 * r2) * g2_ref[...])

    row = lambda w: BS((tm, w), lambda i: (i, 0))
    vec = lambda w: BS((1, w), lambda i: (0, 0))
    return pl.pallas_call(
        body, name="mix_out", out_shape=(SDS((T, D), F32), SDS((T, D), BF16), SDS((T, D), BF16)), grid=(T // tm,),
        in_specs=[row(SB_W), row(GLA_W), BS((tm, 512), lambda i: (i, OG_BLK)), row(D), vec(SB_W), vec(GLA_W),
                  BS((D, D), lambda i: (0, 0)), vec(D)],
        out_specs=(row(D), row(D), row(D)),
        compiler_params=_cp(("parallel",)),
    )(o_sb, o_gla, proj, x, g_sb, g_gla, w_out, g2)


CONV_ROWS = 256
CONV_TC = 256


def _rows_before(ref, r0, first):
    prev = ref[pl.ds(pl.multiple_of(jnp.maximum(r0 - 8, 0), 8), 8), :]
    return jnp.where(first, 0.0, prev)


def _shift_down(cur, prev8, k):
    cat = jnp.concatenate([prev8, cur], axis=0)
    return pltpu.roll(cat, k, 0)[8:]


def _shift_up(cur, next8, k):
    cat = jnp.concatenate([cur, next8], axis=0)
    return pltpu.roll(cat, cat.shape[0] - k, 0)[:cur.shape[0]]


def _conv_at(h_ref, cw, cb, r0, rows, first):
    cur = h_ref[pl.ds(r0, rows), :]
    prev8 = _rows_before(h_ref, r0, first)
    u = cb + cw[0:1, :] * _shift_down(cur, prev8, 2)
    u = u + cw[1:2, :] * _shift_down(cur, prev8, 1)
    return u + cw[2:3, :] * cur


NJ = D_FF // CONV_TC


def _conv_gate(hup, cw, cb, S):
    T = hup.shape[0]
    rows = min(CONV_ROWS, S)
    nr = S // rows

    def body(ha_ref, hv_ref, cwa_ref, cwv_ref, cba_ref, cbv_ref, act_ref):
        cwa, cwv, cba, cbv = cwa_ref[...], cwv_ref[...], cba_ref[...], cbv_ref[...]

        def step(c, _):
            r0 = pl.multiple_of(c * rows, rows)
            ua = _conv_at(ha_ref, cwa, cba, r0, rows, c == 0)
            uv = _conv_at(hv_ref, cwv, cbv, r0, rows, c == 0)
            act_ref[pl.ds(r0, rows), :] = _bf((ua * jax.nn.sigmoid(ua)) * uv)
            return 0

        lax.fori_loop(0, nr, step, 0)

    blk = lambda o: BS((S, CONV_TC), lambda b, j: (b, j + o))
    w3 = lambda o: BS((3, CONV_TC), lambda b, j: (0, j + o))
    w1 = lambda o: BS((1, CONV_TC), lambda b, j: (0, j + o))
    return pl.pallas_call(
        body, name="conv_gate", out_shape=SDS((T, D_FF), BF16), grid=(T // S, NJ),
        in_specs=[blk(0), blk(NJ), w3(0), w3(NJ), w1(0), w1(NJ)], out_specs=blk(0),
        compiler_params=_cp(("parallel", "parallel")),
    )(hup, hup, cw, cw, cb, cb)


def _down_loss(act, w_down, x1, tgt, g3):
    T = x1.shape[0]
    tm = _tile(T, 512)

    def body(a_ref, w_ref, x1_ref, t_ref, g_ref, dx_ref, dg_ref, ls_ref):
        @pl.when(pl.program_id(0) == 0)
        def _():
            dg_ref[...] = jnp.zeros_like(dg_ref)
            ls_ref[...] = jnp.zeros_like(ls_ref)

        g = g_ref[...]
        x2 = x1_ref[...] + _dot(a_ref[...], w_ref[...])
        r = lax.rsqrt(jnp.mean(x2 * x2, axis=-1, keepdims=True) + EPS)
        xh = x2 * r
        e = xh * g - t_ref[...]
        ls_ref[...] += 0.5 * jnp.sum(jnp.mean(e * e, axis=-1, keepdims=True), axis=0, keepdims=True)
        dy = e * (1.0 / D)
        dxh = dy * g
        dx_ref[...] = r * (dxh - xh * jnp.mean(dxh * xh, axis=-1, keepdims=True))
        dg_ref[...] += jnp.sum(dy * xh, axis=0, keepdims=True)

    row = lambda w: BS((tm, w), lambda i: (i, 0))
    return pl.pallas_call(
        body, name="down_loss", out_shape=(SDS((T, D), F32), SDS((1, D), F32), SDS((1, LANE), F32)), grid=(T // tm,),
        in_specs=[row(D_FF), BS((D_FF, D), lambda i: (0, 0)), row(D), row(D), BS((1, D), lambda i: (0, 0))],
        out_specs=(row(D), BS((1, D), lambda i: (0, 0)), BS((1, LANE), lambda i: (0, 0))),
        compiler_params=_cp(("arbitrary",)),
    )(act, w_down, x1, tgt, g3)


def _conv_gate_bwd(hup, dact, cw, cb, S):
    T = hup.shape[0]
    rows = min(CONV_ROWS, S)
    nr = S // rows

    def body(ha_ref, hv_ref, da_ref, cwa_ref, cwv_ref, cba_ref, cbv_ref,
             dha_ref, dhv_ref, dcwa_ref, dcwv_ref, dcba_ref, dcbv_ref):
        @pl.when(pl.program_id(1) == 0)
        def _():
            for r in (dcwa_ref, dcwv_ref, dcba_ref, dcbv_ref):
                r[...] = jnp.zeros_like(r)

        cwa, cwv, cba, cbv = cwa_ref[...], cwv_ref[...], cba_ref[...], cbv_ref[...]

        def du_at(r0, n, first):
            ua = _conv_at(ha_ref, cwa, cba, r0, n, first)
            uv = _conv_at(hv_ref, cwv, cbv, r0, n, first)
            da = da_ref[pl.ds(r0, n), :]
            sg = jax.nn.sigmoid(ua)
            dua = (da * uv) * (sg * (1.0 + ua * (1.0 - sg)))
            duv = da * (ua * sg)
            return dua, duv

        def step(c, _):
            r0 = pl.multiple_of(c * rows, rows)
            first, last = c == 0, c == nr - 1
            dua, duv = du_at(r0, rows, first)
            n0 = pl.multiple_of(jnp.minimum(r0 + rows, S - 8), 8)
            nua, nuv = du_at(n0, 8, False)
            nua = jnp.where(last, 0.0, nua)
            nuv = jnp.where(last, 0.0, nuv)
            for (h_ref, cw, du, nu, dh_ref, dcw_ref, dcb_ref) in (
                    (ha_ref, cwa, dua, nua, dha_ref, dcwa_ref, dcba_ref),
                    (hv_ref, cwv, duv, nuv, dhv_ref, dcwv_ref, dcbv_ref)):
                dh = cw[2:3, :] * du + cw[1:2, :] * _shift_up(du, nu, 1) + cw[0:1, :] * _shift_up(du, nu, 2)
                dh_ref[pl.ds(r0, rows), :] = _bf(dh)
                cur = h_ref[pl.ds(r0, rows), :]
                prev8 = _rows_before(h_ref, r0, first)
                dcw_ref[0:1, :] += jnp.sum(du * _shift_down(cur, prev8, 2), axis=0, keepdims=True)
                dcw_ref[1:2, :] += jnp.sum(du * _shift_down(cur, prev8, 1), axis=0, keepdims=True)
                dcw_ref[2:3, :] += jnp.sum(du * cur, axis=0, keepdims=True)
                dcb_ref[...] += jnp.sum(du, axis=0, keepdims=True)
            return 0

        lax.fori_loop(0, nr, step, 0)

    blk = lambda o: BS((S, CONV_TC), lambda j, b: (b, j + o))
    w3 = lambda o: BS((3, CONV_TC), lambda j, b: (0, j + o))
    w1 = lambda o: BS((1, CONV_TC), lambda j, b: (0, j + o))
    return pl.pallas_call(
        body, name="conv_gate_bwd",
        out_shape=(SDS((T, D_FF), BF16), SDS((T, D_FF), BF16), SDS((3, D_FF), F32), SDS((3, D_FF), F32),
                   SDS((1, D_FF), F32), SDS((1, D_FF), F32)),
        grid=(NJ, T // S),
        in_specs=[blk(0), blk(NJ), blk(0), w3(0), w3(NJ), w1(0), w1(NJ)],
        out_specs=(blk(0), blk(0), w3(0), w3(0), w1(0), w1(0)),
        compiler_params=_cp(("parallel", "arbitrary")),
    )(hup, hup, dact, cw, cw, cb, cb)


def _mix_bwd(docat, o_sb, o_gla, proj, g_sb, g_gla):
    T = docat.shape[0]
    tm = _tile(T, 512)

    def body(d_ref, osb_ref, ogl_ref, og_ref, gsb_ref, ggl_ref, dsb_ref, dgl_ref, dog_ref, dgsb_ref, dggl_ref):
        @pl.when(pl.program_id(0) == 0)
        def _():
            dgsb_ref[...] = jnp.zeros_like(dgsb_ref)
            dggl_ref[...] = jnp.zeros_like(dggl_ref)

        bd64 = _head_blockdiag(SB_W, SB_HD)
        bd128 = _head_blockdiag(GLA_W, GLA_DV)
        d = d_ref[:, 0:SB_W]
        o = osb_ref[...]
        r = lax.rsqrt(_split_dot(o * o, bd64, 2) * (1.0 / SB_HD) + EPS)
        n = o * r
        dn = d * gsb_ref[...]
        dgsb_ref[...] += jnp.sum(d * n, axis=0, keepdims=True)
        dsb_ref[...] = r * (dn - n * (_split_dot(dn * n, bd64, 2) * (1.0 / SB_HD)))

        d = d_ref[:, SB_W:D]
        o = ogl_ref[...]
        r = lax.rsqrt(_split_dot(o * o, bd128, 2) * (1.0 / GLA_DV) + EPS)
        n = o * r
        og = og_ref[...]
        sg = jax.nn.sigmoid(og)
        dm = d * (og * sg)
        dog_ref[...] = _bf((d * (n * ggl_ref[...])) * (sg * (1.0 + og * (1.0 - sg))))
        dn = dm * ggl_ref[...]
        dggl_ref[...] += jnp.sum(dm * n, axis=0, keepdims=True)
        dgl_ref[...] = r * (dn - n * (_split_dot(dn * n, bd128, 2) * (1.0 / GLA_DV)))

    row = lambda w: BS((tm, w), lambda i: (i, 0))
    vec = lambda w: BS((1, w), lambda i: (0, 0))
    ogb = BS((tm, 512), lambda i: (i, OG_BLK))
    return pl.pallas_call(
        body, name="mix_bwd",
        out_shape=(SDS((T, SB_W), F32), SDS((T, GLA_W), F32), SDS((T, PROJ_W), BF16), SDS((1, SB_W), F32),
                   SDS((1, GLA_W), F32)),
        grid=(T // tm,),
        in_specs=[row(D), row(SB_W), row(GLA_W), ogb, vec(SB_W), vec(GLA_W)],
        out_specs=(row(SB_W), row(GLA_W), ogb, vec(SB_W), vec(GLA_W)),
        compiler_params=_cp(("arbitrary",)),
    )(docat, o_sb, o_gla, proj, g_sb, g_gla)


def _sb_bwd(proj, tt, do, dproj, S, pieces):
    T = proj.shape[0]
    nq = S // QT
    scale = SB_HD ** -0.5
    nb, ns = T // S, len(pieces)

    def body(qkv_ref, tt_ref, do_ref, dp_in_ref, *rest):
        del dp_in_ref
        pc_refs, dp_ref, got_refs = rest[:ns], rest[ns], rest[ns + 1:2 * ns + 1]
        dk_acc, dv_acc = rest[2 * ns + 1:2 * ns + 3]
        sems = rest[2 * ns + 3:]
        first = jnp.logical_and(pl.program_id(0) == 0, pl.program_id(1) == 0)
        last = jnp.logical_and(pl.program_id(0) == nb - 1, pl.program_id(1) == 3)

        @pl.when(first)
        def _():
            _exchange_ops(pc_refs, got_refs, False, sems, "start")

        row, col, lane, kr, kc = _sb_masks()
        mincl = _bf(kr <= kc)
        mexcl = _bf(kr < kc)
        dk_acc[...] = jnp.zeros_like(dk_acc)
        dv_acc[...] = jnp.zeros_like(dv_acc)

        def qloop(qt, _):
            r0 = pl.multiple_of(qt * QT, QT)
            qs = _stack_heads(qkv_ref[pl.ds(r0, QT), 0:128] * scale, lane)
            dos = _stack_heads(do_ref[pl.ds(r0, QT), :], lane)
            ttv = tt_ref[pl.ds(r0, QT), :]
            tot = jnp.concatenate([ttv[:, 0:1], ttv[:, SB_HD:SB_HD + 1]], axis=0)
            walked = jnp.max(ttv[:, CNT_LANE:CNT_LANE + 1]).astype(jnp.int32)

            def step(kt, st):
                dq, lc, pc = st
                k0 = pl.multiple_of(kt * TK, TK)
                kv = _bf(qkv_ref[pl.ds(k0, TK), 128:256])
                vv = _bf(qkv_ref[pl.ds(k0, TK), 256:384])
                strict = (col + (kt - qt) * TK) < row
                z = jnp.where(strict, _dot_nt(qs, kv), SB_MASKED)
                sp = _softplus(z)
                lg = -sp
                after = tot - (lc + _split_dot(lg, mincl, 2))
                gl = z - sp
                w = jnp.exp(gl + after)
                du = w * _dot_nt(dos, vv)
                beta = jnp.exp(gl)
                pex = pc + _split_dot(du, mexcl, 2)
                dz = _bf(du - beta * (du + pex))
                dk_acc[pl.ds(k0, TK), :] += _dot_tn(dz, qs)
                dv_acc[pl.ds(k0, TK), :] += _dot_tn(_bf(w), dos)
                return (dq + _dot(dz, kv), lc + jnp.sum(lg, axis=1, keepdims=True),
                        pc + jnp.sum(du, axis=1, keepdims=True))

            zc = jnp.zeros((2 * QT, 1), F32)
            dq, _, _ = lax.fori_loop(qt - walked + 1, qt + 1, step, (jnp.zeros((2 * QT, LANE), F32), zc, zc))
            dp_ref[pl.ds(r0, QT), 0:128] = _bf(jnp.where(lane < SB_HD, dq[:QT], dq[QT:]) * scale)
            return 0

        lax.fori_loop(0, nq, qloop, 0)
        dp_ref[:, 128:256] = _bf(dk_acc[...])
        dp_ref[:, 256:384] = _bf(dv_acc[...])

        @pl.when(last)
        def _():
            _exchange_ops(pc_refs, got_refs, False, sems, "wait")

    blk = BS((S, 384), lambda b, p: (b, p))
    col_spec = BS((S, LANE), lambda b, p: (b, p))
    hbm = BS(memory_space=pltpu.HBM)
    return pl.pallas_call(
        body, name="sb_bwd", out_shape=(SDS((T, PROJ_W), BF16),) + tuple(SDS(s.shape, s.dtype) for s in pieces),
        grid=(nb, 4),
        in_specs=[blk, col_spec, col_spec, BS(memory_space=pl.ANY)] + [hbm] * ns, out_specs=(blk,) + (hbm,) * ns,
        scratch_shapes=[pltpu.VMEM((S, LANE), F32), pltpu.VMEM((S, LANE), F32)] + _exchange_sems(ns),
        input_output_aliases={3: 0},
        compiler_params=_cp(("arbitrary", "arbitrary")),
    )(proj, tt, do, dproj, *pieces)


def _gla_bwd(proj, wgu, bgu, do, dproj, S):
    T = proj.shape[0]
    nc, ng = S // CHUNK, S // GR

    def body(blk_ref, glr_ref, wgu_ref, bgu_ref, do_ref, dp_in_ref, dp_ref, dpre_ref, st_ref):
        del dp_in_ref
        causal, tri_incl, tri_rev, lane = _gla_masks()
        wg = _bf(wgu_ref[...])
        bg = bgu_ref[...]

        def fwd_group(g, states):
            r0 = pl.multiple_of(g * GR, GR)
            v, _, _, _, _, _, _, ke, decay = _gla_group_terms(blk_ref, glr_ref, wg, bg, r0, tri_incl)
            keb = _bf(ke)
            new_states = []
            for hh in range(2):
                vh = _bf(v[:, 128 * hh:128 * hh + 128])
                st = states[hh]
                for n in range(GLA_G):
                    rows = _chunk_rows(n)
                    st_ref[hh, g * GLA_G + n] = st
                    st = st * decay[n * CHUNK:n * CHUNK + 1] + _dot_tn(vh[rows], keb[rows])
                new_states.append(st)
            return tuple(new_states)

        z = jnp.zeros((GLA_DV, LANE), F32)
        lax.fori_loop(0, ng, fwd_group, (z, z))

        def bwd_group(it, dstates):
            g = ng - 1 - it
            r0 = pl.multiple_of(g * GR, GR)
            v, pre, b, b_last, eb, qd, ki, ke, decay = _gla_group_terms(
                blk_ref, glr_ref, wg, bg, r0, tri_incl)
            kib = _bf(ki)
            dqd = jnp.zeros((GR, LANE), F32)
            dki = jnp.zeros((GR, LANE), F32)
            dke = jnp.zeros((GR, LANE), F32)
            ddec = jnp.zeros((GR, LANE), F32)
            new_dstates, dvs = [], []
            for hh in range(2):
                hm = (lane // GLA_DK) == hh
                qm = _bf(jnp.where(hm, qd, 0.0))
                kem = _bf(jnp.where(hm, ke, 0.0))
                vh = _bf(v[:, 128 * hh:128 * hh + 128])
                doh = _bf(do_ref[pl.ds(r0, GR), 128 * hh:128 * hh + 128])
                attn = _bf(jnp.where(causal, _dot_nt(qm, kib), 0.0))
                dattn = _bf(jnp.where(causal, _dot_nt(doh, vh), 0.0))
                dv_intra = _dot_tn(attn, doh)
                dqd_intra = _dot(dattn, kib)
                dki = dki + _dot_tn(dattn, qm)
                dst = dstates[hh]
                dv_p, dqd_p, dke_p, ddec_p = [None] * GLA_G, [None] * GLA_G, [None] * GLA_G, [None] * GLA_G
                for n in reversed(range(GLA_G)):
                    rows = _chunk_rows(n)
                    st = st_ref[hh, g * GLA_G + n]
                    dv_p[n] = dv_intra[rows] + _dot_nt(kem[rows], _bf(dst))
                    dqd_p[n] = dqd_intra[rows] + _dot(doh[rows], _bf(st))
                    dke_p[n] = _dot(vh[rows], _bf(dst))
                    ddec_p[n] = jnp.broadcast_to(jnp.sum(dst * st, axis=0, keepdims=True), (CHUNK, LANE))
                    dst = dst * decay[n * CHUNK:n * CHUNK + 1] + _dot_tn(doh[rows], qm[rows])
                dvs.append(jnp.concatenate(dv_p, axis=0))
                dqd = dqd + jnp.where(hm, jnp.concatenate(dqd_p, axis=0), 0.0)
                dke = dke + jnp.where(hm, jnp.concatenate(dke_p, axis=0), 0.0)
                ddec = ddec + jnp.where(hm, jnp.concatenate(ddec_p, axis=0), 0.0)
                new_dstates.append(dst)
            einv = jnp.exp(-b)
            eend = jnp.exp(b_last - b)
            dq = (dqd * eb) * (GLA_DK ** -0.5)
            dk = dki * einv + dke * eend
            db = dqd * qd - dki * ki - dke * ke
            dkk = dke * ke
            db_last = _per_chunk(lambda rows: jnp.sum(dkk[rows], axis=0, keepdims=True)) + ddec * decay
            dla = _split_dot(db, tri_rev, 2, left=False) + db_last
            dpre_ref[pl.ds(r0, GR), :] = (dla * (1.0 / GATE_NORM)) * (1.0 - jax.nn.sigmoid(pre))
            dp_ref[pl.ds(r0, GR), 0:128] = _bf(dq)
            dp_ref[pl.ds(r0, GR), 128:256] = _bf(dk)
            dp_ref[pl.ds(r0, GR), 256:512] = _bf(jnp.concatenate(dvs, axis=1))
            return tuple(new_dstates)

        lax.fori_loop(0, ng, bwd_group, (z, z))

    return pl.pallas_call(
        body, name="gla_bwd", out_shape=(SDS((T, PROJ_W), BF16), SDS((T, GLA_KW), F32)), grid=(T // S, 2),
        in_specs=[BS((S, 512), lambda b, p: (b, 3 + p)), BS((S, LANE), lambda b, p: (b, GLR_BLK)),
                  BS((LANE, LANE), lambda b, p: (0, p)), BS((1, LANE), lambda b, p: (0, p)),
                  BS((S, 256), lambda b, p: (b, p)), BS(memory_space=pl.ANY)],
        out_specs=(BS((S, 512), lambda b, p: (b, 3 + p)), BS((S, LANE), lambda b, p: (b, p))),
        scratch_shapes=[pltpu.VMEM((2, nc, GLA_DV, LANE), F32)],
        input_output_aliases={5: 0},
        compiler_params=_cp(("parallel", "parallel")),
    )(proj, proj, wgu, bgu, do, dproj)


def _gate_bwd(dpre, proj, wgu, dproj):
    T = dpre.shape[0]
    tm = _tile(T, 512)

    def body(dpre_ref, glr_ref, wgu_ref, dp_in_ref, dp_ref, dw_ref, db_ref):
        del dp_in_ref

        @pl.when(pl.program_id(0) == 0)
        def _():
            dw_ref[...] = jnp.zeros_like(dw_ref)
            db_ref[...] = jnp.zeros_like(db_ref)

        dpre = dpre_ref[...]
        dp_ref[...] = _bf(_dot_nt(_bf(dpre), _bf(wgu_ref[...])))
        dw_ref[...] += _dot_tn(_bf(glr_ref[...]), _bf(dpre))
        db_ref[...] += jnp.sum(dpre, axis=0, keepdims=True)

    glr = BS((tm, LANE), lambda i: (i, GLR_BLK))
    return pl.pallas_call(
        body, name="gate_bwd",
        out_shape=(SDS((T, PROJ_W), BF16), SDS((LANE, GLA_KW), F32), SDS((1, GLA_KW), F32)), grid=(T // tm,),
        in_specs=[BS((tm, GLA_KW), lambda i: (i, 0)), glr, BS((LANE, GLA_KW), lambda i: (0, 0)),
                  BS(memory_space=pl.ANY)],
        out_specs=(glr, BS((LANE, GLA_KW), lambda i: (0, 0)), BS((1, GLA_KW), lambda i: (0, 0))),
        input_output_aliases={3: 0},
        compiler_params=_cp(("arbitrary",)),
    )(dpre, proj, wgu, dproj)


def _exchange_sems(n):
    return [pltpu.SemaphoreType.DMA((n * (N_DEV - 1),)), pltpu.SemaphoreType.DMA((n * (N_DEV - 1),)),
            pltpu.SemaphoreType.DMA((n,))]


def _exchange_ops(srcs, outs, gather, sems, act):
    ssem, rsem, lsem = sems
    x, y, c = lax.axis_index("x"), lax.axis_index("y"), lax.axis_index("c")
    me = 4 * x + 2 * y + c
    for i, (s_ref, o_ref) in enumerate(zip(srcs, outs)):
        for k in range(1, N_DEV):
            px = (x + ((k >> 2) & 1)) % 2
            py = (y + ((k >> 1) & 1)) % 2
            pc = (c + (k & 1)) % 2
            peer = 4 * px + 2 * py + pc
            n = i * (N_DEV - 1) + k - 1
            out = pltpu.make_async_remote_copy(
                src_ref=s_ref if gather else s_ref.at[peer], dst_ref=o_ref.at[me],
                send_sem=ssem.at[n], recv_sem=rsem.at[n],
                device_id=(px, py, pc), device_id_type=pl.DeviceIdType.MESH)
            if act == "start":
                out.start()
            else:
                out.wait_send()
                pltpu.make_async_remote_copy(
                    src_ref=s_ref if gather else s_ref.at[me], dst_ref=o_ref.at[peer],
                    send_sem=ssem.at[n], recv_sem=rsem.at[n],
                    device_id=(x, y, c), device_id_type=pl.DeviceIdType.MESH).wait_recv()
        mine = pltpu.make_async_copy(s_ref if gather else s_ref.at[me], o_ref.at[me], lsem.at[i])
        if act == "start":
            mine.start()
        else:
            mine.wait()


def _gather_two_level(src, name):
    def body(s_ref, o_ref, ssem, rsem, lsem):
        x, y, c = lax.axis_index("x"), lax.axis_index("y"), lax.axis_index("c")
        me, sibling = (x, y, c), (x, y, 1 - c)
        chips = [(1 - x, y), (x, 1 - y), (1 - x, 1 - y)]

        def slab(px, py, pc):
            return o_ref.at[4 * px + 2 * py + pc]

        def copy(k, block, to, src_ref=None):
            return pltpu.make_async_remote_copy(
                src_ref=slab(*block) if src_ref is None else src_ref, dst_ref=slab(*block),
                send_sem=ssem.at[k], recv_sem=rsem.at[k], device_id=to, device_id_type=pl.DeviceIdType.MESH)

        mine = pltpu.make_async_copy(s_ref, slab(*me), lsem)
        mine.start()
        first = [copy(0, me, sibling, s_ref)] + [copy(1 + j, me, (*chip, c), s_ref) for j, chip in enumerate(chips)]
        for cp in first:
            cp.start()
        passed = [copy(4 + j, (*chip, c), sibling) for j, chip in enumerate(chips)]
        for j, chip in enumerate(chips):
            copy(1 + j, (*chip, c), me).wait_recv()
            passed[j].start()
        copy(0, sibling, me).wait_recv()
        for j, chip in enumerate(chips):
            copy(4 + j, (*chip, 1 - c), me).wait_recv()
        for cp in first + passed:
            cp.wait_send()
        mine.wait()

    hbm = BS(memory_space=pltpu.HBM)
    return pl.pallas_call(
        body, name=name, out_shape=SDS((N_DEV,) + src.shape, src.dtype), in_specs=[hbm], out_specs=hbm,
        scratch_shapes=[pltpu.SemaphoreType.DMA((N_DEV - 1,)), pltpu.SemaphoreType.DMA((N_DEV - 1,)),
                        pltpu.SemaphoreType.DMA(())],
    )(src)


def _exchange(srcs, gather, name):
    n = len(srcs)
    shapes = [SDS((N_DEV,) + s.shape if gather else s.shape, s.dtype) for s in srcs]

    def body(*refs):
        s_refs, o_refs, sems = refs[:n], refs[n:2 * n], refs[2 * n:]
        _exchange_ops(s_refs, o_refs, gather, sems, "start")
        _exchange_ops(s_refs, o_refs, gather, sems, "wait")

    hbm = BS(memory_space=pltpu.HBM)
    return pl.pallas_call(
        body, name=name, out_shape=tuple(shapes), in_specs=[hbm] * n, out_specs=(hbm,) * n,
        scratch_shapes=_exchange_sems(n),
    )(*srcs)


def _adamw_math(w, g, m, v):
    m = ADAM_B1 * m + (1.0 - ADAM_B1) * g
    v = ADAM_B2 * v + (1.0 - ADAM_B2) * (g * g)
    m_hat = m / (1.0 - ADAM_B1 ** ADAM_STEP)
    v_hat = v / (1.0 - ADAM_B2 ** ADAM_STEP)
    delta = -ADAM_LR * (m_hat / (jnp.sqrt(v_hat) + ADAM_EPS) + ADAM_WD * w)
    return delta, m, v


def _sum_adamw(parts, w, m, v, tr, name):
    R, C = w.shape

    def body(p_ref, w_ref, m_ref, v_ref, g_ref, d_ref, nm_ref, nv_ref):
        g = p_ref[0].astype(F32)
        for d in range(1, N_DEV):
            g = g + p_ref[d].astype(F32)
        delta, nm, nv = _adamw_math(w_ref[...], g, m_ref[...], v_ref[...])
        g_ref[...] = g
        d_ref[...] = delta
        nm_ref[...] = nm
        nv_ref[...] = nv

    blk = BS((tr, C), lambda i: (i, 0))
    out = SDS((R, C), F32)
    return pl.pallas_call(
        body, name=name, out_shape=(out, out, out, out), grid=(R // tr,),
        in_specs=[BS((N_DEV, tr, C), lambda i: (0, i, 0)), blk, blk, blk], out_specs=(blk, blk, blk, blk),
        compiler_params=_cp(("parallel",)),
    )(parts, w, m, v)


def _flat_pad_rows(parts, rows):
    flat = jnp.concatenate([p.reshape(-1) for p in parts])
    return jnp.pad(flat, (0, rows * D - flat.shape[0])).reshape(rows, D)


SMALL_ROWS = 16
SHARD_SMALL_ROWS = 3


def kernel(x, attn_norm_g, w_in, w_gate_up, b_gate_up, sb_out_g, gla_out_g, w_out, ffn_norm_g, w_ffn_up, conv_w, conv_b, w_ffn_down, final_norm_g, loss_target, m_attn_norm_g, m_w_in, m_w_gate_up, m_b_gate_up, m_sb_out_g, m_gla_out_g, m_w_out, m_ffn_norm_g, m_w_ffn_up, m_conv_w, m_conv_b, m_w_ffn_down, m_final_norm_g, v_attn_norm_g, v_w_in, v_w_gate_up, v_b_gate_up, v_sb_out_g, v_gla_out_g, v_w_out, v_ffn_norm_g, v_w_ffn_up, v_conv_w, v_conv_b, v_w_ffn_down, v_final_norm_g):
    Bd, S, _ = x.shape
    T = Bd * S
    x2d = x.reshape(T, D)
    tgt = loss_target.reshape(T, D)
    c_up = w_ffn_up.shape[2]
    c_gu = w_gate_up.shape[2]
    c_in = w_in.shape[2]

    n_gu = GATE_RANK * c_gu
    rows_bf = lambda w: w[0].T.astype(BF16)
    small_w = lambda wgu, cw: _flat_pad_rows([wgu, cw], SHARD_SMALL_ROWS)

    g_in = _gather_two_level(rows_bf(w_in), "gather_w_in")
    w_in_pt = jnp.pad(g_in.reshape(IN_COLS, D), ((0, 1), (0, 0)))[_PERM]
    g3 = final_norm_g.reshape(1, D)

    proj, h1 = _norm_proj(x2d, attn_norm_g, w_in_pt)
    o_sb, tt, g_up, g_down, g_out, gs = _sb_fwd(
        proj, S, [rows_bf(w_ffn_up), w_ffn_down[0].astype(BF16), w_out[0].astype(BF16),
                  _flat_pad_rows([w_gate_up, conv_w], 8)])
    w_out_f = g_out.reshape(D, D)
    gsf = gs.reshape(N_DEV, -1)
    wgu_f = jnp.transpose(gsf[:, :n_gu].reshape(N_DEV, GATE_RANK, c_gu), (1, 0, 2)).reshape(GATE_RANK, GLA_KW)
    cw_f = jnp.transpose(gsf[:, n_gu:n_gu + 3 * c_up].reshape(N_DEV, 3, c_up), (1, 0, 2)).reshape(3, 2 * D_FF)
    wgu_p = jnp.pad(wgu_f, ((0, LANE - GATE_RANK), (0, 0)))
    w_up_t = g_up.reshape(2 * D_FF, D)
    w_down_f = g_down.reshape(D_FF, D)
    o_gla = _gla_fwd(proj, wgu_p, b_gate_up, S)
    x1, ocat, h2 = _mix_out(o_sb, o_gla, proj, x2d, sb_out_g, gla_out_g, w_out_f, ffn_norm_g)
    hup = _mm(h2, w_up_t, "nt", "ffn_up", tm=1024, tn=1408, tk=1024)
    act = _conv_gate(hup, cw_f, conv_b, S)
    dx2, dg3, loss_dev = _down_loss(act, w_down_f, x1, tgt, g3)

    dw_down = _mm(act, dx2, "tn", "dw_down", out_dtype=BF16, tm=D_FF, tn=1024, tk=512)
    dact = _mm(dx2, w_down_f, "nt", "dact", tm=1024, tn=1408, tk=1024)
    dhup_a, dhup_v, dcw_a, dcw_v, dcb_a, dcb_v = _conv_gate_bwd(hup, dact, cw_f, conv_b, S)
    dw_up_t = _mm(dhup_a, h2, "tn", "dw_up_a", out_dtype=BF16, tm=D_FF, tn=1024, tk=512, out_rows=2 * D_FF)
    dw_up_t = _mm(dhup_v, h2, "tn", "dw_up_v", out_dtype=BF16, tm=D_FF, tn=1024, tk=512, out_rows=2 * D_FF,
                  out_row0=D_FF, into=dw_up_t)
    dh2 = _mm(dhup_a, w_up_t, "nn", "dh2_a", tm=1024, tn=1024, tk=1408)
    dx1, dg2 = _mm(dhup_v, w_up_t, "nn", "dh2_v", c=dh2, tm=512, tn=1024, tk=1408, b_row0=D_FF,
                   norm_bwd=(x1, ffn_norm_g, dx2))

    dw_out = _mm(ocat, dx1, "tn", "dw_out", out_dtype=BF16, tm=1024, tn=1024, tk=512)
    docat = _mm(dx1, w_out_f, "nt", "docat", tm=1024, tn=1024, tk=1024)
    do_sb, do_gla, dproj, dg_sb, dg_gla = _mix_bwd(docat, o_sb, o_gla, proj, sb_out_g, gla_out_g)
    dproj, got_up, got_down, got_out = _sb_bwd(
        proj, tt, do_sb, dproj, S,
        [dw_up_t.reshape(N_DEV, c_up, D), dw_down.reshape(N_DEV, -1, D), dw_out.reshape(N_DEV, -1, D)])
    dproj, dpre = _gla_bwd(proj, wgu_p, b_gate_up, do_gla, dproj, S)
    dproj, dwgu, dbgu = _gate_bwd(dpre, proj, wgu_p, dproj)
    dw_in_pt = _mm(dproj, h1, "tn", "dw_in", out_dtype=BF16, tm=PROJ_W, tn=1024, tk=512)
    dx, dg1, got_in = _mm(dproj, w_in_pt, "nn", "dh1", tm=1024, tn=1024, tk=640,
                          xchg=([dw_in_pt[_INV_PERM].reshape(N_DEV, c_in, D)], False),
                          norm_bwd=(x2d, attn_norm_g, dx1))

    dcw = jnp.concatenate([dcw_a, dcw_v], axis=1)
    dwgu_pc = jnp.transpose(dwgu[:GATE_RANK].reshape(GATE_RANK, N_DEV, c_gu), (1, 0, 2)).reshape(N_DEV, -1)
    dcw_pc = jnp.transpose(dcw.reshape(3, N_DEV, c_up), (1, 0, 2)).reshape(N_DEV, -1)
    small_pc = jnp.concatenate([dwgu_pc, dcw_pc], axis=1)
    small_pc = jnp.pad(small_pc, ((0, 0), (0, SHARD_SMALL_ROWS * D - small_pc.shape[1])))
    small_pc = small_pc.reshape(N_DEV, SHARD_SMALL_ROWS, D).astype(BF16)
    rep_names = ["attn_norm_g", "b_gate_up", "sb_out_g", "gla_out_g", "ffn_norm_g", "conv_b", "final_norm_g"]
    rep_g = [dg1, dbgu, dg_sb, dg_gla, dg2, jnp.concatenate([dcb_a, dcb_v], axis=1), dg3]
    rep_w = [attn_norm_g, b_gate_up, sb_out_g, gla_out_g, ffn_norm_g, conv_b, final_norm_g]
    rep_m = [m_attn_norm_g, m_b_gate_up, m_sb_out_g, m_gla_out_g, m_ffn_norm_g, m_conv_b, m_final_norm_g]
    rep_v = [v_attn_norm_g, v_b_gate_up, v_sb_out_g, v_gla_out_g, v_ffn_norm_g, v_conv_b, v_final_norm_g]
    rep_pc = jnp.broadcast_to(_flat_pad_rows(rep_g, SMALL_ROWS), (N_DEV, SMALL_ROWS, D))
    got_sm, got_rep = _exchange([small_pc, rep_pc], False, "scatter_tail")

    rows = lambda w: w[0].T
    cols = lambda r: r.T[None]
    res = {}
    res["w_in"] = [cols(r) for r in _sum_adamw(got_in, rows(w_in), rows(m_w_in), rows(v_w_in), c_in, "adamw_w_in")]
    res["w_out"] = [r[None] for r in _sum_adamw(got_out, w_out[0], m_w_out[0], v_w_out[0], w_out.shape[1],
                                                 "adamw_w_out")]
    res["w_ffn_up"] = [cols(r) for r in _sum_adamw(got_up, rows(w_ffn_up), rows(m_w_ffn_up), rows(v_w_ffn_up),
                                                    c_up // 2, "adamw_w_up")]
    res["w_ffn_down"] = [r[None] for r in _sum_adamw(got_down, w_ffn_down[0], m_w_ffn_down[0], v_w_ffn_down[0],
                                                      w_ffn_down.shape[1], "adamw_w_down")]
    sm = _sum_adamw(got_sm, small_w(w_gate_up, conv_w), small_w(m_w_gate_up, m_conv_w),
                    small_w(v_w_gate_up, v_conv_w), SHARD_SMALL_ROWS, "adamw_small_sharded")
    res["w_gate_up"] = [r.reshape(-1)[:n_gu].reshape(1, GATE_RANK, c_gu) for r in sm]
    res["conv_w"] = [r.reshape(-1)[n_gu:n_gu + 3 * c_up].reshape(1, 3, c_up) for r in sm]
    rep = _sum_adamw(got_rep, _flat_pad_rows(rep_w, SMALL_ROWS), _flat_pad_rows(rep_m, SMALL_ROWS),
                     _flat_pad_rows(rep_v, SMALL_ROWS), SMALL_ROWS, "adamw_replicated")
    o = 0
    for n, w in zip(rep_names, rep_w):
        res[n] = [r.reshape(-1)[o:o + w.size].reshape(w.shape) for r in rep]
        o += w.size

    loss = lax.psum(loss_dev[0, 0], ("x", "y", "c"))
    order = ["attn_norm_g", "w_in", "w_gate_up", "b_gate_up", "sb_out_g", "gla_out_g", "w_out", "ffn_norm_g",
             "w_ffn_up", "conv_w", "conv_b", "w_ffn_down", "final_norm_g"]
    outs = [loss, dx.reshape(Bd, S, D)]
    for k in range(4):
        outs += [res[n][k] for n in order]
    return tuple(outs)
```

```python
import functools

import numpy as np
import jax
import jax.numpy as jnp
from jax import lax
from jax.experimental import pallas as pl
from jax.experimental.pallas import tpu as pltpu

F32 = jnp.float32
BF16 = jnp.bfloat16
SDS = jax.ShapeDtypeStruct
BS = pl.BlockSpec

N_DEV = 8
D = 1024
EPS = 1e-6
SB_HD = 64
SB_W = 512
GLA_DK = 64
GLA_DV = 128
GLA_KW = 256
GLA_W = 512
GATE_RANK = 16
GATE_NORM = 16.0
CHUNK = 64
GLA_G = 4
GR = GLA_G * CHUNK
QT = 256
D_FF = 2816
IN_COLS = 3088
PROJ_W = 3200
LANE = 128
VMEM_LIMIT = 56 * 1024 * 1024

ADAM_LR, ADAM_B1, ADAM_B2, ADAM_EPS, ADAM_WD, ADAM_STEP = 0.001, 0.9, 0.999, 1e-08, 0.01, 10


def _proj_perm():
    sbq, sbk, sbv = 0, 512, 1024
    gq, gk, gv, glr, gog = 1536, 1792, 2048, 2560, 2576
    cols = []
    for p in range(4):
        for base in (sbq, sbk, sbv):
            cols += list(range(base + 128 * p, base + 128 * p + 128))
    for p in range(2):
        cols += list(range(gq + 128 * p, gq + 128 * p + 128))
        cols += list(range(gk + 128 * p, gk + 128 * p + 128))
        cols += list(range(gv + 256 * p, gv + 256 * p + 256))
    cols += list(range(gog, gog + 512))
    cols += list(range(glr, glr + GATE_RANK)) + [IN_COLS] * (LANE - GATE_RANK)
    perm = np.asarray(cols, np.int32)
    inv = np.zeros((IN_COLS,), np.int32)
    for new, old in enumerate(cols):
        if old < IN_COLS:
            inv[old] = new
    return perm, inv


_PERM, _INV_PERM = _proj_perm()
OG_BLK = 5
GLR_BLK = 24


def _cp(sem=None, vmem=VMEM_LIMIT):
    return pltpu.CompilerParams(dimension_semantics=sem, vmem_limit_bytes=vmem)


def _dot(a, b):
    return lax.dot_general(a, b, (((1,), (0,)), ((), ())), preferred_element_type=F32)


def _dot_nt(a, b):
    return lax.dot_general(a, b, (((1,), (1,)), ((), ())), preferred_element_type=F32)


def _dot_tn(a, b):
    return lax.dot_general(a, b, (((0,), (0,)), ((), ())), preferred_element_type=F32)


def _bf(x):
    return x.astype(BF16)


def _split_dot(x, m, passes, left=True):
    acc = None
    r = x
    for i in range(passes):
        h = r.astype(BF16)
        t = _dot(h, m) if left else _dot(m, h)
        acc = t if acc is None else acc + t
        if i + 1 < passes:
            r = r - h.astype(F32)
    return acc


def _softplus(z):
    return jnp.maximum(z, 0.0) + jnp.log(1.0 + jnp.exp(-jnp.abs(z)))


def _rms_bwd_math(x, g, dh, dres):
    r = lax.rsqrt(jnp.mean(x * x, axis=-1, keepdims=True) + EPS)
    xh = x * r
    dxh = dh * g
    dx = dres + r * (dxh - xh * jnp.mean(dxh * xh, axis=-1, keepdims=True))
    return dx, jnp.sum(dh * xh, axis=0, keepdims=True)


def _tile(n, pref, mult=LANE):
    best = None
    for t in range(mult, min(n, pref) + 1, mult):
        if n % t == 0:
            best = t
    return best if best is not None else n


def _mm(a, b, mode, name, out_dtype=F32, c=None, tm=512, tn=512, tk=512, b_row0=0, out_rows=None, out_row0=0,
        into=None, xchg=None, norm_bwd=None):
    if mode == "nn":
        (M, K), N = a.shape, b.shape[1]
    elif mode == "nt":
        (M, K), N = a.shape, b.shape[0]
    else:
        (K, M), N = a.shape, b.shape[1]
    tm, tn, tk = _tile(M, tm), _tile(N, tn), _tile(K, tk)
    nk = K // tk
    kb0, ob0 = b_row0 // tk, out_row0 // tm
    assert kb0 * tk == b_row0 and ob0 * tm == out_row0 and (mode == "nn" or b_row0 == 0)
    ni, nj = M // tm, N // tn
    j_outer = nk == 1 and (nj - 1) * a.size * a.dtype.itemsize < (ni - 1) * K * N * b.dtype.itemsize
    ix = (lambda f: (lambda j, i, k: f(i, j, k))) if j_outer else (lambda f: f)
    a_spec = BS((tk, tm), ix(lambda i, j, k: (k, i))) if mode == "tn" else BS((tm, tk), ix(lambda i, j, k: (i, k)))
    b_spec = (BS((tn, tk), ix(lambda i, j, k: (j, k))) if mode == "nt"
              else BS((tk, tn), ix(lambda i, j, k: (k + kb0, j))))
    dotfn = {"nn": _dot, "nt": _dot_nt, "tn": _dot_tn}[mode]
    has_c = c is not None
    has_into = into is not None
    nx = 0 if xchg is None else len(xchg[0])
    has_nb = norm_bwd is not None
    assert not has_nb or (nj == 1 and not j_outer and out_dtype == F32)
    n_in = 2 + has_c + has_into + 3 * has_nb

    def body(*refs):
        a_ref, b_ref = refs[:2]
        c_ref = refs[2] if has_c else None
        x_src = refs[n_in:n_in + nx]
        outs = refs[n_in + nx:n_in + 2 * nx + 1 + has_nb]
        o_ref, x_out = outs[0], outs[1 + has_nb:]
        acc = refs[n_in + 2 * nx + 1 + has_nb]
        sems = refs[n_in + 2 * nx + 2 + has_nb:]
        k = pl.program_id(2)
        g0, g1 = pl.program_id(0), pl.program_id(1)
        n0, n1 = (nj, ni) if j_outer else (ni, nj)
        first = jnp.logical_and(jnp.logical_and(g0 == 0, g1 == 0), k == 0)
        if nx:
            @pl.when(first)
            def _():
                _exchange_ops(x_src, x_out, xchg[1], sems, "start")

        if has_nb:
            @pl.when(first)
            def _():
                outs[1][...] = jnp.zeros_like(outs[1])

        @pl.when(k == 0)
        def _():
            acc[...] = jnp.zeros_like(acc)

        acc[...] += dotfn(_bf(a_ref[...]), _bf(b_ref[...]))

        @pl.when(k == nk - 1)
        def _():
            r = acc[...]
            if has_c:
                r = r + c_ref[...]
            if has_nb:
                x_ref, g_ref, dres_ref = refs[n_in - 3:n_in]
                dx, dg = _rms_bwd_math(x_ref[...], g_ref[...], r, dres_ref[...])
                o_ref[...] = dx
                outs[1][...] += dg
            else:
                o_ref[...] = r.astype(out_dtype)

        if nx:
            @pl.when(jnp.logical_and(jnp.logical_and(g0 == n0 - 1, g1 == n1 - 1), k == nk - 1))
            def _():
                _exchange_ops(x_src, x_out, xchg[1], sems, "wait")

    tile = BS((tm, tn), ix(lambda i, j, k: (i, j)))
    in_specs = [a_spec, b_spec]
    args = [a, b]
    if has_c:
        in_specs.append(tile)
        args.append(c)
    aliases = {}
    if has_into:
        aliases = {len(args): 0}
        in_specs.append(BS(memory_space=pl.ANY))
        args.append(into)
    out_shape = [SDS((out_rows or M, N), out_dtype)]
    out_specs = [BS((tm, tn), ix(lambda i, j, k: (i + ob0, j)))]
    scratch = [pltpu.VMEM((tm, tn), F32)]
    if has_nb:
        in_specs += [tile, BS((1, tn), lambda i, j, k: (0, 0)), tile]
        args += list(norm_bwd)
        out_shape.append(SDS((1, N), F32))
        out_specs.append(BS((1, tn), lambda i, j, k: (0, 0)))
    if nx:
        hbm = BS(memory_space=pltpu.HBM)
        in_specs += [hbm] * nx
        args += list(xchg[0])
        out_shape += [SDS((N_DEV,) + s.shape if xchg[1] else s.shape, s.dtype) for s in xchg[0]]
        out_specs += [hbm] * nx
        scratch += _exchange_sems(nx)
    serial = nx or has_nb
    res = pl.pallas_call(
        body, name=name, out_shape=tuple(out_shape), grid=(nj, ni, nk) if j_outer else (ni, nj, nk),
        in_specs=in_specs, out_specs=tuple(out_specs),
        scratch_shapes=scratch, input_output_aliases=aliases,
        compiler_params=_cp(("arbitrary",) * 3 if serial else ("parallel", "parallel", "arbitrary")),
    )(*args)
    return res if serial else res[0]


def _norm_proj(x, g, w):
    T, N = x.shape[0], w.shape[0]
    tm = _tile(T, 512)

    def body(x_ref, g_ref, w_ref, p_ref, h_ref):
        xv = x_ref[...]
        r = lax.rsqrt(jnp.mean(xv * xv, axis=-1, keepdims=True) + EPS)
        h = _bf((xv * r) * g_ref[...])
        h_ref[...] = h
        p_ref[...] = _dot_nt(h, w_ref[...])

    return pl.pallas_call(
        body, name="norm_proj", out_shape=(SDS((T, N), F32), SDS((T, D), BF16)), grid=(T // tm,),
        in_specs=[BS((tm, D), lambda i: (i, 0)), BS((1, D), lambda i: (0, 0)), BS((N, D), lambda i: (0, 0))],
        out_specs=(BS((tm, N), lambda i: (i, 0)), BS((tm, D), lambda i: (i, 0))),
        compiler_params=_cp(("parallel",)),
    )(x, g, w)


TK = 256
SB_DEAD = -104.0
SB_MASKED = -1e30
CNT_LANE = SB_HD - 1


def _sb_masks():
    row = lax.broadcasted_iota(jnp.int32, (2 * QT, TK), 0) & (QT - 1)
    col = lax.broadcasted_iota(jnp.int32, (2 * QT, TK), 1)
    lane = lax.broadcasted_iota(jnp.int32, (1, LANE), 1)
    kr = lax.broadcasted_iota(jnp.int32, (TK, TK), 0)
    kc = lax.broadcasted_iota(jnp.int32, (TK, TK), 1)
    return row, col, lane, kr, kc


def _stack_heads(x, lane):
    return jnp.concatenate([_bf(jnp.where((lane // SB_HD) == hh, x, 0.0)) for hh in range(2)], axis=0)


def _sb_fwd(proj, S, shards):
    T = proj.shape[0]
    nq = S // QT
    scale = SB_HD ** -0.5
    nb, ns = T // S, len(shards)

    def body(qkv_ref, *rest):
        sh_refs, (o_ref, tt_ref), g_refs = rest[:ns], rest[ns:ns + 2], rest[ns + 2:2 * ns + 2]
        sems = rest[2 * ns + 2:]
        first = jnp.logical_and(pl.program_id(0) == 0, pl.program_id(1) == 0)
        last = jnp.logical_and(pl.program_id(0) == nb - 1, pl.program_id(1) == 3)

        @pl.when(first)
        def _():
            _exchange_ops(sh_refs, g_refs, True, sems, "start")

        row, col, lane, kr, kc = _sb_masks()
        msuf = _bf(kr > kc)

        def qloop(qt, _):
            r0 = pl.multiple_of(qt * QT, QT)
            qs = _stack_heads(qkv_ref[pl.ds(r0, QT), 0:128] * scale, lane)

            def live(st):
                it, _, cy = st
                return it <= qt

            def step(st):
                it, acc, cy = st
                kt = qt - it
                k0 = pl.multiple_of(kt * TK, TK)
                kv = _bf(qkv_ref[pl.ds(k0, TK), 128:256])
                vv = _bf(qkv_ref[pl.ds(k0, TK), 256:384])
                strict = (col + (kt - qt) * TK) < row
                z = jnp.where(strict, _dot_nt(qs, kv), SB_MASKED)
                sp = _softplus(z)
                lg = -sp
                after = cy + _split_dot(lg, msuf, 2)
                w = jnp.exp((z - sp) + after)
                return it + 1, acc + _dot(_bf(w), vv), cy + jnp.sum(lg, axis=1, keepdims=True)

            it, acc, cy = lax.while_loop(
                live, step, (jnp.int32(0), jnp.zeros((2 * QT, LANE), F32), jnp.zeros((2 * QT, 1), F32)))
            o_ref[pl.ds(r0, QT), :] = jnp.where(lane < SB_HD, acc[:QT], acc[QT:])
            tt = jnp.where(lane < SB_HD, cy[:QT], cy[QT:])
            tt_ref[pl.ds(r0, QT), :] = jnp.where(lane == CNT_LANE, it.astype(F32), tt)
            return 0

        lax.fori_loop(0, nq, qloop, 0)

        @pl.when(last)
        def _():
            _exchange_ops(sh_refs, g_refs, True, sems, "wait")

    hbm = BS(memory_space=pltpu.HBM)
    col_spec = BS((S, LANE), lambda b, p: (b, p))
    return pl.pallas_call(
        body, name="sb_fwd",
        out_shape=(SDS((T, SB_W), F32), SDS((T, SB_W), F32)) + tuple(SDS((N_DEV,) + s.shape, s.dtype) for s in shards),
        grid=(nb, 4),
        in_specs=[BS((S, 384), lambda b, p: (b, p))] + [hbm] * ns,
        out_specs=(col_spec, col_spec) + (hbm,) * ns,
        scratch_shapes=_exchange_sems(ns),
        compiler_params=_cp(("arbitrary", "arbitrary")),
    )(proj, *shards)


def _log_sigmoid(x):
    return jnp.minimum(x, 0.0) - jnp.log1p(jnp.exp(-jnp.abs(x)))


def _gla_masks():
    r = lax.broadcasted_iota(jnp.int32, (GR, GR), 0)
    c = lax.broadcasted_iota(jnp.int32, (GR, GR), 1)
    same = (r // CHUNK) == (c // CHUNK)
    causal = jnp.logical_and(same, r >= c)
    lane = lax.broadcasted_iota(jnp.int32, (1, LANE), 1)
    return causal, _bf(causal), _bf(jnp.logical_and(same, r <= c)), lane


def _gla_group_terms(blk_ref, glr_ref, wgu, bgu, r0, tri_incl):
    q = blk_ref[pl.ds(r0, GR), 0:128]
    k = blk_ref[pl.ds(r0, GR), 128:256]
    v = blk_ref[pl.ds(r0, GR), 256:512]
    pre = _dot(_bf(glr_ref[pl.ds(r0, GR), :]), wgu) + bgu
    la = _log_sigmoid(pre) / GATE_NORM
    b = _split_dot(la, tri_incl, 3, left=False)
    b_last = _per_chunk(lambda rows: b[rows.stop - 1:rows.stop])
    eb = jnp.exp(b)
    qd = (q * (GLA_DK ** -0.5)) * eb
    ki = k * jnp.exp(-b)
    ke = k * jnp.exp(b_last - b)
    decay = jnp.exp(b_last)
    return v, pre, b, b_last, eb, qd, ki, ke, decay


def _chunk_rows(n):
    return slice(n * CHUNK, (n + 1) * CHUNK)


def _per_chunk(row_fn):
    return jnp.concatenate(
        [jnp.broadcast_to(row_fn(_chunk_rows(n)), (CHUNK, LANE)) for n in range(GLA_G)], axis=0)


def _gla_fwd(proj, wgu, bgu, S):
    T = proj.shape[0]
    ng = S // GR

    def body(blk_ref, glr_ref, wgu_ref, bgu_ref, o_ref):
        causal, tri_incl, _, lane = _gla_masks()
        wg = _bf(wgu_ref[...])
        bg = bgu_ref[...]

        def group(g, states):
            r0 = pl.multiple_of(g * GR, GR)
            v, _, _, _, _, qd, ki, ke, decay = _gla_group_terms(blk_ref, glr_ref, wg, bg, r0, tri_incl)
            kib, keb = _bf(ki), _bf(ke)
            new_states, outs = [], []
            for hh in range(2):
                hm = (lane // GLA_DK) == hh
                qm = _bf(jnp.where(hm, qd, 0.0))
                vh = _bf(v[:, 128 * hh:128 * hh + 128])
                attn = jnp.where(causal, _dot_nt(qm, kib), 0.0)
                o_intra = _dot(_bf(attn), vh)
                st = states[hh]
                parts = []
                for n in range(GLA_G):
                    rows = _chunk_rows(n)
                    parts.append(o_intra[rows] + _dot_nt(qm[rows], _bf(st)))
                    st = st * decay[n * CHUNK:n * CHUNK + 1] + _dot_tn(vh[rows], keb[rows])
                outs.append(jnp.concatenate(parts, axis=0))
                new_states.append(st)
            o_ref[pl.ds(r0, GR), :] = jnp.concatenate(outs, axis=1)
            return tuple(new_states)

        z = jnp.zeros((GLA_DV, LANE), F32)
        lax.fori_loop(0, ng, group, (z, z))

    return pl.pallas_call(
        body, name="gla_fwd", out_shape=SDS((T, GLA_W), F32), grid=(T // S, 2),
        in_specs=[BS((S, 512), lambda b, p: (b, 3 + p)), BS((S, LANE), lambda b, p: (b, GLR_BLK)),
                  BS((LANE, LANE), lambda b, p: (0, p)), BS((1, LANE), lambda b, p: (0, p))],
        out_specs=BS((S, 256), lambda b, p: (b, p)),
        compiler_params=_cp(("parallel", "parallel")),
    )(proj, proj, wgu, bgu)


def _head_blockdiag(width, hd):
    r = lax.broadcasted_iota(jnp.int32, (width, width), 0) // hd
    c = lax.broadcasted_iota(jnp.int32, (width, width), 1) // hd
    return _bf(r == c)


def _mix_out(o_sb, o_gla, proj, x, g_sb, g_gla, w_out, g2):
    T = x.shape[0]
    tm = _tile(T, 512)

    def body(osb_ref, ogl_ref, og_ref, x_ref, gsb_ref, ggl_ref, w_ref, g2_ref, x1_ref, oc_ref, h2_ref):
        bd64 = _head_blockdiag(SB_W, SB_HD)
        bd128 = _head_blockdiag(GLA_W, GLA_DV)
        o = osb_ref[...]
        r = lax.rsqrt(_split_dot(o * o, bd64, 2) * (1.0 / SB_HD) + EPS)
        c_sb = (o * r) * gsb_ref[...]
        o = ogl_ref[...]
        r = lax.rsqrt(_split_dot(o * o, bd128, 2) * (1.0 / GLA_DV) + EPS)
        og = og_ref[...]
        c_gl = ((o * r) * ggl_ref[...]) * (og * jax.nn.sigmoid(og))
        oc = _bf(jnp.concatenate([c_sb, c_gl], axis=1))
        oc_ref[...] = oc
        x1 = x_ref[...] + _dot(oc, w_ref[...])
        x1_ref[...] = x1
        r2 = lax.rsqrt(jnp.mean(x1 * x1, axis=-1, keepdims=True) + EPS)
        h2_ref[...] = _bf((x1 * r2) * g2_ref[...])

    row = lambda w: BS((tm, w), lambda i: (i, 0))
    vec = lambda w: BS((1, w), lambda i: (0, 0))
    return pl.pallas_call(
        body, name="mix_out", out_shape=(SDS((T, D), F32), SDS((T, D), BF16), SDS((T, D), BF16)), grid=(T // tm,),
        in_specs=[row(SB_W), row(GLA_W), BS((tm, 512), lambda i: (i, OG_BLK)), row(D), vec(SB_W), vec(GLA_W),
                  BS((D, D), lambda i: (0, 0)), vec(D)],
        out_specs=(row(D), row(D), row(D)),
        compiler_params=_cp(("parallel",)),
    )(o_sb, o_gla, proj, x, g_sb, g_gla, w_out, g2)


CONV_ROWS = 256
CONV_TC = 256


def _rows_before(ref, r0, first):
    prev = ref[pl.ds(pl.multiple_of(jnp.maximum(r0 - 8, 0), 8), 8), :]
    return jnp.where(first, 0.0, prev)


def _shift_down(cur, prev8, k):
    cat = jnp.concatenate([prev8, cur], axis=0)
    return pltpu.roll(cat, k, 0)[8:]


def _shift_up(cur, next8, k):
    cat = jnp.concatenate([cur, next8], axis=0)
    return pltpu.roll(cat, cat.shape[0] - k, 0)[:cur.shape[0]]


def _conv_at(h_ref, cw, cb, r0, rows, first):
    cur = h_ref[pl.ds(r0, rows), :]
    prev8 = _rows_before(h_ref, r0, first)
    u = cb + cw[0:1, :] * _shift_down(cur, prev8, 2)
    u = u + cw[1:2, :] * _shift_down(cur, prev8, 1)
    return u + cw[2:3, :] * cur


NJ = D_FF // CONV_TC


def _conv_gate(hup, cw, cb, S):
    T = hup.shape[0]
    rows = min(CONV_ROWS, S)
    nr = S // rows

    def body(ha_ref, hv_ref, cwa_ref, cwv_ref, cba_ref, cbv_ref, act_ref):
        cwa, cwv, cba, cbv = cwa_ref[...], cwv_ref[...], cba_ref[...], cbv_ref[...]

        def step(c, _):
            r0 = pl.multiple_of(c * rows, rows)
            ua = _conv_at(ha_ref, cwa, cba, r0, rows, c == 0)
            uv = _conv_at(hv_ref, cwv, cbv, r0, rows, c == 0)
            act_ref[pl.ds(r0, rows), :] = _bf((ua * jax.nn.sigmoid(ua)) * uv)
            return 0

        lax.fori_loop(0, nr, step, 0)

    blk = lambda o: BS((S, CONV_TC), lambda b, j: (b, j + o))
    w3 = lambda o: BS((3, CONV_TC), lambda b, j: (0, j + o))
    w1 = lambda o: BS((1, CONV_TC), lambda b, j: (0, j + o))
    return pl.pallas_call(
        body, name="conv_gate", out_shape=SDS((T, D_FF), BF16), grid=(T // S, NJ),
        in_specs=[blk(0), blk(NJ), w3(0), w3(NJ), w1(0), w1(NJ)], out_specs=blk(0),
        compiler_params=_cp(("parallel", "parallel")),
    )(hup, hup, cw, cw, cb, cb)


def _down_loss(act, w_down, x1, tgt, g3):
    T = x1.shape[0]
    tm = _tile(T, 512)

    def body(a_ref, w_ref, x1_ref, t_ref, g_ref, dx_ref, dg_ref, ls_ref):
        @pl.when(pl.program_id(0) == 0)
        def _():
            dg_ref[...] = jnp.zeros_like(dg_ref)
            ls_ref[...] = jnp.zeros_like(ls_ref)

        g = g_ref[...]
        x2 = x1_ref[...] + _dot(a_ref[...], w_ref[...])
        r = lax.rsqrt(jnp.mean(x2 * x2, axis=-1, keepdims=True) + EPS)
        xh = x2 * r
        e = xh * g - t_ref[...]
        ls_ref[...] += 0.5 * jnp.sum(jnp.mean(e * e, axis=-1, keepdims=True), axis=0, keepdims=True)
        dy = e * (1.0 / D)
        dxh = dy * g
        dx_ref[...] = r * (dxh - xh * jnp.mean(dxh * xh, axis=-1, keepdims=True))
        dg_ref[...] += jnp.sum(dy * xh, axis=0, keepdims=True)

    row = lambda w: BS((tm, w), lambda i: (i, 0))
    return pl.pallas_call(
        body, name="down_loss", out_shape=(SDS((T, D), F32), SDS((1, D), F32), SDS((1, LANE), F32)), grid=(T // tm,),
        in_specs=[row(D_FF), BS((D_FF, D), lambda i: (0, 0)), row(D), row(D), BS((1, D), lambda i: (0, 0))],
        out_specs=(row(D), BS((1, D), lambda i: (0, 0)), BS((1, LANE), lambda i: (0, 0))),
        compiler_params=_cp(("arbitrary",)),
    )(act, w_down, x1, tgt, g3)


def _conv_gate_bwd(hup, dact, cw, cb, S):
    T = hup.shape[0]
    rows = min(CONV_ROWS, S)
    nr = S // rows

    def body(ha_ref, hv_ref, da_ref, cwa_ref, cwv_ref, cba_ref, cbv_ref,
             dha_ref, dhv_ref, dcwa_ref, dcwv_ref, dcba_ref, dcbv_ref):
        @pl.when(pl.program_id(1) == 0)
        def _():
            for r in (dcwa_ref, dcwv_ref, dcba_ref, dcbv_ref):
                r[...] = jnp.zeros_like(r)

        cwa, cwv, cba, cbv = cwa_ref[...], cwv_ref[...], cba_ref[...], cbv_ref[...]

        def du_at(r0, n, first):
            ua = _conv_at(ha_ref, cwa, cba, r0, n, first)
            uv = _conv_at(hv_ref, cwv, cbv, r0, n, first)
            da = da_ref[pl.ds(r0, n), :]
            sg = jax.nn.sigmoid(ua)
            dua = (da * uv) * (sg * (1.0 + ua * (1.0 - sg)))
            duv = da * (ua * sg)
            return dua, duv

        def step(c, _):
            r0 = pl.multiple_of(c * rows, rows)
            first, last = c == 0, c == nr - 1
            dua, duv = du_at(r0, rows, first)
            n0 = pl.multiple_of(jnp.minimum(r0 + rows, S - 8), 8)
            nua, nuv = du_at(n0, 8, False)
            nua = jnp.where(last, 0.0, nua)
            nuv = jnp.where(last, 0.0, nuv)
            for (h_ref, cw, du, nu, dh_ref, dcw_ref, dcb_ref) in (
                    (ha_ref, cwa, dua, nua, dha_ref, dcwa_ref, dcba_ref),
                    (hv_ref, cwv, duv, nuv, dhv_ref, dcwv_ref, dcbv_ref)):
                dh = cw[2:3, :] * du + cw[1:2, :] * _shift_up(du, nu, 1) + cw[0:1, :] * _shift_up(du, nu, 2)
                dh_ref[pl.ds(r0, rows), :] = _bf(dh)
                cur = h_ref[pl.ds(r0, rows), :]
                prev8 = _rows_before(h_ref, r0, first)
                dcw_ref[0:1, :] += jnp.sum(du * _shift_down(cur, prev8, 2), axis=0, keepdims=True)
                dcw_ref[1:2, :] += jnp.sum(du * _shift_down(cur, prev8, 1), axis=0, keepdims=True)
                dcw_ref[2:3, :] += jnp.sum(du * cur, axis=0, keepdims=True)
                dcb_ref[...] += jnp.sum(du, axis=0, keepdims=True)
            return 0

        lax.fori_loop(0, nr, step, 0)

    blk = lambda o: BS((S, CONV_TC), lambda j, b: (b, j + o))
    w3 = lambda o: BS((3, CONV_TC), lambda j, b: (0, j + o))
    w1 = lambda o: BS((1, CONV_TC), lambda j, b: (0, j + o))
    return pl.pallas_call(
        body, name="conv_gate_bwd",
        out_shape=(SDS((T, D_FF), BF16), SDS((T, D_FF), BF16), SDS((3, D_FF), F32), SDS((3, D_FF), F32),
                   SDS((1, D_FF), F32), SDS((1, D_FF), F32)),
        grid=(NJ, T // S),
        in_specs=[blk(0), blk(NJ), blk(0), w3(0), w3(NJ), w1(0), w1(NJ)],
        out_specs=(blk(0), blk(0), w3(0), w3(0), w1(0), w1(0)),
        compiler_params=_cp(("parallel", "arbitrary")),
    )(hup, hup, dact, cw, cw, cb, cb)


def _mix_bwd(docat, o_sb, o_gla, proj, g_sb, g_gla):
    T = docat.shape[0]
    tm = _tile(T, 512)

    def body(d_ref, osb_ref, ogl_ref, og_ref, gsb_ref, ggl_ref, dsb_ref, dgl_ref, dog_ref, dgsb_ref, dggl_ref):
        @pl.when(pl.program_id(0) == 0)
        def _():
            dgsb_ref[...] = jnp.zeros_like(dgsb_ref)
            dggl_ref[...] = jnp.zeros_like(dggl_ref)

        bd64 = _head_blockdiag(SB_W, SB_HD)
        bd128 = _head_blockdiag(GLA_W, GLA_DV)
        d = d_ref[:, 0:SB_W]
        o = osb_ref[...]
        r = lax.rsqrt(_split_dot(o * o, bd64, 2) * (1.0 / SB_HD) + EPS)
        n = o * r
        dn = d * gsb_ref[...]
        dgsb_ref[...] += jnp.sum(d * n, axis=0, keepdims=True)
        dsb_ref[...] = r * (dn - n * (_split_dot(dn * n, bd64, 2) * (1.0 / SB_HD)))

        d = d_ref[:, SB_W:D]
        o = ogl_ref[...]
        r = lax.rsqrt(_split_dot(o * o, bd128, 2) * (1.0 / GLA_DV) + EPS)
        n = o * r
        og = og_ref[...]
        sg = jax.nn.sigmoid(og)
        dm = d * (og * sg)
        dog_ref[...] = _bf((d * (n * ggl_ref[...])) * (sg * (1.0 + og * (1.0 - sg))))
        dn = dm * ggl_ref[...]
        dggl_ref[...] += jnp.sum(dm * n, axis=0, keepdims=True)
        dgl_ref[...] = r * (dn - n * (_split_dot(dn * n, bd128, 2) * (1.0 / GLA_DV)))

    row = lambda w: BS((tm, w), lambda i: (i, 0))
    vec = lambda w: BS((1, w), lambda i: (0, 0))
    ogb = BS((tm, 512), lambda i: (i, OG_BLK))
    return pl.pallas_call(
        body, name="mix_bwd",
        out_shape=(SDS((T, SB_W), F32), SDS((T, GLA_W), F32), SDS((T, PROJ_W), BF16), SDS((1, SB_W), F32),
                   SDS((1, GLA_W), F32)),
        grid=(T // tm,),
        in_specs=[row(D), row(SB_W), row(GLA_W), ogb, vec(SB_W), vec(GLA_W)],
        out_specs=(row(SB_W), row(GLA_W), ogb, vec(SB_W), vec(GLA_W)),
        compiler_params=_cp(("arbitrary",)),
    )(docat, o_sb, o_gla, proj, g_sb, g_gla)


def _sb_bwd(proj, tt, do, dproj, S, pieces):
    T = proj.shape[0]
    nq = S // QT
    scale = SB_HD ** -0.5
    nb, ns = T // S, len(pieces)

    def body(qkv_ref, tt_ref, do_ref, dp_in_ref, *rest):
        del dp_in_ref
        pc_refs, dp_ref, got_refs = rest[:ns], rest[ns], rest[ns + 1:2 * ns + 1]
        dk_acc, dv_acc = rest[2 * ns + 1:2 * ns + 3]
        sems = rest[2 * ns + 3:]
        first = jnp.logical_and(pl.program_id(0) == 0, pl.program_id(1) == 0)
        last = jnp.logical_and(pl.program_id(0) == nb - 1, pl.program_id(1) == 3)

        @pl.when(first)
        def _():
            _exchange_ops(pc_refs, got_refs, False, sems, "start")

        row, col, lane, kr, kc = _sb_masks()
        mincl = _bf(kr <= kc)
        mexcl = _bf(kr < kc)
        dk_acc[...] = jnp.zeros_like(dk_acc)
        dv_acc[...] = jnp.zeros_like(dv_acc)

        def qloop(qt, _):
            r0 = pl.multiple_of(qt * QT, QT)
            qs = _stack_heads(qkv_ref[pl.ds(r0, QT), 0:128] * scale, lane)
            dos = _stack_heads(do_ref[pl.ds(r0, QT), :], lane)
            ttv = tt_ref[pl.ds(r0, QT), :]
            tot = jnp.concatenate([ttv[:, 0:1], ttv[:, SB_HD:SB_HD + 1]], axis=0)
            walked = qt + 1

            def step(kt, st):
                dq, lc, pc = st
                k0 = pl.multiple_of(kt * TK, TK)
                kv = _bf(qkv_ref[pl.ds(k0, TK), 128:256])
                vv = _bf(qkv_ref[pl.ds(k0, TK), 256:384])
                strict = (col + (kt - qt) * TK) < row
                z = jnp.where(strict, _dot_nt(qs, kv), SB_MASKED)
                sp = _softplus(z)
                lg = -sp
                after = tot - (lc + _split_dot(lg, mincl, 2))
                gl = z - sp
                w = jnp.exp(gl + after)
                du = w * _dot_nt(dos, vv)
                beta = jnp.exp(gl)
                pex = pc + _split_dot(du, mexcl, 2)
                dz = _bf(du - beta * (du + pex))
                dk_acc[pl.ds(k0, TK), :] += _dot_tn(dz, qs)
                dv_acc[pl.ds(k0, TK), :] += _dot_tn(_bf(w), dos)
                return (dq + _dot(dz, kv), lc + jnp.sum(lg, axis=1, keepdims=True),
                        pc + jnp.sum(du, axis=1, keepdims=True))

            zc = jnp.zeros((2 * QT, 1), F32)
            dq, _, _ = lax.fori_loop(qt - walked + 1, qt + 1, step, (jnp.zeros((2 * QT, LANE), F32), zc, zc))
            dp_ref[pl.ds(r0, QT), 0:128] = _bf(jnp.where(lane < SB_HD, dq[:QT], dq[QT:]) * scale)
            return 0

        lax.fori_loop(0, nq, qloop, 0)
        dp_ref[:, 128:256] = _bf(dk_acc[...])
        dp_ref[:, 256:384] = _bf(dv_acc[...])

        @pl.when(last)
        def _():
            _exchange_ops(pc_refs, got_refs, False, sems, "wait")

    blk = BS((S, 384), lambda b, p: (b, p))
    col_spec = BS((S, LANE), lambda b, p: (b, p))
    hbm = BS(memory_space=pltpu.HBM)
    return pl.pallas_call(
        body, name="sb_bwd", out_shape=(SDS((T, PROJ_W), BF16),) + tuple(SDS(s.shape, s.dtype) for s in pieces),
        grid=(nb, 4),
        in_specs=[blk, col_spec, col_spec, BS(memory_space=pl.ANY)] + [hbm] * ns, out_specs=(blk,) + (hbm,) * ns,
        scratch_shapes=[pltpu.VMEM((S, LANE), F32), pltpu.VMEM((S, LANE), F32)] + _exchange_sems(ns),
        input_output_aliases={3: 0},
        compiler_params=_cp(("arbitrary", "arbitrary")),
    )(proj, tt, do, dproj, *pieces)


def _gla_bwd(proj, wgu, bgu, do, dproj, S):
    T = proj.shape[0]
    nc, ng = S // CHUNK, S // GR

    def body(blk_ref, glr_ref, wgu_ref, bgu_ref, do_ref, dp_in_ref, dp_ref, dpre_ref, st_ref):
        del dp_in_ref
        causal, tri_incl, tri_rev, lane = _gla_masks()
        wg = _bf(wgu_ref[...])
        bg = bgu_ref[...]

        def fwd_group(g, states):
            r0 = pl.multiple_of(g * GR, GR)
            v, _, _, _, _, _, _, ke, decay = _gla_group_terms(blk_ref, glr_ref, wg, bg, r0, tri_incl)
            keb = _bf(ke)
            new_states = []
            for hh in range(2):
                vh = _bf(v[:, 128 * hh:128 * hh + 128])
                st = states[hh]
                for n in range(GLA_G):
                    rows = _chunk_rows(n)
                    st_ref[hh, g * GLA_G + n] = st
                    st = st * decay[n * CHUNK:n * CHUNK + 1] + _dot_tn(vh[rows], keb[rows])
                new_states.append(st)
            return tuple(new_states)

        z = jnp.zeros((GLA_DV, LANE), F32)
        lax.fori_loop(0, ng, fwd_group, (z, z))

        def bwd_group(it, dstates):
            g = ng - 1 - it
            r0 = pl.multiple_of(g * GR, GR)
            v, pre, b, b_last, eb, qd, ki, ke, decay = _gla_group_terms(
                blk_ref, glr_ref, wg, bg, r0, tri_incl)
            kib = _bf(ki)
            dqd = jnp.zeros((GR, LANE), F32)
            dki = jnp.zeros((GR, LANE), F32)
            dke = jnp.zeros((GR, LANE), F32)
            ddec = jnp.zeros((GR, LANE), F32)
            new_dstates, dvs = [], []
            for hh in range(2):
                hm = (lane // GLA_DK) == hh
                qm = _bf(jnp.where(hm, qd, 0.0))
                kem = _bf(jnp.where(hm, ke, 0.0))
                vh = _bf(v[:, 128 * hh:128 * hh + 128])
                doh = _bf(do_ref[pl.ds(r0, GR), 128 * hh:128 * hh + 128])
                attn = _bf(jnp.where(causal, _dot_nt(qm, kib), 0.0))
                dattn = _bf(jnp.where(causal, _dot_nt(doh, vh), 0.0))
                dv_intra = _dot_tn(attn, doh)
                dqd_intra = _dot(dattn, kib)
                dki = dki + _dot_tn(dattn, qm)
                dst = dstates[hh]
                dv_p, dqd_p, dke_p, ddec_p = [None] * GLA_G, [None] * GLA_G, [None] * GLA_G, [None] * GLA_G
                for n in reversed(range(GLA_G)):
                    rows = _chunk_rows(n)
                    st = st_ref[hh, g * GLA_G + n]
                    dv_p[n] = dv_intra[rows] + _dot_nt(kem[rows], _bf(dst))
                    dqd_p[n] = dqd_intra[rows] + _dot(doh[rows], _bf(st))
                    dke_p[n] = _dot(vh[rows], _bf(dst))
                    ddec_p[n] = jnp.broadcast_to(jnp.sum(dst * st, axis=0, keepdims=True), (CHUNK, LANE))
                    dst = dst * decay[n * CHUNK:n * CHUNK + 1] + _dot_tn(doh[rows], qm[rows])
                dvs.append(jnp.concatenate(dv_p, axis=0))
                dqd = dqd + jnp.where(hm, jnp.concatenate(dqd_p, axis=0), 0.0)
                dke = dke + jnp.where(hm, jnp.concatenate(dke_p, axis=0), 0.0)
                ddec = ddec + jnp.where(hm, jnp.concatenate(ddec_p, axis=0), 0.0)
                new_dstates.append(dst)
            einv = jnp.exp(-b)
            eend = jnp.exp(b_last - b)
            dq = (dqd * eb) * (GLA_DK ** -0.5)
            dk = dki * einv + dke * eend
            db = dqd * qd - dki * ki - dke * ke
            dkk = dke * ke
            db_last = _per_chunk(lambda rows: jnp.sum(dkk[rows], axis=0, keepdims=True)) + ddec * decay
            dla = _split_dot(db, tri_rev, 2, left=False) + db_last
            dpre_ref[pl.ds(r0, GR), :] = (dla * (1.0 / GATE_NORM)) * (1.0 - jax.nn.sigmoid(pre))
            dp_ref[pl.ds(r0, GR), 0:128] = _bf(dq)
            dp_ref[pl.ds(r0, GR), 128:256] = _bf(dk)
            dp_ref[pl.ds(r0, GR), 256:512] = _bf(jnp.concatenate(dvs, axis=1))
            return tuple(new_dstates)

        lax.fori_loop(0, ng, bwd_group, (z, z))

    return pl.pallas_call(
        body, name="gla_bwd", out_shape=(SDS((T, PROJ_W), BF16), SDS((T, GLA_KW), F32)), grid=(T // S, 2),
        in_specs=[BS((S, 512), lambda b, p: (b, 3 + p)), BS((S, LANE), lambda b, p: (b, GLR_BLK)),
                  BS((LANE, LANE), lambda b, p: (0, p)), BS((1, LANE), lambda b, p: (0, p)),
                  BS((S, 256), lambda b, p: (b, p)), BS(memory_space=pl.ANY)],
        out_specs=(BS((S, 512), lambda b, p: (b, 3 + p)), BS((S, LANE), lambda b, p: (b, p))),
        scratch_shapes=[pltpu.VMEM((2, nc, GLA_DV, LANE), F32)],
        input_output_aliases={5: 0},
        compiler_params=_cp(("parallel", "parallel")),
    )(proj, proj, wgu, bgu, do, dproj)


def _gate_bwd(dpre, proj, wgu, dproj):
    T = dpre.shape[0]
    tm = _tile(T, 512)

    def body(dpre_ref, glr_ref, wgu_ref, dp_in_ref, dp_ref, dw_ref, db_ref):
        del dp_in_ref

        @pl.when(pl.program_id(0) == 0)
        def _():
            dw_ref[...] = jnp.zeros_like(dw_ref)
            db_ref[...] = jnp.zeros_like(db_ref)

        dpre = dpre_ref[...]
        dp_ref[...] = _bf(_dot_nt(_bf(dpre), _bf(wgu_ref[...])))
        dw_ref[...] += _dot_tn(_bf(glr_ref[...]), _bf(dpre))
        db_ref[...] += jnp.sum(dpre, axis=0, keepdims=True)

    glr = BS((tm, LANE), lambda i: (i, GLR_BLK))
    return pl.pallas_call(
        body, name="gate_bwd",
        out_shape=(SDS((T, PROJ_W), BF16), SDS((LANE, GLA_KW), F32), SDS((1, GLA_KW), F32)), grid=(T // tm,),
        in_specs=[BS((tm, GLA_KW), lambda i: (i, 0)), glr, BS((LANE, GLA_KW), lambda i: (0, 0)),
                  BS(memory_space=pl.ANY)],
        out_specs=(glr, BS((LANE, GLA_KW), lambda i: (0, 0)), BS((1, GLA_KW), lambda i: (0, 0))),
        input_output_aliases={3: 0},
        compiler_params=_cp(("arbitrary",)),
    )(dpre, proj, wgu, dproj)


def _exchange_sems(n):
    return [pltpu.SemaphoreType.DMA((n * (N_DEV - 1),)), pltpu.SemaphoreType.DMA((n * (N_DEV - 1),)),
            pltpu.SemaphoreType.DMA((n,))]


def _exchange_ops(srcs, outs, gather, sems, act):
    ssem, rsem, lsem = sems
    x, y, c = lax.axis_index("x"), lax.axis_index("y"), lax.axis_index("c")
    me = 4 * x + 2 * y + c
    for i, (s_ref, o_ref) in enumerate(zip(srcs, outs)):
        for k in range(1, N_DEV):
            px = (x + ((k >> 2) & 1)) % 2
            py = (y + ((k >> 1) & 1)) % 2
            pc = (c + (k & 1)) % 2
            peer = 4 * px + 2 * py + pc
            n = i * (N_DEV - 1) + k - 1
            out = pltpu.make_async_remote_copy(
                src_ref=s_ref if gather else s_ref.at[peer], dst_ref=o_ref.at[me],
                send_sem=ssem.at[n], recv_sem=rsem.at[n],
                device_id=(px, py, pc), device_id_type=pl.DeviceIdType.MESH)
            if act == "start":
                out.start()
            else:
                out.wait_send()
                pltpu.make_async_remote_copy(
                    src_ref=s_ref if gather else s_ref.at[me], dst_ref=o_ref.at[peer],
                    send_sem=ssem.at[n], recv_sem=rsem.at[n],
                    device_id=(x, y, c), device_id_type=pl.DeviceIdType.MESH).wait_recv()
        mine = pltpu.make_async_copy(s_ref if gather else s_ref.at[me], o_ref.at[me], lsem.at[i])
        if act == "start":
            mine.start()
        else:
            mine.wait()


def _gather_two_level(src, name):
    def body(s_ref, o_ref, ssem, rsem, lsem):
        x, y, c = lax.axis_index("x"), lax.axis_index("y"), lax.axis_index("c")
        me, sibling = (x, y, c), (x, y, 1 - c)
        chips = [(1 - x, y), (x, 1 - y), (1 - x, 1 - y)]

        def slab(px, py, pc):
            return o_ref.at[4 * px + 2 * py + pc]

        def copy(k, block, to, src_ref=None):
            return pltpu.make_async_remote_copy(
                src_ref=slab(*block) if src_ref is None else src_ref, dst_ref=slab(*block),
                send_sem=ssem.at[k], recv_sem=rsem.at[k], device_id=to, device_id_type=pl.DeviceIdType.MESH)

        mine = pltpu.make_async_copy(s_ref, slab(*me), lsem)
        mine.start()
        first = [copy(0, me, sibling, s_ref)] + [copy(1 + j, me, (*chip, c), s_ref) for j, chip in enumerate(chips)]
        for cp in first:
            cp.start()
        passed = [copy(4 + j, (*chip, c), sibling) for j, chip in enumerate(chips)]
        for j, chip in enumerate(chips):
            copy(1 + j, (*chip, c), me).wait_recv()
            passed[j].start()
        copy(0, sibling, me).wait_recv()
        for j, chip in enumerate(chips):
            copy(4 + j, (*chip, 1 - c), me).wait_recv()
        for cp in first + passed:
            cp.wait_send()
        mine.wait()

    hbm = BS(memory_space=pltpu.HBM)
    return pl.pallas_call(
        body, name=name, out_shape=SDS((N_DEV,) + src.shape, src.dtype), in_specs=[hbm], out_specs=hbm,
        scratch_shapes=[pltpu.SemaphoreType.DMA((N_DEV - 1,)), pltpu.SemaphoreType.DMA((N_DEV - 1,)),
                        pltpu.SemaphoreType.DMA(())],
    )(src)


def _exchange(srcs, gather, name):
    n = len(srcs)
    shapes = [SDS((N_DEV,) + s.shape if gather else s.shape, s.dtype) for s in srcs]

    def body(*refs):
        s_refs, o_refs, sems = refs[:n], refs[n:2 * n], refs[2 * n:]
        _exchange_ops(s_refs, o_refs, gather, sems, "start")
        _exchange_ops(s_refs, o_refs, gather, sems, "wait")

    hbm = BS(memory_space=pltpu.HBM)
    return pl.pallas_call(
        body, name=name, out_shape=tuple(shapes), in_specs=[hbm] * n, out_specs=(hbm,) * n,
        scratch_shapes=_exchange_sems(n),
    )(*srcs)


def _adamw_math(w, g, m, v):
    m = ADAM_B1 * m + (1.0 - ADAM_B1) * g
    v = ADAM_B2 * v + (1.0 - ADAM_B2) * (g * g)
    m_hat = m / (1.0 - ADAM_B1 ** ADAM_STEP)
    v_hat = v / (1.0 - ADAM_B2 ** ADAM_STEP)
    delta = -ADAM_LR * (m_hat / (jnp.sqrt(v_hat) + ADAM_EPS) + ADAM_WD * w)
    return delta, m, v


def _sum_adamw(parts, w, m, v, tr, name):
    R, C = w.shape

    def body(p_ref, w_ref, m_ref, v_ref, g_ref, d_ref, nm_ref, nv_ref):
        g = p_ref[0].astype(F32)
        for d in range(1, N_DEV):
            g = g + p_ref[d].astype(F32)
        delta, nm, nv = _adamw_math(w_ref[...], g, m_ref[...], v_ref[...])
        g_ref[...] = g
        d_ref[...] = delta
        nm_ref[...] = nm
        nv_ref[...] = nv

    blk = BS((tr, C), lambda i: (i, 0))
    out = SDS((R, C), F32)
    return pl.pallas_call(
        body, name=name, out_shape=(out, out, out, out), grid=(R // tr,),
        in_specs=[BS((N_DEV, tr, C), lambda i: (0, i, 0)), blk, blk, blk], out_specs=(blk, blk, blk, blk),
        compiler_params=_cp(("parallel",)),
    )(parts, w, m, v)


def _flat_pad_rows(parts, rows):
    flat = jnp.concatenate([p.reshape(-1) for p in parts])
    return jnp.pad(flat, (0, rows * D - flat.shape[0])).reshape(rows, D)


SMALL_ROWS = 16
SHARD_SMALL_ROWS = 3


def kernel(x, attn_norm_g, w_in, w_gate_up, b_gate_up, sb_out_g, gla_out_g, w_out, ffn_norm_g, w_ffn_up, conv_w, conv_b, w_ffn_down, final_norm_g, loss_target, m_attn_norm_g, m_w_in, m_w_gate_up, m_b_gate_up, m_sb_out_g, m_gla_out_g, m_w_out, m_ffn_norm_g, m_w_ffn_up, m_conv_w, m_conv_b, m_w_ffn_down, m_final_norm_g, v_attn_norm_g, v_w_in, v_w_gate_up, v_b_gate_up, v_sb_out_g, v_gla_out_g, v_w_out, v_ffn_norm_g, v_w_ffn_up, v_conv_w, v_conv_b, v_w_ffn_down, v_final_norm_g):
    Bd, S, _ = x.shape
    T = Bd * S
    x2d = x.reshape(T, D)
    tgt = loss_target.reshape(T, D)
    c_up = w_ffn_up.shape[2]
    c_gu = w_gate_up.shape[2]
    c_in = w_in.shape[2]

    n_gu = GATE_RANK * c_gu
    rows_bf = lambda w: w[0].T.astype(BF16)
    small_w = lambda wgu, cw: _flat_pad_rows([wgu, cw], SHARD_SMALL_ROWS)

    g_in = _gather_two_level(rows_bf(w_in), "gather_w_in")
    w_in_pt = jnp.pad(g_in.reshape(IN_COLS, D), ((0, 1), (0, 0)))[_PERM]
    g3 = final_norm_g.reshape(1, D)

    proj, h1 = _norm_proj(x2d, attn_norm_g, w_in_pt)
    o_sb, tt, g_up, g_down, g_out, gs = _sb_fwd(
        proj, S, [rows_bf(w_ffn_up), w_ffn_down[0].astype(BF16), w_out[0].astype(BF16),
                  _flat_pad_rows([w_gate_up, conv_w], 8)])
    w_out_f = g_out.reshape(D, D)
    gsf = gs.reshape(N_DEV, -1)
    wgu_f = jnp.transpose(gsf[:, :n_gu].reshape(N_DEV, GATE_RANK, c_gu), (1, 0, 2)).reshape(GATE_RANK, GLA_KW)
    cw_f = jnp.transpose(gsf[:, n_gu:n_gu + 3 * c_up].reshape(N_DEV, 3, c_up), (1, 0, 2)).reshape(3, 2 * D_FF)
    wgu_p = jnp.pad(wgu_f, ((0, LANE - GATE_RANK), (0, 0)))
    w_up_t = g_up.reshape(2 * D_FF, D)
    w_down_f = g_down.reshape(D_FF, D)
    o_gla = _gla_fwd(proj, wgu_p, b_gate_up, S)
    x1, ocat, h2 = _mix_out(o_sb, o_gla, proj, x2d, sb_out_g, gla_out_g, w_out_f, ffn_norm_g)
    hup = _mm(h2, w_up_t, "nt", "ffn_up", tm=1024, tn=1408, tk=1024)
    act = _conv_gate(hup, cw_f, conv_b, S)
    dx2, dg3, loss_dev = _down_loss(act, w_down_f, x1, tgt, g3)

    dw_down = _mm(act, dx2, "tn", "dw_down", out_dtype=BF16, tm=D_FF, tn=1024, tk=512)
    dact = _mm(dx2, w_down_f, "nt", "dact", tm=1024, tn=1408, tk=1024)
    dhup_a, dhup_v, dcw_a, dcw_v, dcb_a, dcb_v = _conv_gate_bwd(hup, dact, cw_f, conv_b, S)
    dw_up_t = _mm(dhup_a, h2, "tn", "dw_up_a", out_dtype=BF16, tm=D_FF, tn=1024, tk=512, out_rows=2 * D_FF)
    dw_up_t = _mm(dhup_v, h2, "tn", "dw_up_v", out_dtype=BF16, tm=D_FF, tn=1024, tk=512, out_rows=2 * D_FF,
                  out_row0=D_FF, into=dw_up_t)
    dh2 = _mm(dhup_a, w_up_t, "nn", "dh2_a", tm=1024, tn=1024, tk=1408)
    dx1, dg2 = _mm(dhup_v, w_up_t, "nn", "dh2_v", c=dh2, tm=512, tn=1024, tk=1408, b_row0=D_FF,
                   norm_bwd=(x1, ffn_norm_g, dx2))

    dw_out = _mm(ocat, dx1, "tn", "dw_out", out_dtype=BF16, tm=1024, tn=1024, tk=512)
    docat = _mm(dx1, w_out_f, "nt", "docat", tm=1024, tn=1024, tk=1024)
    do_sb, do_gla, dproj, dg_sb, dg_gla = _mix_bwd(docat, o_sb, o_gla, proj, sb_out_g, gla_out_g)
    dproj, got_up, got_down, got_out = _sb_bwd(
        proj, tt, do_sb, dproj, S,
        [dw_up_t.reshape(N_DEV, c_up, D), dw_down.reshape(N_DEV, -1, D), dw_out.reshape(N_DEV, -1, D)])
    dproj, dpre = _gla_bwd(proj, wgu_p, b_gate_up, do_gla, dproj, S)
    dproj, dwgu, dbgu = _gate_bwd(dpre, proj, wgu_p, dproj)
    dw_in_pt = _mm(dproj, h1, "tn", "dw_in", out_dtype=BF16, tm=PROJ_W, tn=1024, tk=512)
    dx, dg1, got_in = _mm(dproj, w_in_pt, "nn", "dh1", tm=1024, tn=1024, tk=640,
                          xchg=([dw_in_pt[_INV_PERM].reshape(N_DEV, c_in, D)], False),
                          norm_bwd=(x2d, attn_norm_g, dx1))

    dcw = jnp.concatenate([dcw_a, dcw_v], axis=1)
    dwgu_pc = jnp.transpose(dwgu[:GATE_RANK].reshape(GATE_RANK, N_DEV, c_gu), (1, 0, 2)).reshape(N_DEV, -1)
    dcw_pc = jnp.transpose(dcw.reshape(3, N_DEV, c_up), (1, 0, 2)).reshape(N_DEV, -1)
    small_pc = jnp.concatenate([dwgu_pc, dcw_pc], axis=1)
    small_pc = jnp.pad(small_pc, ((0, 0), (0, SHARD_SMALL_ROWS * D - small_pc.shape[1])))
    small_pc = small_pc.reshape(N_DEV, SHARD_SMALL_ROWS, D).astype(BF16)
    rep_names = ["attn_norm_g", "b_gate_up", "sb_out_g", "gla_out_g", "ffn_norm_g", "conv_b", "final_norm_g"]
    rep_g = [dg1, dbgu, dg_sb, dg_gla, dg2, jnp.concatenate([dcb_a, dcb_v], axis=1), dg3]
    rep_w = [attn_norm_g, b_gate_up, sb_out_g, gla_out_g, ffn_norm_g, conv_b, final_norm_g]
    rep_m = [m_attn_norm_g, m_b_gate_up, m_sb_out_g, m_gla_out_g, m_ffn_norm_g, m_conv_b, m_final_norm_g]
    rep_v = [v_attn_norm_g, v_b_gate_up, v_sb_out_g, v_gla_out_g, v_ffn_norm_g, v_conv_b, v_final_norm_g]
    rep_pc = jnp.broadcast_to(_flat_pad_rows(rep_g, SMALL_ROWS), (N_DEV, SMALL_ROWS, D))
    got_sm, got_rep = _exchange([small_pc, rep_pc], False, "scatter_tail")

    rows = lambda w: w[0].T
    cols = lambda r: r.T[None]
    res = {}
    res["w_in"] = [cols(r) for r in _sum_adamw(got_in, rows(w_in), rows(m_w_in), rows(v_w_in), c_in, "adamw_w_in")]
    res["w_out"] = [r[None] for r in _sum_adamw(got_out, w_out[0], m_w_out[0], v_w_out[0], w_out.shape[1],
                                                 "adamw_w_out")]
    res["w_ffn_up"] = [cols(r) for r in _sum_adamw(got_up, rows(w_ffn_up), rows(m_w_ffn_up), rows(v_w_ffn_up),
                                                    c_up // 2, "adamw_w_up")]
    res["w_ffn_down"] = [r[None] for r in _sum_adamw(got_down, w_ffn_down[0], m_w_ffn_down[0], v_w_ffn_down[0],
                                                      w_ffn_down.shape[1], "adamw_w_down")]
    sm = _sum_adamw(got_sm, small_w(w_gate_up, conv_w), small_w(m_w_gate_up, m_conv_w),
                    small_w(v_w_gate_up, v_conv_w), SHARD_SMALL_ROWS, "adamw_small_sharded")
    res["w_gate_up"] = [r.reshape(-1)[:n_gu].reshape(1, GATE_RANK, c_gu) for r in sm]
    res["conv_w"] = [r.reshape(-1)[n_gu:n_gu + 3 * c_up].reshape(1, 3, c_up) for r in sm]
    rep = _sum_adamw(got_rep, _flat_pad_rows(rep_w, SMALL_ROWS), _flat_pad_rows(rep_m, SMALL_ROWS),
                     _flat_pad_rows(rep_v, SMALL_ROWS), SMALL_ROWS, "adamw_replicated")
    o = 0
    for n, w in zip(rep_names, rep_w):
        res[n] = [r.reshape(-1)[o:o + w.size].reshape(w.shape) for r in rep]
        o += w.size

    loss = lax.psum(loss_dev[0, 0], ("x", "y", "c"))
    order = ["attn_norm_g", "w_in", "w_gate_up", "b_gate_up", "sb_out_g", "gla_out_g", "w_out", "ffn_norm_g",
             "w_ffn_up", "conv_w", "conv_b", "w_ffn_down", "final_norm_g"]
    outs = [loss, dx.reshape(Bd, S, D)]
    for k in range(4):
        outs += [res[n][k] for n in order]
    return tuple(outs)
```

```python
import functools

import numpy as np
import jax
import jax.numpy as jnp
from jax import lax
from jax.experimental import pallas as pl
from jax.experimental.pallas import tpu as pltpu

F32 = jnp.float32
BF16 = jnp.bfloat16
SDS = jax.ShapeDtypeStruct
BS = pl.BlockSpec

N_DEV = 8
D = 1024
EPS = 1e-6
SB_HD = 64
SB_W = 512
GLA_DK = 64
GLA_DV = 128
GLA_KW = 256
GLA_W = 512
GATE_RANK = 16
GATE_NORM = 16.0
CHUNK = 64
GLA_G = 4
GR = GLA_G * CHUNK
QT = 256
D_FF = 2816
IN_COLS = 3088
PROJ_W = 3200
LANE = 128
VMEM_LIMIT = 56 * 1024 * 1024

ADAM_LR, ADAM_B1, ADAM_B2, ADAM_EPS, ADAM_WD, ADAM_STEP = 0.001, 0.9, 0.999, 1e-08, 0.01, 10


def _proj_perm():
    sbq, sbk, sbv = 0, 512, 1024
    gq, gk, gv, glr, gog = 1536, 1792, 2048, 2560, 2576
    cols = []
    for p in range(4):
        for base in (sbq, sbk, sbv):
            cols += list(range(base + 128 * p, base + 128 * p + 128))
    for p in range(2):
        cols += list(range(gq + 128 * p, gq + 128 * p + 128))
        cols += list(range(gk + 128 * p, gk + 128 * p + 128))
        cols += list(range(gv + 256 * p, gv + 256 * p + 256))
    cols += list(range(gog, gog + 512))
    cols += list(range(glr, glr + GATE_RANK)) + [IN_COLS] * (LANE - GATE_RANK)
    perm = np.asarray(cols, np.int32)
    inv = np.zeros((IN_COLS,), np.int32)
    for new, old in enumerate(cols):
        if old < IN_COLS:
            inv[old] = new
    return perm, inv


_PERM, _INV_PERM = _proj_perm()
OG_BLK = 5
GLR_BLK = 24


def _cp(sem=None, vmem=VMEM_LIMIT):
    return pltpu.CompilerParams(dimension_semantics=sem, vmem_limit_bytes=vmem)


def _dot(a, b):
    return lax.dot_general(a, b, (((1,), (0,)), ((), ())), preferred_element_type=F32)


def _dot_nt(a, b):
    return lax.dot_general(a, b, (((1,), (1,)), ((), ())), preferred_element_type=F32)


def _dot_tn(a, b):
    return lax.dot_general(a, b, (((0,), (0,)), ((), ())), preferred_element_type=F32)


def _bf(x):
    return x.astype(BF16)


def _split_dot(x, m, passes, left=True):
    acc = None
    r = x
    for i in range(passes):
        h = r.astype(BF16)
        t = _dot(h, m) if left else _dot(m, h)
        acc = t if acc is None else acc + t
        if i + 1 < passes:
            r = r - h.astype(F32)
    return acc


def _softplus(z):
    return jnp.maximum(z, 0.0) + jnp.log(1.0 + jnp.exp(-jnp.abs(z)))


def _rms_bwd_math(x, g, dh, dres):
    r = lax.rsqrt(jnp.mean(x * x, axis=-1, keepdims=True) + EPS)
    xh = x * r
    dxh = dh * g
    dx = dres + r * (dxh - xh * jnp.mean(dxh * xh, axis=-1, keepdims=True))
    return dx, jnp.sum(dh * xh, axis=0, keepdims=True)


def _tile(n, pref, mult=LANE):
    best = None
    for t in range(mult, min(n, pref) + 1, mult):
        if n % t == 0:
            best = t
    return best if best is not None else n


def _mm(a, b, mode, name, out_dtype=F32, c=None, tm=512, tn=512, tk=512, b_row0=0, out_rows=None, out_row0=0,
        into=None, xchg=None, norm_bwd=None):
    if mode == "nn":
        (M, K), N = a.shape, b.shape[1]
    elif mode == "nt":
        (M, K), N = a.shape, b.shape[0]
    else:
        (K, M), N = a.shape, b.shape[1]
    tm, tn, tk = _tile(M, tm), _tile(N, tn), _tile(K, tk)
    nk = K // tk
    kb0, ob0 = b_row0 // tk, out_row0 // tm
    assert kb0 * tk == b_row0 and ob0 * tm == out_row0 and (mode == "nn" or b_row0 == 0)
    ni, nj = M // tm, N // tn
    j_outer = nk == 1 and (nj - 1) * a.size * a.dtype.itemsize < (ni - 1) * K * N * b.dtype.itemsize
    ix = (lambda f: (lambda j, i, k: f(i, j, k))) if j_outer else (lambda f: f)
    a_spec = BS((tk, tm), ix(lambda i, j, k: (k, i))) if mode == "tn" else BS((tm, tk), ix(lambda i, j, k: (i, k)))
    b_spec = (BS((tn, tk), ix(lambda i, j, k: (j, k))) if mode == "nt"
              else BS((tk, tn), ix(lambda i, j, k: (k + kb0, j))))
    dotfn = {"nn": _dot, "nt": _dot_nt, "tn": _dot_tn}[mode]
    has_c = c is not None
    has_into = into is not None
    nx = 0 if xchg is None else len(xchg[0])
    has_nb = norm_bwd is not None
    assert not has_nb or (nj == 1 and not j_outer and out_dtype == F32)
    n_in = 2 + has_c + has_into + 3 * has_nb

    def body(*refs):
        a_ref, b_ref = refs[:2]
        c_ref = refs[2] if has_c else None
        x_src = refs[n_in:n_in + nx]
        outs = refs[n_in + nx:n_in + 2 * nx + 1 + has_nb]
        o_ref, x_out = outs[0], outs[1 + has_nb:]
        acc = refs[n_in + 2 * nx + 1 + has_nb]
        sems = refs[n_in + 2 * nx + 2 + has_nb:]
        k = pl.program_id(2)
        g0, g1 = pl.program_id(0), pl.program_id(1)
        n0, n1 = (nj, ni) if j_outer else (ni, nj)
        first = jnp.logical_and(jnp.logical_and(g0 == 0, g1 == 0), k == 0)
        if nx:
            @pl.when(first)
            def _():
                _exchange_ops(x_src, x_out, xchg[1], sems, "start")

        if has_nb:
            @pl.when(first)
            def _():
                outs[1][...] = jnp.zeros_like(outs[1])

        @pl.when(k == 0)
        def _():
            acc[...] = jnp.zeros_like(acc)

        acc[...] += dotfn(_bf(a_ref[...]), _bf(b_ref[...]))

        @pl.when(k == nk - 1)
        def _():
            r = acc[...]
            if has_c:
                r = r + c_ref[...]
            if has_nb:
                x_ref, g_ref, dres_ref = refs[n_in - 3:n_in]
                dx, dg = _rms_bwd_math(x_ref[...], g_ref[...], r, dres_ref[...])
                o_ref[...] = dx
                outs[1][...] += dg
            else:
                o_ref[...] = r.astype(out_dtype)

        if nx:
            @pl.when(jnp.logical_and(jnp.logical_and(g0 == n0 - 1, g1 == n1 - 1), k == nk - 1))
            def _():
                _exchange_ops(x_src, x_out, xchg[1], sems, "wait")

    tile = BS((tm, tn), ix(lambda i, j, k: (i, j)))
    in_specs = [a_spec, b_spec]
    args = [a, b]
    if has_c:
        in_specs.append(tile)
        args.append(c)
    aliases = {}
    if has_into:
        aliases = {len(args): 0}
        in_specs.append(BS(memory_space=pl.ANY))
        args.append(into)
    out_shape = [SDS((out_rows or M, N), out_dtype)]
    out_specs = [BS((tm, tn), ix(lambda i, j, k: (i + ob0, j)))]
    scratch = [pltpu.VMEM((tm, tn), F32)]
    if has_nb:
        in_specs += [tile, BS((1, tn), lambda i, j, k: (0, 0)), tile]
        args += list(norm_bwd)
        out_shape.append(SDS((1, N), F32))
        out_specs.append(BS((1, tn), lambda i, j, k: (0, 0)))
    if nx:
        hbm = BS(memory_space=pltpu.HBM)
        in_specs += [hbm] * nx
        args += list(xchg[0])
        out_shape += [SDS((N_DEV,) + s.shape if xchg[1] else s.shape, s.dtype) for s in xchg[0]]
        out_specs += [hbm] * nx
        scratch += _exchange_sems(nx)
    serial = nx or has_nb
    res = pl.pallas_call(
        body, name=name, out_shape=tuple(out_shape), grid=(nj, ni, nk) if j_outer else (ni, nj, nk),
        in_specs=in_specs, out_specs=tuple(out_specs),
        scratch_shapes=scratch, input_output_aliases=aliases,
        compiler_params=_cp(("arbitrary",) * 3 if serial else ("parallel", "parallel", "arbitrary")),
    )(*args)
    return res if serial else res[0]


def _norm_proj(x, g, w):
    T, N = x.shape[0], w.shape[0]
    tm = _tile(T, 512)

    def body(x_ref, g_ref, w_ref, p_ref, h_ref):
        xv = x_ref[...]
        r = lax.rsqrt(jnp.mean(xv * xv, axis=-1, keepdims=True) + EPS)
        h = _bf((xv * r) * g_ref[...])
        h_ref[...] = h
        p_ref[...] = _dot_nt(h, w_ref[...])

    return pl.pallas_call(
        body, name="norm_proj", out_shape=(SDS((T, N), F32), SDS((T, D), BF16)), grid=(T // tm,),
        in_specs=[BS((tm, D), lambda i: (i, 0)), BS((1, D), lambda i: (0, 0)), BS((N, D), lambda i: (0, 0))],
        out_specs=(BS((tm, N), lambda i: (i, 0)), BS((tm, D), lambda i: (i, 0))),
        compiler_params=_cp(("parallel",)),
    )(x, g, w)


TK = 256
SB_DEAD = -104.0
SB_MASKED = -1e30
CNT_LANE = SB_HD - 1


def _sb_masks():
    row = lax.broadcasted_iota(jnp.int32, (2 * QT, TK), 0) & (QT - 1)
    col = lax.broadcasted_iota(jnp.int32, (2 * QT, TK), 1)
    lane = lax.broadcasted_iota(jnp.int32, (1, LANE), 1)
    kr = lax.broadcasted_iota(jnp.int32, (TK, TK), 0)
    kc = lax.broadcasted_iota(jnp.int32, (TK, TK), 1)
    return row, col, lane, kr, kc


def _stack_heads(x, lane):
    return jnp.concatenate([_bf(jnp.where((lane // SB_HD) == hh, x, 0.0)) for hh in range(2)], axis=0)


def _sb_fwd(proj, S, shards):
    T = proj.shape[0]
    nq = S // QT
    scale = SB_HD ** -0.5
    nb, ns = T // S, len(shards)

    def body(qkv_ref, *rest):
        sh_refs, (o_ref, tt_ref), g_refs = rest[:ns], rest[ns:ns + 2], rest[ns + 2:2 * ns + 2]
        sems = rest[2 * ns + 2:]
        first = jnp.logical_and(pl.program_id(0) == 0, pl.program_id(1) == 0)
        last = jnp.logical_and(pl.program_id(0) == nb - 1, pl.program_id(1) == 3)

        @pl.when(first)
        def _():
            _exchange_ops(sh_refs, g_refs, True, sems, "start")

        row, col, lane, kr, kc = _sb_masks()
        msuf = _bf(kr > kc)

        def qloop(qt, _):
            r0 = pl.multiple_of(qt * QT, QT)
            qs = _stack_heads(qkv_ref[pl.ds(r0, QT), 0:128] * scale, lane)

            def live(st):
                it, _, cy = st
                return jnp.logical_and(it <= qt, jnp.max(cy) > SB_DEAD)

            def step(st):
                it, acc, cy = st
                kt = qt - it
                k0 = pl.multiple_of(kt * TK, TK)
                kv = _bf(qkv_ref[pl.ds(k0, TK), 128:256])
                vv = _bf(qkv_ref[pl.ds(k0, TK), 256:384])
                strict = (col + (kt - qt) * TK) < row
                z = jnp.where(strict, _dot_nt(qs, kv), SB_MASKED)
                sp = _softplus(z)
                lg = -sp
                after = cy + _split_dot(lg, msuf, 2)
                w = jnp.exp((z - sp) + after)
                return it + 1, acc + _dot(_bf(w), vv), cy + jnp.sum(lg, axis=1, keepdims=True)

            it, acc, cy = lax.while_loop(
                live, step, (jnp.int32(0), jnp.zeros((2 * QT, LANE), F32), jnp.zeros((2 * QT, 1), F32)))
            o_ref[pl.ds(r0, QT), :] = jnp.where(lane < SB_HD, acc[:QT], acc[QT:])
            tt = jnp.where(lane < SB_HD, cy[:QT], cy[QT:])
            tt_ref[pl.ds(r0, QT), :] = jnp.where(lane == CNT_LANE, it.astype(F32), tt)
            return 0

        lax.fori_loop(0, nq, qloop, 0)

        @pl.when(last)
        def _():
            _exchange_ops(sh_refs, g_refs, True, sems, "wait")

    hbm = BS(memory_space=pltpu.HBM)
    col_spec = BS((S, LANE), lambda b, p: (b, p))
    return pl.pallas_call(
        body, name="sb_fwd",
        out_shape=(SDS((T, SB_W), F32), SDS((T, SB_W), F32)) + tuple(SDS((N_DEV,) + s.shape, s.dtype) for s in shards),
        grid=(nb, 4),
        in_specs=[BS((S, 384), lambda b, p: (b, p))] + [hbm] * ns,
        out_specs=(col_spec, col_spec) + (hbm,) * ns,
        scratch_shapes=_exchange_sems(ns),
        compiler_params=_cp(("arbitrary", "arbitrary")),
    )(proj, *shards)


def _log_sigmoid(x):
    return jnp.minimum(x, 0.0) - jnp.log1p(jnp.exp(-jnp.abs(x)))


def _gla_masks():
    r = lax.broadcasted_iota(jnp.int32, (GR, GR), 0)
    c = lax.broadcasted_iota(jnp.int32, (GR, GR), 1)
    same = (r // CHUNK) == (c // CHUNK)
    causal = jnp.logical_and(same, r >= c)
    lane = lax.broadcasted_iota(jnp.int32, (1, LANE), 1)
    return causal, _bf(causal), _bf(jnp.logical_and(same, r <= c)), lane


def _gla_group_terms(blk_ref, glr_ref, wgu, bgu, r0, tri_incl):
    q = blk_ref[pl.ds(r0, GR), 0:128]
    k = blk_ref[pl.ds(r0, GR), 128:256]
    v = blk_ref[pl.ds(r0, GR), 256:512]
    pre = _dot(_bf(glr_ref[pl.ds(r0, GR), :]), wgu) + bgu
    la = _log_sigmoid(pre) / GATE_NORM
    b = _split_dot(la, tri_incl, 3, left=False)
    b_last = _per_chunk(lambda rows: b[rows.stop - 1:rows.stop])
    eb = jnp.exp(b)
    qd = (q * (GLA_DK ** -0.5)) * eb
    ki = k * jnp.exp(-b)
    ke = k * jnp.exp(b_last - b)
    decay = jnp.exp(b_last)
    return v, pre, b, b_last, eb, qd, ki, ke, decay


def _chunk_rows(n):
    return slice(n * CHUNK, (n + 1) * CHUNK)


def _per_chunk(row_fn):
    return jnp.concatenate(
        [jnp.broadcast_to(row_fn(_chunk_rows(n)), (CHUNK, LANE)) for n in range(GLA_G)], axis=0)


def _gla_fwd(proj, wgu, bgu, S):
    T = proj.shape[0]
    ng = S // GR

    def body(blk_ref, glr_ref, wgu_ref, bgu_ref, o_ref):
        causal, tri_incl, _, lane = _gla_masks()
        wg = _bf(wgu_ref[...])
        bg = bgu_ref[...]

        def group(g, states):
            r0 = pl.multiple_of(g * GR, GR)
            v, _, _, _, _, qd, ki, ke, decay = _gla_group_terms(blk_ref, glr_ref, wg, bg, r0, tri_incl)
            kib, keb = _bf(ki), _bf(ke)
            new_states, outs = [], []
            for hh in range(2):
                hm = (lane // GLA_DK) == hh
                qm = _bf(jnp.where(hm, qd, 0.0))
                vh = _bf(v[:, 128 * hh:128 * hh + 128])
                attn = jnp.where(causal, _dot_nt(qm, kib), 0.0)
                o_intra = _dot(_bf(attn), vh)
                st = states[hh]
                parts = []
                for n in range(GLA_G):
                    rows = _chunk_rows(n)
                    parts.append(o_intra[rows] + _dot_nt(qm[rows], _bf(st)))
                    st = st * decay[n * CHUNK:n * CHUNK + 1] + _dot_tn(vh[rows], keb[rows])
                outs.append(jnp.concatenate(parts, axis=0))
                new_states.append(st)
            o_ref[pl.ds(r0, GR), :] = jnp.concatenate(outs, axis=1)
            return tuple(new_states)

        z = jnp.zeros((GLA_DV, LANE), F32)
        lax.fori_loop(0, ng, group, (z, z))

    return pl.pallas_call(
        body, name="gla_fwd", out_shape=SDS((T, GLA_W), F32), grid=(T // S, 2),
        in_specs=[BS((S, 512), lambda b, p: (b, 3 + p)), BS((S, LANE), lambda b, p: (b, GLR_BLK)),
                  BS((LANE, LANE), lambda b, p: (0, p)), BS((1, LANE), lambda b, p: (0, p))],
        out_specs=BS((S, 256), lambda b, p: (b, p)),
        compiler_params=_cp(("parallel", "parallel")),
    )(proj, proj, wgu, bgu)


def _head_blockdiag(width, hd):
    r = lax.broadcasted_iota(jnp.int32, (width, width), 0) // hd
    c = lax.broadcasted_iota(jnp.int32, (width, width), 1) // hd
    return _bf(r == c)


def _mix_out(o_sb, o_gla, proj, x, g_sb, g_gla, w_out, g2):
    T = x.shape[0]
    tm = _tile(T, 512)

    def body(osb_ref, ogl_ref, og_ref, x_ref, gsb_ref, ggl_ref, w_ref, g2_ref, x1_ref, oc_ref, h2_ref):
        bd64 = _head_blockdiag(SB_W, SB_HD)
        bd128 = _head_blockdiag(GLA_W, GLA_DV)
        o = osb_ref[...]
        r = lax.rsqrt(_split_dot(o * o, bd64, 2) * (1.0 / SB_HD) + EPS)
        c_sb = (o * r) * gsb_ref[...]
        o = ogl_ref[...]
        r = lax.rsqrt(_split_dot(o * o, bd128, 2) * (1.0 / GLA_DV) + EPS)
        og = og_ref[...]
        c_gl = ((o * r) * ggl_ref[...]) * (og * jax.nn.sigmoid(og))
        oc = _bf(jnp.concatenate([c_sb, c_gl], axis=1))
        oc_ref[...] = oc
        x1 = x_ref[...] + _dot(oc, w_ref[...])
        x1_ref[...] = x1
        r2 = lax.rsqrt(jnp.mean(x1 * x1, axis=-1, keepdims=True) + EPS)
        h2_ref[...] = _bf((x1 * r2) * g2_ref[...])

    row = lambda w: BS((tm, w), lambda i: (i, 0))
    vec = lambda w: BS((1, w), lambda i: (0, 0))
    return pl.pallas_call(
        body, name="mix_out", out_shape=(SDS((T, D), F32), SDS((T, D), BF16), SDS((T, D), BF16)), grid=(T // tm,),
        in_specs=[row(SB_W), row(GLA_W), BS((tm, 512), lambda i: (i, OG_BLK)), row(D), vec(SB_W), vec(GLA_W),
                  BS((D, D), lambda i: (0, 0)), vec(D)],
        out_specs=(row(D), row(D), row(D)),
        compiler_params=_cp(("parallel",)),
    )(o_sb, o_gla, proj, x, g_sb, g_gla, w_out, g2)


CONV_ROWS = 256
CONV_TC = 256


def _rows_before(ref, r0, first):
    prev = ref[pl.ds(pl.multiple_of(jnp.maximum(r0 - 8, 0), 8), 8), :]
    return jnp.where(first, 0.0, prev)


def _shift_down(cur, prev8, k):
    cat = jnp.concatenate([prev8, cur], axis=0)
    return pltpu.roll(cat, k, 0)[8:]


def _shift_up(cur, next8, k):
    cat = jnp.concatenate([cur, next8], axis=0)
    return pltpu.roll(cat, cat.shape[0] - k, 0)[:cur.shape[0]]


def _conv_at(h_ref, cw, cb, r0, rows, first):
    cur = h_ref[pl.ds(r0, rows), :]
    prev8 = _rows_before(h_ref, r0, first)
    u = cb + cw[0:1, :] * _shift_down(cur, prev8, 2)
    u = u + cw[1:2, :] * _shift_down(cur, prev8, 1)
    return u + cw[2:3, :] * cur


NJ = D_FF // CONV_TC


def _conv_gate(hup, cw, cb, S):
    T = hup.shape[0]
    rows = min(CONV_ROWS, S)
    nr = S // rows

    def body(ha_ref, hv_ref, cwa_ref, cwv_ref, cba_ref, cbv_ref, act_ref, ua_ref, uv_ref):
        cwa, cwv, cba, cbv = cwa_ref[...], cwv_ref[...], cba_ref[...], cbv_ref[...]

        def step(c, _):
            r0 = pl.multiple_of(c * rows, rows)
            ua = _conv_at(ha_ref, cwa, cba, r0, rows, c == 0)
            uv = _conv_at(hv_ref, cwv, cbv, r0, rows, c == 0)
            ua_ref[pl.ds(r0, rows), :] = ua
            uv_ref[pl.ds(r0, rows), :] = uv
            act_ref[pl.ds(r0, rows), :] = _bf((ua * jax.nn.sigmoid(ua)) * uv)
            return 0

        lax.fori_loop(0, nr, step, 0)

    blk = lambda o: BS((S, CONV_TC), lambda b, j: (b, j + o))
    w3 = lambda o: BS((3, CONV_TC), lambda b, j: (0, j + o))
    w1 = lambda o: BS((1, CONV_TC), lambda b, j: (0, j + o))
    return pl.pallas_call(
        body, name="conv_gate", out_shape=(SDS((T, D_FF), BF16), SDS((T, D_FF), F32), SDS((T, D_FF), F32)),
        grid=(T // S, NJ),
        in_specs=[blk(0), blk(NJ), w3(0), w3(NJ), w1(0), w1(NJ)], out_specs=(blk(0), blk(0), blk(0)),
        compiler_params=_cp(("parallel", "parallel")),
    )(hup, hup, cw, cw, cb, cb)


def _down_loss(act, w_down, x1, tgt, g3):
    T = x1.shape[0]
    tm = _tile(T, 512)

    def body(a_ref, w_ref, x1_ref, t_ref, g_ref, dx_ref, dg_ref, ls_ref):
        @pl.when(pl.program_id(0) == 0)
        def _():
            dg_ref[...] = jnp.zeros_like(dg_ref)
            ls_ref[...] = jnp.zeros_like(ls_ref)

        g = g_ref[...]
        x2 = x1_ref[...] + _dot(a_ref[...], w_ref[...])
        r = lax.rsqrt(jnp.mean(x2 * x2, axis=-1, keepdims=True) + EPS)
        xh = x2 * r
        e = xh * g - t_ref[...]
        ls_ref[...] += 0.5 * jnp.sum(jnp.mean(e * e, axis=-1, keepdims=True), axis=0, keepdims=True)
        dy = e * (1.0 / D)
        dxh = dy * g
        dx_ref[...] = r * (dxh - xh * jnp.mean(dxh * xh, axis=-1, keepdims=True))
        dg_ref[...] += jnp.sum(dy * xh, axis=0, keepdims=True)

    row = lambda w: BS((tm, w), lambda i: (i, 0))
    return pl.pallas_call(
        body, name="down_loss", out_shape=(SDS((T, D), F32), SDS((1, D), F32), SDS((1, LANE), F32)), grid=(T // tm,),
        in_specs=[row(D_FF), BS((D_FF, D), lambda i: (0, 0)), row(D), row(D), BS((1, D), lambda i: (0, 0))],
        out_specs=(row(D), BS((1, D), lambda i: (0, 0)), BS((1, LANE), lambda i: (0, 0))),
        compiler_params=_cp(("arbitrary",)),
    )(act, w_down, x1, tgt, g3)


def _conv_gate_bwd(hup, u_a, u_v, dact, cw, S):
    T = hup.shape[0]
    rows = min(CONV_ROWS, S)
    nr = S // rows

    def body(ha_ref, hv_ref, ua_ref, uv_ref, da_ref, cwa_ref, cwv_ref,
             dha_ref, dhv_ref, dcwa_ref, dcwv_ref, dcba_ref, dcbv_ref):
        @pl.when(pl.program_id(1) == 0)
        def _():
            for r in (dcwa_ref, dcwv_ref, dcba_ref, dcbv_ref):
                r[...] = jnp.zeros_like(r)

        cwa, cwv = cwa_ref[...], cwv_ref[...]

        def du_at(r0, n):
            ua = ua_ref[pl.ds(r0, n), :]
            uv = uv_ref[pl.ds(r0, n), :]
            da = da_ref[pl.ds(r0, n), :]
            sg = jax.nn.sigmoid(ua)
            dua = (da * uv) * (sg * (1.0 + ua * (1.0 - sg)))
            duv = da * (ua * sg)
            return dua, duv

        def step(c, _):
            r0 = pl.multiple_of(c * rows, rows)
            last = c == nr - 1
            dua, duv = du_at(r0, rows)
            n0 = pl.multiple_of(jnp.minimum(r0 + rows, S - 8), 8)
            nua, nuv = du_at(n0, 8)
            nua = jnp.where(last, 0.0, nua)
            nuv = jnp.where(last, 0.0, nuv)
            for (h_ref, cw, du, nu, dh_ref, dcw_ref, dcb_ref) in (
                    (ha_ref, cwa, dua, nua, dha_ref, dcwa_ref, dcba_ref),
                    (hv_ref, cwv, duv, nuv, dhv_ref, dcwv_ref, dcbv_ref)):
                up1, up2 = _shift_up(du, nu, 1), _shift_up(du, nu, 2)
                dh_ref[pl.ds(r0, rows), :] = _bf(cw[2:3, :] * du + cw[1:2, :] * up1 + cw[0:1, :] * up2)
                cur = h_ref[pl.ds(r0, rows), :]
                dcw_ref[0:1, :] += jnp.sum(up2 * cur, axis=0, keepdims=True)
                dcw_ref[1:2, :] += jnp.sum(up1 * cur, axis=0, keepdims=True)
                dcw_ref[2:3, :] += jnp.sum(du * cur, axis=0, keepdims=True)
                dcb_ref[...] += jnp.sum(du, axis=0, keepdims=True)
            return 0

        lax.fori_loop(0, nr, step, 0)

    blk = lambda o: BS((S, CONV_TC), lambda j, b: (b, j + o))
    w3 = lambda o: BS((3, CONV_TC), lambda j, b: (0, j + o))
    w1 = BS((1, CONV_TC), lambda j, b: (0, j))
    return pl.pallas_call(
        body, name="conv_gate_bwd",
        out_shape=(SDS((T, D_FF), BF16), SDS((T, D_FF), BF16), SDS((3, D_FF), F32), SDS((3, D_FF), F32),
                   SDS((1, D_FF), F32), SDS((1, D_FF), F32)),
        grid=(NJ, T // S),
        in_specs=[blk(0), blk(NJ), blk(0), blk(0), blk(0), w3(0), w3(NJ)],
        out_specs=(blk(0), blk(0), w3(0), w3(0), w1, w1),
        compiler_params=_cp(("parallel", "arbitrary")),
    )(hup, hup, u_a, u_v, dact, cw, cw)


def _mix_bwd(docat, o_sb, o_gla, proj, g_sb, g_gla):
    T = docat.shape[0]
    tm = _tile(T, 512)

    def body(d_ref, osb_ref, ogl_ref, og_ref, gsb_ref, ggl_ref, dsb_ref, dgl_ref, dog_ref, dgsb_ref, dggl_ref):
        @pl.when(pl.program_id(0) == 0)
        def _():
            dgsb_ref[...] = jnp.zeros_like(dgsb_ref)
            dggl_ref[...] = jnp.zeros_like(dggl_ref)

        bd64 = _head_blockdiag(SB_W, SB_HD)
        bd128 = _head_blockdiag(GLA_W, GLA_DV)
        d = d_ref[:, 0:SB_W]
        o = osb_ref[...]
        r = lax.rsqrt(_split_dot(o * o, bd64, 2) * (1.0 / SB_HD) + EPS)
        n = o * r
        dn = d * gsb_ref[...]
        dgsb_ref[...] += jnp.sum(d * n, axis=0, keepdims=True)
        dsb_ref[...] = r * (dn - n * (_split_dot(dn * n, bd64, 2) * (1.0 / SB_HD)))

        d = d_ref[:, SB_W:D]
        o = ogl_ref[...]
        r = lax.rsqrt(_split_dot(o * o, bd128, 2) * (1.0 / GLA_DV) + EPS)
        n = o * r
        og = og_ref[...]
        sg = jax.nn.sigmoid(og)
        dm = d * (og * sg)
        dog_ref[...] = _bf((d * (n * ggl_ref[...])) * (sg * (1.0 + og * (1.0 - sg))))
        dn = dm * ggl_ref[...]
        dggl_ref[...] += jnp.sum(dm * n, axis=0, keepdims=True)
        dgl_ref[...] = r * (dn - n * (_split_dot(dn * n, bd128, 2) * (1.0 / GLA_DV)))

    row = lambda w: BS((tm, w), lambda i: (i, 0))
    vec = lambda w: BS((1, w), lambda i: (0, 0))
    ogb = BS((tm, 512), lambda i: (i, OG_BLK))
    return pl.pallas_call(
        body, name="mix_bwd",
        out_shape=(SDS((T, SB_W), F32), SDS((T, GLA_W), F32), SDS((T, PROJ_W), BF16), SDS((1, SB_W), F32),
                   SDS((1, GLA_W), F32)),
        grid=(T // tm,),
        in_specs=[row(D), row(SB_W), row(GLA_W), ogb, vec(SB_W), vec(GLA_W)],
        out_specs=(row(SB_W), row(GLA_W), ogb, vec(SB_W), vec(GLA_W)),
        compiler_params=_cp(("arbitrary",)),
    )(docat, o_sb, o_gla, proj, g_sb, g_gla)


def _sb_bwd(proj, tt, do, dproj, S, pieces):
    T = proj.shape[0]
    nq = S // QT
    scale = SB_HD ** -0.5
    nb, ns = T // S, len(pieces)

    def body(qkv_ref, tt_ref, do_ref, dp_in_ref, *rest):
        del dp_in_ref
        pc_refs, dp_ref, got_refs = rest[:ns], rest[ns], rest[ns + 1:2 * ns + 1]
        dk_acc, dv_acc = rest[2 * ns + 1:2 * ns + 3]
        sems = rest[2 * ns + 3:]
        first = jnp.logical_and(pl.program_id(0) == 0, pl.program_id(1) == 0)
        last = jnp.logical_and(pl.program_id(0) == nb - 1, pl.program_id(1) == 3)

        @pl.when(first)
        def _():
            _exchange_ops(pc_refs, got_refs, False, sems, "start")

        row, col, lane, kr, kc = _sb_masks()
        mincl = _bf(kr <= kc)
        mexcl = _bf(kr < kc)
        dk_acc[...] = jnp.zeros_like(dk_acc)
        dv_acc[...] = jnp.zeros_like(dv_acc)

        def qloop(qt, _):
            r0 = pl.multiple_of(qt * QT, QT)
            qs = _stack_heads(qkv_ref[pl.ds(r0, QT), 0:128] * scale, lane)
            dos = _stack_heads(do_ref[pl.ds(r0, QT), :], lane)
            ttv = tt_ref[pl.ds(r0, QT), :]
            tot = jnp.concatenate([ttv[:, 0:1], ttv[:, SB_HD:SB_HD + 1]], axis=0)
            walked = jnp.max(ttv[:, CNT_LANE:CNT_LANE + 1]).astype(jnp.int32)

            def step(kt, st):
                dq, lc, pc = st
                k0 = pl.multiple_of(kt * TK, TK)
                kv = _bf(qkv_ref[pl.ds(k0, TK), 128:256])
                vv = _bf(qkv_ref[pl.ds(k0, TK), 256:384])
                strict = (col + (kt - qt) * TK) < row
                z = jnp.where(strict, _dot_nt(qs, kv), SB_MASKED)
                sp = _softplus(z)
                lg = -sp
                after = tot - (lc + _split_dot(lg, mincl, 2))
                gl = z - sp
                w = jnp.exp(gl + after)
                du = w * _dot_nt(dos, vv)
                beta = jnp.exp(gl)
                pex = pc + _split_dot(du, mexcl, 2)
                dz = _bf(du - beta * (du + pex))
                dk_acc[pl.ds(k0, TK), :] += _dot_tn(dz, qs)
                dv_acc[pl.ds(k0, TK), :] += _dot_tn(_bf(w), dos)
                return (dq + _dot(dz, kv), lc + jnp.sum(lg, axis=1, keepdims=True),
                        pc + jnp.sum(du, axis=1, keepdims=True))

            zc = jnp.zeros((2 * QT, 1), F32)
            dq, _, _ = lax.fori_loop(qt - walked + 1, qt + 1, step, (jnp.zeros((2 * QT, LANE), F32), zc, zc))
            dp_ref[pl.ds(r0, QT), 0:128] = _bf(jnp.where(lane < SB_HD, dq[:QT], dq[QT:]) * scale)
            return 0

        lax.fori_loop(0, nq, qloop, 0)
        dp_ref[:, 128:256] = _bf(dk_acc[...])
        dp_ref[:, 256:384] = _bf(dv_acc[...])

        @pl.when(last)
        def _():
            _exchange_ops(pc_refs, got_refs, False, sems, "wait")

    blk = BS((S, 384), lambda b, p: (b, p))
    col_spec = BS((S, LANE), lambda b, p: (b, p))
    hbm = BS(memory_space=pltpu.HBM)
    return pl.pallas_call(
        body, name="sb_bwd", out_shape=(SDS((T, PROJ_W), BF16),) + tuple(SDS(s.shape, s.dtype) for s in pieces),
        grid=(nb, 4),
        in_specs=[blk, col_spec, col_spec, BS(memory_space=pl.ANY)] + [hbm] * ns, out_specs=(blk,) + (hbm,) * ns,
        scratch_shapes=[pltpu.VMEM((S, LANE), F32), pltpu.VMEM((S, LANE), F32)] + _exchange_sems(ns),
        input_output_aliases={3: 0},
        compiler_params=_cp(("arbitrary", "arbitrary")),
    )(proj, tt, do, dproj, *pieces)


def _gla_bwd(proj, wgu, bgu, do, dproj, S):
    T = proj.shape[0]
    nc, ng = S // CHUNK, S // GR

    def body(blk_ref, glr_ref, wgu_ref, bgu_ref, do_ref, dp_in_ref, dp_ref, dpre_ref, st_ref):
        del dp_in_ref
        causal, tri_incl, tri_rev, lane = _gla_masks()
        wg = _bf(wgu_ref[...])
        bg = bgu_ref[...]

        def fwd_group(g, states):
            r0 = pl.multiple_of(g * GR, GR)
            v, _, _, _, _, _, _, ke, decay = _gla_group_terms(blk_ref, glr_ref, wg, bg, r0, tri_incl)
            keb = _bf(ke)
            new_states = []
            for hh in range(2):
                vh = _bf(v[:, 128 * hh:128 * hh + 128])
                st = states[hh]
                for n in range(GLA_G):
                    rows = _chunk_rows(n)
                    st_ref[hh, g * GLA_G + n] = st
                    st = st * decay[n * CHUNK:n * CHUNK + 1] + _dot_tn(vh[rows], keb[rows])
                new_states.append(st)
            return tuple(new_states)

        z = jnp.zeros((GLA_DV, LANE), F32)
        lax.fori_loop(0, ng, fwd_group, (z, z))

        def bwd_group(it, dstates):
            g = ng - 1 - it
            r0 = pl.multiple_of(g * GR, GR)
            v, pre, b, b_last, eb, qd, ki, ke, decay = _gla_group_terms(
                blk_ref, glr_ref, wg, bg, r0, tri_incl)
            kib = _bf(ki)
            dqd = jnp.zeros((GR, LANE), F32)
            dki = jnp.zeros((GR, LANE), F32)
            dke = jnp.zeros((GR, LANE), F32)
            ddec = jnp.zeros((GR, LANE), F32)
            new_dstates, dvs = [], []
            for hh in range(2):
                hm = (lane // GLA_DK) == hh
                qm = _bf(jnp.where(hm, qd, 0.0))
                kem = _bf(jnp.where(hm, ke, 0.0))
                vh = _bf(v[:, 128 * hh:128 * hh + 128])
                doh = _bf(do_ref[pl.ds(r0, GR), 128 * hh:128 * hh + 128])
                attn = _bf(jnp.where(causal, _dot_nt(qm, kib), 0.0))
                dattn = _bf(jnp.where(causal, _dot_nt(doh, vh), 0.0))
                dv_intra = _dot_tn(attn, doh)
                dqd_intra = _dot(dattn, kib)
                dki = dki + _dot_tn(dattn, qm)
                dst = dstates[hh]
                dv_p, dqd_p, dke_p, ddec_p = [None] * GLA_G, [None] * GLA_G, [None] * GLA_G, [None] * GLA_G
                for n in reversed(range(GLA_G)):
                    rows = _chunk_rows(n)
                    st = st_ref[hh, g * GLA_G + n]
                    dv_p[n] = dv_intra[rows] + _dot_nt(kem[rows], _bf(dst))
                    dqd_p[n] = dqd_intra[rows] + _dot(doh[rows], _bf(st))
                    dke_p[n] = _dot(vh[rows], _bf(dst))
                    ddec_p[n] = jnp.broadcast_to(jnp.sum(dst * st, axis=0, keepdims=True), (CHUNK, LANE))
                    dst = dst * decay[n * CHUNK:n * CHUNK + 1] + _dot_tn(doh[rows], qm[rows])
                dvs.append(jnp.concatenate(dv_p, axis=0))
                dqd = dqd + jnp.where(hm, jnp.concatenate(dqd_p, axis=0), 0.0)
                dke = dke + jnp.where(hm, jnp.concatenate(dke_p, axis=0), 0.0)
                ddec = ddec + jnp.where(hm, jnp.concatenate(ddec_p, axis=0), 0.0)
                new_dstates.append(dst)
            einv = jnp.exp(-b)
            eend = jnp.exp(b_last - b)
            dq = (dqd * eb) * (GLA_DK ** -0.5)
            dk = dki * einv + dke * eend
            db = dqd * qd - dki * ki - dke * ke
            dkk = dke * ke
            db_last = _per_chunk(lambda rows: jnp.sum(dkk[rows], axis=0, keepdims=True)) + ddec * decay
            dla = _split_dot(db, tri_rev, 2, left=False) + db_last
            dpre_ref[pl.ds(r0, GR), :] = (dla * (1.0 / GATE_NORM)) * (1.0 - jax.nn.sigmoid(pre))
            dp_ref[pl.ds(r0, GR), 0:128] = _bf(dq)
            dp_ref[pl.ds(r0, GR), 128:256] = _bf(dk)
            dp_ref[pl.ds(r0, GR), 256:512] = _bf(jnp.concatenate(dvs, axis=1))
            return tuple(new_dstates)

        lax.fori_loop(0, ng, bwd_group, (z, z))

    return pl.pallas_call(
        body, name="gla_bwd", out_shape=(SDS((T, PROJ_W), BF16), SDS((T, GLA_KW), F32)), grid=(T // S, 2),
        in_specs=[BS((S, 512), lambda b, p: (b, 3 + p)), BS((S, LANE), lambda b, p: (b, GLR_BLK)),
                  BS((LANE, LANE), lambda b, p: (0, p)), BS((1, LANE), lambda b, p: (0, p)),
                  BS((S, 256), lambda b, p: (b, p)), BS(memory_space=pl.ANY)],
        out_specs=(BS((S, 512), lambda b, p: (b, 3 + p)), BS((S, LANE), lambda b, p: (b, p))),
        scratch_shapes=[pltpu.VMEM((2, nc, GLA_DV, LANE), F32)],
        input_output_aliases={5: 0},
        compiler_params=_cp(("parallel", "parallel")),
    )(proj, proj, wgu, bgu, do, dproj)


def _gate_bwd(dpre, proj, wgu, dproj):
    T = dpre.shape[0]
    tm = _tile(T, 512)

    def body(dpre_ref, glr_ref, wgu_ref, dp_in_ref, dp_ref, dw_ref, db_ref):
        del dp_in_ref

        @pl.when(pl.program_id(0) == 0)
        def _():
            dw_ref[...] = jnp.zeros_like(dw_ref)
            db_ref[...] = jnp.zeros_like(db_ref)

        dpre = dpre_ref[...]
        dp_ref[...] = _bf(_dot_nt(_bf(dpre), _bf(wgu_ref[...])))
        dw_ref[...] += _dot_tn(_bf(glr_ref[...]), _bf(dpre))
        db_ref[...] += jnp.sum(dpre, axis=0, keepdims=True)

    glr = BS((tm, LANE), lambda i: (i, GLR_BLK))
    return pl.pallas_call(
        body, name="gate_bwd",
        out_shape=(SDS((T, PROJ_W), BF16), SDS((LANE, GLA_KW), F32), SDS((1, GLA_KW), F32)), grid=(T // tm,),
        in_specs=[BS((tm, GLA_KW), lambda i: (i, 0)), glr, BS((LANE, GLA_KW), lambda i: (0, 0)),
                  BS(memory_space=pl.ANY)],
        out_specs=(glr, BS((LANE, GLA_KW), lambda i: (0, 0)), BS((1, GLA_KW), lambda i: (0, 0))),
        input_output_aliases={3: 0},
        compiler_params=_cp(("arbitrary",)),
    )(dpre, proj, wgu, dproj)


def _exchange_sems(n):
    return [pltpu.SemaphoreType.DMA((n * (N_DEV - 1),)), pltpu.SemaphoreType.DMA((n * (N_DEV - 1),)),
            pltpu.SemaphoreType.DMA((n,))]


def _exchange_ops(srcs, outs, gather, sems, act):
    ssem, rsem, lsem = sems
    x, y, c = lax.axis_index("x"), lax.axis_index("y"), lax.axis_index("c")
    me = 4 * x + 2 * y + c
    for i, (s_ref, o_ref) in enumerate(zip(srcs, outs)):
        for k in range(1, N_DEV):
            px = (x + ((k >> 2) & 1)) % 2
            py = (y + ((k >> 1) & 1)) % 2
            pc = (c + (k & 1)) % 2
            peer = 4 * px + 2 * py + pc
            n = i * (N_DEV - 1) + k - 1
            out = pltpu.make_async_remote_copy(
                src_ref=s_ref if gather else s_ref.at[peer], dst_ref=o_ref.at[me],
                send_sem=ssem.at[n], recv_sem=rsem.at[n],
                device_id=(px, py, pc), device_id_type=pl.DeviceIdType.MESH)
            if act == "start":
                out.start()
            else:
                out.wait_send()
                pltpu.make_async_remote_copy(
                    src_ref=s_ref if gather else s_ref.at[me], dst_ref=o_ref.at[peer],
                    send_sem=ssem.at[n], recv_sem=rsem.at[n],
                    device_id=(x, y, c), device_id_type=pl.DeviceIdType.MESH).wait_recv()
        mine = pltpu.make_async_copy(s_ref if gather else s_ref.at[me], o_ref.at[me], lsem.at[i])
        if act == "start":
            mine.start()
        else:
            mine.wait()


def _gather_two_level(src, name):
    def body(s_ref, o_ref, ssem, rsem, lsem):
        x, y, c = lax.axis_index("x"), lax.axis_index("y"), lax.axis_index("c")
        me, sibling = (x, y, c), (x, y, 1 - c)
        chips = [(1 - x, y), (x, 1 - y), (1 - x, 1 - y)]

        def slab(px, py, pc):
            return o_ref.at[4 * px + 2 * py + pc]

        def copy(k, block, to, src_ref=None):
            return pltpu.make_async_remote_copy(
                src_ref=slab(*block) if src_ref is None else src_ref, dst_ref=slab(*block),
                send_sem=ssem.at[k], recv_sem=rsem.at[k], device_id=to, device_id_type=pl.DeviceIdType.MESH)

        mine = pltpu.make_async_copy(s_ref, slab(*me), lsem)
        mine.start()
        first = [copy(0, me, sibling, s_ref)] + [copy(1 + j, me, (*chip, c), s_ref) for j, chip in enumerate(chips)]
        for cp in first:
            cp.start()
        passed = [copy(4 + j, (*chip, c), sibling) for j, chip in enumerate(chips)]
        for j, chip in enumerate(chips):
            copy(1 + j, (*chip, c), me).wait_recv()
            passed[j].start()
        copy(0, sibling, me).wait_recv()
        for j, chip in enumerate(chips):
            copy(4 + j, (*chip, 1 - c), me).wait_recv()
        for cp in first + passed:
            cp.wait_send()
        mine.wait()

    hbm = BS(memory_space=pltpu.HBM)
    return pl.pallas_call(
        body, name=name, out_shape=SDS((N_DEV,) + src.shape, src.dtype), in_specs=[hbm], out_specs=hbm,
        scratch_shapes=[pltpu.SemaphoreType.DMA((N_DEV - 1,)), pltpu.SemaphoreType.DMA((N_DEV - 1,)),
                        pltpu.SemaphoreType.DMA(())],
    )(src)


def _exchange(srcs, gather, name):
    n = len(srcs)
    shapes = [SDS((N_DEV,) + s.shape if gather else s.shape, s.dtype) for s in srcs]

    def body(*refs):
        s_refs, o_refs, sems = refs[:n], refs[n:2 * n], refs[2 * n:]
        _exchange_ops(s_refs, o_refs, gather, sems, "start")
        _exchange_ops(s_refs, o_refs, gather, sems, "wait")

    hbm = BS(memory_space=pltpu.HBM)
    return pl.pallas_call(
        body, name=name, out_shape=tuple(shapes), in_specs=[hbm] * n, out_specs=(hbm,) * n,
        scratch_shapes=_exchange_sems(n),
    )(*srcs)


def _adamw_math(w, g, m, v):
    m = ADAM_B1 * m + (1.0 - ADAM_B1) * g
    v = ADAM_B2 * v + (1.0 - ADAM_B2) * (g * g)
    m_hat = m / (1.0 - ADAM_B1 ** ADAM_STEP)
    v_hat = v / (1.0 - ADAM_B2 ** ADAM_STEP)
    delta = -ADAM_LR * (m_hat / (jnp.sqrt(v_hat) + ADAM_EPS) + ADAM_WD * w)
    return delta, m, v


def _sum_adamw(parts, w, m, v, tr, name):
    R, C = w.shape

    def body(p_ref, w_ref, m_ref, v_ref, g_ref, d_ref, nm_ref, nv_ref):
        g = p_ref[0].astype(F32)
        for d in range(1, N_DEV):
            g = g + p_ref[d].astype(F32)
        delta, nm, nv = _adamw_math(w_ref[...], g, m_ref[...], v_ref[...])
        g_ref[...] = g
        d_ref[...] = delta
        nm_ref[...] = nm
        nv_ref[...] = nv

    blk = BS((tr, C), lambda i: (i, 0))
    out = SDS((R, C), F32)
    return pl.pallas_call(
        body, name=name, out_shape=(out, out, out, out), grid=(R // tr,),
        in_specs=[BS((N_DEV, tr, C), lambda i: (0, i, 0)), blk, blk, blk], out_specs=(blk, blk, blk, blk),
        compiler_params=_cp(("parallel",)),
    )(parts, w, m, v)


def _flat_pad_rows(parts, rows):
    flat = jnp.concatenate([p.reshape(-1) for p in parts])
    return jnp.pad(flat, (0, rows * D - flat.shape[0])).reshape(rows, D)


SMALL_ROWS = 16
SHARD_SMALL_ROWS = 3


def kernel(x, attn_norm_g, w_in, w_gate_up, b_gate_up, sb_out_g, gla_out_g, w_out, ffn_norm_g, w_ffn_up, conv_w, conv_b, w_ffn_down, final_norm_g, loss_target, m_attn_norm_g, m_w_in, m_w_gate_up, m_b_gate_up, m_sb_out_g, m_gla_out_g, m_w_out, m_ffn_norm_g, m_w_ffn_up, m_conv_w, m_conv_b, m_w_ffn_down, m_final_norm_g, v_attn_norm_g, v_w_in, v_w_gate_up, v_b_gate_up, v_sb_out_g, v_gla_out_g, v_w_out, v_ffn_norm_g, v_w_ffn_up, v_conv_w, v_conv_b, v_w_ffn_down, v_final_norm_g):
    Bd, S, _ = x.shape
    T = Bd * S
    x2d = x.reshape(T, D)
    tgt = loss_target.reshape(T, D)
    c_up = w_ffn_up.shape[2]
    c_gu = w_gate_up.shape[2]
    c_in = w_in.shape[2]

    n_gu = GATE_RANK * c_gu
    rows_bf = lambda w: w[0].T.astype(BF16)
    small_w = lambda wgu, cw: _flat_pad_rows([wgu, cw], SHARD_SMALL_ROWS)

    g_in = _gather_two_level(rows_bf(w_in), "gather_w_in")
    w_in_pt = jnp.pad(g_in.reshape(IN_COLS, D), ((0, 1), (0, 0)))[_PERM]
    g3 = final_norm_g.reshape(1, D)

    proj, h1 = _norm_proj(x2d, attn_norm_g, w_in_pt)
    o_sb, tt, g_up, g_down, g_out, gs = _sb_fwd(
        proj, S, [rows_bf(w_ffn_up), w_ffn_down[0].astype(BF16), w_out[0].astype(BF16),
                  _flat_pad_rows([w_gate_up, conv_w], 8)])
    w_out_f = g_out.reshape(D, D)
    gsf = gs.reshape(N_DEV, -1)
    wgu_f = jnp.transpose(gsf[:, :n_gu].reshape(N_DEV, GATE_RANK, c_gu), (1, 0, 2)).reshape(GATE_RANK, GLA_KW)
    cw_f = jnp.transpose(gsf[:, n_gu:n_gu + 3 * c_up].reshape(N_DEV, 3, c_up), (1, 0, 2)).reshape(3, 2 * D_FF)
    wgu_p = jnp.pad(wgu_f, ((0, LANE - GATE_RANK), (0, 0)))
    w_up_t = g_up.reshape(2 * D_FF, D)
    w_down_f = g_down.reshape(D_FF, D)
    o_gla = _gla_fwd(proj, wgu_p, b_gate_up, S)
    x1, ocat, h2 = _mix_out(o_sb, o_gla, proj, x2d, sb_out_g, gla_out_g, w_out_f, ffn_norm_g)
    hup = _mm(h2, w_up_t, "nt", "ffn_up", tm=1024, tn=1408, tk=1024)
    act, u_a, u_v = _conv_gate(hup, cw_f, conv_b, S)
    dx2, dg3, loss_dev = _down_loss(act, w_down_f, x1, tgt, g3)

    dw_down = _mm(act, dx2, "tn", "dw_down", out_dtype=BF16, tm=D_FF, tn=1024, tk=512)
    dact = _mm(dx2, w_down_f, "nt", "dact", tm=1024, tn=1408, tk=1024)
    dhup_a, dhup_v, dcw_a, dcw_v, dcb_a, dcb_v = _conv_gate_bwd(hup, u_a, u_v, dact, cw_f, S)
    dw_up_t = _mm(dhup_a, h2, "tn", "dw_up_a", out_dtype=BF16, tm=D_FF, tn=1024, tk=512, out_rows=2 * D_FF)
    dw_up_t = _mm(dhup_v, h2, "tn", "dw_up_v", out_dtype=BF16, tm=D_FF, tn=1024, tk=512, out_rows=2 * D_FF,
                  out_row0=D_FF, into=dw_up_t)
    dh2 = _mm(dhup_a, w_up_t, "nn", "dh2_a", tm=1024, tn=1024, tk=1408)
    dx1, dg2 = _mm(dhup_v, w_up_t, "nn", "dh2_v", c=dh2, tm=512, tn=1024, tk=1408, b_row0=D_FF,
                   norm_bwd=(x1, ffn_norm_g, dx2))

    dw_out = _mm(ocat, dx1, "tn", "dw_out", out_dtype=BF16, tm=1024, tn=1024, tk=512)
    docat = _mm(dx1, w_out_f, "nt", "docat", tm=1024, tn=1024, tk=1024)
    do_sb, do_gla, dproj, dg_sb, dg_gla = _mix_bwd(docat, o_sb, o_gla, proj, sb_out_g, gla_out_g)
    dproj, got_up, got_down, got_out = _sb_bwd(
        proj, tt, do_sb, dproj, S,
        [dw_up_t.reshape(N_DEV, c_up, D), dw_down.reshape(N_DEV, -1, D), dw_out.reshape(N_DEV, -1, D)])
    dproj, dpre = _gla_bwd(proj, wgu_p, b_gate_up, do_gla, dproj, S)
    dproj, dwgu, dbgu = _gate_bwd(dpre, proj, wgu_p, dproj)
    dw_in_pt = _mm(dproj, h1, "tn", "dw_in", out_dtype=BF16, tm=PROJ_W, tn=1024, tk=512)
    dx, dg1, got_in = _mm(dproj, w_in_pt, "nn", "dh1", tm=1024, tn=1024, tk=640,
                          xchg=([dw_in_pt[_INV_PERM].reshape(N_DEV, c_in, D)], False),
                          norm_bwd=(x2d, attn_norm_g, dx1))

    dcw = jnp.concatenate([dcw_a, dcw_v], axis=1)
    dwgu_pc = jnp.transpose(dwgu[:GATE_RANK].reshape(GATE_RANK, N_DEV, c_gu), (1, 0, 2)).reshape(N_DEV, -1)
    dcw_pc = jnp.transpose(dcw.reshape(3, N_DEV, c_up), (1, 0, 2)).reshape(N_DEV, -1)
    small_pc = jnp.concatenate([dwgu_pc, dcw_pc], axis=1)
    small_pc = jnp.pad(small_pc, ((0, 0), (0, SHARD_SMALL_ROWS * D - small_pc.shape[1])))
    small_pc = small_pc.reshape(N_DEV, SHARD_SMALL_ROWS, D).astype(BF16)
    rep_names = ["attn_norm_g", "b_gate_up", "sb_out_g", "gla_out_g", "ffn_norm_g", "conv_b", "final_norm_g"]
    rep_g = [dg1, dbgu, dg_sb, dg_gla, dg2, jnp.concatenate([dcb_a, dcb_v], axis=1), dg3]
    rep_w = [attn_norm_g, b_gate_up, sb_out_g, gla_out_g, ffn_norm_g, conv_b, final_norm_g]
    rep_m = [m_attn_norm_g, m_b_gate_up, m_sb_out_g, m_gla_out_g, m_ffn_norm_g, m_conv_b, m_final_norm_g]
    rep_v = [v_attn_norm_g, v_b_gate_up, v_sb_out_g, v_gla_out_g, v_ffn_norm_g, v_conv_b, v_final_norm_g]
    rep_pc = jnp.broadcast_to(_flat_pad_rows(rep_g, SMALL_ROWS), (N_DEV, SMALL_ROWS, D))
    got_sm, got_rep = _exchange([small_pc, rep_pc], False, "scatter_tail")

    rows = lambda w: w[0].T
    cols = lambda r: r.T[None]
    res = {}
    res["w_in"] = [cols(r) for r in _sum_adamw(got_in, rows(w_in), rows(m_w_in), rows(v_w_in), c_in, "adamw_w_in")]
    res["w_out"] = [r[None] for r in _sum_adamw(got_out, w_out[0], m_w_out[0], v_w_out[0], w_out.shape[1],
                                                 "adamw_w_out")]
    res["w_ffn_up"] = [cols(r) for r in _sum_adamw(got_up, rows(w_ffn_up), rows(m_w_ffn_up), rows(v_w_ffn_up),
                                                    c_up // 2, "adamw_w_up")]
    res["w_ffn_down"] = [r[None] for r in _sum_adamw(got_down, w_ffn_down[0], m_w_ffn_down[0], v_w_ffn_down[0],
                                                      w_ffn_down.shape[1], "adamw_w_down")]
    sm = _sum_adamw(got_sm, small_w(w_gate_up, conv_w), small_w(m_w_gate_up, m_conv_w),
                    small_w(v_w_gate_up, v_conv_w), SHARD_SMALL_ROWS, "adamw_small_sharded")
    res["w_gate_up"] = [r.reshape(-1)[:n_gu].reshape(1, GATE_RANK, c_gu) for r in sm]
    res["conv_w"] = [r.reshape(-1)[n_gu:n_gu + 3 * c_up].reshape(1, 3, c_up) for r in sm]
    rep = _sum_adamw(got_rep, _flat_pad_rows(rep_w, SMALL_ROWS), _flat_pad_rows(rep_m, SMALL_ROWS),
                     _flat_pad_rows(rep_v, SMALL_ROWS), SMALL_ROWS, "adamw_replicated")
    o = 0
    for n, w in zip(rep_names, rep_w):
        res[n] = [r.reshape(-1)[o:o + w.size].reshape(w.shape) for r in rep]
        o += w.size

    loss = lax.psum(loss_dev[0, 0], ("x", "y", "c"))
    order = ["attn_norm_g", "w_in", "w_gate_up", "b_gate_up", "sb_out_g", "gla_out_g", "w_out", "ffn_norm_g",
             "w_ffn_up", "conv_w", "conv_b", "w_ffn_down", "final_norm_g"]
    outs = [loss, dx.reshape(Bd, S, D)]
    for k in range(4):
        outs += [res[n][k] for n in order]
    return tuple(outs)
```

```python
import functools

import numpy as np
import jax
import jax.numpy as jnp
from jax import lax
from jax.experimental import pallas as pl
from jax.experimental.pallas import tpu as pltpu

F32 = jnp.float32
BF16 = jnp.bfloat16
SDS = jax.ShapeDtypeStruct
BS = pl.BlockSpec

N_DEV = 8
D = 1024
EPS = 1e-6
SB_HD = 64
SB_W = 512
GLA_DK = 64
GLA_DV = 128
GLA_KW = 256
GLA_W = 512
GATE_RANK = 16
GATE_NORM = 16.0
CHUNK = 64
GLA_G = 4
GR = GLA_G * CHUNK
QT = 256
D_FF = 2816
IN_COLS = 3088
PROJ_W = 3200
LANE = 128
VMEM_LIMIT = 56 * 1024 * 1024

ADAM_LR, ADAM_B1, ADAM_B2, ADAM_EPS, ADAM_WD, ADAM_STEP = 0.001, 0.9, 0.999, 1e-08, 0.01, 10


def _proj_perm():
    sbq, sbk, sbv = 0, 512, 1024
    gq, gk, gv, glr, gog = 1536, 1792, 2048, 2560, 2576
    cols = []
    for p in range(4):
        for base in (sbq, sbk, sbv):
            cols += list(range(base + 128 * p, base + 128 * p + 128))
    for p in range(2):
        cols += list(range(gq + 128 * p, gq + 128 * p + 128))
        cols += list(range(gk + 128 * p, gk + 128 * p + 128))
        cols += list(range(gv + 256 * p, gv + 256 * p + 256))
    cols += list(range(gog, gog + 512))
    cols += list(range(glr, glr + GATE_RANK)) + [IN_COLS] * (LANE - GATE_RANK)
    perm = np.asarray(cols, np.int32)
    inv = np.zeros((IN_COLS,), np.int32)
    for new, old in enumerate(cols):
        if old < IN_COLS:
            inv[old] = new
    return perm, inv


_PERM, _INV_PERM = _proj_perm()
OG_BLK = 5
GLR_BLK = 24


def _cp(sem=None, vmem=VMEM_LIMIT):
    return pltpu.CompilerParams(dimension_semantics=sem, vmem_limit_bytes=vmem)


def _dot(a, b):
    return lax.dot_general(a, b, (((1,), (0,)), ((), ())), preferred_element_type=F32)


def _dot_nt(a, b):
    return lax.dot_general(a, b, (((1,), (1,)), ((), ())), preferred_element_type=F32)


def _dot_tn(a, b):
    return lax.dot_general(a, b, (((0,), (0,)), ((), ())), preferred_element_type=F32)


def _bf(x):
    return x.astype(BF16)


def _split_dot(x, m, passes, left=True):
    acc = None
    r = x
    for i in range(passes):
        h = r.astype(BF16)
        t = _dot(h, m) if left else _dot(m, h)
        acc = t if acc is None else acc + t
        if i + 1 < passes:
            r = r - h.astype(F32)
    return acc


def _softplus(z):
    return jnp.maximum(z, 0.0) + jnp.log(1.0 + jnp.exp(-jnp.abs(z)))


def _rms_bwd_math(x, g, dh, dres):
    r = lax.rsqrt(jnp.mean(x * x, axis=-1, keepdims=True) + EPS)
    xh = x * r
    dxh = dh * g
    dx = dres + r * (dxh - xh * jnp.mean(dxh * xh, axis=-1, keepdims=True))
    return dx, jnp.sum(dh * xh, axis=0, keepdims=True)


def _tile(n, pref, mult=LANE):
    best = None
    for t in range(mult, min(n, pref) + 1, mult):
        if n % t == 0:
            best = t
    return best if best is not None else n


def _mm(a, b, mode, name, out_dtype=F32, c=None, tm=512, tn=512, tk=512, b_row0=0, out_rows=None, out_row0=0,
        into=None, xchg=None, norm_bwd=None, a2=None):
    if mode == "nn":
        (M, K), N = a.shape, b.shape[1]
    elif mode == "nt":
        (M, K), N = a.shape, b.shape[0]
    else:
        (K, M), N = a.shape, b.shape[1]
    tm, tn, tk = _tile(M, tm), _tile(N, tn), _tile(K, tk)
    has_a2 = a2 is not None
    assert not has_a2 or (mode == "nn" and a2.shape == a.shape)
    nka = K // tk
    nk = nka * (2 if has_a2 else 1)
    kb0, ob0 = b_row0 // tk, out_row0 // tm
    assert kb0 * tk == b_row0 and ob0 * tm == out_row0 and (mode == "nn" or b_row0 == 0)
    ni, nj = M // tm, N // tn
    j_outer = nk == 1 and (nj - 1) * a.size * a.dtype.itemsize < (ni - 1) * K * N * b.dtype.itemsize
    ix = (lambda f: (lambda j, i, k: f(i, j, k))) if j_outer else (lambda f: f)
    if mode == "tn":
        a_spec = BS((tk, tm), ix(lambda i, j, k: (k, i)))
    else:
        a_spec = BS((tm, tk), ix(lambda i, j, k: (i, jnp.minimum(k, nka - 1))))
    b_spec = (BS((tn, tk), ix(lambda i, j, k: (j, k))) if mode == "nt"
              else BS((tk, tn), ix(lambda i, j, k: (k + kb0, j))))
    dotfn = {"nn": _dot, "nt": _dot_nt, "tn": _dot_tn}[mode]
    has_c = c is not None
    has_into = into is not None
    nx = 0 if xchg is None else len(xchg[0])
    has_nb = norm_bwd is not None
    assert not has_nb or (nj == 1 and not j_outer and out_dtype == F32)
    n_in = 2 + has_a2 + has_c + has_into + 3 * has_nb

    def body(*refs):
        a_ref, b_ref = refs[:2]
        c_ref = refs[2 + has_a2] if has_c else None
        x_src = refs[n_in:n_in + nx]
        outs = refs[n_in + nx:n_in + 2 * nx + 1 + has_nb]
        o_ref, x_out = outs[0], outs[1 + has_nb:]
        acc = refs[n_in + 2 * nx + 1 + has_nb]
        sems = refs[n_in + 2 * nx + 2 + has_nb:]
        k = pl.program_id(2)
        g0, g1 = pl.program_id(0), pl.program_id(1)
        n0, n1 = (nj, ni) if j_outer else (ni, nj)
        first = jnp.logical_and(jnp.logical_and(g0 == 0, g1 == 0), k == 0)
        if nx:
            @pl.when(first)
            def _():
                _exchange_ops(x_src, x_out, xchg[1], sems, "start")

        if has_nb:
            @pl.when(first)
            def _():
                outs[1][...] = jnp.zeros_like(outs[1])

        @pl.when(k == 0)
        def _():
            acc[...] = jnp.zeros_like(acc)

        if has_a2:
            @pl.when(k < nka)
            def _():
                acc[...] += dotfn(_bf(a_ref[...]), _bf(b_ref[...]))

            @pl.when(k >= nka)
            def _():
                acc[...] += dotfn(_bf(refs[2][...]), _bf(b_ref[...]))
        else:
            acc[...] += dotfn(_bf(a_ref[...]), _bf(b_ref[...]))

        @pl.when(k == nk - 1)
        def _():
            r = acc[...]
            if has_c:
                r = r + c_ref[...]
            if has_nb:
                x_ref, g_ref, dres_ref = refs[n_in - 3:n_in]
                dx, dg = _rms_bwd_math(x_ref[...], g_ref[...], r, dres_ref[...])
                o_ref[...] = dx
                outs[1][...] += dg
            else:
                o_ref[...] = r.astype(out_dtype)

        if nx:
            @pl.when(jnp.logical_and(jnp.logical_and(g0 == n0 - 1, g1 == n1 - 1), k == nk - 1))
            def _():
                _exchange_ops(x_src, x_out, xchg[1], sems, "wait")

    tile = BS((tm, tn), ix(lambda i, j, k: (i, j)))
    in_specs = [a_spec, b_spec]
    args = [a, b]
    if has_a2:
        in_specs.append(BS((tm, tk), lambda i, j, k: (i, jnp.maximum(k - nka, 0))))
        args.append(a2)
    if has_c:
        in_specs.append(tile)
        args.append(c)
    aliases = {}
    if has_into:
        aliases = {len(args): 0}
        in_specs.append(BS(memory_space=pl.ANY))
        args.append(into)
    out_shape = [SDS((out_rows or M, N), out_dtype)]
    out_specs = [BS((tm, tn), ix(lambda i, j, k: (i + ob0, j)))]
    scratch = [pltpu.VMEM((tm, tn), F32)]
    if has_nb:
        in_specs += [tile, BS((1, tn), lambda i, j, k: (0, 0)), tile]
        args += list(norm_bwd)
        out_shape.append(SDS((1, N), F32))
        out_specs.append(BS((1, tn), lambda i, j, k: (0, 0)))
    if nx:
        hbm = BS(memory_space=pltpu.HBM)
        in_specs += [hbm] * nx
        args += list(xchg[0])
        out_shape += [SDS((N_DEV,) + s.shape if xchg[1] else s.shape, s.dtype) for s in xchg[0]]
        out_specs += [hbm] * nx
        scratch += _exchange_sems(nx)
    serial = nx or has_nb
    res = pl.pallas_call(
        body, name=name, out_shape=tuple(out_shape), grid=(nj, ni, nk) if j_outer else (ni, nj, nk),
        in_specs=in_specs, out_specs=tuple(out_specs),
        scratch_shapes=scratch, input_output_aliases=aliases,
        compiler_params=_cp(("arbitrary",) * 3 if serial else ("parallel", "parallel", "arbitrary")),
    )(*args)
    return res if serial else res[0]


def _norm_proj(x, g, w):
    T, N = x.shape[0], w.shape[0]
    tm = _tile(T, 512)

    def body(x_ref, g_ref, w_ref, p_ref, h_ref):
        xv = x_ref[...]
        r = lax.rsqrt(jnp.mean(xv * xv, axis=-1, keepdims=True) + EPS)
        h = _bf((xv * r) * g_ref[...])
        h_ref[...] = h
        p_ref[...] = _dot_nt(h, w_ref[...])

    return pl.pallas_call(
        body, name="norm_proj", out_shape=(SDS((T, N), F32), SDS((T, D), BF16)), grid=(T // tm,),
        in_specs=[BS((tm, D), lambda i: (i, 0)), BS((1, D), lambda i: (0, 0)), BS((N, D), lambda i: (0, 0))],
        out_specs=(BS((tm, N), lambda i: (i, 0)), BS((tm, D), lambda i: (i, 0))),
        compiler_params=_cp(("parallel",)),
    )(x, g, w)


TK = 256
SB_DEAD = -104.0
SB_MASKED = -1e30
CNT_LANE = SB_HD - 1


def _sb_masks():
    row = lax.broadcasted_iota(jnp.int32, (2 * QT, TK), 0) & (QT - 1)
    col = lax.broadcasted_iota(jnp.int32, (2 * QT, TK), 1)
    lane = lax.broadcasted_iota(jnp.int32, (1, LANE), 1)
    kr = lax.broadcasted_iota(jnp.int32, (TK, TK), 0)
    kc = lax.broadcasted_iota(jnp.int32, (TK, TK), 1)
    return row, col, lane, kr, kc


def _stack_heads(x, lane):
    return jnp.concatenate([_bf(jnp.where((lane // SB_HD) == hh, x, 0.0)) for hh in range(2)], axis=0)


def _sb_fwd(proj, S, shards):
    T = proj.shape[0]
    nq = S // QT
    scale = SB_HD ** -0.5
    nb, ns = T // S, len(shards)

    def body(qkv_ref, *rest):
        sh_refs, (o_ref, tt_ref), g_refs = rest[:ns], rest[ns:ns + 2], rest[ns + 2:2 * ns + 2]
        sems = rest[2 * ns + 2:]
        first = jnp.logical_and(pl.program_id(0) == 0, pl.program_id(1) == 0)
        last = jnp.logical_and(pl.program_id(0) == nb - 1, pl.program_id(1) == 3)

        @pl.when(first)
        def _():
            _exchange_ops(sh_refs, g_refs, True, sems, "start")

        row, col, lane, kr, kc = _sb_masks()
        msuf = _bf(kr > kc)

        def qloop(qt, _):
            r0 = pl.multiple_of(qt * QT, QT)
            qs = _stack_heads(qkv_ref[pl.ds(r0, QT), 0:128] * scale, lane)

            def live(st):
                it, _, cy = st
                return jnp.logical_and(it <= qt, jnp.max(cy) > SB_DEAD)

            def step(st):
                it, acc, cy = st
                kt = qt - it
                k0 = pl.multiple_of(kt * TK, TK)
                kv = _bf(qkv_ref[pl.ds(k0, TK), 128:256])
                vv = _bf(qkv_ref[pl.ds(k0, TK), 256:384])
                strict = (col + (kt - qt) * TK) < row
                z = jnp.where(strict, _dot_nt(qs, kv), SB_MASKED)
                sp = _softplus(z)
                lg = -sp
                after = cy + _split_dot(lg, msuf, 2)
                w = jnp.exp((z - sp) + after)
                return it + 1, acc + _dot(_bf(w), vv), cy + jnp.sum(lg, axis=1, keepdims=True)

            it, acc, cy = lax.while_loop(
                live, step, (jnp.int32(0), jnp.zeros((2 * QT, LANE), F32), jnp.zeros((2 * QT, 1), F32)))
            o_ref[pl.ds(r0, QT), :] = jnp.where(lane < SB_HD, acc[:QT], acc[QT:])
            tt = jnp.where(lane < SB_HD, cy[:QT], cy[QT:])
            tt_ref[pl.ds(r0, QT), :] = jnp.where(lane == CNT_LANE, it.astype(F32), tt)
            return 0

        lax.fori_loop(0, nq, qloop, 0)

        @pl.when(last)
        def _():
            _exchange_ops(sh_refs, g_refs, True, sems, "wait")

    hbm = BS(memory_space=pltpu.HBM)
    col_spec = BS((S, LANE), lambda b, p: (b, p))
    return pl.pallas_call(
        body, name="sb_fwd",
        out_shape=(SDS((T, SB_W), F32), SDS((T, SB_W), F32)) + tuple(SDS((N_DEV,) + s.shape, s.dtype) for s in shards),
        grid=(nb, 4),
        in_specs=[BS((S, 384), lambda b, p: (b, p))] + [hbm] * ns,
        out_specs=(col_spec, col_spec) + (hbm,) * ns,
        scratch_shapes=_exchange_sems(ns),
        compiler_params=_cp(("arbitrary", "arbitrary")),
    )(proj, *shards)


def _log_sigmoid(x):
    return jnp.minimum(x, 0.0) - jnp.log1p(jnp.exp(-jnp.abs(x)))


def _gla_masks():
    r = lax.broadcasted_iota(jnp.int32, (GR, GR), 0)
    c = lax.broadcasted_iota(jnp.int32, (GR, GR), 1)
    same = (r // CHUNK) == (c // CHUNK)
    causal = jnp.logical_and(same, r >= c)
    lane = lax.broadcasted_iota(jnp.int32, (1, LANE), 1)
    return causal, _bf(causal), _bf(jnp.logical_and(same, r <= c)), lane


def _gla_group_terms(blk_ref, glr_ref, wgu, bgu, r0, tri_incl):
    q = blk_ref[pl.ds(r0, GR), 0:128]
    k = blk_ref[pl.ds(r0, GR), 128:256]
    v = blk_ref[pl.ds(r0, GR), 256:512]
    pre = _dot(_bf(glr_ref[pl.ds(r0, GR), :]), wgu) + bgu
    la = _log_sigmoid(pre) / GATE_NORM
    b = _split_dot(la, tri_incl, 3, left=False)
    b_last = _per_chunk(lambda rows: b[rows.stop - 1:rows.stop])
    eb = jnp.exp(b)
    qd = (q * (GLA_DK ** -0.5)) * eb
    ki = k * jnp.exp(-b)
    ke = k * jnp.exp(b_last - b)
    decay = jnp.exp(b_last)
    return v, pre, b, b_last, eb, qd, ki, ke, decay


def _chunk_rows(n):
    return slice(n * CHUNK, (n + 1) * CHUNK)


def _per_chunk(row_fn):
    return jnp.concatenate(
        [jnp.broadcast_to(row_fn(_chunk_rows(n)), (CHUNK, LANE)) for n in range(GLA_G)], axis=0)


def _gla_fwd(proj, wgu, bgu, S):
    T = proj.shape[0]
    ng = S // GR

    def body(blk_ref, glr_ref, wgu_ref, bgu_ref, o_ref):
        causal, tri_incl, _, lane = _gla_masks()
        wg = _bf(wgu_ref[...])
        bg = bgu_ref[...]

        def group(g, states):
            r0 = pl.multiple_of(g * GR, GR)
            v, _, _, _, _, qd, ki, ke, decay = _gla_group_terms(blk_ref, glr_ref, wg, bg, r0, tri_incl)
            kib, keb = _bf(ki), _bf(ke)
            new_states, outs = [], []
            for hh in range(2):
                hm = (lane // GLA_DK) == hh
                qm = _bf(jnp.where(hm, qd, 0.0))
                vh = _bf(v[:, 128 * hh:128 * hh + 128])
                attn = jnp.where(causal, _dot_nt(qm, kib), 0.0)
                o_intra = _dot(_bf(attn), vh)
                st = states[hh]
                parts = []
                for n in range(GLA_G):
                    rows = _chunk_rows(n)
                    parts.append(o_intra[rows] + _dot_nt(qm[rows], _bf(st)))
                    st = st * decay[n * CHUNK:n * CHUNK + 1] + _dot_tn(vh[rows], keb[rows])
                outs.append(jnp.concatenate(parts, axis=0))
                new_states.append(st)
            o_ref[pl.ds(r0, GR), :] = jnp.concatenate(outs, axis=1)
            return tuple(new_states)

        z = jnp.zeros((GLA_DV, LANE), F32)
        lax.fori_loop(0, ng, group, (z, z))

    return pl.pallas_call(
        body, name="gla_fwd", out_shape=SDS((T, GLA_W), F32), grid=(T // S, 2),
        in_specs=[BS((S, 512), lambda b, p: (b, 3 + p)), BS((S, LANE), lambda b, p: (b, GLR_BLK)),
                  BS((LANE, LANE), lambda b, p: (0, p)), BS((1, LANE), lambda b, p: (0, p))],
        out_specs=BS((S, 256), lambda b, p: (b, p)),
        compiler_params=_cp(("parallel", "parallel")),
    )(proj, proj, wgu, bgu)


def _head_blockdiag(width, hd):
    r = lax.broadcasted_iota(jnp.int32, (width, width), 0) // hd
    c = lax.broadcasted_iota(jnp.int32, (width, width), 1) // hd
    return _bf(r == c)


def _mix_out(o_sb, o_gla, proj, x, g_sb, g_gla, w_out, g2):
    T = x.shape[0]
    tm = _tile(T, 512)

    def body(osb_ref, ogl_ref, og_ref, x_ref, gsb_ref, ggl_ref, w_ref, g2_ref, x1_ref, oc_ref, h2_ref):
        bd64 = _head_blockdiag(SB_W, SB_HD)
        bd128 = _head_blockdiag(GLA_W, GLA_DV)
        o = osb_ref[...]
        r = lax.rsqrt(_split_dot(o * o, bd64, 2) * (1.0 / SB_HD) + EPS)
        c_sb = (o * r) * gsb_ref[...]
        o = ogl_ref[...]
        r = lax.rsqrt(_split_dot(o * o, bd128, 2) * (1.0 / GLA_DV) + EPS)
        og = og_ref[...]
        c_gl = ((o * r) * ggl_ref[...]) * (og * jax.nn.sigmoid(og))
        oc = _bf(jnp.concatenate([c_sb, c_gl], axis=1))
        oc_ref[...] = oc
        x1 = x_ref[...] + _dot(oc, w_ref[...])
        x1_ref[...] = x1
        r2 = lax.rsqrt(jnp.mean(x1 * x1, axis=-1, keepdims=True) + EPS)
        h2_ref[...] = _bf((x1 * r2) * g2_ref[...])

    row = lambda w: BS((tm, w), lambda i: (i, 0))
    vec = lambda w: BS((1, w), lambda i: (0, 0))
    return pl.pallas_call(
        body, name="mix_out", out_shape=(SDS((T, D), F32), SDS((T, D), BF16), SDS((T, D), BF16)), grid=(T // tm,),
        in_specs=[row(SB_W), row(GLA_W), BS((tm, 512), lambda i: (i, OG_BLK)), row(D), vec(SB_W), vec(GLA_W),
                  BS((D, D), lambda i: (0, 0)), vec(D)],
        out_specs=(row(D), row(D), row(D)),
        compiler_params=_cp(("parallel",)),
    )(o_sb, o_gla, proj, x, g_sb, g_gla, w_out, g2)


CONV_ROWS = 256
CONV_TC = 256


def _rows_before(ref, r0, first):
    prev = ref[pl.ds(pl.multiple_of(jnp.maximum(r0 - 8, 0), 8), 8), :]
    return jnp.where(first, 0.0, prev)


def _shift_down(cur, prev8, k):
    cat = jnp.concatenate([prev8, cur], axis=0)
    return pltpu.roll(cat, k, 0)[8:]


def _shift_up(cur, next8, k):
    cat = jnp.concatenate([cur, next8], axis=0)
    return pltpu.roll(cat, cat.shape[0] - k, 0)[:cur.shape[0]]


def _conv_at(h_ref, cw, cb, r0, rows, first):
    cur = h_ref[pl.ds(r0, rows), :]
    prev8 = _rows_before(h_ref, r0, first)
    u = cb + cw[0:1, :] * _shift_down(cur, prev8, 2)
    u = u + cw[1:2, :] * _shift_down(cur, prev8, 1)
    return u + cw[2:3, :] * cur


NJ = D_FF // CONV_TC


def _conv_gate(hup, cw, cb, S):
    T = hup.shape[0]
    rows = min(CONV_ROWS, S)
    nr = S // rows

    def body(ha_ref, hv_ref, cwa_ref, cwv_ref, cba_ref, cbv_ref, act_ref, ua_ref, uv_ref):
        cwa, cwv, cba, cbv = cwa_ref[...], cwv_ref[...], cba_ref[...], cbv_ref[...]

        def step(c, _):
            r0 = pl.multiple_of(c * rows, rows)
            ua = _conv_at(ha_ref, cwa, cba, r0, rows, c == 0)
            uv = _conv_at(hv_ref, cwv, cbv, r0, rows, c == 0)
            ua_ref[pl.ds(r0, rows), :] = ua
            uv_ref[pl.ds(r0, rows), :] = uv
            act_ref[pl.ds(r0, rows), :] = _bf((ua * jax.nn.sigmoid(ua)) * uv)
            return 0

        lax.fori_loop(0, nr, step, 0)

    blk = lambda o: BS((S, CONV_TC), lambda b, j: (b, j + o))
    w3 = lambda o: BS((3, CONV_TC), lambda b, j: (0, j + o))
    w1 = lambda o: BS((1, CONV_TC), lambda b, j: (0, j + o))
    return pl.pallas_call(
        body, name="conv_gate", out_shape=(SDS((T, D_FF), BF16), SDS((T, D_FF), F32), SDS((T, D_FF), F32)),
        grid=(T // S, NJ),
        in_specs=[blk(0), blk(NJ), w3(0), w3(NJ), w1(0), w1(NJ)], out_specs=(blk(0), blk(0), blk(0)),
        compiler_params=_cp(("parallel", "parallel")),
    )(hup, hup, cw, cw, cb, cb)


def _down_loss(act, w_down, x1, tgt, g3):
    T = x1.shape[0]
    tm = _tile(T, 512)

    def body(a_ref, w_ref, x1_ref, t_ref, g_ref, dx_ref, dg_ref, ls_ref):
        @pl.when(pl.program_id(0) == 0)
        def _():
            dg_ref[...] = jnp.zeros_like(dg_ref)
            ls_ref[...] = jnp.zeros_like(ls_ref)

        g = g_ref[...]
        x2 = x1_ref[...] + _dot(a_ref[...], w_ref[...])
        r = lax.rsqrt(jnp.mean(x2 * x2, axis=-1, keepdims=True) + EPS)
        xh = x2 * r
        e = xh * g - t_ref[...]
        ls_ref[...] += 0.5 * jnp.sum(jnp.mean(e * e, axis=-1, keepdims=True), axis=0, keepdims=True)
        dy = e * (1.0 / D)
        dxh = dy * g
        dx_ref[...] = r * (dxh - xh * jnp.mean(dxh * xh, axis=-1, keepdims=True))
        dg_ref[...] += jnp.sum(dy * xh, axis=0, keepdims=True)

    row = lambda w: BS((tm, w), lambda i: (i, 0))
    return pl.pallas_call(
        body, name="down_loss", out_shape=(SDS((T, D), F32), SDS((1, D), F32), SDS((1, LANE), F32)), grid=(T // tm,),
        in_specs=[row(D_FF), BS((D_FF, D), lambda i: (0, 0)), row(D), row(D), BS((1, D), lambda i: (0, 0))],
        out_specs=(row(D), BS((1, D), lambda i: (0, 0)), BS((1, LANE), lambda i: (0, 0))),
        compiler_params=_cp(("arbitrary",)),
    )(act, w_down, x1, tgt, g3)


def _conv_gate_bwd(hup, u_a, u_v, dact, cw, S):
    T = hup.shape[0]
    rows = min(CONV_ROWS, S)
    nr = S // rows

    def body(ha_ref, hv_ref, ua_ref, uv_ref, da_ref, cwa_ref, cwv_ref,
             dha_ref, dhv_ref, dcwa_ref, dcwv_ref, dcba_ref, dcbv_ref):
        @pl.when(pl.program_id(1) == 0)
        def _():
            for r in (dcwa_ref, dcwv_ref, dcba_ref, dcbv_ref):
                r[...] = jnp.zeros_like(r)

        cwa, cwv = cwa_ref[...], cwv_ref[...]

        def du_at(r0, n):
            ua = ua_ref[pl.ds(r0, n), :]
            uv = uv_ref[pl.ds(r0, n), :]
            da = da_ref[pl.ds(r0, n), :]
            sg = jax.nn.sigmoid(ua)
            dua = (da * uv) * (sg * (1.0 + ua * (1.0 - sg)))
            duv = da * (ua * sg)
            return dua, duv

        def step(c, _):
            r0 = pl.multiple_of(c * rows, rows)
            last = c == nr - 1
            dua, duv = du_at(r0, rows)
            n0 = pl.multiple_of(jnp.minimum(r0 + rows, S - 8), 8)
            nua, nuv = du_at(n0, 8)
            nua = jnp.where(last, 0.0, nua)
            nuv = jnp.where(last, 0.0, nuv)
            for (h_ref, cw, du, nu, dh_ref, dcw_ref, dcb_ref) in (
                    (ha_ref, cwa, dua, nua, dha_ref, dcwa_ref, dcba_ref),
                    (hv_ref, cwv, duv, nuv, dhv_ref, dcwv_ref, dcbv_ref)):
                up1, up2 = _shift_up(du, nu, 1), _shift_up(du, nu, 2)
                dh_ref[pl.ds(r0, rows), :] = _bf(cw[2:3, :] * du + cw[1:2, :] * up1 + cw[0:1, :] * up2)
                cur = h_ref[pl.ds(r0, rows), :]
                dcw_ref[0:1, :] += jnp.sum(up2 * cur, axis=0, keepdims=True)
                dcw_ref[1:2, :] += jnp.sum(up1 * cur, axis=0, keepdims=True)
                dcw_ref[2:3, :] += jnp.sum(du * cur, axis=0, keepdims=True)
                dcb_ref[...] += jnp.sum(du, axis=0, keepdims=True)
            return 0

        lax.fori_loop(0, nr, step, 0)

    blk = lambda o: BS((S, CONV_TC), lambda j, b: (b, j + o))
    w3 = lambda o: BS((3, CONV_TC), lambda j, b: (0, j + o))
    w1 = BS((1, CONV_TC), lambda j, b: (0, j))
    return pl.pallas_call(
        body, name="conv_gate_bwd",
        out_shape=(SDS((T, D_FF), BF16), SDS((T, D_FF), BF16), SDS((3, D_FF), F32), SDS((3, D_FF), F32),
                   SDS((1, D_FF), F32), SDS((1, D_FF), F32)),
        grid=(NJ, T // S),
        in_specs=[blk(0), blk(NJ), blk(0), blk(0), blk(0), w3(0), w3(NJ)],
        out_specs=(blk(0), blk(0), w3(0), w3(0), w1, w1),
        compiler_params=_cp(("parallel", "arbitrary")),
    )(hup, hup, u_a, u_v, dact, cw, cw)


def _mix_bwd(docat, o_sb, o_gla, proj, g_sb, g_gla):
    T = docat.shape[0]
    tm = _tile(T, 512)

    def body(d_ref, osb_ref, ogl_ref, og_ref, gsb_ref, ggl_ref, dsb_ref, dgl_ref, dog_ref, dgsb_ref, dggl_ref):
        @pl.when(pl.program_id(0) == 0)
        def _():
            dgsb_ref[...] = jnp.zeros_like(dgsb_ref)
            dggl_ref[...] = jnp.zeros_like(dggl_ref)

        bd64 = _head_blockdiag(SB_W, SB_HD)
        bd128 = _head_blockdiag(GLA_W, GLA_DV)
        d = d_ref[:, 0:SB_W]
        o = osb_ref[...]
        r = lax.rsqrt(_split_dot(o * o, bd64, 2) * (1.0 / SB_HD) + EPS)
        n = o * r
        dn = d * gsb_ref[...]
        dgsb_ref[...] += jnp.sum(d * n, axis=0, keepdims=True)
        dsb_ref[...] = r * (dn - n * (_split_dot(dn * n, bd64, 2) * (1.0 / SB_HD)))

        d = d_ref[:, SB_W:D]
        o = ogl_ref[...]
        r = lax.rsqrt(_split_dot(o * o, bd128, 2) * (1.0 / GLA_DV) + EPS)
        n = o * r
        og = og_ref[...]
        sg = jax.nn.sigmoid(og)
        dm = d * (og * sg)
        dog_ref[...] = _bf((d * (n * ggl_ref[...])) * (sg * (1.0 + og * (1.0 - sg))))
        dn = dm * ggl_ref[...]
        dggl_ref[...] += jnp.sum(dm * n, axis=0, keepdims=True)
        dgl_ref[...] = r * (dn - n * (_split_dot(dn * n, bd128, 2) * (1.0 / GLA_DV)))

    row = lambda w: BS((tm, w), lambda i: (i, 0))
    vec = lambda w: BS((1, w), lambda i: (0, 0))
    ogb = BS((tm, 512), lambda i: (i, OG_BLK))
    return pl.pallas_call(
        body, name="mix_bwd",
        out_shape=(SDS((T, SB_W), F32), SDS((T, GLA_W), F32), SDS((T, PROJ_W), BF16), SDS((1, SB_W), F32),
                   SDS((1, GLA_W), F32)),
        grid=(T // tm,),
        in_specs=[row(D), row(SB_W), row(GLA_W), ogb, vec(SB_W), vec(GLA_W)],
        out_specs=(row(SB_W), row(GLA_W), ogb, vec(SB_W), vec(GLA_W)),
        compiler_params=_cp(("arbitrary",)),
    )(docat, o_sb, o_gla, proj, g_sb, g_gla)


def _sb_bwd(proj, tt, do, dproj, S, pieces):
    T = proj.shape[0]
    nq = S // QT
    scale = SB_HD ** -0.5
    nb, ns = T // S, len(pieces)

    def body(qkv_ref, tt_ref, do_ref, dp_in_ref, *rest):
        del dp_in_ref
        pc_refs, dp_ref, got_refs = rest[:ns], rest[ns], rest[ns + 1:2 * ns + 1]
        dk_acc, dv_acc = rest[2 * ns + 1:2 * ns + 3]
        sems = rest[2 * ns + 3:]
        first = jnp.logical_and(pl.program_id(0) == 0, pl.program_id(1) == 0)
        last = jnp.logical_and(pl.program_id(0) == nb - 1, pl.program_id(1) == 3)

        @pl.when(first)
        def _():
            _exchange_ops(pc_refs, got_refs, False, sems, "start")

        row, col, lane, kr, kc = _sb_masks()
        mincl = _bf(kr <= kc)
        mexcl = _bf(kr < kc)
        dk_acc[...] = jnp.zeros_like(dk_acc)
        dv_acc[...] = jnp.zeros_like(dv_acc)

        def qloop(qt, _):
            r0 = pl.multiple_of(qt * QT, QT)
            qs = _stack_heads(qkv_ref[pl.ds(r0, QT), 0:128] * scale, lane)
            dos = _stack_heads(do_ref[pl.ds(r0, QT), :], lane)
            ttv = tt_ref[pl.ds(r0, QT), :]
            tot = jnp.concatenate([ttv[:, 0:1], ttv[:, SB_HD:SB_HD + 1]], axis=0)
            walked = jnp.max(ttv[:, CNT_LANE:CNT_LANE + 1]).astype(jnp.int32)

            def step(kt, st):
                dq, lc, pc = st
                k0 = pl.multiple_of(kt * TK, TK)
                kv = _bf(qkv_ref[pl.ds(k0, TK), 128:256])
                vv = _bf(qkv_ref[pl.ds(k0, TK), 256:384])
                strict = (col + (kt - qt) * TK) < row
                z = jnp.where(strict, _dot_nt(qs, kv), SB_MASKED)
                sp = _softplus(z)
                lg = -sp
                after = tot - (lc + _split_dot(lg, mincl, 2))
                gl = z - sp
                w = jnp.exp(gl + after)
                du = w * _dot_nt(dos, vv)
                beta = jnp.exp(gl)
                pex = pc + _split_dot(du, mexcl, 2)
                dz = _bf(du - beta * (du + pex))
                dk_acc[pl.ds(k0, TK), :] += _dot_tn(dz, qs)
                dv_acc[pl.ds(k0, TK), :] += _dot_tn(_bf(w), dos)
                return (dq + _dot(dz, kv), lc + jnp.sum(lg, axis=1, keepdims=True),
                        pc + jnp.sum(du, axis=1, keepdims=True))

            zc = jnp.zeros((2 * QT, 1), F32)
            dq, _, _ = lax.fori_loop(qt - walked + 1, qt + 1, step, (jnp.zeros((2 * QT, LANE), F32), zc, zc))
            dp_ref[pl.ds(r0, QT), 0:128] = _bf(jnp.where(lane < SB_HD, dq[:QT], dq[QT:]) * scale)
            return 0

        lax.fori_loop(0, nq, qloop, 0)
        dp_ref[:, 128:256] = _bf(dk_acc[...])
        dp_ref[:, 256:384] = _bf(dv_acc[...])

        @pl.when(last)
        def _():
            _exchange_ops(pc_refs, got_refs, False, sems, "wait")

    blk = BS((S, 384), lambda b, p: (b, p))
    col_spec = BS((S, LANE), lambda b, p: (b, p))
    hbm = BS(memory_space=pltpu.HBM)
    return pl.pallas_call(
        body, name="sb_bwd", out_shape=(SDS((T, PROJ_W), BF16),) + tuple(SDS(s.shape, s.dtype) for s in pieces),
        grid=(nb, 4),
        in_specs=[blk, col_spec, col_spec, BS(memory_space=pl.ANY)] + [hbm] * ns, out_specs=(blk,) + (hbm,) * ns,
        scratch_shapes=[pltpu.VMEM((S, LANE), F32), pltpu.VMEM((S, LANE), F32)] + _exchange_sems(ns),
        input_output_aliases={3: 0},
        compiler_params=_cp(("arbitrary", "arbitrary")),
    )(proj, tt, do, dproj, *pieces)


def _gla_bwd(proj, wgu, bgu, do, dproj, S):
    T = proj.shape[0]
    nc, ng = S // CHUNK, S // GR

    def body(blk_ref, glr_ref, wgu_ref, bgu_ref, do_ref, dp_in_ref, dp_ref, dpre_ref, st_ref):
        del dp_in_ref
        causal, tri_incl, tri_rev, lane = _gla_masks()
        wg = _bf(wgu_ref[...])
        bg = bgu_ref[...]

        def fwd_group(g, states):
            r0 = pl.multiple_of(g * GR, GR)
            v, _, _, _, _, _, _, ke, decay = _gla_group_terms(blk_ref, glr_ref, wg, bg, r0, tri_incl)
            keb = _bf(ke)
            new_states = []
            for hh in range(2):
                vh = _bf(v[:, 128 * hh:128 * hh + 128])
                st = states[hh]
                for n in range(GLA_G):
                    rows = _chunk_rows(n)
                    st_ref[hh, g * GLA_G + n] = st
                    st = st * decay[n * CHUNK:n * CHUNK + 1] + _dot_tn(vh[rows], keb[rows])
                new_states.append(st)
            return tuple(new_states)

        z = jnp.zeros((GLA_DV, LANE), F32)
        lax.fori_loop(0, ng, fwd_group, (z, z))

        def bwd_group(it, dstates):
            g = ng - 1 - it
            r0 = pl.multiple_of(g * GR, GR)
            v, pre, b, b_last, eb, qd, ki, ke, decay = _gla_group_terms(
                blk_ref, glr_ref, wg, bg, r0, tri_incl)
            kib = _bf(ki)
            dqd = jnp.zeros((GR, LANE), F32)
            dki = jnp.zeros((GR, LANE), F32)
            dke = jnp.zeros((GR, LANE), F32)
            ddec = jnp.zeros((GR, LANE), F32)
            new_dstates, dvs = [], []
            for hh in range(2):
                hm = (lane // GLA_DK) == hh
                qm = _bf(jnp.where(hm, qd, 0.0))
                kem = _bf(jnp.where(hm, ke, 0.0))
                vh = _bf(v[:, 128 * hh:128 * hh + 128])
                doh = _bf(do_ref[pl.ds(r0, GR), 128 * hh:128 * hh + 128])
                attn = _bf(jnp.where(causal, _dot_nt(qm, kib), 0.0))
                dattn = _bf(jnp.where(causal, _dot_nt(doh, vh), 0.0))
                dv_intra = _dot_tn(attn, doh)
                dqd_intra = _dot(dattn, kib)
                dki = dki + _dot_tn(dattn, qm)
                dst = dstates[hh]
                dv_p, dqd_p, dke_p, ddec_p = [None] * GLA_G, [None] * GLA_G, [None] * GLA_G, [None] * GLA_G
                for n in reversed(range(GLA_G)):
                    rows = _chunk_rows(n)
                    st = st_ref[hh, g * GLA_G + n]
                    dv_p[n] = dv_intra[rows] + _dot_nt(kem[rows], _bf(dst))
                    dqd_p[n] = dqd_intra[rows] + _dot(doh[rows], _bf(st))
                    dke_p[n] = _dot(vh[rows], _bf(dst))
                    ddec_p[n] = jnp.broadcast_to(jnp.sum(dst * st, axis=0, keepdims=True), (CHUNK, LANE))
                    dst = dst * decay[n * CHUNK:n * CHUNK + 1] + _dot_tn(doh[rows], qm[rows])
                dvs.append(jnp.concatenate(dv_p, axis=0))
                dqd = dqd + jnp.where(hm, jnp.concatenate(dqd_p, axis=0), 0.0)
                dke = dke + jnp.where(hm, jnp.concatenate(dke_p, axis=0), 0.0)
                ddec = ddec + jnp.where(hm, jnp.concatenate(ddec_p, axis=0), 0.0)
                new_dstates.append(dst)
            einv = jnp.exp(-b)
            eend = jnp.exp(b_last - b)
            dq = (dqd * eb) * (GLA_DK ** -0.5)
            dk = dki * einv + dke * eend
            db = dqd * qd - dki * ki - dke * ke
            dkk = dke * ke
            db_last = _per_chunk(lambda rows: jnp.sum(dkk[rows], axis=0, keepdims=True)) + ddec * decay
            dla = _split_dot(db, tri_rev, 2, left=False) + db_last
            dpre_ref[pl.ds(r0, GR), :] = (dla * (1.0 / GATE_NORM)) * (1.0 - jax.nn.sigmoid(pre))
            dp_ref[pl.ds(r0, GR), 0:128] = _bf(dq)
            dp_ref[pl.ds(r0, GR), 128:256] = _bf(dk)
            dp_ref[pl.ds(r0, GR), 256:512] = _bf(jnp.concatenate(dvs, axis=1))
            return tuple(new_dstates)

        lax.fori_loop(0, ng, bwd_group, (z, z))

    return pl.pallas_call(
        body, name="gla_bwd", out_shape=(SDS((T, PROJ_W), BF16), SDS((T, GLA_KW), F32)), grid=(T // S, 2),
        in_specs=[BS((S, 512), lambda b, p: (b, 3 + p)), BS((S, LANE), lambda b, p: (b, GLR_BLK)),
                  BS((LANE, LANE), lambda b, p: (0, p)), BS((1, LANE), lambda b, p: (0, p)),
                  BS((S, 256), lambda b, p: (b, p)), BS(memory_space=pl.ANY)],
        out_specs=(BS((S, 512), lambda b, p: (b, 3 + p)), BS((S, LANE), lambda b, p: (b, p))),
        scratch_shapes=[pltpu.VMEM((2, nc, GLA_DV, LANE), F32)],
        input_output_aliases={5: 0},
        compiler_params=_cp(("parallel", "parallel")),
    )(proj, proj, wgu, bgu, do, dproj)


def _gate_bwd(dpre, proj, wgu, dproj):
    T = dpre.shape[0]
    tm = _tile(T, 512)

    def body(dpre_ref, glr_ref, wgu_ref, dp_in_ref, dp_ref, dw_ref, db_ref):
        del dp_in_ref

        @pl.when(pl.program_id(0) == 0)
        def _():
            dw_ref[...] = jnp.zeros_like(dw_ref)
            db_ref[...] = jnp.zeros_like(db_ref)

        dpre = dpre_ref[...]
        dp_ref[...] = _bf(_dot_nt(_bf(dpre), _bf(wgu_ref[...])))
        dw_ref[...] += _dot_tn(_bf(glr_ref[...]), _bf(dpre))
        db_ref[...] += jnp.sum(dpre, axis=0, keepdims=True)

    glr = BS((tm, LANE), lambda i: (i, GLR_BLK))
    return pl.pallas_call(
        body, name="gate_bwd",
        out_shape=(SDS((T, PROJ_W), BF16), SDS((LANE, GLA_KW), F32), SDS((1, GLA_KW), F32)), grid=(T // tm,),
        in_specs=[BS((tm, GLA_KW), lambda i: (i, 0)), glr, BS((LANE, GLA_KW), lambda i: (0, 0)),
                  BS(memory_space=pl.ANY)],
        out_specs=(glr, BS((LANE, GLA_KW), lambda i: (0, 0)), BS((1, GLA_KW), lambda i: (0, 0))),
        input_output_aliases={3: 0},
        compiler_params=_cp(("arbitrary",)),
    )(dpre, proj, wgu, dproj)


def _exchange_sems(n):
    return [pltpu.SemaphoreType.DMA((n * (N_DEV - 1),)), pltpu.SemaphoreType.DMA((n * (N_DEV - 1),)),
            pltpu.SemaphoreType.DMA((n,))]


def _exchange_ops(srcs, outs, gather, sems, act):
    ssem, rsem, lsem = sems
    x, y, c = lax.axis_index("x"), lax.axis_index("y"), lax.axis_index("c")
    me = 4 * x + 2 * y + c
    for i, (s_ref, o_ref) in enumerate(zip(srcs, outs)):
        for k in range(1, N_DEV):
            px = (x + ((k >> 2) & 1)) % 2
            py = (y + ((k >> 1) & 1)) % 2
            pc = (c + (k & 1)) % 2
            peer = 4 * px + 2 * py + pc
            n = i * (N_DEV - 1) + k - 1
            out = pltpu.make_async_remote_copy(
                src_ref=s_ref if gather else s_ref.at[peer], dst_ref=o_ref.at[me],
                send_sem=ssem.at[n], recv_sem=rsem.at[n],
                device_id=(px, py, pc), device_id_type=pl.DeviceIdType.MESH)
            if act == "start":
                out.start()
            else:
                out.wait_send()
                pltpu.make_async_remote_copy(
                    src_ref=s_ref if gather else s_ref.at[me], dst_ref=o_ref.at[peer],
                    send_sem=ssem.at[n], recv_sem=rsem.at[n],
                    device_id=(x, y, c), device_id_type=pl.DeviceIdType.MESH).wait_recv()
        mine = pltpu.make_async_copy(s_ref if gather else s_ref.at[me], o_ref.at[me], lsem.at[i])
        if act == "start":
            mine.start()
        else:
            mine.wait()


def _gather_two_level(src, name):
    def body(s_ref, o_ref, ssem, rsem, lsem):
        x, y, c = lax.axis_index("x"), lax.axis_index("y"), lax.axis_index("c")
        me, sibling = (x, y, c), (x, y, 1 - c)
        chips = [(1 - x, y), (x, 1 - y), (1 - x, 1 - y)]

        def slab(px, py, pc):
            return o_ref.at[4 * px + 2 * py + pc]

        def copy(k, block, to, src_ref=None):
            return pltpu.make_async_remote_copy(
                src_ref=slab(*block) if src_ref is None else src_ref, dst_ref=slab(*block),
                send_sem=ssem.at[k], recv_sem=rsem.at[k], device_id=to, device_id_type=pl.DeviceIdType.MESH)

        mine = pltpu.make_async_copy(s_ref, slab(*me), lsem)
        mine.start()
        first = [copy(0, me, sibling, s_ref)] + [copy(1 + j, me, (*chip, c), s_ref) for j, chip in enumerate(chips)]
        for cp in first:
            cp.start()
        passed = [copy(4 + j, (*chip, c), sibling) for j, chip in enumerate(chips)]
        for j, chip in enumerate(chips):
            copy(1 + j, (*chip, c), me).wait_recv()
            passed[j].start()
        copy(0, sibling, me).wait_recv()
        for j, chip in enumerate(chips):
            copy(4 + j, (*chip, 1 - c), me).wait_recv()
        for cp in first + passed:
            cp.wait_send()
        mine.wait()

    hbm = BS(memory_space=pltpu.HBM)
    return pl.pallas_call(
        body, name=name, out_shape=SDS((N_DEV,) + src.shape, src.dtype), in_specs=[hbm], out_specs=hbm,
        scratch_shapes=[pltpu.SemaphoreType.DMA((N_DEV - 1,)), pltpu.SemaphoreType.DMA((N_DEV - 1,)),
                        pltpu.SemaphoreType.DMA(())],
    )(src)


def _exchange(srcs, gather, name):
    n = len(srcs)
    shapes = [SDS((N_DEV,) + s.shape if gather else s.shape, s.dtype) for s in srcs]

    def body(*refs):
        s_refs, o_refs, sems = refs[:n], refs[n:2 * n], refs[2 * n:]
        _exchange_ops(s_refs, o_refs, gather, sems, "start")
        _exchange_ops(s_refs, o_refs, gather, sems, "wait")

    hbm = BS(memory_space=pltpu.HBM)
    return pl.pallas_call(
        body, name=name, out_shape=tuple(shapes), in_specs=[hbm] * n, out_specs=(hbm,) * n,
        scratch_shapes=_exchange_sems(n),
    )(*srcs)


def _adamw_math(w, g, m, v):
    m = ADAM_B1 * m + (1.0 - ADAM_B1) * g
    v = ADAM_B2 * v + (1.0 - ADAM_B2) * (g * g)
    m_hat = m / (1.0 - ADAM_B1 ** ADAM_STEP)
    v_hat = v / (1.0 - ADAM_B2 ** ADAM_STEP)
    delta = -ADAM_LR * (m_hat / (jnp.sqrt(v_hat) + ADAM_EPS) + ADAM_WD * w)
    return delta, m, v


def _sum_adamw(parts, w, m, v, tr, name):
    R, C = w.shape

    def body(p_ref, w_ref, m_ref, v_ref, g_ref, d_ref, nm_ref, nv_ref):
        g = p_ref[0].astype(F32)
        for d in range(1, N_DEV):
            g = g + p_ref[d].astype(F32)
        delta, nm, nv = _adamw_math(w_ref[...], g, m_ref[...], v_ref[...])
        g_ref[...] = g
        d_ref[...] = delta
        nm_ref[...] = nm
        nv_ref[...] = nv

    blk = BS((tr, C), lambda i: (i, 0))
    out = SDS((R, C), F32)
    return pl.pallas_call(
        body, name=name, out_shape=(out, out, out, out), grid=(R // tr,),
        in_specs=[BS((N_DEV, tr, C), lambda i: (0, i, 0)), blk, blk, blk], out_specs=(blk, blk, blk, blk),
        compiler_params=_cp(("parallel",)),
    )(parts, w, m, v)


def _flat_pad_rows(parts, rows):
    flat = jnp.concatenate([p.reshape(-1) for p in parts])
    return jnp.pad(flat, (0, rows * D - flat.shape[0])).reshape(rows, D)


SMALL_ROWS = 16
SHARD_SMALL_ROWS = 3


def kernel(x, attn_norm_g, w_in, w_gate_up, b_gate_up, sb_out_g, gla_out_g, w_out, ffn_norm_g, w_ffn_up, conv_w, conv_b, w_ffn_down, final_norm_g, loss_target, m_attn_norm_g, m_w_in, m_w_gate_up, m_b_gate_up, m_sb_out_g, m_gla_out_g, m_w_out, m_ffn_norm_g, m_w_ffn_up, m_conv_w, m_conv_b, m_w_ffn_down, m_final_norm_g, v_attn_norm_g, v_w_in, v_w_gate_up, v_b_gate_up, v_sb_out_g, v_gla_out_g, v_w_out, v_ffn_norm_g, v_w_ffn_up, v_conv_w, v_conv_b, v_w_ffn_down, v_final_norm_g):
    Bd, S, _ = x.shape
    T = Bd * S
    x2d = x.reshape(T, D)
    tgt = loss_target.reshape(T, D)
    c_up = w_ffn_up.shape[2]
    c_gu = w_gate_up.shape[2]
    c_in = w_in.shape[2]

    n_gu = GATE_RANK * c_gu
    rows_bf = lambda w: w[0].T.astype(BF16)
    small_w = lambda wgu, cw: _flat_pad_rows([wgu, cw], SHARD_SMALL_ROWS)

    g_in = _gather_two_level(rows_bf(w_in), "gather_w_in")
    w_in_pt = jnp.pad(g_in.reshape(IN_COLS, D), ((0, 1), (0, 0)))[_PERM]
    g3 = final_norm_g.reshape(1, D)

    proj, h1 = _norm_proj(x2d, attn_norm_g, w_in_pt)
    o_sb, tt, g_up, g_down, g_out, gs = _sb_fwd(
        proj, S, [rows_bf(w_ffn_up), w_ffn_down[0].astype(BF16), w_out[0].astype(BF16),
                  _flat_pad_rows([w_gate_up, conv_w], 8)])
    w_out_f = g_out.reshape(D, D)
    gsf = gs.reshape(N_DEV, -1)
    wgu_f = jnp.transpose(gsf[:, :n_gu].reshape(N_DEV, GATE_RANK, c_gu), (1, 0, 2)).reshape(GATE_RANK, GLA_KW)
    cw_f = jnp.transpose(gsf[:, n_gu:n_gu + 3 * c_up].reshape(N_DEV, 3, c_up), (1, 0, 2)).reshape(3, 2 * D_FF)
    wgu_p = jnp.pad(wgu_f, ((0, LANE - GATE_RANK), (0, 0)))
    w_up_t = g_up.reshape(2 * D_FF, D)
    w_down_f = g_down.reshape(D_FF, D)
    o_gla = _gla_fwd(proj, wgu_p, b_gate_up, S)
    x1, ocat, h2 = _mix_out(o_sb, o_gla, proj, x2d, sb_out_g, gla_out_g, w_out_f, ffn_norm_g)
    hup = _mm(h2, w_up_t, "nt", "ffn_up", tm=1024, tn=1408, tk=1024)
    act, u_a, u_v = _conv_gate(hup, cw_f, conv_b, S)
    dx2, dg3, loss_dev = _down_loss(act, w_down_f, x1, tgt, g3)

    dw_down = _mm(act, dx2, "tn", "dw_down", out_dtype=BF16, tm=D_FF, tn=1024, tk=512)
    dact = _mm(dx2, w_down_f, "nt", "dact", tm=1024, tn=1408, tk=1024)
    dhup_a, dhup_v, dcw_a, dcw_v, dcb_a, dcb_v = _conv_gate_bwd(hup, u_a, u_v, dact, cw_f, S)
    dw_up_t = _mm(dhup_a, h2, "tn", "dw_up_a", out_dtype=BF16, tm=D_FF, tn=1024, tk=512, out_rows=2 * D_FF)
    dw_up_t = _mm(dhup_v, h2, "tn", "dw_up_v", out_dtype=BF16, tm=D_FF, tn=1024, tk=512, out_rows=2 * D_FF,
                  out_row0=D_FF, into=dw_up_t)
    dx1, dg2 = _mm(dhup_a, w_up_t, "nn", "dh2", a2=dhup_v, tm=512, tn=1024, tk=1408,
                   norm_bwd=(x1, ffn_norm_g, dx2))

    dw_out = _mm(ocat, dx1, "tn", "dw_out", out_dtype=BF16, tm=1024, tn=1024, tk=512)
    docat = _mm(dx1, w_out_f, "nt", "docat", tm=1024, tn=1024, tk=1024)
    do_sb, do_gla, dproj, dg_sb, dg_gla = _mix_bwd(docat, o_sb, o_gla, proj, sb_out_g, gla_out_g)
    dproj, got_up, got_down, got_out = _sb_bwd(
        proj, tt, do_sb, dproj, S,
        [dw_up_t.reshape(N_DEV, c_up, D), dw_down.reshape(N_DEV, -1, D), dw_out.reshape(N_DEV, -1, D)])
    dproj, dpre = _gla_bwd(proj, wgu_p, b_gate_up, do_gla, dproj, S)
    dproj, dwgu, dbgu = _gate_bwd(dpre, proj, wgu_p, dproj)
    dw_in_pt = _mm(dproj, h1, "tn", "dw_in", out_dtype=BF16, tm=PROJ_W, tn=1024, tk=512)
    dx, dg1, got_in = _mm(dproj, w_in_pt, "nn", "dh1", tm=1024, tn=1024, tk=640,
                          xchg=([dw_in_pt[_INV_PERM].reshape(N_DEV, c_in, D)], False),
                          norm_bwd=(x2d, attn_norm_g, dx1))

    dcw = jnp.concatenate([dcw_a, dcw_v], axis=1)
    dwgu_pc = jnp.transpose(dwgu[:GATE_RANK].reshape(GATE_RANK, N_DEV, c_gu), (1, 0, 2)).reshape(N_DEV, -1)
    dcw_pc = jnp.transpose(dcw.reshape(3, N_DEV, c_up), (1, 0, 2)).reshape(N_DEV, -1)
    small_pc = jnp.concatenate([dwgu_pc, dcw_pc], axis=1)
    small_pc = jnp.pad(small_pc, ((0, 0), (0, SHARD_SMALL_ROWS * D - small_pc.shape[1])))
    small_pc = small_pc.reshape(N_DEV, SHARD_SMALL_ROWS, D).astype(BF16)
    rep_names = ["attn_norm_g", "b_gate_up", "sb_out_g", "gla_out_g", "ffn_norm_g", "conv_b", "final_norm_g"]
    rep_g = [dg1, dbgu, dg_sb, dg_gla, dg2, jnp.concatenate([dcb_a, dcb_v], axis=1), dg3]
    rep_w = [attn_norm_g, b_gate_up, sb_out_g, gla_out_g, ffn_norm_g, conv_b, final_norm_g]
    rep_m = [m_attn_norm_g, m_b_gate_up, m_sb_out_g, m_gla_out_g, m_ffn_norm_g, m_conv_b, m_final_norm_g]
    rep_v = [v_attn_norm_g, v_b_gate_up, v_sb_out_g, v_gla_out_g, v_ffn_norm_g, v_conv_b, v_final_norm_g]
    rep_pc = jnp.broadcast_to(_flat_pad_rows(rep_g + [loss_dev[:, 0:1]], SMALL_ROWS), (N_DEV, SMALL_ROWS, D))
    got_sm, got_rep = _exchange([small_pc, rep_pc], False, "scatter_tail")

    rows = lambda w: w[0].T
    cols = lambda r: r.T[None]
    res = {}
    res["w_in"] = [cols(r) for r in _sum_adamw(got_in, rows(w_in), rows(m_w_in), rows(v_w_in), c_in, "adamw_w_in")]
    res["w_out"] = [r[None] for r in _sum_adamw(got_out, w_out[0], m_w_out[0], v_w_out[0], w_out.shape[1],
                                                 "adamw_w_out")]
    res["w_ffn_up"] = [cols(r) for r in _sum_adamw(got_up, rows(w_ffn_up), rows(m_w_ffn_up), rows(v_w_ffn_up),
                                                    c_up // 2, "adamw_w_up")]
    res["w_ffn_down"] = [r[None] for r in _sum_adamw(got_down, w_ffn_down[0], m_w_ffn_down[0], v_w_ffn_down[0],
                                                      w_ffn_down.shape[1], "adamw_w_down")]
    sm = _sum_adamw(got_sm, small_w(w_gate_up, conv_w), small_w(m_w_gate_up, m_conv_w),
                    small_w(v_w_gate_up, v_conv_w), SHARD_SMALL_ROWS, "adamw_small_sharded")
    res["w_gate_up"] = [r.reshape(-1)[:n_gu].reshape(1, GATE_RANK, c_gu) for r in sm]
    res["conv_w"] = [r.reshape(-1)[n_gu:n_gu + 3 * c_up].reshape(1, 3, c_up) for r in sm]
    rep = _sum_adamw(got_rep, _flat_pad_rows(rep_w, SMALL_ROWS), _flat_pad_rows(rep_m, SMALL_ROWS),
                     _flat_pad_rows(rep_v, SMALL_ROWS), SMALL_ROWS, "adamw_replicated")
    o = 0
    for n, w in zip(rep_names, rep_w):
        res[n] = [r.reshape(-1)[o:o + w.size].reshape(w.shape) for r in rep]
        o += w.size

    loss = rep[0].reshape(-1)[o]
    order = ["attn_norm_g", "w_in", "w_gate_up", "b_gate_up", "sb_out_g", "gla_out_g", "w_out", "ffn_norm_g",
             "w_ffn_up", "conv_w", "conv_b", "w_ffn_down", "final_norm_g"]
    outs = [loss, dx.reshape(Bd, S, D)]
    for k in range(4):
        outs += [res[n][k] for n in order]
    return tuple(outs)
```

```python
import numpy as np
import jax
import jax.numpy as jnp
from jax import lax
from jax.experimental import pallas as pl
from jax.experimental.pallas import tpu as pltpu

F32 = jnp.float32
BF16 = jnp.bfloat16
SDS = jax.ShapeDtypeStruct
BS = pl.BlockSpec

N_DEV = 8
D = 1024
EPS = 1e-6
SB_HD = 64
SB_W = 512
GLA_DK = 64
GLA_DV = 128
GLA_KW = 256
GLA_W = 512
GATE_RANK = 16
GATE_NORM = 16.0
CHUNK = 64
GLA_G = 4
GR = GLA_G * CHUNK
QT = 256
D_FF = 2816
IN_COLS = 3088
PROJ_W = 3200
LANE = 128
VMEM_LIMIT = 56 * 1024 * 1024

ADAM_LR, ADAM_B1, ADAM_B2, ADAM_EPS, ADAM_WD, ADAM_STEP = 0.001, 0.9, 0.999, 1e-08, 0.01, 10


def _proj_perm():
    sbq, sbk, sbv = 0, 512, 1024
    gq, gk, gv, glr, gog = 1536, 1792, 2048, 2560, 2576
    cols = []
    for p in range(4):
        for base in (sbq, sbk, sbv):
            cols += list(range(base + 128 * p, base + 128 * p + 128))
    for p in range(2):
        cols += list(range(gq + 128 * p, gq + 128 * p + 128))
        cols += list(range(gk + 128 * p, gk + 128 * p + 128))
        cols += list(range(gv + 256 * p, gv + 256 * p + 256))
    cols += list(range(gog, gog + 512))
    cols += list(range(glr, glr + GATE_RANK)) + [IN_COLS] * (LANE - GATE_RANK)
    perm = np.asarray(cols, np.int32)
    inv = np.zeros((IN_COLS,), np.int32)
    for new, old in enumerate(cols):
        if old < IN_COLS:
            inv[old] = new
    return perm, inv


_PERM, _INV_PERM = _proj_perm()
OG_BLK = 5
GLR_BLK = 24


def _cp(sem=None, vmem=VMEM_LIMIT):
    return pltpu.CompilerParams(dimension_semantics=sem, vmem_limit_bytes=vmem)


def _dot(a, b):
    return lax.dot_general(a, b, (((1,), (0,)), ((), ())), preferred_element_type=F32)


def _dot_nt(a, b):
    return lax.dot_general(a, b, (((1,), (1,)), ((), ())), preferred_element_type=F32)


def _dot_tn(a, b):
    return lax.dot_general(a, b, (((0,), (0,)), ((), ())), preferred_element_type=F32)


def _bf(x):
    return x.astype(BF16)


def _split_dot(x, m, passes, left=True):
    acc = None
    r = x
    for i in range(passes):
        h = r.astype(BF16)
        t = _dot(h, m) if left else _dot(m, h)
        acc = t if acc is None else acc + t
        if i + 1 < passes:
            r = r - h.astype(F32)
    return acc


def _softplus(z):
    return jnp.maximum(z, 0.0) + jnp.log(1.0 + jnp.exp(-jnp.abs(z)))


def _rms_bwd_math(x, g, dh, dres):
    r = lax.rsqrt(jnp.mean(x * x, axis=-1, keepdims=True) + EPS)
    xh = x * r
    dxh = dh * g
    dx = dres + r * (dxh - xh * jnp.mean(dxh * xh, axis=-1, keepdims=True))
    return dx, jnp.sum(dh * xh, axis=0, keepdims=True)


def _tile(n, pref, mult=LANE):
    best = None
    for t in range(mult, min(n, pref) + 1, mult):
        if n % t == 0:
            best = t
    return best if best is not None else n


def _mm(a, b, mode, name, out_dtype=F32, c=None, tm=512, tn=512, tk=512, b_row0=0, out_rows=None, out_row0=0,
        into=None, xchg=None, norm_bwd=None, a2=None):
    if mode == "nn":
        (M, K), N = a.shape, b.shape[1]
    elif mode == "nt":
        (M, K), N = a.shape, b.shape[0]
    else:
        (K, M), N = a.shape, b.shape[1]
    tm, tn, tk = _tile(M, tm), _tile(N, tn), _tile(K, tk)
    has_a2 = a2 is not None
    assert not has_a2 or (mode == "nn" and a2.shape == a.shape)
    nka = K // tk
    nk = nka * (2 if has_a2 else 1)
    kb0, ob0 = b_row0 // tk, out_row0 // tm
    assert kb0 * tk == b_row0 and ob0 * tm == out_row0 and (mode == "nn" or b_row0 == 0)
    ni, nj = M // tm, N // tn
    j_outer = nk == 1 and (nj - 1) * a.size * a.dtype.itemsize < (ni - 1) * K * N * b.dtype.itemsize
    ix = (lambda f: (lambda j, i, k: f(i, j, k))) if j_outer else (lambda f: f)
    if mode == "tn":
        a_spec = BS((tk, tm), ix(lambda i, j, k: (k, i)))
    else:
        a_spec = BS((tm, tk), ix(lambda i, j, k: (i, jnp.minimum(k, nka - 1))))
    b_spec = (BS((tn, tk), ix(lambda i, j, k: (j, k))) if mode == "nt"
              else BS((tk, tn), ix(lambda i, j, k: (k + kb0, j))))
    dotfn = {"nn": _dot, "nt": _dot_nt, "tn": _dot_tn}[mode]
    has_c = c is not None
    has_into = into is not None
    nx = 0 if xchg is None else len(xchg[0])
    has_nb = norm_bwd is not None
    assert not has_nb or (nj == 1 and not j_outer and out_dtype == F32)
    n_in = 2 + has_a2 + has_c + has_into + 3 * has_nb

    def body(*refs):
        a_ref, b_ref = refs[:2]
        c_ref = refs[2 + has_a2] if has_c else None
        x_src = refs[n_in:n_in + nx]
        outs = refs[n_in + nx:n_in + 2 * nx + 1 + has_nb]
        o_ref, x_out = outs[0], outs[1 + has_nb:]
        acc = refs[n_in + 2 * nx + 1 + has_nb]
        sems = refs[n_in + 2 * nx + 2 + has_nb:]
        k = pl.program_id(2)
        g0, g1 = pl.program_id(0), pl.program_id(1)
        n0, n1 = (nj, ni) if j_outer else (ni, nj)
        first = jnp.logical_and(jnp.logical_and(g0 == 0, g1 == 0), k == 0)
        if nx:
            @pl.when(first)
            def _():
                _exchange_ops(x_src, x_out, xchg[1], sems, "start")

        if has_nb:
            @pl.when(first)
            def _():
                outs[1][...] = jnp.zeros_like(outs[1])

        @pl.when(k == 0)
        def _():
            acc[...] = jnp.zeros_like(acc)

        if has_a2:
            @pl.when(k < nka)
            def _():
                acc[...] += dotfn(_bf(a_ref[...]), _bf(b_ref[...]))

            @pl.when(k >= nka)
            def _():
                acc[...] += dotfn(_bf(refs[2][...]), _bf(b_ref[...]))
        else:
            acc[...] += dotfn(_bf(a_ref[...]), _bf(b_ref[...]))

        @pl.when(k == nk - 1)
        def _():
            r = acc[...]
            if has_c:
                r = r + c_ref[...]
            if has_nb:
                x_ref, g_ref, dres_ref = refs[n_in - 3:n_in]
                dx, dg = _rms_bwd_math(x_ref[...], g_ref[...], r, dres_ref[...])
                o_ref[...] = dx
                outs[1][...] += dg
            else:
                o_ref[...] = r.astype(out_dtype)

        if nx:
            @pl.when(jnp.logical_and(jnp.logical_and(g0 == n0 - 1, g1 == n1 - 1), k == nk - 1))
            def _():
                _exchange_ops(x_src, x_out, xchg[1], sems, "wait")

    tile = BS((tm, tn), ix(lambda i, j, k: (i, j)))
    in_specs = [a_spec, b_spec]
    args = [a, b]
    if has_a2:
        in_specs.append(BS((tm, tk), lambda i, j, k: (i, jnp.maximum(k - nka, 0))))
        args.append(a2)
    if has_c:
        in_specs.append(tile)
        args.append(c)
    aliases = {}
    if has_into:
        aliases = {len(args): 0}
        in_specs.append(BS(memory_space=pl.ANY))
        args.append(into)
    out_shape = [SDS((out_rows or M, N), out_dtype)]
    out_specs = [BS((tm, tn), ix(lambda i, j, k: (i + ob0, j)))]
    scratch = [pltpu.VMEM((tm, tn), F32)]
    if has_nb:
        in_specs += [tile, BS((1, tn), lambda i, j, k: (0, 0)), tile]
        args += list(norm_bwd)
        out_shape.append(SDS((1, N), F32))
        out_specs.append(BS((1, tn), lambda i, j, k: (0, 0)))
    if nx:
        hbm = BS(memory_space=pltpu.HBM)
        in_specs += [hbm] * nx
        args += list(xchg[0])
        out_shape += [SDS((N_DEV,) + s.shape if xchg[1] else s.shape, s.dtype) for s in xchg[0]]
        out_specs += [hbm] * nx
        scratch += _exchange_sems(nx)
    serial = nx or has_nb
    res = pl.pallas_call(
        body, name=name, out_shape=tuple(out_shape), grid=(nj, ni, nk) if j_outer else (ni, nj, nk),
        in_specs=in_specs, out_specs=tuple(out_specs),
        scratch_shapes=scratch, input_output_aliases=aliases,
        compiler_params=_cp(("arbitrary",) * 3 if serial else ("parallel", "parallel", "arbitrary")),
    )(*args)
    return res if serial else res[0]


def _norm_proj(x, g, w):
    T, N = x.shape[0], w.shape[0]
    tm = _tile(T, 512)

    def body(x_ref, g_ref, w_ref, p_ref, h_ref):
        xv = x_ref[...]
        r = lax.rsqrt(jnp.mean(xv * xv, axis=-1, keepdims=True) + EPS)
        h = _bf((xv * r) * g_ref[...])
        h_ref[...] = h
        p_ref[...] = _dot_nt(h, w_ref[...])

    return pl.pallas_call(
        body, name="norm_proj", out_shape=(SDS((T, N), F32), SDS((T, D), BF16)), grid=(T // tm,),
        in_specs=[BS((tm, D), lambda i: (i, 0)), BS((1, D), lambda i: (0, 0)), BS((N, D), lambda i: (0, 0))],
        out_specs=(BS((tm, N), lambda i: (i, 0)), BS((tm, D), lambda i: (i, 0))),
        compiler_params=_cp(("parallel",)),
    )(x, g, w)


TK = 256
SB_DEAD = -104.0
SB_MASKED = -1e30
CNT_LANE = SB_HD - 1


def _sb_masks():
    row = lax.broadcasted_iota(jnp.int32, (2 * QT, TK), 0) & (QT - 1)
    col = lax.broadcasted_iota(jnp.int32, (2 * QT, TK), 1)
    lane = lax.broadcasted_iota(jnp.int32, (1, LANE), 1)
    kr = lax.broadcasted_iota(jnp.int32, (TK, TK), 0)
    kc = lax.broadcasted_iota(jnp.int32, (TK, TK), 1)
    return row, col, lane, kr, kc


def _stack_heads(x, lane):
    return jnp.concatenate([_bf(jnp.where((lane // SB_HD) == hh, x, 0.0)) for hh in range(2)], axis=0)


def _sb_fwd(proj, S, shards):
    T = proj.shape[0]
    nq = S // QT
    scale = SB_HD ** -0.5
    nb, ns = T // S, len(shards)

    def body(qkv_ref, *rest):
        sh_refs, (o_ref, tt_ref), g_refs = rest[:ns], rest[ns:ns + 2], rest[ns + 2:2 * ns + 2]
        sems = rest[2 * ns + 2:]
        first = jnp.logical_and(pl.program_id(0) == 0, pl.program_id(1) == 0)
        last = jnp.logical_and(pl.program_id(0) == nb - 1, pl.program_id(1) == 3)

        @pl.when(first)
        def _():
            _exchange_ops(sh_refs, g_refs, True, sems, "start")

        row, col, lane, kr, kc = _sb_masks()
        msuf = _bf(kr > kc)

        def qloop(qt, _):
            r0 = pl.multiple_of(qt * QT, QT)
            qs = _stack_heads(qkv_ref[pl.ds(r0, QT), 0:128] * scale, lane)

            def live(st):
                it, _, cy = st
                return jnp.logical_and(it <= qt, jnp.max(cy) > SB_DEAD)

            def step(st):
                it, acc, cy = st
                kt = qt - it
                k0 = pl.multiple_of(kt * TK, TK)
                kv = _bf(qkv_ref[pl.ds(k0, TK), 128:256])
                vv = _bf(qkv_ref[pl.ds(k0, TK), 256:384])
                strict = (col + (kt - qt) * TK) < row
                z = jnp.where(strict, _dot_nt(qs, kv), SB_MASKED)
                sp = _softplus(z)
                lg = -sp
                after = cy + _split_dot(lg, msuf, 2)
                w = jnp.exp((z - sp) + after)
                return it + 1, acc + _dot(_bf(w), vv), cy + jnp.sum(lg, axis=1, keepdims=True)

            it, acc, cy = lax.while_loop(
                live, step, (jnp.int32(0), jnp.zeros((2 * QT, LANE), F32), jnp.zeros((2 * QT, 1), F32)))
            o_ref[pl.ds(r0, QT), :] = jnp.where(lane < SB_HD, acc[:QT], acc[QT:])
            tt = jnp.where(lane < SB_HD, cy[:QT], cy[QT:])
            tt_ref[pl.ds(r0, QT), :] = jnp.where(lane == CNT_LANE, it.astype(F32), tt)
            return 0

        lax.fori_loop(0, nq, qloop, 0)

        @pl.when(last)
        def _():
            _exchange_ops(sh_refs, g_refs, True, sems, "wait")

    hbm = BS(memory_space=pltpu.HBM)
    col_spec = BS((S, LANE), lambda b, p: (b, p))
    return pl.pallas_call(
        body, name="sb_fwd",
        out_shape=(SDS((T, SB_W), F32), SDS((T, SB_W), F32)) + tuple(SDS((N_DEV,) + s.shape, s.dtype) for s in shards),
        grid=(nb, 4),
        in_specs=[BS((S, 384), lambda b, p: (b, p))] + [hbm] * ns,
        out_specs=(col_spec, col_spec) + (hbm,) * ns,
        scratch_shapes=_exchange_sems(ns),
        compiler_params=_cp(("arbitrary", "arbitrary")),
    )(proj, *shards)


def _log_sigmoid(x):
    return jnp.minimum(x, 0.0) - jnp.log1p(jnp.exp(-jnp.abs(x)))


def _gla_masks():
    r = lax.broadcasted_iota(jnp.int32, (GR, GR), 0)
    c = lax.broadcasted_iota(jnp.int32, (GR, GR), 1)
    same = (r // CHUNK) == (c // CHUNK)
    causal = jnp.logical_and(same, r >= c)
    lane = lax.broadcasted_iota(jnp.int32, (1, LANE), 1)
    return causal, _bf(causal), _bf(jnp.logical_and(same, r <= c)), lane


def _gla_group_terms(blk_ref, glr_ref, wgu, bgu, r0, tri_incl):
    q = blk_ref[pl.ds(r0, GR), 0:128]
    k = blk_ref[pl.ds(r0, GR), 128:256]
    v = blk_ref[pl.ds(r0, GR), 256:512]
    pre = _dot(_bf(glr_ref[pl.ds(r0, GR), :]), wgu) + bgu
    la = _log_sigmoid(pre) / GATE_NORM
    b = _split_dot(la, tri_incl, 3, left=False)
    b_last = _per_chunk(lambda rows: b[rows.stop - 1:rows.stop])
    eb = jnp.exp(b)
    qd = (q * (GLA_DK ** -0.5)) * eb
    ki = k * jnp.exp(-b)
    ke = k * jnp.exp(b_last - b)
    decay = jnp.exp(b_last)
    return v, pre, b, b_last, eb, qd, ki, ke, decay


def _chunk_rows(n):
    return slice(n * CHUNK, (n + 1) * CHUNK)


def _per_chunk(row_fn):
    return jnp.concatenate(
        [jnp.broadcast_to(row_fn(_chunk_rows(n)), (CHUNK, LANE)) for n in range(GLA_G)], axis=0)


def _gla_fwd(proj, wgu, bgu, S):
    T = proj.shape[0]
    ng = S // GR

    def body(blk_ref, glr_ref, wgu_ref, bgu_ref, o_ref):
        causal, tri_incl, _, lane = _gla_masks()
        wg = _bf(wgu_ref[...])
        bg = bgu_ref[...]

        def group(g, states):
            r0 = pl.multiple_of(g * GR, GR)
            v, _, _, _, _, qd, ki, ke, decay = _gla_group_terms(blk_ref, glr_ref, wg, bg, r0, tri_incl)
            kib, keb = _bf(ki), _bf(ke)
            new_states, outs = [], []
            for hh in range(2):
                hm = (lane // GLA_DK) == hh
                qm = _bf(jnp.where(hm, qd, 0.0))
                vh = _bf(v[:, 128 * hh:128 * hh + 128])
                attn = jnp.where(causal, _dot_nt(qm, kib), 0.0)
                o_intra = _dot(_bf(attn), vh)
                st = states[hh]
                parts = []
                for n in range(GLA_G):
                    rows = _chunk_rows(n)
                    parts.append(o_intra[rows] + _dot_nt(qm[rows], _bf(st)))
                    st = st * decay[n * CHUNK:n * CHUNK + 1] + _dot_tn(vh[rows], keb[rows])
                outs.append(jnp.concatenate(parts, axis=0))
                new_states.append(st)
            o_ref[pl.ds(r0, GR), :] = jnp.concatenate(outs, axis=1)
            return tuple(new_states)

        z = jnp.zeros((GLA_DV, LANE), F32)
        lax.fori_loop(0, ng, group, (z, z))

    return pl.pallas_call(
        body, name="gla_fwd", out_shape=SDS((T, GLA_W), F32), grid=(T // S, 2),
        in_specs=[BS((S, 512), lambda b, p: (b, 3 + p)), BS((S, LANE), lambda b, p: (b, GLR_BLK)),
                  BS((LANE, LANE), lambda b, p: (0, p)), BS((1, LANE), lambda b, p: (0, p))],
        out_specs=BS((S, 256), lambda b, p: (b, p)),
        compiler_params=_cp(("parallel", "parallel")),
    )(proj, proj, wgu, bgu)


def _head_blockdiag(width, hd):
    r = lax.broadcasted_iota(jnp.int32, (width, width), 0) // hd
    c = lax.broadcasted_iota(jnp.int32, (width, width), 1) // hd
    return _bf(r == c)


def _mix_out(o_sb, o_gla, proj, x, g_sb, g_gla, w_out, g2):
    T = x.shape[0]
    tm = _tile(T, 512)

    def body(osb_ref, ogl_ref, og_ref, x_ref, gsb_ref, ggl_ref, w_ref, g2_ref, x1_ref, oc_ref, h2_ref):
        bd64 = _head_blockdiag(SB_W, SB_HD)
        bd128 = _head_blockdiag(GLA_W, GLA_DV)
        o = osb_ref[...]
        r = lax.rsqrt(_split_dot(o * o, bd64, 2) * (1.0 / SB_HD) + EPS)
        c_sb = (o * r) * gsb_ref[...]
        o = ogl_ref[...]
        r = lax.rsqrt(_split_dot(o * o, bd128, 2) * (1.0 / GLA_DV) + EPS)
        og = og_ref[...]
        c_gl = ((o * r) * ggl_ref[...]) * (og * jax.nn.sigmoid(og))
        oc = _bf(jnp.concatenate([c_sb, c_gl], axis=1))
        oc_ref[...] = oc
        x1 = x_ref[...] + _dot(oc, w_ref[...])
        x1_ref[...] = x1
        r2 = lax.rsqrt(jnp.mean(x1 * x1, axis=-1, keepdims=True) + EPS)
        h2_ref[...] = _bf((x1 * r2) * g2_ref[...])

    row = lambda w: BS((tm, w), lambda i: (i, 0))
    vec = lambda w: BS((1, w), lambda i: (0, 0))
    return pl.pallas_call(
        body, name="mix_out", out_shape=(SDS((T, D), F32), SDS((T, D), BF16), SDS((T, D), BF16)), grid=(T // tm,),
        in_specs=[row(SB_W), row(GLA_W), BS((tm, 512), lambda i: (i, OG_BLK)), row(D), vec(SB_W), vec(GLA_W),
                  BS((D, D), lambda i: (0, 0)), vec(D)],
        out_specs=(row(D), row(D), row(D)),
        compiler_params=_cp(("parallel",)),
    )(o_sb, o_gla, proj, x, g_sb, g_gla, w_out, g2)


CONV_ROWS = 256
CONV_TC = 256


def _rows_before(ref, r0, first):
    prev = ref[pl.ds(pl.multiple_of(jnp.maximum(r0 - 8, 0), 8), 8), :]
    return jnp.where(first, 0.0, prev)


def _shift_down(cur, prev8, k):
    cat = jnp.concatenate([prev8, cur], axis=0)
    return pltpu.roll(cat, k, 0)[8:]


def _shift_up(cur, next8, k):
    cat = jnp.concatenate([cur, next8], axis=0)
    return pltpu.roll(cat, cat.shape[0] - k, 0)[:cur.shape[0]]


def _conv_at(h_ref, cw, cb, r0, rows, first):
    cur = h_ref[pl.ds(r0, rows), :]
    prev8 = _rows_before(h_ref, r0, first)
    u = cb + cw[0:1, :] * _shift_down(cur, prev8, 2)
    u = u + cw[1:2, :] * _shift_down(cur, prev8, 1)
    return u + cw[2:3, :] * cur


NJ = D_FF // CONV_TC


def _conv_gate(hup, cw, cb, S):
    T = hup.shape[0]
    rows = min(CONV_ROWS, S)
    nr = S // rows

    def body(ha_ref, hv_ref, cwa_ref, cwv_ref, cba_ref, cbv_ref, act_ref, ua_ref, uv_ref):
        cwa, cwv, cba, cbv = cwa_ref[...], cwv_ref[...], cba_ref[...], cbv_ref[...]

        def step(c, _):
            r0 = pl.multiple_of(c * rows, rows)
            ua = _conv_at(ha_ref, cwa, cba, r0, rows, c == 0)
            uv = _conv_at(hv_ref, cwv, cbv, r0, rows, c == 0)
            ua_ref[pl.ds(r0, rows), :] = _bf(ua)
            uv_ref[pl.ds(r0, rows), :] = _bf(uv)
            act_ref[pl.ds(r0, rows), :] = _bf((ua * jax.nn.sigmoid(ua)) * uv)
            return 0

        lax.fori_loop(0, nr, step, 0)

    blk = lambda o: BS((S, CONV_TC), lambda b, j: (b, j + o))
    w3 = lambda o: BS((3, CONV_TC), lambda b, j: (0, j + o))
    w1 = lambda o: BS((1, CONV_TC), lambda b, j: (0, j + o))
    return pl.pallas_call(
        body, name="conv_gate", out_shape=(SDS((T, D_FF), BF16), SDS((T, D_FF), BF16), SDS((T, D_FF), BF16)),
        grid=(T // S, NJ),
        in_specs=[blk(0), blk(NJ), w3(0), w3(NJ), w1(0), w1(NJ)], out_specs=(blk(0), blk(0), blk(0)),
        compiler_params=_cp(("parallel", "parallel")),
    )(hup, hup, cw, cw, cb, cb)


def _down_loss(act, w_down, x1, tgt, g3):
    T = x1.shape[0]
    tm = _tile(T, 512)

    def body(a_ref, w_ref, x1_ref, t_ref, g_ref, dx_ref, dg_ref, ls_ref):
        @pl.when(pl.program_id(0) == 0)
        def _():
            dg_ref[...] = jnp.zeros_like(dg_ref)
            ls_ref[...] = jnp.zeros_like(ls_ref)

        g = g_ref[...]
        x2 = x1_ref[...] + _dot(a_ref[...], w_ref[...])
        r = lax.rsqrt(jnp.mean(x2 * x2, axis=-1, keepdims=True) + EPS)
        xh = x2 * r
        e = xh * g - t_ref[...]
        ls_ref[...] += 0.5 * jnp.sum(jnp.mean(e * e, axis=-1, keepdims=True), axis=0, keepdims=True)
        dy = e * (1.0 / D)
        dxh = dy * g
        dx_ref[...] = r * (dxh - xh * jnp.mean(dxh * xh, axis=-1, keepdims=True))
        dg_ref[...] += jnp.sum(dy * xh, axis=0, keepdims=True)

    row = lambda w: BS((tm, w), lambda i: (i, 0))
    return pl.pallas_call(
        body, name="down_loss", out_shape=(SDS((T, D), F32), SDS((1, D), F32), SDS((1, LANE), F32)), grid=(T // tm,),
        in_specs=[row(D_FF), BS((D_FF, D), lambda i: (0, 0)), row(D), row(D), BS((1, D), lambda i: (0, 0))],
        out_specs=(row(D), BS((1, D), lambda i: (0, 0)), BS((1, LANE), lambda i: (0, 0))),
        compiler_params=_cp(("arbitrary",)),
    )(act, w_down, x1, tgt, g3)


def _conv_gate_bwd(hup, u_a, u_v, dact, cw, S):
    T = hup.shape[0]
    rows = min(CONV_ROWS, S)
    nr = S // rows

    def body(ha_ref, hv_ref, ua_ref, uv_ref, da_ref, cwa_ref, cwv_ref,
             dha_ref, dhv_ref, dcwa_ref, dcwv_ref, dcba_ref, dcbv_ref):
        @pl.when(pl.program_id(1) == 0)
        def _():
            for r in (dcwa_ref, dcwv_ref, dcba_ref, dcbv_ref):
                r[...] = jnp.zeros_like(r)

        cwa, cwv = cwa_ref[...], cwv_ref[...]

        def du_at(r0, n):
            ua = ua_ref[pl.ds(r0, n), :].astype(F32)
            uv = uv_ref[pl.ds(r0, n), :].astype(F32)
            da = da_ref[pl.ds(r0, n), :]
            sg = jax.nn.sigmoid(ua)
            dua = (da * uv) * (sg * (1.0 + ua * (1.0 - sg)))
            duv = da * (ua * sg)
            return dua, duv

        def step(c, _):
            r0 = pl.multiple_of(c * rows, rows)
            last = c == nr - 1
            dua, duv = du_at(r0, rows)
            n0 = pl.multiple_of(jnp.minimum(r0 + rows, S - 16), 16)
            nua, nuv = du_at(n0, 16)
            nua = jnp.where(last, 0.0, nua[:8])
            nuv = jnp.where(last, 0.0, nuv[:8])
            for (h_ref, cw, du, nu, dh_ref, dcw_ref, dcb_ref) in (
                    (ha_ref, cwa, dua, nua, dha_ref, dcwa_ref, dcba_ref),
                    (hv_ref, cwv, duv, nuv, dhv_ref, dcwv_ref, dcbv_ref)):
                up1, up2 = _shift_up(du, nu, 1), _shift_up(du, nu, 2)
                dh_ref[pl.ds(r0, rows), :] = _bf(cw[2:3, :] * du + cw[1:2, :] * up1 + cw[0:1, :] * up2)
                cur = h_ref[pl.ds(r0, rows), :]
                dcw_ref[0:1, :] += jnp.sum(up2 * cur, axis=0, keepdims=True)
                dcw_ref[1:2, :] += jnp.sum(up1 * cur, axis=0, keepdims=True)
                dcw_ref[2:3, :] += jnp.sum(du * cur, axis=0, keepdims=True)
                dcb_ref[...] += jnp.sum(du, axis=0, keepdims=True)
            return 0

        lax.fori_loop(0, nr, step, 0)

    blk = lambda o: BS((S, CONV_TC), lambda j, b: (b, j + o))
    w3 = lambda o: BS((3, CONV_TC), lambda j, b: (0, j + o))
    w1 = BS((1, CONV_TC), lambda j, b: (0, j))
    return pl.pallas_call(
        body, name="conv_gate_bwd",
        out_shape=(SDS((T, D_FF), BF16), SDS((T, D_FF), BF16), SDS((3, D_FF), F32), SDS((3, D_FF), F32),
                   SDS((1, D_FF), F32), SDS((1, D_FF), F32)),
        grid=(NJ, T // S),
        in_specs=[blk(0), blk(NJ), blk(0), blk(0), blk(0), w3(0), w3(NJ)],
        out_specs=(blk(0), blk(0), w3(0), w3(0), w1, w1),
        compiler_params=_cp(("parallel", "arbitrary")),
    )(hup, hup, u_a, u_v, dact, cw, cw)


def _mix_bwd(docat, o_sb, o_gla, proj, g_sb, g_gla):
    T = docat.shape[0]
    tm = _tile(T, 512)

    def body(d_ref, osb_ref, ogl_ref, og_ref, gsb_ref, ggl_ref, dsb_ref, dgl_ref, dog_ref, dgsb_ref, dggl_ref):
        @pl.when(pl.program_id(0) == 0)
        def _():
            dgsb_ref[...] = jnp.zeros_like(dgsb_ref)
            dggl_ref[...] = jnp.zeros_like(dggl_ref)

        bd64 = _head_blockdiag(SB_W, SB_HD)
        bd128 = _head_blockdiag(GLA_W, GLA_DV)
        d = d_ref[:, 0:SB_W]
        o = osb_ref[...]
        r = lax.rsqrt(_split_dot(o * o, bd64, 2) * (1.0 / SB_HD) + EPS)
        n = o * r
        dn = d * gsb_ref[...]
        dgsb_ref[...] += jnp.sum(d * n, axis=0, keepdims=True)
        dsb_ref[...] = r * (dn - n * (_split_dot(dn * n, bd64, 2) * (1.0 / SB_HD)))

        d = d_ref[:, SB_W:D]
        o = ogl_ref[...]
        r = lax.rsqrt(_split_dot(o * o, bd128, 2) * (1.0 / GLA_DV) + EPS)
        n = o * r
        og = og_ref[...]
        sg = jax.nn.sigmoid(og)
        dm = d * (og * sg)
        dog_ref[...] = _bf((d * (n * ggl_ref[...])) * (sg * (1.0 + og * (1.0 - sg))))
        dn = dm * ggl_ref[...]
        dggl_ref[...] += jnp.sum(dm * n, axis=0, keepdims=True)
        dgl_ref[...] = r * (dn - n * (_split_dot(dn * n, bd128, 2) * (1.0 / GLA_DV)))

    row = lambda w: BS((tm, w), lambda i: (i, 0))
    vec = lambda w: BS((1, w), lambda i: (0, 0))
    ogb = BS((tm, 512), lambda i: (i, OG_BLK))
    return pl.pallas_call(
        body, name="mix_bwd",
        out_shape=(SDS((T, SB_W), F32), SDS((T, GLA_W), F32), SDS((T, PROJ_W), BF16), SDS((1, SB_W), F32),
                   SDS((1, GLA_W), F32)),
        grid=(T // tm,),
        in_specs=[row(D), row(SB_W), row(GLA_W), ogb, vec(SB_W), vec(GLA_W)],
        out_specs=(row(SB_W), row(GLA_W), ogb, vec(SB_W), vec(GLA_W)),
        compiler_params=_cp(("arbitrary",)),
    )(docat, o_sb, o_gla, proj, g_sb, g_gla)


def _sb_bwd(proj, tt, do, dproj, S, pieces):
    T = proj.shape[0]
    nq = S // QT
    scale = SB_HD ** -0.5
    nb, ns = T // S, len(pieces)

    def body(qkv_ref, tt_ref, do_ref, dp_in_ref, *rest):
        del dp_in_ref
        pc_refs, dp_ref, got_refs = rest[:ns], rest[ns], rest[ns + 1:2 * ns + 1]
        dk_acc, dv_acc = rest[2 * ns + 1:2 * ns + 3]
        sems = rest[2 * ns + 3:]
        first = jnp.logical_and(pl.program_id(0) == 0, pl.program_id(1) == 0)
        last = jnp.logical_and(pl.program_id(0) == nb - 1, pl.program_id(1) == 3)

        @pl.when(first)
        def _():
            _exchange_ops(pc_refs, got_refs, False, sems, "start")

        row, col, lane, kr, kc = _sb_masks()
        mincl = _bf(kr <= kc)
        mexcl = _bf(kr < kc)
        dk_acc[...] = jnp.zeros_like(dk_acc)
        dv_acc[...] = jnp.zeros_like(dv_acc)

        def qloop(qt, _):
            r0 = pl.multiple_of(qt * QT, QT)
            qs = _stack_heads(qkv_ref[pl.ds(r0, QT), 0:128] * scale, lane)
            dos = _stack_heads(do_ref[pl.ds(r0, QT), :], lane)
            ttv = tt_ref[pl.ds(r0, QT), :]
            tot = jnp.concatenate([ttv[:, 0:1], ttv[:, SB_HD:SB_HD + 1]], axis=0)
            walked = jnp.max(ttv[:, CNT_LANE:CNT_LANE + 1]).astype(jnp.int32)

            def step(kt, st):
                dq, lc, pc = st
                k0 = pl.multiple_of(kt * TK, TK)
                kv = _bf(qkv_ref[pl.ds(k0, TK), 128:256])
                vv = _bf(qkv_ref[pl.ds(k0, TK), 256:384])
                strict = (col + (kt - qt) * TK) < row
                z = jnp.where(strict, _dot_nt(qs, kv), SB_MASKED)
                sp = _softplus(z)
                lg = -sp
                after = tot - (lc + _split_dot(lg, mincl, 2))
                gl = z - sp
                w = jnp.exp(gl + after)
                du = w * _dot_nt(dos, vv)
                beta = jnp.exp(gl)
                pex = pc + _split_dot(du, mexcl, 2)
                dz = _bf(du - beta * (du + pex))
                dk_acc[pl.ds(k0, TK), :] += _dot_tn(dz, qs)
                dv_acc[pl.ds(k0, TK), :] += _dot_tn(_bf(w), dos)
                return (dq + _dot(dz, kv), lc + jnp.sum(lg, axis=1, keepdims=True),
                        pc + jnp.sum(du, axis=1, keepdims=True))

            zc = jnp.zeros((2 * QT, 1), F32)
            dq, _, _ = lax.fori_loop(qt - walked + 1, qt + 1, step, (jnp.zeros((2 * QT, LANE), F32), zc, zc))
            dp_ref[pl.ds(r0, QT), 0:128] = _bf(jnp.where(lane < SB_HD, dq[:QT], dq[QT:]) * scale)
            return 0

        lax.fori_loop(0, nq, qloop, 0)
        dp_ref[:, 128:256] = _bf(dk_acc[...])
        dp_ref[:, 256:384] = _bf(dv_acc[...])

        @pl.when(last)
        def _():
            _exchange_ops(pc_refs, got_refs, False, sems, "wait")

    blk = BS((S, 384), lambda b, p: (b, p))
    col_spec = BS((S, LANE), lambda b, p: (b, p))
    hbm = BS(memory_space=pltpu.HBM)
    return pl.pallas_call(
        body, name="sb_bwd", out_shape=(SDS((T, PROJ_W), BF16),) + tuple(SDS(s.shape, s.dtype) for s in pieces),
        grid=(nb, 4),
        in_specs=[blk, col_spec, col_spec, BS(memory_space=pl.ANY)] + [hbm] * ns, out_specs=(blk,) + (hbm,) * ns,
        scratch_shapes=[pltpu.VMEM((S, LANE), F32), pltpu.VMEM((S, LANE), F32)] + _exchange_sems(ns),
        input_output_aliases={3: 0},
        compiler_params=_cp(("arbitrary", "arbitrary")),
    )(proj, tt, do, dproj, *pieces)


def _gla_bwd(proj, wgu, bgu, do, dproj, S):
    T = proj.shape[0]
    nc, ng = S // CHUNK, S // GR

    def body(blk_ref, glr_ref, wgu_ref, bgu_ref, do_ref, dp_in_ref, dp_ref, dpre_ref, st_ref):
        del dp_in_ref
        causal, tri_incl, tri_rev, lane = _gla_masks()
        wg = _bf(wgu_ref[...])
        bg = bgu_ref[...]

        def fwd_group(g, states):
            r0 = pl.multiple_of(g * GR, GR)
            v, _, _, _, _, _, _, ke, decay = _gla_group_terms(blk_ref, glr_ref, wg, bg, r0, tri_incl)
            keb = _bf(ke)
            new_states = []
            for hh in range(2):
                vh = _bf(v[:, 128 * hh:128 * hh + 128])
                st = states[hh]
                for n in range(GLA_G):
                    rows = _chunk_rows(n)
                    st_ref[hh, g * GLA_G + n] = st
                    st = st * decay[n * CHUNK:n * CHUNK + 1] + _dot_tn(vh[rows], keb[rows])
                new_states.append(st)
            return tuple(new_states)

        z = jnp.zeros((GLA_DV, LANE), F32)
        lax.fori_loop(0, ng, fwd_group, (z, z))

        def bwd_group(it, dstates):
            g = ng - 1 - it
            r0 = pl.multiple_of(g * GR, GR)
            v, pre, b, b_last, eb, qd, ki, ke, decay = _gla_group_terms(
                blk_ref, glr_ref, wg, bg, r0, tri_incl)
            kib = _bf(ki)
            dqd = jnp.zeros((GR, LANE), F32)
            dki = jnp.zeros((GR, LANE), F32)
            dke = jnp.zeros((GR, LANE), F32)
            ddec = jnp.zeros((GR, LANE), F32)
            new_dstates, dvs = [], []
            for hh in range(2):
                hm = (lane // GLA_DK) == hh
                qm = _bf(jnp.where(hm, qd, 0.0))
                kem = _bf(jnp.where(hm, ke, 0.0))
                vh = _bf(v[:, 128 * hh:128 * hh + 128])
                doh = _bf(do_ref[pl.ds(r0, GR), 128 * hh:128 * hh + 128])
                attn = _bf(jnp.where(causal, _dot_nt(qm, kib), 0.0))
                dattn = _bf(jnp.where(causal, _dot_nt(doh, vh), 0.0))
                dv_intra = _dot_tn(attn, doh)
                dqd_intra = _dot(dattn, kib)
                dki = dki + _dot_tn(dattn, qm)
                dst = dstates[hh]
                dv_p, dqd_p, dke_p, ddec_p = [None] * GLA_G, [None] * GLA_G, [None] * GLA_G, [None] * GLA_G
                for n in reversed(range(GLA_G)):
                    rows = _chunk_rows(n)
                    st = st_ref[hh, g * GLA_G + n]
                    dv_p[n] = dv_intra[rows] + _dot_nt(kem[rows], _bf(dst))
                    dqd_p[n] = dqd_intra[rows] + _dot(doh[rows], _bf(st))
                    dke_p[n] = _dot(vh[rows], _bf(dst))
                    ddec_p[n] = jnp.broadcast_to(jnp.sum(dst * st, axis=0, keepdims=True), (CHUNK, LANE))
                    dst = dst * decay[n * CHUNK:n * CHUNK + 1] + _dot_tn(doh[rows], qm[rows])
                dvs.append(jnp.concatenate(dv_p, axis=0))
                dqd = dqd + jnp.where(hm, jnp.concatenate(dqd_p, axis=0), 0.0)
                dke = dke + jnp.where(hm, jnp.concatenate(dke_p, axis=0), 0.0)
                ddec = ddec + jnp.where(hm, jnp.concatenate(ddec_p, axis=0), 0.0)
                new_dstates.append(dst)
            einv = jnp.exp(-b)
            eend = jnp.exp(b_last - b)
            dq = (dqd * eb) * (GLA_DK ** -0.5)
            dk = dki * einv + dke * eend
            db = dqd * qd - dki * ki - dke * ke
            dkk = dke * ke
            db_last = _per_chunk(lambda rows: jnp.sum(dkk[rows], axis=0, keepdims=True)) + ddec * decay
            dla = _split_dot(db, tri_rev, 2, left=False) + db_last
            dpre_ref[pl.ds(r0, GR), :] = (dla * (1.0 / GATE_NORM)) * (1.0 - jax.nn.sigmoid(pre))
            dp_ref[pl.ds(r0, GR), 0:128] = _bf(dq)
            dp_ref[pl.ds(r0, GR), 128:256] = _bf(dk)
            dp_ref[pl.ds(r0, GR), 256:512] = _bf(jnp.concatenate(dvs, axis=1))
            return tuple(new_dstates)

        lax.fori_loop(0, ng, bwd_group, (z, z))

    return pl.pallas_call(
        body, name="gla_bwd", out_shape=(SDS((T, PROJ_W), BF16), SDS((T, GLA_KW), F32)), grid=(T // S, 2),
        in_specs=[BS((S, 512), lambda b, p: (b, 3 + p)), BS((S, LANE), lambda b, p: (b, GLR_BLK)),
                  BS((LANE, LANE), lambda b, p: (0, p)), BS((1, LANE), lambda b, p: (0, p)),
                  BS((S, 256), lambda b, p: (b, p)), BS(memory_space=pl.ANY)],
        out_specs=(BS((S, 512), lambda b, p: (b, 3 + p)), BS((S, LANE), lambda b, p: (b, p))),
        scratch_shapes=[pltpu.VMEM((2, nc, GLA_DV, LANE), F32)],
        input_output_aliases={5: 0},
        compiler_params=_cp(("parallel", "parallel")),
    )(proj, proj, wgu, bgu, do, dproj)


def _gate_bwd(dpre, proj, wgu, dproj):
    T = dpre.shape[0]
    tm = _tile(T, 512)

    def body(dpre_ref, glr_ref, wgu_ref, dp_in_ref, dp_ref, dw_ref, db_ref):
        del dp_in_ref

        @pl.when(pl.program_id(0) == 0)
        def _():
            dw_ref[...] = jnp.zeros_like(dw_ref)
            db_ref[...] = jnp.zeros_like(db_ref)

        dpre = dpre_ref[...]
        dp_ref[...] = _bf(_dot_nt(_bf(dpre), _bf(wgu_ref[...])))
        dw_ref[...] += _dot_tn(_bf(glr_ref[...]), _bf(dpre))
        db_ref[...] += jnp.sum(dpre, axis=0, keepdims=True)

    glr = BS((tm, LANE), lambda i: (i, GLR_BLK))
    return pl.pallas_call(
        body, name="gate_bwd",
        out_shape=(SDS((T, PROJ_W), BF16), SDS((LANE, GLA_KW), F32), SDS((1, GLA_KW), F32)), grid=(T // tm,),
        in_specs=[BS((tm, GLA_KW), lambda i: (i, 0)), glr, BS((LANE, GLA_KW), lambda i: (0, 0)),
                  BS(memory_space=pl.ANY)],
        out_specs=(glr, BS((LANE, GLA_KW), lambda i: (0, 0)), BS((1, GLA_KW), lambda i: (0, 0))),
        input_output_aliases={3: 0},
        compiler_params=_cp(("arbitrary",)),
    )(dpre, proj, wgu, dproj)


def _exchange_sems(n):
    return [pltpu.SemaphoreType.DMA((n * (N_DEV - 1),)), pltpu.SemaphoreType.DMA((n * (N_DEV - 1),)),
            pltpu.SemaphoreType.DMA((n,))]


def _exchange_ops(srcs, outs, gather, sems, act):
    ssem, rsem, lsem = sems
    x, y, c = lax.axis_index("x"), lax.axis_index("y"), lax.axis_index("c")
    me = 4 * x + 2 * y + c
    for i, (s_ref, o_ref) in enumerate(zip(srcs, outs)):
        for k in range(1, N_DEV):
            px = (x + ((k >> 2) & 1)) % 2
            py = (y + ((k >> 1) & 1)) % 2
            pc = (c + (k & 1)) % 2
            peer = 4 * px + 2 * py + pc
            n = i * (N_DEV - 1) + k - 1
            out = pltpu.make_async_remote_copy(
                src_ref=s_ref if gather else s_ref.at[peer], dst_ref=o_ref.at[me],
                send_sem=ssem.at[n], recv_sem=rsem.at[n],
                device_id=(px, py, pc), device_id_type=pl.DeviceIdType.MESH)
            if act == "start":
                out.start()
            else:
                out.wait_send()
                pltpu.make_async_remote_copy(
                    src_ref=s_ref if gather else s_ref.at[me], dst_ref=o_ref.at[peer],
                    send_sem=ssem.at[n], recv_sem=rsem.at[n],
                    device_id=(x, y, c), device_id_type=pl.DeviceIdType.MESH).wait_recv()
        mine = pltpu.make_async_copy(s_ref if gather else s_ref.at[me], o_ref.at[me], lsem.at[i])
        if act == "start":
            mine.start()
        else:
            mine.wait()


def _gather_two_level(src, name):
    def body(s_ref, o_ref, ssem, rsem, lsem):
        x, y, c = lax.axis_index("x"), lax.axis_index("y"), lax.axis_index("c")
        me, sibling = (x, y, c), (x, y, 1 - c)
        chips = [(1 - x, y), (x, 1 - y), (1 - x, 1 - y)]

        def slab(px, py, pc):
            return o_ref.at[4 * px + 2 * py + pc]

        def copy(k, block, to, src_ref=None):
            return pltpu.make_async_remote_copy(
                src_ref=slab(*block) if src_ref is None else src_ref, dst_ref=slab(*block),
                send_sem=ssem.at[k], recv_sem=rsem.at[k], device_id=to, device_id_type=pl.DeviceIdType.MESH)

        mine = pltpu.make_async_copy(s_ref, slab(*me), lsem)
        mine.start()
        first = [copy(0, me, sibling, s_ref)] + [copy(1 + j, me, (*chip, c), s_ref) for j, chip in enumerate(chips)]
        for cp in first:
            cp.start()
        passed = [copy(4 + j, (*chip, c), sibling) for j, chip in enumerate(chips)]
        for j, chip in enumerate(chips):
            copy(1 + j, (*chip, c), me).wait_recv()
            passed[j].start()
        copy(0, sibling, me).wait_recv()
        for j, chip in enumerate(chips):
            copy(4 + j, (*chip, 1 - c), me).wait_recv()
        for cp in first + passed:
            cp.wait_send()
        mine.wait()

    hbm = BS(memory_space=pltpu.HBM)
    return pl.pallas_call(
        body, name=name, out_shape=SDS((N_DEV,) + src.shape, src.dtype), in_specs=[hbm], out_specs=hbm,
        scratch_shapes=[pltpu.SemaphoreType.DMA((N_DEV - 1,)), pltpu.SemaphoreType.DMA((N_DEV - 1,)),
                        pltpu.SemaphoreType.DMA(())],
    )(src)


def _exchange(srcs, gather, name):
    n = len(srcs)
    shapes = [SDS((N_DEV,) + s.shape if gather else s.shape, s.dtype) for s in srcs]

    def body(*refs):
        s_refs, o_refs, sems = refs[:n], refs[n:2 * n], refs[2 * n:]
        _exchange_ops(s_refs, o_refs, gather, sems, "start")
        _exchange_ops(s_refs, o_refs, gather, sems, "wait")

    hbm = BS(memory_space=pltpu.HBM)
    return pl.pallas_call(
        body, name=name, out_shape=tuple(shapes), in_specs=[hbm] * n, out_specs=(hbm,) * n,
        scratch_shapes=_exchange_sems(n),
    )(*srcs)


def _adamw_math(w, g, m, v):
    m = ADAM_B1 * m + (1.0 - ADAM_B1) * g
    v = ADAM_B2 * v + (1.0 - ADAM_B2) * (g * g)
    m_hat = m / (1.0 - ADAM_B1 ** ADAM_STEP)
    v_hat = v / (1.0 - ADAM_B2 ** ADAM_STEP)
    delta = -ADAM_LR * (m_hat / (jnp.sqrt(v_hat) + ADAM_EPS) + ADAM_WD * w)
    return delta, m, v


def _sum_adamw(parts, w, m, v, tr, name):
    R, C = w.shape

    def body(p_ref, w_ref, m_ref, v_ref, g_ref, d_ref, nm_ref, nv_ref):
        g = p_ref[0].astype(F32)
        for d in range(1, N_DEV):
            g = g + p_ref[d].astype(F32)
        delta, nm, nv = _adamw_math(w_ref[...], g, m_ref[...], v_ref[...])
        g_ref[...] = g
        d_ref[...] = delta
        nm_ref[...] = nm
        nv_ref[...] = nv

    blk = BS((tr, C), lambda i: (i, 0))
    out = SDS((R, C), F32)
    return pl.pallas_call(
        body, name=name, out_shape=(out, out, out, out), grid=(R // tr,),
        in_specs=[BS((N_DEV, tr, C), lambda i: (0, i, 0)), blk, blk, blk], out_specs=(blk, blk, blk, blk),
        compiler_params=_cp(("parallel",)),
    )(parts, w, m, v)


def _flat_pad_rows(parts, rows):
    flat = jnp.concatenate([p.reshape(-1) for p in parts])
    return jnp.pad(flat, (0, rows * D - flat.shape[0])).reshape(rows, D)


SMALL_ROWS = 16
SHARD_SMALL_ROWS = 3


def kernel(x, attn_norm_g, w_in, w_gate_up, b_gate_up, sb_out_g, gla_out_g, w_out, ffn_norm_g, w_ffn_up, conv_w, conv_b, w_ffn_down, final_norm_g, loss_target, m_attn_norm_g, m_w_in, m_w_gate_up, m_b_gate_up, m_sb_out_g, m_gla_out_g, m_w_out, m_ffn_norm_g, m_w_ffn_up, m_conv_w, m_conv_b, m_w_ffn_down, m_final_norm_g, v_attn_norm_g, v_w_in, v_w_gate_up, v_b_gate_up, v_sb_out_g, v_gla_out_g, v_w_out, v_ffn_norm_g, v_w_ffn_up, v_conv_w, v_conv_b, v_w_ffn_down, v_final_norm_g):
    Bd, S, _ = x.shape
    T = Bd * S
    x2d = x.reshape(T, D)
    tgt = loss_target.reshape(T, D)
    c_up = w_ffn_up.shape[2]
    c_gu = w_gate_up.shape[2]
    c_in = w_in.shape[2]

    n_gu = GATE_RANK * c_gu
    rows_bf = lambda w: w[0].T.astype(BF16)
    small_w = lambda wgu, cw: _flat_pad_rows([wgu, cw], SHARD_SMALL_ROWS)

    g_in = _gather_two_level(rows_bf(w_in), "gather_w_in")
    w_in_pt = jnp.pad(g_in.reshape(IN_COLS, D), ((0, 1), (0, 0)))[_PERM]
    g3 = final_norm_g.reshape(1, D)

    proj, h1 = _norm_proj(x2d, attn_norm_g, w_in_pt)
    o_sb, tt, g_up, g_down, g_out, gs = _sb_fwd(
        proj, S, [rows_bf(w_ffn_up), w_ffn_down[0].astype(BF16), w_out[0].astype(BF16),
                  _flat_pad_rows([w_gate_up, conv_w], 8)])
    w_out_f = g_out.reshape(D, D)
    gsf = gs.reshape(N_DEV, -1)
    wgu_f = jnp.transpose(gsf[:, :n_gu].reshape(N_DEV, GATE_RANK, c_gu), (1, 0, 2)).reshape(GATE_RANK, GLA_KW)
    cw_f = jnp.transpose(gsf[:, n_gu:n_gu + 3 * c_up].reshape(N_DEV, 3, c_up), (1, 0, 2)).reshape(3, 2 * D_FF)
    wgu_p = jnp.pad(wgu_f, ((0, LANE - GATE_RANK), (0, 0)))
    w_up_t = g_up.reshape(2 * D_FF, D)
    w_down_f = g_down.reshape(D_FF, D)
    o_gla = _gla_fwd(proj, wgu_p, b_gate_up, S)
    x1, ocat, h2 = _mix_out(o_sb, o_gla, proj, x2d, sb_out_g, gla_out_g, w_out_f, ffn_norm_g)
    hup = _mm(h2, w_up_t, "nt", "ffn_up", tm=1024, tn=1408, tk=1024)
    act, u_a, u_v = _conv_gate(hup, cw_f, conv_b, S)
    dx2, dg3, loss_dev = _down_loss(act, w_down_f, x1, tgt, g3)

    dw_down = _mm(act, dx2, "tn", "dw_down", out_dtype=BF16, tm=D_FF, tn=1024, tk=512)
    dact = _mm(dx2, w_down_f, "nt", "dact", tm=1024, tn=1408, tk=1024)
    dhup_a, dhup_v, dcw_a, dcw_v, dcb_a, dcb_v = _conv_gate_bwd(hup, u_a, u_v, dact, cw_f, S)
    dw_up_t = _mm(dhup_a, h2, "tn", "dw_up_a", out_dtype=BF16, tm=D_FF, tn=1024, tk=512, out_rows=2 * D_FF)
    dw_up_t = _mm(dhup_v, h2, "tn", "dw_up_v", out_dtype=BF16, tm=D_FF, tn=1024, tk=512, out_rows=2 * D_FF,
                  out_row0=D_FF, into=dw_up_t)
    dx1, dg2 = _mm(dhup_a, w_up_t, "nn", "dh2", a2=dhup_v, tm=512, tn=1024, tk=1408,
                   norm_bwd=(x1, ffn_norm_g, dx2))

    dw_out = _mm(ocat, dx1, "tn", "dw_out", out_dtype=BF16, tm=1024, tn=1024, tk=512)
    docat = _mm(dx1, w_out_f, "nt", "docat", tm=1024, tn=1024, tk=1024)
    do_sb, do_gla, dproj, dg_sb, dg_gla = _mix_bwd(docat, o_sb, o_gla, proj, sb_out_g, gla_out_g)
    dproj, got_up, got_down, got_out = _sb_bwd(
        proj, tt, do_sb, dproj, S,
        [dw_up_t.reshape(N_DEV, c_up, D), dw_down.reshape(N_DEV, -1, D), dw_out.reshape(N_DEV, -1, D)])
    dproj, dpre = _gla_bwd(proj, wgu_p, b_gate_up, do_gla, dproj, S)
    dproj, dwgu, dbgu = _gate_bwd(dpre, proj, wgu_p, dproj)
    dw_in_pt = _mm(dproj, h1, "tn", "dw_in", out_dtype=BF16, tm=PROJ_W, tn=1024, tk=512)
    dx, dg1, got_in = _mm(dproj, w_in_pt, "nn", "dh1", tm=1024, tn=1024, tk=640,
                          xchg=([dw_in_pt[_INV_PERM].reshape(N_DEV, c_in, D)], False),
                          norm_bwd=(x2d, attn_norm_g, dx1))

    dcw = jnp.concatenate([dcw_a, dcw_v], axis=1)
    dwgu_pc = jnp.transpose(dwgu[:GATE_RANK].reshape(GATE_RANK, N_DEV, c_gu), (1, 0, 2)).reshape(N_DEV, -1)
    dcw_pc = jnp.transpose(dcw.reshape(3, N_DEV, c_up), (1, 0, 2)).reshape(N_DEV, -1)
    small_pc = jnp.concatenate([dwgu_pc, dcw_pc], axis=1)
    small_pc = jnp.pad(small_pc, ((0, 0), (0, SHARD_SMALL_ROWS * D - small_pc.shape[1])))
    small_pc = small_pc.reshape(N_DEV, SHARD_SMALL_ROWS, D).astype(BF16)
    rep_names = ["attn_norm_g", "b_gate_up", "sb_out_g", "gla_out_g", "ffn_norm_g", "conv_b", "final_norm_g"]
    rep_g = [dg1, dbgu, dg_sb, dg_gla, dg2, jnp.concatenate([dcb_a, dcb_v], axis=1), dg3]
    rep_w = [attn_norm_g, b_gate_up, sb_out_g, gla_out_g, ffn_norm_g, conv_b, final_norm_g]
    rep_m = [m_attn_norm_g, m_b_gate_up, m_sb_out_g, m_gla_out_g, m_ffn_norm_g, m_conv_b, m_final_norm_g]
    rep_v = [v_attn_norm_g, v_b_gate_up, v_sb_out_g, v_gla_out_g, v_ffn_norm_g, v_conv_b, v_final_norm_g]
    rep_pc = jnp.broadcast_to(_flat_pad_rows(rep_g + [loss_dev[:, 0:1]], SMALL_ROWS), (N_DEV, SMALL_ROWS, D))
    got_sm, got_rep = _exchange([small_pc, rep_pc], False, "scatter_tail")

    rows = lambda w: w[0].T
    cols = lambda r: r.T[None]
    res = {}
    res["w_in"] = [cols(r) for r in _sum_adamw(got_in, rows(w_in), rows(m_w_in), rows(v_w_in), c_in, "adamw_w_in")]
    res["w_out"] = [r[None] for r in _sum_adamw(got_out, w_out[0], m_w_out[0], v_w_out[0], w_out.shape[1],
                                                 "adamw_w_out")]
    res["w_ffn_up"] = [cols(r) for r in _sum_adamw(got_up, rows(w_ffn_up), rows(m_w_ffn_up), rows(v_w_ffn_up),
                                                    c_up // 2, "adamw_w_up")]
    res["w_ffn_down"] = [r[None] for r in _sum_adamw(got_down, w_ffn_down[0], m_w_ffn_down[0], v_w_ffn_down[0],
                                                      w_ffn_down.shape[1], "adamw_w_down")]
    sm = _sum_adamw(got_sm, small_w(w_gate_up, conv_w), small_w(m_w_gate_up, m_conv_w),
                    small_w(v_w_gate_up, v_conv_w), SHARD_SMALL_ROWS, "adamw_small_sharded")
    res["w_gate_up"] = [r.reshape(-1)[:n_gu].reshape(1, GATE_RANK, c_gu) for r in sm]
    res["conv_w"] = [r.reshape(-1)[n_gu:n_gu + 3 * c_up].reshape(1, 3, c_up) for r in sm]
    rep = _sum_adamw(got_rep, _flat_pad_rows(rep_w, SMALL_ROWS), _flat_pad_rows(rep_m, SMALL_ROWS),
                     _flat_pad_rows(rep_v, SMALL_ROWS), SMALL_ROWS, "adamw_replicated")
    o = 0
    for n, w in zip(rep_names, rep_w):
        res[n] = [r.reshape(-1)[o:o + w.size].reshape(w.shape) for r in rep]
        o += w.size

    loss = rep[0].reshape(-1)[o]
    order = ["attn_norm_g", "w_in", "w_gate_up", "b_gate_up", "sb_out_g", "gla_out_g", "w_out", "ffn_norm_g",
             "w_ffn_up", "conv_w", "conv_b", "w_ffn_down", "final_norm_g"]
    outs = [loss, dx.reshape(Bd, S, D)]
    for k in range(4):
        outs += [res[n][k] for n in order]
    return tuple(outs)
```

```python
import numpy as np
import jax
import jax.numpy as jnp
from jax import lax
from jax.experimental import pallas as pl
from jax.experimental.pallas import tpu as pltpu

F32 = jnp.float32
BF16 = jnp.bfloat16
SDS = jax.ShapeDtypeStruct
BS = pl.BlockSpec

N_DEV = 8
D = 1024
EPS = 1e-6
SB_HD = 64
SB_W = 512
GLA_DK = 64
GLA_DV = 128
GLA_KW = 256
GLA_W = 512
GATE_RANK = 16
GATE_NORM = 16.0
CHUNK = 64
GLA_G = 4
GR = GLA_G * CHUNK
QT = 256
D_FF = 2816
IN_COLS = 3088
PROJ_W = 3200
LANE = 128
VMEM_LIMIT = 56 * 1024 * 1024

ADAM_LR, ADAM_B1, ADAM_B2, ADAM_EPS, ADAM_WD, ADAM_STEP = 0.001, 0.9, 0.999, 1e-08, 0.01, 10


def _proj_perm():
    sbq, sbk, sbv = 0, 512, 1024
    gq, gk, gv, glr, gog = 1536, 1792, 2048, 2560, 2576
    cols = []
    for p in range(4):
        for base in (sbq, sbk, sbv):
            cols += list(range(base + 128 * p, base + 128 * p + 128))
    for p in range(2):
        cols += list(range(gq + 128 * p, gq + 128 * p + 128))
        cols += list(range(gk + 128 * p, gk + 128 * p + 128))
        cols += list(range(gv + 256 * p, gv + 256 * p + 256))
    cols += list(range(gog, gog + 512))
    cols += list(range(glr, glr + GATE_RANK)) + [IN_COLS] * (LANE - GATE_RANK)
    perm = np.asarray(cols, np.int32)
    inv = np.zeros((IN_COLS,), np.int32)
    for new, old in enumerate(cols):
        if old < IN_COLS:
            inv[old] = new
    return perm, inv


_PERM, _INV_PERM = _proj_perm()
OG_BLK = 5
GLR_BLK = 24


def _cp(sem=None, vmem=VMEM_LIMIT):
    return pltpu.CompilerParams(dimension_semantics=sem, vmem_limit_bytes=vmem)


def _dot(a, b):
    return lax.dot_general(a, b, (((1,), (0,)), ((), ())), preferred_element_type=F32)


def _dot_nt(a, b):
    return lax.dot_general(a, b, (((1,), (1,)), ((), ())), preferred_element_type=F32)


def _dot_tn(a, b):
    return lax.dot_general(a, b, (((0,), (0,)), ((), ())), preferred_element_type=F32)


def _bf(x):
    return x.astype(BF16)


def _split_dot(x, m, passes, left=True):
    acc = None
    r = x
    for i in range(passes):
        h = r.astype(BF16)
        t = _dot(h, m) if left else _dot(m, h)
        acc = t if acc is None else acc + t
        if i + 1 < passes:
            r = r - h.astype(F32)
    return acc


def _softplus(z):
    return jnp.maximum(z, 0.0) + jnp.log(1.0 + jnp.exp(-jnp.abs(z)))


def _rms_bwd_math(x, g, dh, dres):
    r = lax.rsqrt(jnp.mean(x * x, axis=-1, keepdims=True) + EPS)
    xh = x * r
    dxh = dh * g
    dx = dres + r * (dxh - xh * jnp.mean(dxh * xh, axis=-1, keepdims=True))
    return dx, jnp.sum(dh * xh, axis=0, keepdims=True)


def _tile(n, pref, mult=LANE):
    best = None
    for t in range(mult, min(n, pref) + 1, mult):
        if n % t == 0:
            best = t
    return best if best is not None else n


def _mm(a, b, mode, name, out_dtype=F32, c=None, tm=512, tn=512, tk=512, b_row0=0, out_rows=None, out_row0=0,
        into=None, xchg=None, norm_bwd=None, a2=None):
    if mode == "nn":
        (M, K), N = a.shape, b.shape[1]
    elif mode == "nt":
        (M, K), N = a.shape, b.shape[0]
    else:
        (K, M), N = a.shape, b.shape[1]
    tm, tn, tk = _tile(M, tm), _tile(N, tn), _tile(K, tk)
    has_a2 = a2 is not None
    assert not has_a2 or (mode == "nn" and a2.shape == a.shape)
    nka = K // tk
    nk = nka * (2 if has_a2 else 1)
    kb0, ob0 = b_row0 // tk, out_row0 // tm
    assert kb0 * tk == b_row0 and ob0 * tm == out_row0 and (mode == "nn" or b_row0 == 0)
    ni, nj = M // tm, N // tn
    j_outer = nk == 1 and (nj - 1) * a.size * a.dtype.itemsize < (ni - 1) * K * N * b.dtype.itemsize
    ix = (lambda f: (lambda j, i, k: f(i, j, k))) if j_outer else (lambda f: f)
    if mode == "tn":
        a_spec = BS((tk, tm), ix(lambda i, j, k: (k, i)))
    else:
        a_spec = BS((tm, tk), ix(lambda i, j, k: (i, jnp.minimum(k, nka - 1))))
    b_spec = (BS((tn, tk), ix(lambda i, j, k: (j, k))) if mode == "nt"
              else BS((tk, tn), ix(lambda i, j, k: (k + kb0, j))))
    dotfn = {"nn": _dot, "nt": _dot_nt, "tn": _dot_tn}[mode]
    has_c = c is not None
    has_into = into is not None
    nx = 0 if xchg is None else len(xchg[0])
    has_nb = norm_bwd is not None
    assert not has_nb or (nj == 1 and not j_outer and out_dtype == F32)
    n_in = 2 + has_a2 + has_c + has_into + 3 * has_nb

    def body(*refs):
        a_ref, b_ref = refs[:2]
        c_ref = refs[2 + has_a2] if has_c else None
        x_src = refs[n_in:n_in + nx]
        outs = refs[n_in + nx:n_in + 2 * nx + 1 + has_nb]
        o_ref, x_out = outs[0], outs[1 + has_nb:]
        acc = refs[n_in + 2 * nx + 1 + has_nb]
        sems = refs[n_in + 2 * nx + 2 + has_nb:]
        k = pl.program_id(2)
        g0, g1 = pl.program_id(0), pl.program_id(1)
        n0, n1 = (nj, ni) if j_outer else (ni, nj)
        first = jnp.logical_and(jnp.logical_and(g0 == 0, g1 == 0), k == 0)
        if nx:
            @pl.when(first)
            def _():
                _exchange_ops(x_src, x_out, xchg[1], sems, "start")

        if has_nb:
            @pl.when(first)
            def _():
                outs[1][...] = jnp.zeros_like(outs[1])

        @pl.when(k == 0)
        def _():
            acc[...] = jnp.zeros_like(acc)

        if has_a2:
            @pl.when(k < nka)
            def _():
                acc[...] += dotfn(_bf(a_ref[...]), _bf(b_ref[...]))

            @pl.when(k >= nka)
            def _():
                acc[...] += dotfn(_bf(refs[2][...]), _bf(b_ref[...]))
        else:
            acc[...] += dotfn(_bf(a_ref[...]), _bf(b_ref[...]))

        @pl.when(k == nk - 1)
        def _():
            r = acc[...]
            if has_c:
                r = r + c_ref[...]
            if has_nb:
                x_ref, g_ref, dres_ref = refs[n_in - 3:n_in]
                dx, dg = _rms_bwd_math(x_ref[...], g_ref[...], r, dres_ref[...])
                o_ref[...] = dx
                outs[1][...] += dg
            else:
                o_ref[...] = r.astype(out_dtype)

        if nx:
            @pl.when(jnp.logical_and(jnp.logical_and(g0 == n0 - 1, g1 == n1 - 1), k == nk - 1))
            def _():
                _exchange_ops(x_src, x_out, xchg[1], sems, "wait")

    tile = BS((tm, tn), ix(lambda i, j, k: (i, j)))
    in_specs = [a_spec, b_spec]
    args = [a, b]
    if has_a2:
        in_specs.append(BS((tm, tk), lambda i, j, k: (i, jnp.maximum(k - nka, 0))))
        args.append(a2)
    if has_c:
        in_specs.append(tile)
        args.append(c)
    aliases = {}
    if has_into:
        aliases = {len(args): 0}
        in_specs.append(BS(memory_space=pl.ANY))
        args.append(into)
    out_shape = [SDS((out_rows or M, N), out_dtype)]
    out_specs = [BS((tm, tn), ix(lambda i, j, k: (i + ob0, j)))]
    scratch = [pltpu.VMEM((tm, tn), F32)]
    if has_nb:
        in_specs += [tile, BS((1, tn), lambda i, j, k: (0, 0)), tile]
        args += list(norm_bwd)
        out_shape.append(SDS((1, N), F32))
        out_specs.append(BS((1, tn), lambda i, j, k: (0, 0)))
    if nx:
        hbm = BS(memory_space=pltpu.HBM)
        in_specs += [hbm] * nx
        args += list(xchg[0])
        out_shape += [SDS((N_DEV,) + s.shape if xchg[1] else s.shape, s.dtype) for s in xchg[0]]
        out_specs += [hbm] * nx
        scratch += _exchange_sems(nx)
    serial = nx or has_nb
    res = pl.pallas_call(
        body, name=name, out_shape=tuple(out_shape), grid=(nj, ni, nk) if j_outer else (ni, nj, nk),
        in_specs=in_specs, out_specs=tuple(out_specs),
        scratch_shapes=scratch, input_output_aliases=aliases,
        compiler_params=_cp(("arbitrary",) * 3 if serial else ("parallel", "parallel", "arbitrary")),
    )(*args)
    return res if serial else res[0]


def _norm_proj(x, g, w):
    T, N = x.shape[0], w.shape[0]
    tm = _tile(T, 512)

    def body(x_ref, g_ref, w_ref, p_ref, h_ref):
        xv = x_ref[...]
        r = lax.rsqrt(jnp.mean(xv * xv, axis=-1, keepdims=True) + EPS)
        h = _bf((xv * r) * g_ref[...])
        h_ref[...] = h
        p_ref[...] = _dot_nt(h, w_ref[...])

    return pl.pallas_call(
        body, name="norm_proj", out_shape=(SDS((T, N), F32), SDS((T, D), BF16)), grid=(T // tm,),
        in_specs=[BS((tm, D), lambda i: (i, 0)), BS((1, D), lambda i: (0, 0)), BS((N, D), lambda i: (0, 0))],
        out_specs=(BS((tm, N), lambda i: (i, 0)), BS((tm, D), lambda i: (i, 0))),
        compiler_params=_cp(("parallel",)),
    )(x, g, w)


TK = 256
SB_DEAD = -104.0
SB_MASKED = -1e30
CNT_LANE = SB_HD - 1


def _sb_masks():
    row = lax.broadcasted_iota(jnp.int32, (2 * QT, TK), 0) & (QT - 1)
    col = lax.broadcasted_iota(jnp.int32, (2 * QT, TK), 1)
    lane = lax.broadcasted_iota(jnp.int32, (1, LANE), 1)
    kr = lax.broadcasted_iota(jnp.int32, (TK, TK), 0)
    kc = lax.broadcasted_iota(jnp.int32, (TK, TK), 1)
    return row, col, lane, kr, kc


def _stack_heads(x, lane):
    return jnp.concatenate([_bf(jnp.where((lane // SB_HD) == hh, x, 0.0)) for hh in range(2)], axis=0)


def _sb_fwd(proj, S, shards):
    T = proj.shape[0]
    nq = S // QT
    scale = SB_HD ** -0.5
    nb, ns = T // S, len(shards)

    def body(qkv_ref, *rest):
        sh_refs, (o_ref, tt_ref), g_refs = rest[:ns], rest[ns:ns + 2], rest[ns + 2:2 * ns + 2]
        sems = rest[2 * ns + 2:]
        first = jnp.logical_and(pl.program_id(0) == 0, pl.program_id(1) == 0)
        last = jnp.logical_and(pl.program_id(0) == nb - 1, pl.program_id(1) == 3)

        @pl.when(first)
        def _():
            _exchange_ops(sh_refs, g_refs, True, sems, "start")

        row, col, lane, kr, kc = _sb_masks()
        msuf = _bf(kr > kc)

        def qloop(qt, _):
            r0 = pl.multiple_of(qt * QT, QT)
            qs = _stack_heads(qkv_ref[pl.ds(r0, QT), 0:128] * scale, lane)

            def live(st):
                it, _, cy = st
                return jnp.logical_and(it <= qt, jnp.max(cy) > SB_DEAD)

            def step(st):
                it, acc, cy = st
                kt = qt - it
                k0 = pl.multiple_of(kt * TK, TK)
                kv = _bf(qkv_ref[pl.ds(k0, TK), 128:256])
                vv = _bf(qkv_ref[pl.ds(k0, TK), 256:384])
                strict = (col + (kt - qt) * TK) < row
                z = jnp.where(strict, _dot_nt(qs, kv), SB_MASKED)
                sp = _softplus(z)
                lg = -sp
                after = cy + _split_dot(lg, msuf, 2)
                w = jnp.exp((z - sp) + after)
                return it + 1, acc + _dot(_bf(w), vv), cy + jnp.sum(lg, axis=1, keepdims=True)

            it, acc, cy = lax.while_loop(
                live, step, (jnp.int32(0), jnp.zeros((2 * QT, LANE), F32), jnp.zeros((2 * QT, 1), F32)))
            o_ref[pl.ds(r0, QT), :] = jnp.where(lane < SB_HD, acc[:QT], acc[QT:])
            tt = jnp.where(lane < SB_HD, cy[:QT], cy[QT:])
            tt_ref[pl.ds(r0, QT), :] = jnp.where(lane == CNT_LANE, it.astype(F32), tt)
            return 0

        lax.fori_loop(0, nq, qloop, 0)

        @pl.when(last)
        def _():
            _exchange_ops(sh_refs, g_refs, True, sems, "wait")

    hbm = BS(memory_space=pltpu.HBM)
    col_spec = BS((S, LANE), lambda b, p: (b, p))
    return pl.pallas_call(
        body, name="sb_fwd",
        out_shape=(SDS((T, SB_W), F32), SDS((T, SB_W), F32)) + tuple(SDS((N_DEV,) + s.shape, s.dtype) for s in shards),
        grid=(nb, 4),
        in_specs=[BS((S, 384), lambda b, p: (b, p))] + [hbm] * ns,
        out_specs=(col_spec, col_spec) + (hbm,) * ns,
        scratch_shapes=_exchange_sems(ns),
        compiler_params=_cp(("arbitrary", "arbitrary")),
    )(proj, *shards)


def _log_sigmoid(x):
    return jnp.minimum(x, 0.0) - jnp.log1p(jnp.exp(-jnp.abs(x)))


def _gla_masks():
    r = lax.broadcasted_iota(jnp.int32, (GR, GR), 0)
    c = lax.broadcasted_iota(jnp.int32, (GR, GR), 1)
    same = (r // CHUNK) == (c // CHUNK)
    causal = jnp.logical_and(same, r >= c)
    lane = lax.broadcasted_iota(jnp.int32, (1, LANE), 1)
    return causal, _bf(causal), _bf(jnp.logical_and(same, r <= c)), lane


def _gla_group_terms(blk_ref, glr_ref, wgu, bgu, r0, tri_incl):
    q = blk_ref[pl.ds(r0, GR), 0:128]
    k = blk_ref[pl.ds(r0, GR), 128:256]
    v = blk_ref[pl.ds(r0, GR), 256:512]
    pre = _dot(_bf(glr_ref[pl.ds(r0, GR), :]), wgu) + bgu
    la = _log_sigmoid(pre) / GATE_NORM
    b = _split_dot(la, tri_incl, 3, left=False)
    b_last = _per_chunk(lambda rows: b[rows.stop - 1:rows.stop])
    eb = jnp.exp(b)
    qd = (q * (GLA_DK ** -0.5)) * eb
    ki = k * jnp.exp(-b)
    ke = k * jnp.exp(b_last - b)
    decay = jnp.exp(b_last)
    return v, pre, b, b_last, eb, qd, ki, ke, decay


def _chunk_rows(n):
    return slice(n * CHUNK, (n + 1) * CHUNK)


def _per_chunk(row_fn):
    return jnp.concatenate(
        [jnp.broadcast_to(row_fn(_chunk_rows(n)), (CHUNK, LANE)) for n in range(GLA_G)], axis=0)


def _gla_fwd(proj, wgu, bgu, S):
    T = proj.shape[0]
    ng = S // GR

    def body(blk_ref, glr_ref, wgu_ref, bgu_ref, o_ref):
        causal, tri_incl, _, lane = _gla_masks()
        wg = _bf(wgu_ref[...])
        bg = bgu_ref[...]

        def group(g, states):
            r0 = pl.multiple_of(g * GR, GR)
            v, _, _, _, _, qd, ki, ke, decay = _gla_group_terms(blk_ref, glr_ref, wg, bg, r0, tri_incl)
            kib, keb = _bf(ki), _bf(ke)
            new_states, outs = [], []
            for hh in range(2):
                hm = (lane // GLA_DK) == hh
                qm = _bf(jnp.where(hm, qd, 0.0))
                vh = _bf(v[:, 128 * hh:128 * hh + 128])
                attn = jnp.where(causal, _dot_nt(qm, kib), 0.0)
                o_intra = _dot(_bf(attn), vh)
                st = states[hh]
                parts = []
                for n in range(GLA_G):
                    rows = _chunk_rows(n)
                    parts.append(o_intra[rows] + _dot_nt(qm[rows], _bf(st)))
                    st = st * decay[n * CHUNK:n * CHUNK + 1] + _dot_tn(vh[rows], keb[rows])
                outs.append(jnp.concatenate(parts, axis=0))
                new_states.append(st)
            o_ref[pl.ds(r0, GR), :] = jnp.concatenate(outs, axis=1)
            return tuple(new_states)

        z = jnp.zeros((GLA_DV, LANE), F32)
        lax.fori_loop(0, ng, group, (z, z))

    return pl.pallas_call(
        body, name="gla_fwd", out_shape=SDS((T, GLA_W), F32), grid=(T // S, 2),
        in_specs=[BS((S, 512), lambda b, p: (b, 3 + p)), BS((S, LANE), lambda b, p: (b, GLR_BLK)),
                  BS((LANE, LANE), lambda b, p: (0, p)), BS((1, LANE), lambda b, p: (0, p))],
        out_specs=BS((S, 256), lambda b, p: (b, p)),
        compiler_params=_cp(("parallel", "parallel")),
    )(proj, proj, wgu, bgu)


def _head_blockdiag(width, hd):
    r = lax.broadcasted_iota(jnp.int32, (width, width), 0) // hd
    c = lax.broadcasted_iota(jnp.int32, (width, width), 1) // hd
    return _bf(r == c)


def _mix_out(o_sb, o_gla, proj, x, g_sb, g_gla, w_out, g2):
    T = x.shape[0]
    tm = _tile(T, 512)

    def body(osb_ref, ogl_ref, og_ref, x_ref, gsb_ref, ggl_ref, w_ref, g2_ref, x1_ref, oc_ref, h2_ref):
        bd64 = _head_blockdiag(SB_W, SB_HD)
        bd128 = _head_blockdiag(GLA_W, GLA_DV)
        o = osb_ref[...]
        r = lax.rsqrt(_split_dot(o * o, bd64, 2) * (1.0 / SB_HD) + EPS)
        c_sb = (o * r) * gsb_ref[...]
        o = ogl_ref[...]
        r = lax.rsqrt(_split_dot(o * o, bd128, 2) * (1.0 / GLA_DV) + EPS)
        og = og_ref[...]
        c_gl = ((o * r) * ggl_ref[...]) * (og * jax.nn.sigmoid(og))
        oc = _bf(jnp.concatenate([c_sb, c_gl], axis=1))
        oc_ref[...] = oc
        x1 = x_ref[...] + _dot(oc, w_ref[...])
        x1_ref[...] = x1
        r2 = lax.rsqrt(jnp.mean(x1 * x1, axis=-1, keepdims=True) + EPS)
        h2_ref[...] = _bf((x1 * r2) * g2_ref[...])

    row = lambda w: BS((tm, w), lambda i: (i, 0))
    vec = lambda w: BS((1, w), lambda i: (0, 0))
    return pl.pallas_call(
        body, name="mix_out", out_shape=(SDS((T, D), F32), SDS((T, D), BF16), SDS((T, D), BF16)), grid=(T // tm,),
        in_specs=[row(SB_W), row(GLA_W), BS((tm, 512), lambda i: (i, OG_BLK)), row(D), vec(SB_W), vec(GLA_W),
                  BS((D, D), lambda i: (0, 0)), vec(D)],
        out_specs=(row(D), row(D), row(D)),
        compiler_params=_cp(("parallel",)),
    )(o_sb, o_gla, proj, x, g_sb, g_gla, w_out, g2)


CONV_ROWS = 256
CONV_TC = 256


def _rows_before(ref, r0, first):
    prev = ref[pl.ds(pl.multiple_of(jnp.maximum(r0 - 8, 0), 8), 8), :]
    return jnp.where(first, 0.0, prev)


def _shift_down(cur, prev8, k):
    cat = jnp.concatenate([prev8, cur], axis=0)
    return pltpu.roll(cat, k, 0)[8:]


def _shift_up(cur, next8, k):
    cat = jnp.concatenate([cur, next8], axis=0)
    return pltpu.roll(cat, cat.shape[0] - k, 0)[:cur.shape[0]]


def _conv_at(h_ref, cw, cb, r0, rows, first):
    cur = h_ref[pl.ds(r0, rows), :]
    prev8 = _rows_before(h_ref, r0, first)
    u = cb + cw[0:1, :] * _shift_down(cur, prev8, 2)
    u = u + cw[1:2, :] * _shift_down(cur, prev8, 1)
    return u + cw[2:3, :] * cur


NJ = D_FF // CONV_TC


def _conv_gate(hup, cw, cb, S):
    T = hup.shape[0]
    rows = min(CONV_ROWS, S)
    nr = S // rows

    def body(ha_ref, hv_ref, cwa_ref, cwv_ref, cba_ref, cbv_ref, act_ref, ua_ref, uv_ref):
        cwa, cwv, cba, cbv = cwa_ref[...], cwv_ref[...], cba_ref[...], cbv_ref[...]

        def step(c, _):
            r0 = pl.multiple_of(c * rows, rows)
            ua = _conv_at(ha_ref, cwa, cba, r0, rows, c == 0)
            uv = _conv_at(hv_ref, cwv, cbv, r0, rows, c == 0)
            ua_ref[pl.ds(r0, rows), :] = _bf(ua)
            uv_ref[pl.ds(r0, rows), :] = _bf(uv)
            act_ref[pl.ds(r0, rows), :] = _bf((ua * jax.nn.sigmoid(ua)) * uv)
            return 0

        lax.fori_loop(0, nr, step, 0)

    blk = lambda o: BS((S, CONV_TC), lambda b, j: (b, j + o))
    w3 = lambda o: BS((3, CONV_TC), lambda b, j: (0, j + o))
    w1 = lambda o: BS((1, CONV_TC), lambda b, j: (0, j + o))
    return pl.pallas_call(
        body, name="conv_gate", out_shape=(SDS((T, D_FF), BF16), SDS((T, D_FF), BF16), SDS((T, D_FF), BF16)),
        grid=(T // S, NJ),
        in_specs=[blk(0), blk(NJ), w3(0), w3(NJ), w1(0), w1(NJ)], out_specs=(blk(0), blk(0), blk(0)),
        compiler_params=_cp(("parallel", "parallel")),
    )(hup, hup, cw, cw, cb, cb)


def _down_loss(act, w_down, x1, tgt, g3):
    T = x1.shape[0]
    tm = _tile(T, 512)

    def body(a_ref, w_ref, x1_ref, t_ref, g_ref, dx_ref, dg_ref, ls_ref):
        @pl.when(pl.program_id(0) == 0)
        def _():
            dg_ref[...] = jnp.zeros_like(dg_ref)
            ls_ref[...] = jnp.zeros_like(ls_ref)

        g = g_ref[...]
        x2 = x1_ref[...] + _dot(a_ref[...], w_ref[...])
        r = lax.rsqrt(jnp.mean(x2 * x2, axis=-1, keepdims=True) + EPS)
        xh = x2 * r
        e = xh * g - t_ref[...]
        ls_ref[...] += 0.5 * jnp.sum(jnp.mean(e * e, axis=-1, keepdims=True), axis=0, keepdims=True)
        dy = e * (1.0 / D)
        dxh = dy * g
        dx_ref[...] = r * (dxh - xh * jnp.mean(dxh * xh, axis=-1, keepdims=True))
        dg_ref[...] += jnp.sum(dy * xh, axis=0, keepdims=True)

    row = lambda w: BS((tm, w), lambda i: (i, 0))
    return pl.pallas_call(
        body, name="down_loss", out_shape=(SDS((T, D), F32), SDS((1, D), F32), SDS((1, LANE), F32)), grid=(T // tm,),
        in_specs=[row(D_FF), BS((D_FF, D), lambda i: (0, 0)), row(D), row(D), BS((1, D), lambda i: (0, 0))],
        out_specs=(row(D), BS((1, D), lambda i: (0, 0)), BS((1, LANE), lambda i: (0, 0))),
        compiler_params=_cp(("arbitrary",)),
    )(act, w_down, x1, tgt, g3)


def _conv_gate_bwd(hup, u_a, u_v, dact, cw, S):
    T = hup.shape[0]
    rows = min(CONV_ROWS, S)
    nr = S // rows

    def body(ha_ref, hv_ref, ua_ref, uv_ref, da_ref, cwa_ref, cwv_ref,
             dha_ref, dhv_ref, dcwa_ref, dcwv_ref, dcba_ref, dcbv_ref):
        @pl.when(pl.program_id(1) == 0)
        def _():
            for r in (dcwa_ref, dcwv_ref, dcba_ref, dcbv_ref):
                r[...] = jnp.zeros_like(r)

        cwa, cwv = cwa_ref[...], cwv_ref[...]

        def du_at(r0, n):
            ua = ua_ref[pl.ds(r0, n), :].astype(F32)
            uv = uv_ref[pl.ds(r0, n), :].astype(F32)
            da = da_ref[pl.ds(r0, n), :]
            sg = jax.nn.sigmoid(ua)
            dua = (da * uv) * (sg * (1.0 + ua * (1.0 - sg)))
            duv = da * (ua * sg)
            return dua, duv

        def step(c, _):
            r0 = pl.multiple_of(c * rows, rows)
            last = c == nr - 1
            dua, duv = du_at(r0, rows)
            n0 = pl.multiple_of(jnp.minimum(r0 + rows, S - 16), 16)
            nua, nuv = du_at(n0, 16)
            nua = jnp.where(last, 0.0, nua[:8])
            nuv = jnp.where(last, 0.0, nuv[:8])
            for (h_ref, cw, du, nu, dh_ref, dcw_ref, dcb_ref) in (
                    (ha_ref, cwa, dua, nua, dha_ref, dcwa_ref, dcba_ref),
                    (hv_ref, cwv, duv, nuv, dhv_ref, dcwv_ref, dcbv_ref)):
                up1, up2 = _shift_up(du, nu, 1), _shift_up(du, nu, 2)
                dh_ref[pl.ds(r0, rows), :] = _bf(cw[2:3, :] * du + cw[1:2, :] * up1 + cw[0:1, :] * up2)
                cur = h_ref[pl.ds(r0, rows), :]
                dcw_ref[0:1, :] += jnp.sum(up2 * cur, axis=0, keepdims=True)
                dcw_ref[1:2, :] += jnp.sum(up1 * cur, axis=0, keepdims=True)
                dcw_ref[2:3, :] += jnp.sum(du * cur, axis=0, keepdims=True)
                dcb_ref[...] += jnp.sum(du, axis=0, keepdims=True)
            return 0

        lax.fori_loop(0, nr, step, 0)

    blk = lambda o: BS((S, CONV_TC), lambda j, b: (b, j + o))
    w3 = lambda o: BS((3, CONV_TC), lambda j, b: (0, j + o))
    w1 = BS((1, CONV_TC), lambda j, b: (0, j))
    return pl.pallas_call(
        body, name="conv_gate_bwd",
        out_shape=(SDS((T, D_FF), BF16), SDS((T, D_FF), BF16), SDS((3, D_FF), F32), SDS((3, D_FF), F32),
                   SDS((1, D_FF), F32), SDS((1, D_FF), F32)),
        grid=(NJ, T // S),
        in_specs=[blk(0), blk(NJ), blk(0), blk(0), blk(0), w3(0), w3(NJ)],
        out_specs=(blk(0), blk(0), w3(0), w3(0), w1, w1),
        compiler_params=_cp(("parallel", "arbitrary")),
    )(hup, hup, u_a, u_v, dact, cw, cw)


def _mix_bwd(docat, o_sb, o_gla, proj, g_sb, g_gla):
    T = docat.shape[0]
    tm = _tile(T, 512)

    def body(d_ref, osb_ref, ogl_ref, og_ref, gsb_ref, ggl_ref, dsb_ref, dgl_ref, dog_ref, dgsb_ref, dggl_ref):
        @pl.when(pl.program_id(0) == 0)
        def _():
            dgsb_ref[...] = jnp.zeros_like(dgsb_ref)
            dggl_ref[...] = jnp.zeros_like(dggl_ref)

        bd64 = _head_blockdiag(SB_W, SB_HD)
        bd128 = _head_blockdiag(GLA_W, GLA_DV)
        d = d_ref[:, 0:SB_W]
        o = osb_ref[...]
        r = lax.rsqrt(_split_dot(o * o, bd64, 2) * (1.0 / SB_HD) + EPS)
        n = o * r
        dn = d * gsb_ref[...]
        dgsb_ref[...] += jnp.sum(d * n, axis=0, keepdims=True)
        dsb_ref[...] = r * (dn - n * (_split_dot(dn * n, bd64, 2) * (1.0 / SB_HD)))

        d = d_ref[:, SB_W:D]
        o = ogl_ref[...]
        r = lax.rsqrt(_split_dot(o * o, bd128, 2) * (1.0 / GLA_DV) + EPS)
        n = o * r
        og = og_ref[...]
        sg = jax.nn.sigmoid(og)
        dm = d * (og * sg)
        dog_ref[...] = _bf((d * (n * ggl_ref[...])) * (sg * (1.0 + og * (1.0 - sg))))
        dn = dm * ggl_ref[...]
        dggl_ref[...] += jnp.sum(dm * n, axis=0, keepdims=True)
        dgl_ref[...] = r * (dn - n * (_split_dot(dn * n, bd128, 2) * (1.0 / GLA_DV)))

    row = lambda w: BS((tm, w), lambda i: (i, 0))
    vec = lambda w: BS((1, w), lambda i: (0, 0))
    ogb = BS((tm, 512), lambda i: (i, OG_BLK))
    return pl.pallas_call(
        body, name="mix_bwd",
        out_shape=(SDS((T, SB_W), F32), SDS((T, GLA_W), F32), SDS((T, PROJ_W), BF16), SDS((1, SB_W), F32),
                   SDS((1, GLA_W), F32)),
        grid=(T // tm,),
        in_specs=[row(D), row(SB_W), row(GLA_W), ogb, vec(SB_W), vec(GLA_W)],
        out_specs=(row(SB_W), row(GLA_W), ogb, vec(SB_W), vec(GLA_W)),
        compiler_params=_cp(("arbitrary",)),
    )(docat, o_sb, o_gla, proj, g_sb, g_gla)


def _sb_bwd(proj, tt, do, dproj, S, pieces):
    T = proj.shape[0]
    nq = S // QT
    scale = SB_HD ** -0.5
    nb, ns = T // S, len(pieces)

    def body(qkv_ref, tt_ref, do_ref, dp_in_ref, *rest):
        del dp_in_ref
        pc_refs, dp_ref, got_refs = rest[:ns], rest[ns], rest[ns + 1:2 * ns + 1]
        dk_acc, dv_acc = rest[2 * ns + 1:2 * ns + 3]
        sems = rest[2 * ns + 3:]
        first = jnp.logical_and(pl.program_id(0) == 0, pl.program_id(1) == 0)
        last = jnp.logical_and(pl.program_id(0) == nb - 1, pl.program_id(1) == 3)

        @pl.when(first)
        def _():
            _exchange_ops(pc_refs, got_refs, False, sems, "start")

        row, col, lane, kr, kc = _sb_masks()
        mincl = _bf(kr <= kc)
        mexcl = _bf(kr < kc)
        dk_acc[...] = jnp.zeros_like(dk_acc)
        dv_acc[...] = jnp.zeros_like(dv_acc)

        def qloop(qt, _):
            r0 = pl.multiple_of(qt * QT, QT)
            qs = _stack_heads(qkv_ref[pl.ds(r0, QT), 0:128] * scale, lane)
            dos = _stack_heads(do_ref[pl.ds(r0, QT), :], lane)
            ttv = tt_ref[pl.ds(r0, QT), :]
            tot = jnp.concatenate([ttv[:, 0:1], ttv[:, SB_HD:SB_HD + 1]], axis=0)
            walked = jnp.max(ttv[:, CNT_LANE:CNT_LANE + 1]).astype(jnp.int32)

            def step(kt, st):
                dq, lc, pc = st
                k0 = pl.multiple_of(kt * TK, TK)
                kv = _bf(qkv_ref[pl.ds(k0, TK), 128:256])
                vv = _bf(qkv_ref[pl.ds(k0, TK), 256:384])
                strict = (col + (kt - qt) * TK) < row
                z = jnp.where(strict, _dot_nt(qs, kv), SB_MASKED)
                sp = _softplus(z)
                lg = -sp
                after = tot - (lc + _split_dot(lg, mincl, 2))
                gl = z - sp
                w = jnp.exp(gl + after)
                du = w * _dot_nt(dos, vv)
                beta = jnp.exp(gl)
                pex = pc + _split_dot(du, mexcl, 2)
                dz = _bf(du - beta * (du + pex))
                dk_acc[pl.ds(k0, TK), :] += _dot_tn(dz, qs)
                dv_acc[pl.ds(k0, TK), :] += _dot_tn(_bf(w), dos)
                return (dq + _dot(dz, kv), lc + jnp.sum(lg, axis=1, keepdims=True),
                        pc + jnp.sum(du, axis=1, keepdims=True))

            zc = jnp.zeros((2 * QT, 1), F32)
            dq, _, _ = lax.fori_loop(qt - walked + 1, qt + 1, step, (jnp.zeros((2 * QT, LANE), F32), zc, zc))
            dp_ref[pl.ds(r0, QT), 0:128] = _bf(jnp.where(lane < SB_HD, dq[:QT], dq[QT:]) * scale)
            return 0

        lax.fori_loop(0, nq, qloop, 0)
        dp_ref[:, 128:256] = _bf(dk_acc[...])
        dp_ref[:, 256:384] = _bf(dv_acc[...])

        @pl.when(last)
        def _():
            _exchange_ops(pc_refs, got_refs, False, sems, "wait")

    blk = BS((S, 384), lambda b, p: (b, p))
    col_spec = BS((S, LANE), lambda b, p: (b, p))
    hbm = BS(memory_space=pltpu.HBM)
    return pl.pallas_call(
        body, name="sb_bwd", out_shape=(SDS((T, PROJ_W), BF16),) + tuple(SDS(s.shape, s.dtype) for s in pieces),
        grid=(nb, 4),
        in_specs=[blk, col_spec, col_spec, BS(memory_space=pl.ANY)] + [hbm] * ns, out_specs=(blk,) + (hbm,) * ns,
        scratch_shapes=[pltpu.VMEM((S, LANE), F32), pltpu.VMEM((S, LANE), F32)] + _exchange_sems(ns),
        input_output_aliases={3: 0},
        compiler_params=_cp(("arbitrary", "arbitrary")),
    )(proj, tt, do, dproj, *pieces)


def _gla_bwd(proj, wgu, bgu, do, dproj, S):
    T = proj.shape[0]
    nc, ng = S // CHUNK, S // GR

    def body(blk_ref, glr_ref, wgu_ref, bgu_ref, do_ref, dp_in_ref, dp_ref, dpre_ref, st_ref):
        del dp_in_ref
        causal, tri_incl, tri_rev, lane = _gla_masks()
        wg = _bf(wgu_ref[...])
        bg = bgu_ref[...]

        def fwd_group(g, states):
            r0 = pl.multiple_of(g * GR, GR)
            v, _, _, _, _, _, _, ke, decay = _gla_group_terms(blk_ref, glr_ref, wg, bg, r0, tri_incl)
            keb = _bf(ke)
            new_states = []
            for hh in range(2):
                vh = _bf(v[:, 128 * hh:128 * hh + 128])
                st = states[hh]
                for n in range(GLA_G):
                    rows = _chunk_rows(n)
                    st_ref[hh, g * GLA_G + n] = st
                    st = st * decay[n * CHUNK:n * CHUNK + 1] + _dot_tn(vh[rows], keb[rows])
                new_states.append(st)
            return tuple(new_states)

        z = jnp.zeros((GLA_DV, LANE), F32)
        lax.fori_loop(0, ng, fwd_group, (z, z))

        def bwd_group(it, dstates):
            g = ng - 1 - it
            r0 = pl.multiple_of(g * GR, GR)
            v, pre, b, b_last, eb, qd, ki, ke, decay = _gla_group_terms(
                blk_ref, glr_ref, wg, bg, r0, tri_incl)
            kib = _bf(ki)
            dqd = jnp.zeros((GR, LANE), F32)
            dki = jnp.zeros((GR, LANE), F32)
            dke = jnp.zeros((GR, LANE), F32)
            ddec = jnp.zeros((GR, LANE), F32)
            new_dstates, dvs = [], []
            for hh in range(2):
                hm = (lane // GLA_DK) == hh
                qm = _bf(jnp.where(hm, qd, 0.0))
                kem = _bf(jnp.where(hm, ke, 0.0))
                vh = _bf(v[:, 128 * hh:128 * hh + 128])
                doh = _bf(do_ref[pl.ds(r0, GR), 128 * hh:128 * hh + 128])
                attn = _bf(jnp.where(causal, _dot_nt(qm, kib), 0.0))
                dattn = _bf(jnp.where(causal, _dot_nt(doh, vh), 0.0))
                dv_intra = _dot_tn(attn, doh)
                dqd_intra = _dot(dattn, kib)
                dki = dki + _dot_tn(dattn, qm)
                dst = dstates[hh]
                dv_p, dqd_p, dke_p, ddec_p = [None] * GLA_G, [None] * GLA_G, [None] * GLA_G, [None] * GLA_G
                for n in reversed(range(GLA_G)):
                    rows = _chunk_rows(n)
                    st = st_ref[hh, g * GLA_G + n]
                    dv_p[n] = dv_intra[rows] + _dot_nt(kem[rows], _bf(dst))
                    dqd_p[n] = dqd_intra[rows] + _dot(doh[rows], _bf(st))
                    dke_p[n] = _dot(vh[rows], _bf(dst))
                    ddec_p[n] = jnp.broadcast_to(jnp.sum(dst * st, axis=0, keepdims=True), (CHUNK, LANE))
                    dst = dst * decay[n * CHUNK:n * CHUNK + 1] + _dot_tn(doh[rows], qm[rows])
                dvs.append(jnp.concatenate(dv_p, axis=0))
                dqd = dqd + jnp.where(hm, jnp.concatenate(dqd_p, axis=0), 0.0)
                dke = dke + jnp.where(hm, jnp.concatenate(dke_p, axis=0), 0.0)
                ddec = ddec + jnp.where(hm, jnp.concatenate(ddec_p, axis=0), 0.0)
                new_dstates.append(dst)
            einv = jnp.exp(-b)
            eend = jnp.exp(b_last - b)
            dq = (dqd * eb) * (GLA_DK ** -0.5)
            dk = dki * einv + dke * eend
            db = dqd * qd - dki * ki - dke * ke
            dkk = dke * ke
            db_last = _per_chunk(lambda rows: jnp.sum(dkk[rows], axis=0, keepdims=True)) + ddec * decay
            dla = _split_dot(db, tri_rev, 2, left=False) + db_last
            dpre_ref[pl.ds(r0, GR), :] = (dla * (1.0 / GATE_NORM)) * (1.0 - jax.nn.sigmoid(pre))
            dp_ref[pl.ds(r0, GR), 0:128] = _bf(dq)
            dp_ref[pl.ds(r0, GR), 128:256] = _bf(dk)
            dp_ref[pl.ds(r0, GR), 256:512] = _bf(jnp.concatenate(dvs, axis=1))
            return tuple(new_dstates)

        lax.fori_loop(0, ng, bwd_group, (z, z))

    return pl.pallas_call(
        body, name="gla_bwd", out_shape=(SDS((T, PROJ_W), BF16), SDS((T, GLA_KW), F32)), grid=(T // S, 2),
        in_specs=[BS((S, 512), lambda b, p: (b, 3 + p)), BS((S, LANE), lambda b, p: (b, GLR_BLK)),
                  BS((LANE, LANE), lambda b, p: (0, p)), BS((1, LANE), lambda b, p: (0, p)),
                  BS((S, 256), lambda b, p: (b, p)), BS(memory_space=pl.ANY)],
        out_specs=(BS((S, 512), lambda b, p: (b, 3 + p)), BS((S, LANE), lambda b, p: (b, p))),
        scratch_shapes=[pltpu.VMEM((2, nc, GLA_DV, LANE), F32)],
        input_output_aliases={5: 0},
        compiler_params=_cp(("parallel", "parallel")),
    )(proj, proj, wgu, bgu, do, dproj)


def _gate_bwd(dpre, proj, wgu, dproj):
    T = dpre.shape[0]
    tm = _tile(T, 512)

    def body(dpre_ref, glr_ref, wgu_ref, dp_in_ref, dp_ref, dw_ref, db_ref):
        del dp_in_ref

        @pl.when(pl.program_id(0) == 0)
        def _():
            dw_ref[...] = jnp.zeros_like(dw_ref)
            db_ref[...] = jnp.zeros_like(db_ref)

        dpre = dpre_ref[...]
        dp_ref[...] = _bf(_dot_nt(_bf(dpre), _bf(wgu_ref[...])))
        dw_ref[...] += _dot_tn(_bf(glr_ref[...]), _bf(dpre))
        db_ref[...] += jnp.sum(dpre, axis=0, keepdims=True)

    glr = BS((tm, LANE), lambda i: (i, GLR_BLK))
    return pl.pallas_call(
        body, name="gate_bwd",
        out_shape=(SDS((T, PROJ_W), BF16), SDS((LANE, GLA_KW), F32), SDS((1, GLA_KW), F32)), grid=(T // tm,),
        in_specs=[BS((tm, GLA_KW), lambda i: (i, 0)), glr, BS((LANE, GLA_KW), lambda i: (0, 0)),
                  BS(memory_space=pl.ANY)],
        out_specs=(glr, BS((LANE, GLA_KW), lambda i: (0, 0)), BS((1, GLA_KW), lambda i: (0, 0))),
        input_output_aliases={3: 0},
        compiler_params=_cp(("arbitrary",)),
    )(dpre, proj, wgu, dproj)


def _exchange_sems(n):
    return [pltpu.SemaphoreType.DMA((n * (N_DEV - 1),)), pltpu.SemaphoreType.DMA((n * (N_DEV - 1),)),
            pltpu.SemaphoreType.DMA((n,))]


def _exchange_ops(srcs, outs, gather, sems, act):
    ssem, rsem, lsem = sems
    x, y, c = lax.axis_index("x"), lax.axis_index("y"), lax.axis_index("c")
    me = 4 * x + 2 * y + c
    for i, (s_ref, o_ref) in enumerate(zip(srcs, outs)):
        for k in range(1, N_DEV):
            px = (x + ((k >> 2) & 1)) % 2
            py = (y + ((k >> 1) & 1)) % 2
            pc = (c + (k & 1)) % 2
            peer = 4 * px + 2 * py + pc
            n = i * (N_DEV - 1) + k - 1
            out = pltpu.make_async_remote_copy(
                src_ref=s_ref if gather else s_ref.at[peer], dst_ref=o_ref.at[me],
                send_sem=ssem.at[n], recv_sem=rsem.at[n],
                device_id=(px, py, pc), device_id_type=pl.DeviceIdType.MESH)
            if act == "start":
                out.start()
            else:
                out.wait_send()
                pltpu.make_async_remote_copy(
                    src_ref=s_ref if gather else s_ref.at[me], dst_ref=o_ref.at[peer],
                    send_sem=ssem.at[n], recv_sem=rsem.at[n],
                    device_id=(x, y, c), device_id_type=pl.DeviceIdType.MESH).wait_recv()
        mine = pltpu.make_async_copy(s_ref if gather else s_ref.at[me], o_ref.at[me], lsem.at[i])
        if act == "start":
            mine.start()
        else:
            mine.wait()


def _gather_two_level(src, name):
    def body(s_ref, o_ref, ssem, rsem, lsem):
        x, y, c = lax.axis_index("x"), lax.axis_index("y"), lax.axis_index("c")
        me, sibling = (x, y, c), (x, y, 1 - c)
        chips = [(1 - x, y), (x, 1 - y), (1 - x, 1 - y)]

        def slab(px, py, pc):
            return o_ref.at[4 * px + 2 * py + pc]

        def copy(k, block, to, src_ref=None):
            return pltpu.make_async_remote_copy(
                src_ref=slab(*block) if src_ref is None else src_ref, dst_ref=slab(*block),
                send_sem=ssem.at[k], recv_sem=rsem.at[k], device_id=to, device_id_type=pl.DeviceIdType.MESH)

        mine = pltpu.make_async_copy(s_ref, slab(*me), lsem)
        mine.start()
        first = [copy(0, me, sibling, s_ref)] + [copy(1 + j, me, (*chip, c), s_ref) for j, chip in enumerate(chips)]
        for cp in first:
            cp.start()
        passed = [copy(4 + j, (*chip, c), sibling) for j, chip in enumerate(chips)]
        for j, chip in enumerate(chips):
            copy(1 + j, (*chip, c), me).wait_recv()
            passed[j].start()
        copy(0, sibling, me).wait_recv()
        for j, chip in enumerate(chips):
            copy(4 + j, (*chip, 1 - c), me).wait_recv()
        for cp in first + passed:
            cp.wait_send()
        mine.wait()

    hbm = BS(memory_space=pltpu.HBM)
    return pl.pallas_call(
        body, name=name, out_shape=SDS((N_DEV,) + src.shape, src.dtype), in_specs=[hbm], out_specs=hbm,
        scratch_shapes=[pltpu.SemaphoreType.DMA((N_DEV - 1,)), pltpu.SemaphoreType.DMA((N_DEV - 1,)),
                        pltpu.SemaphoreType.DMA(())],
    )(src)


def _exchange(srcs, gather, name):
    n = len(srcs)
    shapes = [SDS((N_DEV,) + s.shape if gather else s.shape, s.dtype) for s in srcs]

    def body(*refs):
        s_refs, o_refs, sems = refs[:n], refs[n:2 * n], refs[2 * n:]
        _exchange_ops(s_refs, o_refs, gather, sems, "start")
        _exchange_ops(s_refs, o_refs, gather, sems, "wait")

    hbm = BS(memory_space=pltpu.HBM)
    return pl.pallas_call(
        body, name=name, out_shape=tuple(shapes), in_specs=[hbm] * n, out_specs=(hbm,) * n,
        scratch_shapes=_exchange_sems(n),
    )(*srcs)


def _adamw_math(w, g, m, v):
    m = ADAM_B1 * m + (1.0 - ADAM_B1) * g
    v = ADAM_B2 * v + (1.0 - ADAM_B2) * (g * g)
    m_hat = m / (1.0 - ADAM_B1 ** ADAM_STEP)
    v_hat = v / (1.0 - ADAM_B2 ** ADAM_STEP)
    delta = -ADAM_LR * (m_hat / (jnp.sqrt(v_hat) + ADAM_EPS) + ADAM_WD * w)
    return delta, m, v


def _sum_adamw(parts, w, m, v, tr, name):
    R, C = w.shape

    def body(p_ref, w_ref, m_ref, v_ref, g_ref, d_ref, nm_ref, nv_ref):
        g = p_ref[0].astype(F32)
        for d in range(1, N_DEV):
            g = g + p_ref[d].astype(F32)
        delta, nm, nv = _adamw_math(w_ref[...], g, m_ref[...], v_ref[...])
        g_ref[...] = g
        d_ref[...] = delta
        nm_ref[...] = nm
        nv_ref[...] = nv

    blk = BS((tr, C), lambda i: (i, 0))
    out = SDS((R, C), F32)
    return pl.pallas_call(
        body, name=name, out_shape=(out, out, out, out), grid=(R // tr,),
        in_specs=[BS((N_DEV, tr, C), lambda i: (0, i, 0)), blk, blk, blk], out_specs=(blk, blk, blk, blk),
        compiler_params=_cp(("parallel",)),
    )(parts, w, m, v)


def _flat_pad_rows(parts, rows):
    flat = jnp.concatenate([p.reshape(-1) for p in parts])
    return jnp.pad(flat, (0, rows * D - flat.shape[0])).reshape(rows, D)


SMALL_ROWS = 16
SHARD_SMALL_ROWS = 3


def kernel(x, attn_norm_g, w_in, w_gate_up, b_gate_up, sb_out_g, gla_out_g, w_out, ffn_norm_g, w_ffn_up, conv_w, conv_b, w_ffn_down, final_norm_g, loss_target, m_attn_norm_g, m_w_in, m_w_gate_up, m_b_gate_up, m_sb_out_g, m_gla_out_g, m_w_out, m_ffn_norm_g, m_w_ffn_up, m_conv_w, m_conv_b, m_w_ffn_down, m_final_norm_g, v_attn_norm_g, v_w_in, v_w_gate_up, v_b_gate_up, v_sb_out_g, v_gla_out_g, v_w_out, v_ffn_norm_g, v_w_ffn_up, v_conv_w, v_conv_b, v_w_ffn_down, v_final_norm_g):
    Bd, S, _ = x.shape
    T = Bd * S
    x2d = x.reshape(T, D)
    tgt = loss_target.reshape(T, D)
    c_up = w_ffn_up.shape[2]
    c_gu = w_gate_up.shape[2]
    c_in = w_in.shape[2]

    n_gu = GATE_RANK * c_gu
    rows_bf = lambda w: w[0].T.astype(BF16)
    small_w = lambda wgu, cw: _flat_pad_rows([wgu, cw], SHARD_SMALL_ROWS)

    g_in = _gather_two_level(rows_bf(w_in), "gather_w_in")
    w_in_pt = jnp.pad(g_in.reshape(IN_COLS, D), ((0, 1), (0, 0)))[_PERM]
    g3 = final_norm_g.reshape(1, D)

    proj, h1 = _norm_proj(x2d, attn_norm_g, w_in_pt)
    o_sb, tt, g_up, g_down, g_out, gs = _sb_fwd(
        proj, S, [rows_bf(w_ffn_up), w_ffn_down[0].astype(BF16), w_out[0].astype(BF16),
                  _flat_pad_rows([w_gate_up, conv_w], 8)])
    w_out_f = g_out.reshape(D, D)
    gsf = gs.reshape(N_DEV, -1)
    wgu_f = jnp.transpose(gsf[:, :n_gu].reshape(N_DEV, GATE_RANK, c_gu), (1, 0, 2)).reshape(GATE_RANK, GLA_KW)
    cw_f = jnp.transpose(gsf[:, n_gu:n_gu + 3 * c_up].reshape(N_DEV, 3, c_up), (1, 0, 2)).reshape(3, 2 * D_FF)
    wgu_p = jnp.pad(wgu_f, ((0, LANE - GATE_RANK), (0, 0)))
    w_up_t = g_up.reshape(2 * D_FF, D)
    w_down_f = g_down.reshape(D_FF, D)
    o_gla = _gla_fwd(proj, wgu_p, b_gate_up, S)
    x1, ocat, h2 = _mix_out(o_sb, o_gla, proj, x2d, sb_out_g, gla_out_g, w_out_f, ffn_norm_g)
    hup = _mm(h2, w_up_t, "nt", "ffn_up", tm=1024, tn=1408, tk=1024)
    act, u_a, u_v = _conv_gate(hup, cw_f, conv_b, S)
    dx2, dg3, loss_dev = _down_loss(act, w_down_f, x1, tgt, g3)

    dw_down = _mm(act, dx2, "tn", "dw_down", out_dtype=BF16, tm=D_FF, tn=1024, tk=512)
    dact = _mm(dx2, w_down_f, "nt", "dact", tm=1024, tn=1408, tk=1024)
    dhup_a, dhup_v, dcw_a, dcw_v, dcb_a, dcb_v = _conv_gate_bwd(hup, u_a, u_v, dact, cw_f, S)
    dw_up_t = _mm(dhup_a, h2, "tn", "dw_up_a", out_dtype=BF16, tm=D_FF, tn=1024, tk=512, out_rows=2 * D_FF)
    dw_up_t = _mm(dhup_v, h2, "tn", "dw_up_v", out_dtype=BF16, tm=D_FF, tn=1024, tk=512, out_rows=2 * D_FF,
                  out_row0=D_FF, into=dw_up_t)
    dx1, dg2 = _mm(dhup_a, w_up_t, "nn", "dh2", a2=dhup_v, tm=1024, tn=1024, tk=1408,
                   norm_bwd=(x1, ffn_norm_g, dx2))

    dw_out = _mm(ocat, dx1, "tn", "dw_out", out_dtype=BF16, tm=1024, tn=1024, tk=512)
    docat = _mm(dx1, w_out_f, "nt", "docat", tm=1024, tn=1024, tk=1024)
    do_sb, do_gla, dproj, dg_sb, dg_gla = _mix_bwd(docat, o_sb, o_gla, proj, sb_out_g, gla_out_g)
    dproj, got_up, got_down, got_out = _sb_bwd(
        proj, tt, do_sb, dproj, S,
        [dw_up_t.reshape(N_DEV, c_up, D), dw_down.reshape(N_DEV, -1, D), dw_out.reshape(N_DEV, -1, D)])
    dproj, dpre = _gla_bwd(proj, wgu_p, b_gate_up, do_gla, dproj, S)
    dproj, dwgu, dbgu = _gate_bwd(dpre, proj, wgu_p, dproj)
    dw_in_pt = _mm(dproj, h1, "tn", "dw_in", out_dtype=BF16, tm=PROJ_W, tn=1024, tk=512)
    dx, dg1, got_in = _mm(dproj, w_in_pt, "nn", "dh1", tm=1024, tn=1024, tk=640,
                          xchg=([dw_in_pt[_INV_PERM].reshape(N_DEV, c_in, D)], False),
                          norm_bwd=(x2d, attn_norm_g, dx1))

    dcw = jnp.concatenate([dcw_a, dcw_v], axis=1)
    dwgu_pc = jnp.transpose(dwgu[:GATE_RANK].reshape(GATE_RANK, N_DEV, c_gu), (1, 0, 2)).reshape(N_DEV, -1)
    dcw_pc = jnp.transpose(dcw.reshape(3, N_DEV, c_up), (1, 0, 2)).reshape(N_DEV, -1)
    small_pc = jnp.concatenate([dwgu_pc, dcw_pc], axis=1)
    small_pc = jnp.pad(small_pc, ((0, 0), (0, SHARD_SMALL_ROWS * D - small_pc.shape[1])))
    small_pc = small_pc.reshape(N_DEV, SHARD_SMALL_ROWS, D).astype(BF16)
    rep_names = ["attn_norm_g", "b_gate_up", "sb_out_g", "gla_out_g", "ffn_norm_g", "conv_b", "final_norm_g"]
    rep_g = [dg1, dbgu, dg_sb, dg_gla, dg2, jnp.concatenate([dcb_a, dcb_v], axis=1), dg3]
    rep_w = [attn_norm_g, b_gate_up, sb_out_g, gla_out_g, ffn_norm_g, conv_b, final_norm_g]
    rep_m = [m_attn_norm_g, m_b_gate_up, m_sb_out_g, m_gla_out_g, m_ffn_norm_g, m_conv_b, m_final_norm_g]
    rep_v = [v_attn_norm_g, v_b_gate_up, v_sb_out_g, v_gla_out_g, v_ffn_norm_g, v_conv_b, v_final_norm_g]
    rep_pc = jnp.broadcast_to(_flat_pad_rows(rep_g + [loss_dev[:, 0:1]], SMALL_ROWS), (N_DEV, SMALL_ROWS, D))
    got_sm, got_rep = _exchange([small_pc, rep_pc], False, "scatter_tail")

    rows = lambda w: w[0].T
    cols = lambda r: r.T[None]
    res = {}
    res["w_in"] = [cols(r) for r in _sum_adamw(got_in, rows(w_in), rows(m_w_in), rows(v_w_in), c_in, "adamw_w_in")]
    res["w_out"] = [r[None] for r in _sum_adamw(got_out, w_out[0], m_w_out[0], v_w_out[0], w_out.shape[1],
                                                 "adamw_w_out")]
    res["w_ffn_up"] = [cols(r) for r in _sum_adamw(got_up, rows(w_ffn_up), rows(m_w_ffn_up), rows(v_w_ffn_up),
                                                    c_up // 2, "adamw_w_up")]
    res["w_ffn_down"] = [r[None] for r in _sum_adamw(got_down, w_ffn_down[0], m_w_ffn_down[0], v_w_ffn_down[0],
                                                      w_ffn_down.shape[1], "adamw_w_down")]
    sm = _sum_adamw(got_sm, small_w(w_gate_up, conv_w), small_w(m_w_gate_up, m_conv_w),
                    small_w(v_w_gate_up, v_conv_w), SHARD_SMALL_ROWS, "adamw_small_sharded")
    res["w_gate_up"] = [r.reshape(-1)[:n_gu].reshape(1, GATE_RANK, c_gu) for r in sm]
    res["conv_w"] = [r.reshape(-1)[n_gu:n_gu + 3 * c_up].reshape(1, 3, c_up) for r in sm]
    rep = _sum_adamw(got_rep, _flat_pad_rows(rep_w, SMALL_ROWS), _flat_pad_rows(rep_m, SMALL_ROWS),
                     _flat_pad_rows(rep_v, SMALL_ROWS), SMALL_ROWS, "adamw_replicated")
    o = 0
    for n, w in zip(rep_names, rep_w):
        res[n] = [r.reshape(-1)[o:o + w.size].reshape(w.shape) for r in rep]
        o += w.size

    loss = rep[0].reshape(-1)[o]
    order = ["attn_norm_g", "w_in", "w_gate_up", "b_gate_up", "sb_out_g", "gla_out_g", "w_out", "ffn_norm_g",
             "w_ffn_up", "conv_w", "conv_b", "w_ffn_down", "final_norm_g"]
    outs = [loss, dx.reshape(Bd, S, D)]
    for k in range(4):
        outs += [res[n][k] for n in order]
    return tuple(outs)
```

```python
import numpy as np
import jax
import jax.numpy as jnp
from jax import lax
from jax.experimental import pallas as pl
from jax.experimental.pallas import tpu as pltpu

F32 = jnp.float32
BF16 = jnp.bfloat16
SDS = jax.ShapeDtypeStruct
BS = pl.BlockSpec

N_DEV = 8
D = 1024
EPS = 1e-6
SB_HD = 64
SB_W = 512
GLA_DK = 64
GLA_DV = 128
GLA_KW = 256
GLA_W = 512
GATE_RANK = 16
GATE_NORM = 16.0
CHUNK = 64
GLA_G = 4
GR = GLA_G * CHUNK
QT = 256
D_FF = 2816
IN_COLS = 3088
PROJ_W = 3200
LANE = 128
VMEM_LIMIT = 56 * 1024 * 1024

ADAM_LR, ADAM_B1, ADAM_B2, ADAM_EPS, ADAM_WD, ADAM_STEP = 0.001, 0.9, 0.999, 1e-08, 0.01, 10


def _proj_perm():
    sbq, sbk, sbv = 0, 512, 1024
    gq, gk, gv, glr, gog = 1536, 1792, 2048, 2560, 2576
    cols = []
    for p in range(4):
        for base in (sbq, sbk, sbv):
            cols += list(range(base + 128 * p, base + 128 * p + 128))
    for p in range(2):
        cols += list(range(gq + 128 * p, gq + 128 * p + 128))
        cols += list(range(gk + 128 * p, gk + 128 * p + 128))
        cols += list(range(gv + 256 * p, gv + 256 * p + 256))
    cols += list(range(gog, gog + 512))
    cols += list(range(glr, glr + GATE_RANK)) + [IN_COLS] * (LANE - GATE_RANK)
    perm = np.asarray(cols, np.int32)
    inv = np.zeros((IN_COLS,), np.int32)
    for new, old in enumerate(cols):
        if old < IN_COLS:
            inv[old] = new
    return perm, inv


_PERM, _INV_PERM = _proj_perm()
OG_BLK = 5
GLR_BLK = 24


def _cp(sem=None, vmem=VMEM_LIMIT):
    return pltpu.CompilerParams(dimension_semantics=sem, vmem_limit_bytes=vmem)


def _dot(a, b):
    return lax.dot_general(a, b, (((1,), (0,)), ((), ())), preferred_element_type=F32)


def _dot_nt(a, b):
    return lax.dot_general(a, b, (((1,), (1,)), ((), ())), preferred_element_type=F32)


def _dot_tn(a, b):
    return lax.dot_general(a, b, (((0,), (0,)), ((), ())), preferred_element_type=F32)


def _bf(x):
    return x.astype(BF16)


def _split_dot(x, m, passes, left=True):
    acc = None
    r = x
    for i in range(passes):
        h = r.astype(BF16)
        t = _dot(h, m) if left else _dot(m, h)
        acc = t if acc is None else acc + t
        if i + 1 < passes:
            r = r - h.astype(F32)
    return acc


def _softplus(z):
    return jnp.maximum(z, 0.0) + jnp.log(1.0 + jnp.exp(-jnp.abs(z)))


def _rms_bwd_math(x, g, dh, dres):
    r = lax.rsqrt(jnp.mean(x * x, axis=-1, keepdims=True) + EPS)
    xh = x * r
    dxh = dh * g
    dx = dres + r * (dxh - xh * jnp.mean(dxh * xh, axis=-1, keepdims=True))
    return dx, jnp.sum(dh * xh, axis=0, keepdims=True)


def _tile(n, pref, mult=LANE):
    best = None
    for t in range(mult, min(n, pref) + 1, mult):
        if n % t == 0:
            best = t
    return best if best is not None else n


def _mm(a, b, mode, name, out_dtype=F32, c=None, tm=512, tn=512, tk=512, b_row0=0, out_rows=None, out_row0=0,
        into=None, xchg=None, norm_bwd=None, a2=None):
    if mode == "nn":
        (M, K), N = a.shape, b.shape[1]
    elif mode == "nt":
        (M, K), N = a.shape, b.shape[0]
    else:
        (K, M), N = a.shape, b.shape[1]
    tm, tn, tk = _tile(M, tm), _tile(N, tn), _tile(K, tk)
    has_a2 = a2 is not None
    assert not has_a2 or (mode == "nn" and a2.shape == a.shape)
    nka = K // tk
    nk = nka * (2 if has_a2 else 1)
    kb0, ob0 = b_row0 // tk, out_row0 // tm
    assert kb0 * tk == b_row0 and ob0 * tm == out_row0 and (mode == "nn" or b_row0 == 0)
    ni, nj = M // tm, N // tn
    j_outer = nk == 1 and (nj - 1) * a.size * a.dtype.itemsize < (ni - 1) * K * N * b.dtype.itemsize
    ix = (lambda f: (lambda j, i, k: f(i, j, k))) if j_outer else (lambda f: f)
    if mode == "tn":
        a_spec = BS((tk, tm), ix(lambda i, j, k: (k, i)))
    else:
        a_spec = BS((tm, tk), ix(lambda i, j, k: (i, jnp.minimum(k, nka - 1))))
    b_spec = (BS((tn, tk), ix(lambda i, j, k: (j, k))) if mode == "nt"
              else BS((tk, tn), ix(lambda i, j, k: (k + kb0, j))))
    dotfn = {"nn": _dot, "nt": _dot_nt, "tn": _dot_tn}[mode]
    has_c = c is not None
    has_into = into is not None
    nx = 0 if xchg is None else len(xchg[0])
    has_nb = norm_bwd is not None
    assert not has_nb or (nj == 1 and not j_outer and out_dtype == F32)
    n_in = 2 + has_a2 + has_c + has_into + 3 * has_nb

    def body(*refs):
        a_ref, b_ref = refs[:2]
        c_ref = refs[2 + has_a2] if has_c else None
        x_src = refs[n_in:n_in + nx]
        outs = refs[n_in + nx:n_in + 2 * nx + 1 + has_nb]
        o_ref, x_out = outs[0], outs[1 + has_nb:]
        acc = refs[n_in + 2 * nx + 1 + has_nb]
        sems = refs[n_in + 2 * nx + 2 + has_nb:]
        k = pl.program_id(2)
        g0, g1 = pl.program_id(0), pl.program_id(1)
        n0, n1 = (nj, ni) if j_outer else (ni, nj)
        first = jnp.logical_and(jnp.logical_and(g0 == 0, g1 == 0), k == 0)
        if nx:
            @pl.when(first)
            def _():
                _exchange_ops(x_src, x_out, xchg[1], sems, "start")

        if has_nb:
            @pl.when(first)
            def _():
                outs[1][...] = jnp.zeros_like(outs[1])

        @pl.when(k == 0)
        def _():
            acc[...] = jnp.zeros_like(acc)

        if has_a2:
            @pl.when(k < nka)
            def _():
                acc[...] += dotfn(_bf(a_ref[...]), _bf(b_ref[...]))

            @pl.when(k >= nka)
            def _():
                acc[...] += dotfn(_bf(refs[2][...]), _bf(b_ref[...]))
        else:
            acc[...] += dotfn(_bf(a_ref[...]), _bf(b_ref[...]))

        @pl.when(k == nk - 1)
        def _():
            r = acc[...]
            if has_c:
                r = r + c_ref[...]
            if has_nb:
                x_ref, g_ref, dres_ref = refs[n_in - 3:n_in]
                dx, dg = _rms_bwd_math(x_ref[...], g_ref[...], r, dres_ref[...])
                o_ref[...] = dx
                outs[1][...] += dg
            else:
                o_ref[...] = r.astype(out_dtype)

        if nx:
            @pl.when(jnp.logical_and(jnp.logical_and(g0 == n0 - 1, g1 == n1 - 1), k == nk - 1))
            def _():
                _exchange_ops(x_src, x_out, xchg[1], sems, "wait")

    tile = BS((tm, tn), ix(lambda i, j, k: (i, j)))
    in_specs = [a_spec, b_spec]
    args = [a, b]
    if has_a2:
        in_specs.append(BS((tm, tk), lambda i, j, k: (i, jnp.maximum(k - nka, 0))))
        args.append(a2)
    if has_c:
        in_specs.append(tile)
        args.append(c)
    aliases = {}
    if has_into:
        aliases = {len(args): 0}
        in_specs.append(BS(memory_space=pl.ANY))
        args.append(into)
    out_shape = [SDS((out_rows or M, N), out_dtype)]
    out_specs = [BS((tm, tn), ix(lambda i, j, k: (i + ob0, j)))]
    scratch = [pltpu.VMEM((tm, tn), F32)]
    if has_nb:
        in_specs += [tile, BS((1, tn), lambda i, j, k: (0, 0)), tile]
        args += list(norm_bwd)
        out_shape.append(SDS((1, N), F32))
        out_specs.append(BS((1, tn), lambda i, j, k: (0, 0)))
    if nx:
        hbm = BS(memory_space=pltpu.HBM)
        in_specs += [hbm] * nx
        args += list(xchg[0])
        out_shape += [SDS((N_DEV,) + s.shape if xchg[1] else s.shape, s.dtype) for s in xchg[0]]
        out_specs += [hbm] * nx
        scratch += _exchange_sems(nx)
    serial = nx or has_nb
    res = pl.pallas_call(
        body, name=name, out_shape=tuple(out_shape), grid=(nj, ni, nk) if j_outer else (ni, nj, nk),
        in_specs=in_specs, out_specs=tuple(out_specs),
        scratch_shapes=scratch, input_output_aliases=aliases,
        compiler_params=_cp(("arbitrary",) * 3 if serial else ("parallel", "parallel", "arbitrary")),
    )(*args)
    return res if serial else res[0]


def _norm_proj(x, g, w):
    T, N = x.shape[0], w.shape[0]
    tm = _tile(T, 512)

    def body(x_ref, g_ref, w_ref, p_ref, h_ref):
        xv = x_ref[...]
        r = lax.rsqrt(jnp.mean(xv * xv, axis=-1, keepdims=True) + EPS)
        h = _bf((xv * r) * g_ref[...])
        h_ref[...] = h
        p_ref[...] = _dot_nt(h, w_ref[...])

    return pl.pallas_call(
        body, name="norm_proj", out_shape=(SDS((T, N), F32), SDS((T, D), BF16)), grid=(T // tm,),
        in_specs=[BS((tm, D), lambda i: (i, 0)), BS((1, D), lambda i: (0, 0)), BS((N, D), lambda i: (0, 0))],
        out_specs=(BS((tm, N), lambda i: (i, 0)), BS((tm, D), lambda i: (i, 0))),
        compiler_params=_cp(("parallel",)),
    )(x, g, w)


TK = 256
SB_DEAD = -104.0
SB_MASKED = -1e30
CNT_LANE = SB_HD - 1


def _sb_masks():
    row = lax.broadcasted_iota(jnp.int32, (2 * QT, TK), 0) & (QT - 1)
    col = lax.broadcasted_iota(jnp.int32, (2 * QT, TK), 1)
    lane = lax.broadcasted_iota(jnp.int32, (1, LANE), 1)
    kr = lax.broadcasted_iota(jnp.int32, (TK, TK), 0)
    kc = lax.broadcasted_iota(jnp.int32, (TK, TK), 1)
    return row, col, lane, kr, kc


def _stack_heads(x, lane):
    return jnp.concatenate([_bf(jnp.where((lane // SB_HD) == hh, x, 0.0)) for hh in range(2)], axis=0)


def _sb_fwd(proj, S, shards):
    T = proj.shape[0]
    nq = S // QT
    scale = SB_HD ** -0.5
    nb, ns = T // S, len(shards)

    def body(qkv_ref, *rest):
        sh_refs, (o_ref, tt_ref), g_refs = rest[:ns], rest[ns:ns + 2], rest[ns + 2:2 * ns + 2]
        sems = rest[2 * ns + 2:]
        first = jnp.logical_and(pl.program_id(0) == 0, pl.program_id(1) == 0)
        last = jnp.logical_and(pl.program_id(0) == nb - 1, pl.program_id(1) == 3)

        @pl.when(first)
        def _():
            _exchange_ops(sh_refs, g_refs, True, sems, "start")

        row, col, lane, kr, kc = _sb_masks()
        msuf = _bf(kr > kc)

        def qloop(qt, _):
            r0 = pl.multiple_of(qt * QT, QT)
            qs = _stack_heads(qkv_ref[pl.ds(r0, QT), 0:128] * scale, lane)

            def live(st):
                it, _, cy = st
                return jnp.logical_and(it <= qt, jnp.max(cy) > SB_DEAD)

            def step(st):
                it, acc, cy = st
                kt = qt - it
                k0 = pl.multiple_of(kt * TK, TK)
                kv = _bf(qkv_ref[pl.ds(k0, TK), 128:256])
                vv = _bf(qkv_ref[pl.ds(k0, TK), 256:384])
                strict = (col + (kt - qt) * TK) < row
                z = jnp.where(strict, _dot_nt(qs, kv), SB_MASKED)
                sp = _softplus(z)
                lg = -sp
                after = cy + _split_dot(lg, msuf, 2)
                w = jnp.exp((z - sp) + after)
                return it + 1, acc + _dot(_bf(w), vv), cy + jnp.sum(lg, axis=1, keepdims=True)

            it, acc, cy = lax.while_loop(
                live, step, (jnp.int32(0), jnp.zeros((2 * QT, LANE), F32), jnp.zeros((2 * QT, 1), F32)))
            o_ref[pl.ds(r0, QT), :] = jnp.where(lane < SB_HD, acc[:QT], acc[QT:])
            tt = jnp.where(lane < SB_HD, cy[:QT], cy[QT:])
            tt_ref[pl.ds(r0, QT), :] = jnp.where(lane == CNT_LANE, it.astype(F32), tt)
            return 0

        lax.fori_loop(0, nq, qloop, 0)

        @pl.when(last)
        def _():
            _exchange_ops(sh_refs, g_refs, True, sems, "wait")

    hbm = BS(memory_space=pltpu.HBM)
    col_spec = BS((S, LANE), lambda b, p: (b, p))
    return pl.pallas_call(
        body, name="sb_fwd",
        out_shape=(SDS((T, SB_W), F32), SDS((T, SB_W), F32)) + tuple(SDS((N_DEV,) + s.shape, s.dtype) for s in shards),
        grid=(nb, 4),
        in_specs=[BS((S, 384), lambda b, p: (b, p))] + [hbm] * ns,
        out_specs=(col_spec, col_spec) + (hbm,) * ns,
        scratch_shapes=_exchange_sems(ns),
        compiler_params=_cp(("arbitrary", "arbitrary")),
    )(proj, *shards)


def _log_sigmoid(x):
    return jnp.minimum(x, 0.0) - jnp.log1p(jnp.exp(-jnp.abs(x)))


def _gla_masks():
    r = lax.broadcasted_iota(jnp.int32, (GR, GR), 0)
    c = lax.broadcasted_iota(jnp.int32, (GR, GR), 1)
    same = (r // CHUNK) == (c // CHUNK)
    causal = jnp.logical_and(same, r >= c)
    lane = lax.broadcasted_iota(jnp.int32, (1, LANE), 1)
    return causal, _bf(causal), _bf(jnp.logical_and(same, r <= c)), lane


def _gla_group_terms(blk_ref, glr_ref, wgu, bgu, r0, tri_incl):
    q = blk_ref[pl.ds(r0, GR), 0:128]
    k = blk_ref[pl.ds(r0, GR), 128:256]
    v = blk_ref[pl.ds(r0, GR), 256:512]
    pre = _dot(_bf(glr_ref[pl.ds(r0, GR), :]), wgu) + bgu
    la = _log_sigmoid(pre) / GATE_NORM
    b = _split_dot(la, tri_incl, 3, left=False)
    b_last = _per_chunk(lambda rows: b[rows.stop - 1:rows.stop])
    eb = jnp.exp(b)
    qd = (q * (GLA_DK ** -0.5)) * eb
    ki = k * jnp.exp(-b)
    ke = k * jnp.exp(b_last - b)
    decay = jnp.exp(b_last)
    return v, pre, b, b_last, eb, qd, ki, ke, decay


def _chunk_rows(n):
    return slice(n * CHUNK, (n + 1) * CHUNK)


def _per_chunk(row_fn):
    return jnp.concatenate(
        [jnp.broadcast_to(row_fn(_chunk_rows(n)), (CHUNK, LANE)) for n in range(GLA_G)], axis=0)


def _gla_fwd(proj, wgu, bgu, S):
    T = proj.shape[0]
    ng = S // GR

    def body(blk_ref, glr_ref, wgu_ref, bgu_ref, o_ref):
        causal, tri_incl, _, lane = _gla_masks()
        wg = _bf(wgu_ref[...])
        bg = bgu_ref[...]

        def group(g, states):
            r0 = pl.multiple_of(g * GR, GR)
            v, _, _, _, _, qd, ki, ke, decay = _gla_group_terms(blk_ref, glr_ref, wg, bg, r0, tri_incl)
            kib, keb = _bf(ki), _bf(ke)
            new_states, outs = [], []
            for hh in range(2):
                hm = (lane // GLA_DK) == hh
                qm = _bf(jnp.where(hm, qd, 0.0))
                vh = _bf(v[:, 128 * hh:128 * hh + 128])
                attn = jnp.where(causal, _dot_nt(qm, kib), 0.0)
                o_intra = _dot(_bf(attn), vh)
                st = states[hh]
                parts = []
                for n in range(GLA_G):
                    rows = _chunk_rows(n)
                    parts.append(o_intra[rows] + _dot_nt(qm[rows], _bf(st)))
                    st = st * decay[n * CHUNK:n * CHUNK + 1] + _dot_tn(vh[rows], keb[rows])
                outs.append(jnp.concatenate(parts, axis=0))
                new_states.append(st)
            o_ref[pl.ds(r0, GR), :] = jnp.concatenate(outs, axis=1)
            return tuple(new_states)

        z = jnp.zeros((GLA_DV, LANE), F32)
        lax.fori_loop(0, ng, group, (z, z))

    return pl.pallas_call(
        body, name="gla_fwd", out_shape=SDS((T, GLA_W), F32), grid=(T // S, 2),
        in_specs=[BS((S, 512), lambda b, p: (b, 3 + p)), BS((S, LANE), lambda b, p: (b, GLR_BLK)),
                  BS((LANE, LANE), lambda b, p: (0, p)), BS((1, LANE), lambda b, p: (0, p))],
        out_specs=BS((S, 256), lambda b, p: (b, p)),
        compiler_params=_cp(("parallel", "parallel")),
    )(proj, proj, wgu, bgu)


def _head_blockdiag(width, hd):
    r = lax.broadcasted_iota(jnp.int32, (width, width), 0) // hd
    c = lax.broadcasted_iota(jnp.int32, (width, width), 1) // hd
    return _bf(r == c)


def _mix_out(o_sb, o_gla, proj, x, g_sb, g_gla, w_out, g2):
    T = x.shape[0]
    tm = _tile(T, 512)

    def body(osb_ref, ogl_ref, og_ref, x_ref, gsb_ref, ggl_ref, w_ref, g2_ref, x1_ref, oc_ref, h2_ref):
        bd64 = _head_blockdiag(SB_W, SB_HD)
        bd128 = _head_blockdiag(GLA_W, GLA_DV)
        o = osb_ref[...]
        r = lax.rsqrt(_split_dot(o * o, bd64, 2) * (1.0 / SB_HD) + EPS)
        c_sb = (o * r) * gsb_ref[...]
        o = ogl_ref[...]
        r = lax.rsqrt(_split_dot(o * o, bd128, 2) * (1.0 / GLA_DV) + EPS)
        og = og_ref[...]
        c_gl = ((o * r) * ggl_ref[...]) * (og * jax.nn.sigmoid(og))
        oc = _bf(jnp.concatenate([c_sb, c_gl], axis=1))
        oc_ref[...] = oc
        x1 = x_ref[...] + _dot(oc, w_ref[...])
        x1_ref[...] = x1
        r2 = lax.rsqrt(jnp.mean(x1 * x1, axis=-1, keepdims=True) + EPS)
        h2_ref[...] = _bf((x1 * r2) * g2_ref[...])

    row = lambda w: BS((tm, w), lambda i: (i, 0))
    vec = lambda w: BS((1, w), lambda i: (0, 0))
    return pl.pallas_call(
        body, name="mix_out", out_shape=(SDS((T, D), F32), SDS((T, D), BF16), SDS((T, D), BF16)), grid=(T // tm,),
        in_specs=[row(SB_W), row(GLA_W), BS((tm, 512), lambda i: (i, OG_BLK)), row(D), vec(SB_W), vec(GLA_W),
                  BS((D, D), lambda i: (0, 0)), vec(D)],
        out_specs=(row(D), row(D), row(D)),
        compiler_params=_cp(("parallel",)),
    )(o_sb, o_gla, proj, x, g_sb, g_gla, w_out, g2)


CONV_ROWS = 256
CONV_TC = 256


def _rows_before(ref, r0, first):
    prev = ref[pl.ds(pl.multiple_of(jnp.maximum(r0 - 8, 0), 8), 8), :]
    return jnp.where(first, 0.0, prev)


def _shift_down(cur, prev8, k):
    cat = jnp.concatenate([prev8, cur], axis=0)
    return pltpu.roll(cat, k, 0)[8:]


def _shift_up(cur, next8, k):
    cat = jnp.concatenate([cur, next8], axis=0)
    return pltpu.roll(cat, cat.shape[0] - k, 0)[:cur.shape[0]]


def _conv_at(h_ref, cw, cb, r0, rows, first):
    cur = h_ref[pl.ds(r0, rows), :]
    prev8 = _rows_before(h_ref, r0, first)
    u = cb + cw[0:1, :] * _shift_down(cur, prev8, 2)
    u = u + cw[1:2, :] * _shift_down(cur, prev8, 1)
    return u + cw[2:3, :] * cur


NJ = D_FF // CONV_TC


def _conv_gate(hup, cw, cb, S):
    T = hup.shape[0]
    rows = min(CONV_ROWS, S)
    nr = S // rows

    def body(ha_ref, hv_ref, cwa_ref, cwv_ref, cba_ref, cbv_ref, act_ref, ua_ref, uv_ref):
        cwa, cwv, cba, cbv = cwa_ref[...], cwv_ref[...], cba_ref[...], cbv_ref[...]

        def step(c, _):
            r0 = pl.multiple_of(c * rows, rows)
            ua = _conv_at(ha_ref, cwa, cba, r0, rows, c == 0)
            uv = _conv_at(hv_ref, cwv, cbv, r0, rows, c == 0)
            ua_ref[pl.ds(r0, rows), :] = _bf(ua)
            uv_ref[pl.ds(r0, rows), :] = _bf(uv)
            act_ref[pl.ds(r0, rows), :] = _bf((ua * jax.nn.sigmoid(ua)) * uv)
            return 0

        lax.fori_loop(0, nr, step, 0)

    blk = lambda o: BS((S, CONV_TC), lambda b, j: (b, j + o))
    w3 = lambda o: BS((3, CONV_TC), lambda b, j: (0, j + o))
    w1 = lambda o: BS((1, CONV_TC), lambda b, j: (0, j + o))
    return pl.pallas_call(
        body, name="conv_gate", out_shape=(SDS((T, D_FF), BF16), SDS((T, D_FF), BF16), SDS((T, D_FF), BF16)),
        grid=(T // S, NJ),
        in_specs=[blk(0), blk(NJ), w3(0), w3(NJ), w1(0), w1(NJ)], out_specs=(blk(0), blk(0), blk(0)),
        compiler_params=_cp(("parallel", "parallel")),
    )(hup, hup, cw, cw, cb, cb)


def _down_loss(act, w_down, x1, tgt, g3):
    T = x1.shape[0]
    tm = _tile(T, 512)

    def body(a_ref, w_ref, x1_ref, t_ref, g_ref, dx_ref, dg_ref, ls_ref):
        @pl.when(pl.program_id(0) == 0)
        def _():
            dg_ref[...] = jnp.zeros_like(dg_ref)
            ls_ref[...] = jnp.zeros_like(ls_ref)

        g = g_ref[...]
        x2 = x1_ref[...] + _dot(a_ref[...], w_ref[...])
        r = lax.rsqrt(jnp.mean(x2 * x2, axis=-1, keepdims=True) + EPS)
        xh = x2 * r
        e = xh * g - t_ref[...]
        ls_ref[...] += 0.5 * jnp.sum(jnp.mean(e * e, axis=-1, keepdims=True), axis=0, keepdims=True)
        dy = e * (1.0 / D)
        dxh = dy * g
        dx_ref[...] = r * (dxh - xh * jnp.mean(dxh * xh, axis=-1, keepdims=True))
        dg_ref[...] += jnp.sum(dy * xh, axis=0, keepdims=True)

    row = lambda w: BS((tm, w), lambda i: (i, 0))
    return pl.pallas_call(
        body, name="down_loss", out_shape=(SDS((T, D), F32), SDS((1, D), F32), SDS((1, LANE), F32)), grid=(T // tm,),
        in_specs=[row(D_FF), BS((D_FF, D), lambda i: (0, 0)), row(D), row(D), BS((1, D), lambda i: (0, 0))],
        out_specs=(row(D), BS((1, D), lambda i: (0, 0)), BS((1, LANE), lambda i: (0, 0))),
        compiler_params=_cp(("arbitrary",)),
    )(act, w_down, x1, tgt, g3)


def _conv_gate_bwd(hup, u_a, u_v, dact, cw, S):
    T = hup.shape[0]
    rows = min(CONV_ROWS, S)
    nr = S // rows

    def body(ha_ref, hv_ref, ua_ref, uv_ref, da_ref, cwa_ref, cwv_ref,
             dha_ref, dhv_ref, dcwa_ref, dcwv_ref, dcba_ref, dcbv_ref):
        @pl.when(pl.program_id(1) == 0)
        def _():
            for r in (dcwa_ref, dcwv_ref, dcba_ref, dcbv_ref):
                r[...] = jnp.zeros_like(r)

        cwa, cwv = cwa_ref[...], cwv_ref[...]

        def du_at(r0, n):
            ua = ua_ref[pl.ds(r0, n), :].astype(F32)
            uv = uv_ref[pl.ds(r0, n), :].astype(F32)
            da = da_ref[pl.ds(r0, n), :]
            sg = jax.nn.sigmoid(ua)
            dua = (da * uv) * (sg * (1.0 + ua * (1.0 - sg)))
            duv = da * (ua * sg)
            return dua, duv

        def step(c, _):
            r0 = pl.multiple_of(c * rows, rows)
            last = c == nr - 1
            dua, duv = du_at(r0, rows)
            n0 = pl.multiple_of(jnp.minimum(r0 + rows, S - 16), 16)
            nua, nuv = du_at(n0, 16)
            nua = jnp.where(last, 0.0, nua[:8])
            nuv = jnp.where(last, 0.0, nuv[:8])
            for (h_ref, cw, du, nu, dh_ref, dcw_ref, dcb_ref) in (
                    (ha_ref, cwa, dua, nua, dha_ref, dcwa_ref, dcba_ref),
                    (hv_ref, cwv, duv, nuv, dhv_ref, dcwv_ref, dcbv_ref)):
                up1, up2 = _shift_up(du, nu, 1), _shift_up(du, nu, 2)
                dh_ref[pl.ds(r0, rows), :] = _bf(cw[2:3, :] * du + cw[1:2, :] * up1 + cw[0:1, :] * up2)
                cur = h_ref[pl.ds(r0, rows), :]
                dcw_ref[0:1, :] += jnp.sum(up2 * cur, axis=0, keepdims=True)
                dcw_ref[1:2, :] += jnp.sum(up1 * cur, axis=0, keepdims=True)
                dcw_ref[2:3, :] += jnp.sum(du * cur, axis=0, keepdims=True)
                dcb_ref[...] += jnp.sum(du, axis=0, keepdims=True)
            return 0

        lax.fori_loop(0, nr, step, 0)

    blk = lambda o: BS((S, CONV_TC), lambda j, b: (b, j + o))
    w3 = lambda o: BS((3, CONV_TC), lambda j, b: (0, j + o))
    w1 = BS((1, CONV_TC), lambda j, b: (0, j))
    return pl.pallas_call(
        body, name="conv_gate_bwd",
        out_shape=(SDS((T, D_FF), BF16), SDS((T, D_FF), BF16), SDS((3, D_FF), F32), SDS((3, D_FF), F32),
                   SDS((1, D_FF), F32), SDS((1, D_FF), F32)),
        grid=(NJ, T // S),
        in_specs=[blk(0), blk(NJ), blk(0), blk(0), blk(0), w3(0), w3(NJ)],
        out_specs=(blk(0), blk(0), w3(0), w3(0), w1, w1),
        compiler_params=_cp(("parallel", "arbitrary")),
    )(hup, hup, u_a, u_v, dact, cw, cw)


def _mix_bwd(docat, o_sb, o_gla, proj, g_sb, g_gla):
    T = docat.shape[0]
    tm = _tile(T, 512)

    def body(d_ref, osb_ref, ogl_ref, og_ref, gsb_ref, ggl_ref, dsb_ref, dgl_ref, dog_ref, dgsb_ref, dggl_ref):
        @pl.when(pl.program_id(0) == 0)
        def _():
            dgsb_ref[...] = jnp.zeros_like(dgsb_ref)
            dggl_ref[...] = jnp.zeros_like(dggl_ref)

        bd64 = _head_blockdiag(SB_W, SB_HD)
        bd128 = _head_blockdiag(GLA_W, GLA_DV)
        d = d_ref[:, 0:SB_W]
        o = osb_ref[...]
        r = lax.rsqrt(_split_dot(o * o, bd64, 2) * (1.0 / SB_HD) + EPS)
        n = o * r
        dn = d * gsb_ref[...]
        dgsb_ref[...] += jnp.sum(d * n, axis=0, keepdims=True)
        dsb_ref[...] = r * (dn - n * (_split_dot(dn * n, bd64, 2) * (1.0 / SB_HD)))

        d = d_ref[:, SB_W:D]
        o = ogl_ref[...]
        r = lax.rsqrt(_split_dot(o * o, bd128, 2) * (1.0 / GLA_DV) + EPS)
        n = o * r
        og = og_ref[...]
        sg = jax.nn.sigmoid(og)
        dm = d * (og * sg)
        dog_ref[...] = _bf((d * (n * ggl_ref[...])) * (sg * (1.0 + og * (1.0 - sg))))
        dn = dm * ggl_ref[...]
        dggl_ref[...] += jnp.sum(dm * n, axis=0, keepdims=True)
        dgl_ref[...] = r * (dn - n * (_split_dot(dn * n, bd128, 2) * (1.0 / GLA_DV)))

    row = lambda w: BS((tm, w), lambda i: (i, 0))
    vec = lambda w: BS((1, w), lambda i: (0, 0))
    ogb = BS((tm, 512), lambda i: (i, OG_BLK))
    return pl.pallas_call(
        body, name="mix_bwd",
        out_shape=(SDS((T, SB_W), F32), SDS((T, GLA_W), F32), SDS((T, PROJ_W), BF16), SDS((1, SB_W), F32),
                   SDS((1, GLA_W), F32)),
        grid=(T // tm,),
        in_specs=[row(D), row(SB_W), row(GLA_W), ogb, vec(SB_W), vec(GLA_W)],
        out_specs=(row(SB_W), row(GLA_W), ogb, vec(SB_W), vec(GLA_W)),
        compiler_params=_cp(("arbitrary",)),
    )(docat, o_sb, o_gla, proj, g_sb, g_gla)


def _sb_bwd(proj, tt, do, dproj, S, pieces):
    T = proj.shape[0]
    nq = S // QT
    scale = SB_HD ** -0.5
    nb, ns = T // S, len(pieces)

    def body(qkv_ref, tt_ref, do_ref, dp_in_ref, *rest):
        del dp_in_ref
        pc_refs, dp_ref, got_refs = rest[:ns], rest[ns], rest[ns + 1:2 * ns + 1]
        dk_acc, dv_acc = rest[2 * ns + 1:2 * ns + 3]
        sems = rest[2 * ns + 3:]
        first = jnp.logical_and(pl.program_id(0) == 0, pl.program_id(1) == 0)
        last = jnp.logical_and(pl.program_id(0) == nb - 1, pl.program_id(1) == 3)

        @pl.when(first)
        def _():
            _exchange_ops(pc_refs, got_refs, False, sems, "start")

        row, col, lane, kr, kc = _sb_masks()
        mincl = _bf(kr <= kc)
        mexcl = _bf(kr < kc)
        dk_acc[...] = jnp.zeros_like(dk_acc)
        dv_acc[...] = jnp.zeros_like(dv_acc)

        def qloop(qt, _):
            r0 = pl.multiple_of(qt * QT, QT)
            qs = _stack_heads(qkv_ref[pl.ds(r0, QT), 0:128] * scale, lane)
            dos = _stack_heads(do_ref[pl.ds(r0, QT), :], lane)
            ttv = tt_ref[pl.ds(r0, QT), :]
            tot = jnp.concatenate([ttv[:, 0:1], ttv[:, SB_HD:SB_HD + 1]], axis=0)
            walked = jnp.max(ttv[:, CNT_LANE:CNT_LANE + 1]).astype(jnp.int32)

            def step(kt, st):
                dq, lc, pc = st
                k0 = pl.multiple_of(kt * TK, TK)
                kv = _bf(qkv_ref[pl.ds(k0, TK), 128:256])
                vv = _bf(qkv_ref[pl.ds(k0, TK), 256:384])
                strict = (col + (kt - qt) * TK) < row
                z = jnp.where(strict, _dot_nt(qs, kv), SB_MASKED)
                sp = _softplus(z)
                lg = -sp
                after = tot - (lc + _split_dot(lg, mincl, 2))
                gl = z - sp
                w = jnp.exp(gl + after)
                du = w * _dot_nt(dos, vv)
                beta = jnp.exp(gl)
                pex = pc + _split_dot(du, mexcl, 2)
                dz = _bf(du - beta * (du + pex))
                dk_acc[pl.ds(k0, TK), :] += _dot_tn(dz, qs)
                dv_acc[pl.ds(k0, TK), :] += _dot_tn(_bf(w), dos)
                return (dq + _dot(dz, kv), lc + jnp.sum(lg, axis=1, keepdims=True),
                        pc + jnp.sum(du, axis=1, keepdims=True))

            zc = jnp.zeros((2 * QT, 1), F32)
            dq, _, _ = lax.fori_loop(qt - walked + 1, qt + 1, step, (jnp.zeros((2 * QT, LANE), F32), zc, zc))
            dp_ref[pl.ds(r0, QT), 0:128] = _bf(jnp.where(lane < SB_HD, dq[:QT], dq[QT:]) * scale)
            return 0

        lax.fori_loop(0, nq, qloop, 0)
        dp_ref[:, 128:256] = _bf(dk_acc[...])
        dp_ref[:, 256:384] = _bf(dv_acc[...])

        @pl.when(last)
        def _():
            _exchange_ops(pc_refs, got_refs, False, sems, "wait")

    blk = BS((S, 384), lambda b, p: (b, p))
    col_spec = BS((S, LANE), lambda b, p: (b, p))
    hbm = BS(memory_space=pltpu.HBM)
    return pl.pallas_call(
        body, name="sb_bwd", out_shape=(SDS((T, PROJ_W), BF16),) + tuple(SDS(s.shape, s.dtype) for s in pieces),
        grid=(nb, 4),
        in_specs=[blk, col_spec, col_spec, BS(memory_space=pl.ANY)] + [hbm] * ns, out_specs=(blk,) + (hbm,) * ns,
        scratch_shapes=[pltpu.VMEM((S, LANE), F32), pltpu.VMEM((S, LANE), F32)] + _exchange_sems(ns),
        input_output_aliases={3: 0},
        compiler_params=_cp(("arbitrary", "arbitrary")),
    )(proj, tt, do, dproj, *pieces)


def _gla_bwd(proj, wgu, bgu, do, dproj, S):
    T = proj.shape[0]
    nc, ng = S // CHUNK, S // GR

    def body(blk_ref, glr_ref, wgu_ref, bgu_ref, do_ref, dp_in_ref, dp_ref, dpre_ref, st_ref):
        del dp_in_ref
        causal, tri_incl, tri_rev, lane = _gla_masks()
        wg = _bf(wgu_ref[...])
        bg = bgu_ref[...]

        def fwd_group(g, states):
            r0 = pl.multiple_of(g * GR, GR)
            v, _, _, _, _, _, _, ke, decay = _gla_group_terms(blk_ref, glr_ref, wg, bg, r0, tri_incl)
            keb = _bf(ke)
            new_states = []
            for hh in range(2):
                vh = _bf(v[:, 128 * hh:128 * hh + 128])
                st = states[hh]
                for n in range(GLA_G):
                    rows = _chunk_rows(n)
                    st_ref[hh, g * GLA_G + n] = st
                    st = st * decay[n * CHUNK:n * CHUNK + 1] + _dot_tn(vh[rows], keb[rows])
                new_states.append(st)
            return tuple(new_states)

        z = jnp.zeros((GLA_DV, LANE), F32)
        lax.fori_loop(0, ng, fwd_group, (z, z))

        def bwd_group(it, dstates):
            g = ng - 1 - it
            r0 = pl.multiple_of(g * GR, GR)
            v, pre, b, b_last, eb, qd, ki, ke, decay = _gla_group_terms(
                blk_ref, glr_ref, wg, bg, r0, tri_incl)
            kib = _bf(ki)
            dqd = jnp.zeros((GR, LANE), F32)
            dki = jnp.zeros((GR, LANE), F32)
            dke = jnp.zeros((GR, LANE), F32)
            ddec = jnp.zeros((GR, LANE), F32)
            new_dstates, dvs = [], []
            for hh in range(2):
                hm = (lane // GLA_DK) == hh
                qm = _bf(jnp.where(hm, qd, 0.0))
                kem = _bf(jnp.where(hm, ke, 0.0))
                vh = _bf(v[:, 128 * hh:128 * hh + 128])
                doh = _bf(do_ref[pl.ds(r0, GR), 128 * hh:128 * hh + 128])
                attn = _bf(jnp.where(causal, _dot_nt(qm, kib), 0.0))
                dattn = _bf(jnp.where(causal, _dot_nt(doh, vh), 0.0))
                dv_intra = _dot_tn(attn, doh)
                dqd_intra = _dot(dattn, kib)
                dki = dki + _dot_tn(dattn, qm)
                dst = dstates[hh]
                dv_p, dqd_p, dke_p, ddec_p = [None] * GLA_G, [None] * GLA_G, [None] * GLA_G, [None] * GLA_G
                for n in reversed(range(GLA_G)):
                    rows = _chunk_rows(n)
                    st = st_ref[hh, g * GLA_G + n]
                    dv_p[n] = dv_intra[rows] + _dot_nt(kem[rows], _bf(dst))
                    dqd_p[n] = dqd_intra[rows] + _dot(doh[rows], _bf(st))
                    dke_p[n] = _dot(vh[rows], _bf(dst))
                    ddec_p[n] = jnp.broadcast_to(jnp.sum(dst * st, axis=0, keepdims=True), (CHUNK, LANE))
                    dst = dst * decay[n * CHUNK:n * CHUNK + 1] + _dot_tn(doh[rows], qm[rows])
                dvs.append(jnp.concatenate(dv_p, axis=0))
                dqd = dqd + jnp.where(hm, jnp.concatenate(dqd_p, axis=0), 0.0)
                dke = dke + jnp.where(hm, jnp.concatenate(dke_p, axis=0), 0.0)
                ddec = ddec + jnp.where(hm, jnp.concatenate(ddec_p, axis=0), 0.0)
                new_dstates.append(dst)
            einv = jnp.exp(-b)
            eend = jnp.exp(b_last - b)
            dq = (dqd * eb) * (GLA_DK ** -0.5)
            dk = dki * einv + dke * eend
            db = dqd * qd - dki * ki - dke * ke
            dkk = dke * ke
            db_last = _per_chunk(lambda rows: jnp.sum(dkk[rows], axis=0, keepdims=True)) + ddec * decay
            dla = _split_dot(db, tri_rev, 2, left=False) + db_last
            dpre_ref[pl.ds(r0, GR), :] = (dla * (1.0 / GATE_NORM)) * (1.0 - jax.nn.sigmoid(pre))
            dp_ref[pl.ds(r0, GR), 0:128] = _bf(dq)
            dp_ref[pl.ds(r0, GR), 128:256] = _bf(dk)
            dp_ref[pl.ds(r0, GR), 256:512] = _bf(jnp.concatenate(dvs, axis=1))
            return tuple(new_dstates)

        lax.fori_loop(0, ng, bwd_group, (z, z))

    return pl.pallas_call(
        body, name="gla_bwd", out_shape=(SDS((T, PROJ_W), BF16), SDS((T, GLA_KW), F32)), grid=(T // S, 2),
        in_specs=[BS((S, 512), lambda b, p: (b, 3 + p)), BS((S, LANE), lambda b, p: (b, GLR_BLK)),
                  BS((LANE, LANE), lambda b, p: (0, p)), BS((1, LANE), lambda b, p: (0, p)),
                  BS((S, 256), lambda b, p: (b, p)), BS(memory_space=pl.ANY)],
        out_specs=(BS((S, 512), lambda b, p: (b, 3 + p)), BS((S, LANE), lambda b, p: (b, p))),
        scratch_shapes=[pltpu.VMEM((2, nc, GLA_DV, LANE), F32)],
        input_output_aliases={5: 0},
        compiler_params=_cp(("parallel", "parallel")),
    )(proj, proj, wgu, bgu, do, dproj)


def _gate_bwd(dpre, proj, wgu, dproj):
    T = dpre.shape[0]
    tm = _tile(T, 512)

    def body(dpre_ref, glr_ref, wgu_ref, dp_in_ref, dp_ref, dw_ref, db_ref):
        del dp_in_ref

        @pl.when(pl.program_id(0) == 0)
        def _():
            dw_ref[...] = jnp.zeros_like(dw_ref)
            db_ref[...] = jnp.zeros_like(db_ref)

        dpre = dpre_ref[...]
        dp_ref[...] = _bf(_dot_nt(_bf(dpre), _bf(wgu_ref[...])))
        dw_ref[...] += _dot_tn(_bf(glr_ref[...]), _bf(dpre))
        db_ref[...] += jnp.sum(dpre, axis=0, keepdims=True)

    glr = BS((tm, LANE), lambda i: (i, GLR_BLK))
    return pl.pallas_call(
        body, name="gate_bwd",
        out_shape=(SDS((T, PROJ_W), BF16), SDS((LANE, GLA_KW), F32), SDS((1, GLA_KW), F32)), grid=(T // tm,),
        in_specs=[BS((tm, GLA_KW), lambda i: (i, 0)), glr, BS((LANE, GLA_KW), lambda i: (0, 0)),
                  BS(memory_space=pl.ANY)],
        out_specs=(glr, BS((LANE, GLA_KW), lambda i: (0, 0)), BS((1, GLA_KW), lambda i: (0, 0))),
        input_output_aliases={3: 0},
        compiler_params=_cp(("arbitrary",)),
    )(dpre, proj, wgu, dproj)


def _exchange_sems(n):
    return [pltpu.SemaphoreType.DMA((n * (N_DEV - 1),)), pltpu.SemaphoreType.DMA((n * (N_DEV - 1),)),
            pltpu.SemaphoreType.DMA((n,))]


def _exchange_ops(srcs, outs, gather, sems, act):
    ssem, rsem, lsem = sems
    x, y, c = lax.axis_index("x"), lax.axis_index("y"), lax.axis_index("c")
    me = 4 * x + 2 * y + c
    for i, (s_ref, o_ref) in enumerate(zip(srcs, outs)):
        for k in range(1, N_DEV):
            px = (x + ((k >> 2) & 1)) % 2
            py = (y + ((k >> 1) & 1)) % 2
            pc = (c + (k & 1)) % 2
            peer = 4 * px + 2 * py + pc
            n = i * (N_DEV - 1) + k - 1
            out = pltpu.make_async_remote_copy(
                src_ref=s_ref if gather else s_ref.at[peer], dst_ref=o_ref.at[me],
                send_sem=ssem.at[n], recv_sem=rsem.at[n],
                device_id=(px, py, pc), device_id_type=pl.DeviceIdType.MESH)
            if act == "start":
                out.start()
            else:
                out.wait_send()
                pltpu.make_async_remote_copy(
                    src_ref=s_ref if gather else s_ref.at[me], dst_ref=o_ref.at[peer],
                    send_sem=ssem.at[n], recv_sem=rsem.at[n],
                    device_id=(x, y, c), device_id_type=pl.DeviceIdType.MESH).wait_recv()
        mine = pltpu.make_async_copy(s_ref if gather else s_ref.at[me], o_ref.at[me], lsem.at[i])
        if act == "start":
            mine.start()
        else:
            mine.wait()


def _gather_two_level(src, name):
    def body(s_ref, o_ref, ssem, rsem, lsem):
        x, y, c = lax.axis_index("x"), lax.axis_index("y"), lax.axis_index("c")
        me, sibling = (x, y, c), (x, y, 1 - c)
        chips = [(1 - x, y), (x, 1 - y), (1 - x, 1 - y)]

        def slab(px, py, pc):
            return o_ref.at[4 * px + 2 * py + pc]

        def copy(k, block, to, src_ref=None):
            return pltpu.make_async_remote_copy(
                src_ref=slab(*block) if src_ref is None else src_ref, dst_ref=slab(*block),
                send_sem=ssem.at[k], recv_sem=rsem.at[k], device_id=to, device_id_type=pl.DeviceIdType.MESH)

        mine = pltpu.make_async_copy(s_ref, slab(*me), lsem)
        mine.start()
        first = [copy(0, me, sibling, s_ref)] + [copy(1 + j, me, (*chip, c), s_ref) for j, chip in enumerate(chips)]
        for cp in first:
            cp.start()
        passed = [copy(4 + j, (*chip, c), sibling) for j, chip in enumerate(chips)]
        for j, chip in enumerate(chips):
            copy(1 + j, (*chip, c), me).wait_recv()
            passed[j].start()
        copy(0, sibling, me).wait_recv()
        for j, chip in enumerate(chips):
            copy(4 + j, (*chip, 1 - c), me).wait_recv()
        for cp in first + passed:
            cp.wait_send()
        mine.wait()

    hbm = BS(memory_space=pltpu.HBM)
    return pl.pallas_call(
        body, name=name, out_shape=SDS((N_DEV,) + src.shape, src.dtype), in_specs=[hbm], out_specs=hbm,
        scratch_shapes=[pltpu.SemaphoreType.DMA((N_DEV - 1,)), pltpu.SemaphoreType.DMA((N_DEV - 1,)),
                        pltpu.SemaphoreType.DMA(())],
    )(src)


def _exchange(srcs, gather, name):
    n = len(srcs)
    shapes = [SDS((N_DEV,) + s.shape if gather else s.shape, s.dtype) for s in srcs]

    def body(*refs):
        s_refs, o_refs, sems = refs[:n], refs[n:2 * n], refs[2 * n:]
        _exchange_ops(s_refs, o_refs, gather, sems, "start")
        _exchange_ops(s_refs, o_refs, gather, sems, "wait")

    hbm = BS(memory_space=pltpu.HBM)
    return pl.pallas_call(
        body, name=name, out_shape=tuple(shapes), in_specs=[hbm] * n, out_specs=(hbm,) * n,
        scratch_shapes=_exchange_sems(n),
    )(*srcs)


def _adamw_math(w, g, m, v):
    m = ADAM_B1 * m + (1.0 - ADAM_B1) * g
    v = ADAM_B2 * v + (1.0 - ADAM_B2) * (g * g)
    m_hat = m / (1.0 - ADAM_B1 ** ADAM_STEP)
    v_hat = v / (1.0 - ADAM_B2 ** ADAM_STEP)
    delta = -ADAM_LR * (m_hat / (jnp.sqrt(v_hat) + ADAM_EPS) + ADAM_WD * w)
    return delta, m, v


def _sum_adamw(parts, w, m, v, tr, name):
    R, C = w.shape

    def body(p_ref, w_ref, m_ref, v_ref, g_ref, d_ref, nm_ref, nv_ref):
        g = p_ref[0].astype(F32)
        for d in range(1, N_DEV):
            g = g + p_ref[d].astype(F32)
        delta, nm, nv = _adamw_math(w_ref[...], g, m_ref[...], v_ref[...])
        g_ref[...] = g
        d_ref[...] = delta
        nm_ref[...] = nm
        nv_ref[...] = nv

    blk = BS((tr, C), lambda i: (i, 0))
    out = SDS((R, C), F32)
    return pl.pallas_call(
        body, name=name, out_shape=(out, out, out, out), grid=(R // tr,),
        in_specs=[BS((N_DEV, tr, C), lambda i: (0, i, 0)), blk, blk, blk], out_specs=(blk, blk, blk, blk),
        compiler_params=_cp(("parallel",)),
    )(parts, w, m, v)


def _flat_pad_rows(parts, rows):
    flat = jnp.concatenate([p.reshape(-1) for p in parts])
    return jnp.pad(flat, (0, rows * D - flat.shape[0])).reshape(rows, D)


SMALL_ROWS = 16
SHARD_SMALL_ROWS = 3


def kernel(x, attn_norm_g, w_in, w_gate_up, b_gate_up, sb_out_g, gla_out_g, w_out, ffn_norm_g, w_ffn_up, conv_w, conv_b, w_ffn_down, final_norm_g, loss_target, m_attn_norm_g, m_w_in, m_w_gate_up, m_b_gate_up, m_sb_out_g, m_gla_out_g, m_w_out, m_ffn_norm_g, m_w_ffn_up, m_conv_w, m_conv_b, m_w_ffn_down, m_final_norm_g, v_attn_norm_g, v_w_in, v_w_gate_up, v_b_gate_up, v_sb_out_g, v_gla_out_g, v_w_out, v_ffn_norm_g, v_w_ffn_up, v_conv_w, v_conv_b, v_w_ffn_down, v_final_norm_g):
    Bd, S, _ = x.shape
    T = Bd * S
    x2d = x.reshape(T, D)
    tgt = loss_target.reshape(T, D)
    c_up = w_ffn_up.shape[2]
    c_gu = w_gate_up.shape[2]
    c_in = w_in.shape[2]

    n_gu = GATE_RANK * c_gu
    rows_bf = lambda w: w[0].T.astype(BF16)
    small_w = lambda wgu, cw: _flat_pad_rows([wgu, cw], SHARD_SMALL_ROWS)

    g_in = _gather_two_level(rows_bf(w_in), "gather_w_in")
    w_in_pt = jnp.pad(g_in.reshape(IN_COLS, D), ((0, 1), (0, 0)))[_PERM]
    g3 = final_norm_g.reshape(1, D)

    proj, h1 = _norm_proj(x2d, attn_norm_g, w_in_pt)
    o_sb, tt, g_up, g_down, g_out, gs = _sb_fwd(
        proj, S, [rows_bf(w_ffn_up), w_ffn_down[0].astype(BF16), w_out[0].astype(BF16),
                  _flat_pad_rows([w_gate_up, conv_w], 8)])
    w_out_f = g_out.reshape(D, D)
    gsf = gs.reshape(N_DEV, -1)
    wgu_f = jnp.transpose(gsf[:, :n_gu].reshape(N_DEV, GATE_RANK, c_gu), (1, 0, 2)).reshape(GATE_RANK, GLA_KW)
    cw_f = jnp.transpose(gsf[:, n_gu:n_gu + 3 * c_up].reshape(N_DEV, 3, c_up), (1, 0, 2)).reshape(3, 2 * D_FF)
    wgu_p = jnp.pad(wgu_f, ((0, LANE - GATE_RANK), (0, 0)))
    w_up_t = g_up.reshape(2 * D_FF, D)
    w_down_f = g_down.reshape(D_FF, D)
    o_gla = _gla_fwd(proj, wgu_p, b_gate_up, S)
    x1, ocat, h2 = _mix_out(o_sb, o_gla, proj, x2d, sb_out_g, gla_out_g, w_out_f, ffn_norm_g)
    hup = _mm(h2, w_up_t, "nt", "ffn_up", tm=1024, tn=1408, tk=1024)
    act, u_a, u_v = _conv_gate(hup, cw_f, conv_b, S)
    dx2, dg3, loss_dev = _down_loss(act, w_down_f, x1, tgt, g3)

    dw_down = _mm(act, dx2, "tn", "dw_down", out_dtype=BF16, tm=D_FF, tn=1024, tk=1024)
    dact = _mm(dx2, w_down_f, "nt", "dact", tm=1024, tn=1408, tk=1024)
    dhup_a, dhup_v, dcw_a, dcw_v, dcb_a, dcb_v = _conv_gate_bwd(hup, u_a, u_v, dact, cw_f, S)
    dw_up_t = _mm(dhup_a, h2, "tn", "dw_up_a", out_dtype=BF16, tm=D_FF, tn=1024, tk=1024, out_rows=2 * D_FF)
    dw_up_t = _mm(dhup_v, h2, "tn", "dw_up_v", out_dtype=BF16, tm=D_FF, tn=1024, tk=1024, out_rows=2 * D_FF,
                  out_row0=D_FF, into=dw_up_t)
    dx1, dg2 = _mm(dhup_a, w_up_t, "nn", "dh2", a2=dhup_v, tm=1024, tn=1024, tk=1408,
                   norm_bwd=(x1, ffn_norm_g, dx2))

    dw_out = _mm(ocat, dx1, "tn", "dw_out", out_dtype=BF16, tm=1024, tn=1024, tk=1024)
    docat = _mm(dx1, w_out_f, "nt", "docat", tm=1024, tn=1024, tk=1024)
    do_sb, do_gla, dproj, dg_sb, dg_gla = _mix_bwd(docat, o_sb, o_gla, proj, sb_out_g, gla_out_g)
    dproj, got_up, got_down, got_out = _sb_bwd(
        proj, tt, do_sb, dproj, S,
        [dw_up_t.reshape(N_DEV, c_up, D), dw_down.reshape(N_DEV, -1, D), dw_out.reshape(N_DEV, -1, D)])
    dproj, dpre = _gla_bwd(proj, wgu_p, b_gate_up, do_gla, dproj, S)
    dproj, dwgu, dbgu = _gate_bwd(dpre, proj, wgu_p, dproj)
    dw_in_pt = _mm(dproj, h1, "tn", "dw_in", out_dtype=BF16, tm=PROJ_W, tn=1024, tk=1024)
    dx, dg1, got_in = _mm(dproj, w_in_pt, "nn", "dh1", tm=1024, tn=1024, tk=640,
                          xchg=([dw_in_pt[_INV_PERM].reshape(N_DEV, c_in, D)], False),
                          norm_bwd=(x2d, attn_norm_g, dx1))

    dcw = jnp.concatenate([dcw_a, dcw_v], axis=1)
    dwgu_pc = jnp.transpose(dwgu[:GATE_RANK].reshape(GATE_RANK, N_DEV, c_gu), (1, 0, 2)).reshape(N_DEV, -1)
    dcw_pc = jnp.transpose(dcw.reshape(3, N_DEV, c_up), (1, 0, 2)).reshape(N_DEV, -1)
    small_pc = jnp.concatenate([dwgu_pc, dcw_pc], axis=1)
    small_pc = jnp.pad(small_pc, ((0, 0), (0, SHARD_SMALL_ROWS * D - small_pc.shape[1])))
    small_pc = small_pc.reshape(N_DEV, SHARD_SMALL_ROWS, D).astype(BF16)
    rep_names = ["attn_norm_g", "b_gate_up", "sb_out_g", "gla_out_g", "ffn_norm_g", "conv_b", "final_norm_g"]
    rep_g = [dg1, dbgu, dg_sb, dg_gla, dg2, jnp.concatenate([dcb_a, dcb_v], axis=1), dg3]
    rep_w = [attn_norm_g, b_gate_up, sb_out_g, gla_out_g, ffn_norm_g, conv_b, final_norm_g]
    rep_m = [m_attn_norm_g, m_b_gate_up, m_sb_out_g, m_gla_out_g, m_ffn_norm_g, m_conv_b, m_final_norm_g]
    rep_v = [v_attn_norm_g, v_b_gate_up, v_sb_out_g, v_gla_out_g, v_ffn_norm_g, v_conv_b, v_final_norm_g]
    rep_pc = jnp.broadcast_to(_flat_pad_rows(rep_g + [loss_dev[:, 0:1]], SMALL_ROWS), (N_DEV, SMALL_ROWS, D))
    got_sm, got_rep = _exchange([small_pc, rep_pc], False, "scatter_tail")

    rows = lambda w: w[0].T
    cols = lambda r: r.T[None]
    res = {}
    res["w_in"] = [cols(r) for r in _sum_adamw(got_in, rows(w_in), rows(m_w_in), rows(v_w_in), c_in, "adamw_w_in")]
    res["w_out"] = [r[None] for r in _sum_adamw(got_out, w_out[0], m_w_out[0], v_w_out[0], w_out.shape[1],
                                                 "adamw_w_out")]
    res["w_ffn_up"] = [cols(r) for r in _sum_adamw(got_up, rows(w_ffn_up), rows(m_w_ffn_up), rows(v_w_ffn_up),
                                                    c_up // 2, "adamw_w_up")]
    res["w_ffn_down"] = [r[None] for r in _sum_adamw(got_down, w_ffn_down[0], m_w_ffn_down[0], v_w_ffn_down[0],
                                                      w_ffn_down.shape[1], "adamw_w_down")]
    sm = _sum_adamw(got_sm, small_w(w_gate_up, conv_w), small_w(m_w_gate_up, m_conv_w),
                    small_w(v_w_gate_up, v_conv_w), SHARD_SMALL_ROWS, "adamw_small_sharded")
    res["w_gate_up"] = [r.reshape(-1)[:n_gu].reshape(1, GATE_RANK, c_gu) for r in sm]
    res["conv_w"] = [r.reshape(-1)[n_gu:n_gu + 3 * c_up].reshape(1, 3, c_up) for r in sm]
    rep = _sum_adamw(got_rep, _flat_pad_rows(rep_w, SMALL_ROWS), _flat_pad_rows(rep_m, SMALL_ROWS),
                     _flat_pad_rows(rep_v, SMALL_ROWS), SMALL_ROWS, "adamw_replicated")
    o = 0
    for n, w in zip(rep_names, rep_w):
        res[n] = [r.reshape(-1)[o:o + w.size].reshape(w.shape) for r in rep]
        o += w.size

    loss = rep[0].reshape(-1)[o]
    order = ["attn_norm_g", "w_in", "w_gate_up", "b_gate_up", "sb_out_g", "gla_out_g", "w_out", "ffn_norm_g",
             "w_ffn_up", "conv_w", "conv_b", "w_ffn_down", "final_norm_g"]
    outs = [loss, dx.reshape(Bd, S, D)]
    for k in range(4):
        outs += [res[n][k] for n in order]
    return tuple(outs)
```
